```python
import jax, jax.numpy as jnp
from jax import lax
import numpy as np

D_MODEL = 2048
BATCH = 32
SEQ = 256
DEPTH = 1
DEC_BATCH = 4
DEC_SEQ = 1024
PAST_LEN = 256

GRID_W = 64
HEAD_DIM = 128
N_Q_HEADS = 8
N_KV_HEADS = 2
Q_PER_KV = N_Q_HEADS // N_KV_HEADS
ATTN_WIDTH = N_Q_HEADS * HEAD_DIM
KV_WIDTH = N_KV_HEADS * HEAD_DIM
N_MLP_GROUPS = 8
MLP_GROUP_DIM = 128
MLP_WIDTH = N_MLP_GROUPS * MLP_GROUP_DIM
MIX_WIDTH = ATTN_WIDTH + MLP_WIDTH
Q_END = ATTN_WIDTH
K_END = Q_END + KV_WIDTH
V_END = K_END + KV_WIDTH
U_END = V_END + MLP_WIDTH
IN_WIDTH = U_END + MLP_WIDTH
CHUNK = 128
WINDOW = 128
BLOCK = 128
SPAN = BLOCK + 2 * WINDOW
N_EXPERTS = 16
EXPERT_FF = 2048
CAPACITY_FACTOR = 2
ROPE_BASE = 10000.0
EPS = 1e-6
NEG_INF = -1e30

kernel_name = 'hybrid_diffusion_swa_gmlp_ecmoe_step'


def _rmsnorm(x, g):
    xf = x.astype(jnp.float32)
    y = xf * lax.rsqrt(jnp.mean(xf * xf, axis=-1, keepdims=True) + EPS)
    return (y * g.astype(jnp.float32)).astype(x.dtype)


def _adaln(cvec, w_ada, b_ada, dtype):
    m = jnp.einsum('rd,de->re', jax.nn.silu(cvec.astype(jnp.float32)), w_ada.astype(jnp.float32))
    m = (m + b_ada.astype(jnp.float32)).astype(dtype)[:, None, :]
    return jnp.split(m, 6, axis=-1)


def _modulate(x, g, shift, scale):
    return _rmsnorm(x, g) * (1 + scale) + shift


def _in_proj(h, w_in):
    B, L, _ = h.shape
    z = jnp.einsum('bld,de->ble', h, w_in)
    q, k, v, u, gv = jnp.split(z, [Q_END, K_END, V_END, U_END], axis=-1)
    q = q.reshape(B, L, N_KV_HEADS, Q_PER_KV, HEAD_DIM)
    k = k.reshape(B, L, N_KV_HEADS, HEAD_DIM)
    v = v.reshape(B, L, N_KV_HEADS, HEAD_DIM)
    return q, k, v, u, gv


def _rope_2d(x):
    L = x.shape[1]
    rows = L // GRID_W
    row = jnp.repeat(jnp.arange(rows), GRID_W)
    col = jnp.tile(jnp.arange(GRID_W), rows)
    half = HEAD_DIM // 2
    freqs = ROPE_BASE ** (-jnp.arange(0, half, 2, dtype=jnp.float32) / half)
    bshape = (1, L) + (1,) * (x.ndim - 3) + (half // 2,)
    xf = x.astype(jnp.float32)

    def rot(xa, pos):
        ang = (pos.astype(jnp.float32)[:, None] * freqs[None, :]).reshape(bshape)
        cos, sin = jnp.cos(ang), jnp.sin(ang)
        x1, x2 = jnp.split(xa, 2, axis=-1)
        return jnp.concatenate([x1 * cos - x2 * sin, x2 * cos + x1 * sin], axis=-1)

    return jnp.concatenate([rot(xf[..., :half], row), rot(xf[..., half:], col)], axis=-1).astype(x.dtype)


def _attend(q, k_parts, v_parts, masks, sink):
    B, Q = q.shape[:2]
    scale = HEAD_DIM ** -0.5
    logits = []
    for k, m in zip(k_parts, masks):
        s = jnp.einsum('bqhgd,bkhd->bhgqk', q, k, preferred_element_type=jnp.float32) * scale
        if m is not None:
            s = jnp.where(m, s, NEG_INF)
        logits.append(s)
    sink_col = jnp.broadcast_to(sink.astype(jnp.float32).reshape(1, N_KV_HEADS, Q_PER_KV, 1, 1),
                                (B, N_KV_HEADS, Q_PER_KV, Q, 1))
    p = jax.nn.softmax(jnp.concatenate([sink_col] + logits, axis=-1), axis=-1)
    out = 0
    off = 1
    for v, s in zip(v_parts, logits):
        n = s.shape[-1]
        out = out + jnp.einsum('bhgqk,bkhd->bqhgd', p[..., off:off + n].astype(v.dtype), v)
        off += n
    return out


def _context_attention(q, k, v, sink):
    B, L = q.shape[:2]
    nqb = L // BLOCK
    qb = jnp.moveaxis(q.reshape(B, nqb, BLOCK, N_KV_HEADS, Q_PER_KV, HEAD_DIM), 1, 0)
    o = lax.map(lambda qi: _attend(qi, [k], [v], [None], sink), qb)
    return jnp.moveaxis(o, 0, 1).reshape(B, L, ATTN_WIDTH)


def _latent_attention(q, k, v, k_ctx, v_ctx, sink):
    B, L = q.shape[:2]
    nqb = L // BLOCK
    pad = ((0, 0), (WINDOW, WINDOW), (0, 0), (0, 0))
    kp = jnp.pad(k, pad)
    vp = jnp.pad(v, pad)

    def block(i):
        start = i * BLOCK
        qi = lax.dynamic_slice_in_dim(q, start, BLOCK, axis=1)
        ki = lax.dynamic_slice_in_dim(kp, start, SPAN, axis=1)
        vi = lax.dynamic_slice_in_dim(vp, start, SPAN, axis=1)
        qpos = start + jnp.arange(BLOCK)
        kpos = start - WINDOW + jnp.arange(SPAN)
        mask = ((jnp.abs(qpos[:, None] - kpos[None, :]) <= WINDOW)
                & (kpos >= 0)[None, :] & (kpos < L)[None, :])
        return _attend(qi, [k_ctx, ki], [v_ctx, vi], [None, mask], sink)

    o = lax.map(block, jnp.arange(nqb))
    return jnp.moveaxis(o, 0, 1).reshape(B, L, ATTN_WIDTH)


def _merge(attn, u, gv, w_s, b_s, g_v, g_attn, g_mlp, w_out):
    B, L, _ = u.shape
    nch = L // CHUNK
    s = _rmsnorm(jax.nn.gelu(gv), g_v).reshape(B, nch, CHUNK, N_MLP_GROUPS, MLP_GROUP_DIM)
    s = jnp.einsum('gpq,bnqgc->bnpgc', w_s, s) + jnp.transpose(b_s)[None, None, :, :, None]
    mlp = jax.nn.gelu(u) * s.reshape(B, L, MLP_WIDTH)
    mixed = jnp.concatenate([_rmsnorm(attn, g_attn), _rmsnorm(mlp, g_mlp)], axis=-1)
    return jnp.einsum('blm,md->bld', mixed, w_out)


def _expert_choice_moe(h, w_router, w_gate, w_up, w_down):
    B, L, D = h.shape
    N = B * L
    C = CAPACITY_FACTOR * N // N_EXPERTS
    xs = h.reshape(N, D)
    aff = jax.nn.softmax(jnp.einsum('nd,de->ne', xs, w_router, preferred_element_type=jnp.float32), axis=-1)
    g, idx = lax.top_k(jnp.transpose(aff), C)
    xe = xs[idx]
    hid = jax.nn.silu(jnp.einsum('ecd,edf->ecf', xe, w_gate)) * jnp.einsum('ecd,edf->ecf', xe, w_up)
    out = jnp.einsum('ecf,efd->ecd', hid, w_down) * g[..., None].astype(h.dtype)
    y = jnp.zeros((N, D), h.dtype).at[idx.reshape(-1)].add(out.reshape(-1, D))
    return y.reshape(B, L, D)


def setup_inputs(seed: int = 0) -> dict:
    key = jax.random.key(seed)
    ks = jax.random.split(key, 24)

    def nrm(k, shape, scale):
        return jax.random.normal(k, shape, jnp.float32) * scale

    kv_shape = (DEC_BATCH, DEPTH, PAST_LEN, N_KV_HEADS, HEAD_DIM)
    return {
        'x_prompt': nrm(ks[0], (BATCH, SEQ, D_MODEL), 1.0),
        'x_sample': nrm(ks[1], (DEC_BATCH, DEC_SEQ, D_MODEL), 1.0),
        'cache_k': nrm(ks[2], kv_shape, 1.0),
        'cache_v': nrm(ks[3], kv_shape, 1.0),
        'c': nrm(ks[4], (DEC_BATCH, D_MODEL), 1.0),
        'c_ctx': nrm(ks[5], (D_MODEL,), 1.0),
        'w_ada': nrm(ks[6], (DEPTH, D_MODEL, 6 * D_MODEL), D_MODEL ** -0.5),
        'b_ada': nrm(ks[7], (DEPTH, 6 * D_MODEL), 0.02),
        'norm1': 1.0 + nrm(ks[8], (DEPTH, D_MODEL), 0.02),
        'w_in': nrm(ks[9], (DEPTH, D_MODEL, IN_WIDTH), D_MODEL ** -0.5),
        'attn_sink': nrm(ks[10], (DEPTH, N_Q_HEADS), 0.5),
        'w_s': nrm(ks[11], (DEPTH, N_MLP_GROUPS, CHUNK, CHUNK), CHUNK ** -0.5),
        'b_s': 1.0 + nrm(ks[12], (DEPTH, N_MLP_GROUPS, CHUNK), 0.02),
        'g_v': 1.0 + nrm(ks[13], (DEPTH, MLP_WIDTH), 0.02),
        'g_attn': 1.0 + nrm(ks[14], (DEPTH, ATTN_WIDTH), 0.02),
        'g_mlp': 1.0 + nrm(ks[15], (DEPTH, MLP_WIDTH), 0.02),
        'w_out': nrm(ks[16], (DEPTH, MIX_WIDTH, D_MODEL), MIX_WIDTH ** -0.5),
        'norm2': 1.0 + nrm(ks[17], (DEPTH, D_MODEL), 0.02),
        'w_router': nrm(ks[18], (DEPTH, D_MODEL, N_EXPERTS), D_MODEL ** -0.5),
        'w_gate': nrm(ks[19], (DEPTH, N_EXPERTS, D_MODEL, EXPERT_FF), D_MODEL ** -0.5),
        'w_up': nrm(ks[20], (DEPTH, N_EXPERTS, D_MODEL, EXPERT_FF), D_MODEL ** -0.5),
        'w_down': nrm(ks[21], (DEPTH, N_EXPERTS, EXPERT_FF, D_MODEL), EXPERT_FF ** -0.5),
        'final_norm': 1.0 + nrm(ks[22], (D_MODEL,), 0.02),
    }


def reference(x_prompt, x_sample, cache_k, cache_v, c, c_ctx, w_ada, b_ada, norm1, w_in, attn_sink,
              w_s, b_s, g_v, g_attn, g_mlp, w_out, norm2, w_router, w_gate, w_up, w_down, final_norm):
    xp = x_prompt
    xl = x_sample
    new_k = []
    new_v = []
    for l in range(DEPTH):
        sh1, sc1, ga1, sh2, sc2, ga2 = _adaln(c_ctx[None, :], w_ada[l], b_ada[l], xp.dtype)
        h = _modulate(xp, norm1[l], sh1, sc1)
        q, k, v, u, gv = _in_proj(h, w_in[l])
        attn = _context_attention(q, k, v, attn_sink[l])
        xp = xp + ga1 * _merge(attn, u, gv, w_s[l], b_s[l], g_v[l], g_attn[l], g_mlp[l], w_out[l])
        h = _modulate(xp, norm2[l], sh2, sc2)
        xp = xp + ga2 * _expert_choice_moe(h, w_router[l], w_gate[l], w_up[l], w_down[l])
        new_k.append(k)
        new_v.append(v)

        sh1, sc1, ga1, sh2, sc2, ga2 = _adaln(c, w_ada[l], b_ada[l], xl.dtype)
        h = _modulate(xl, norm1[l], sh1, sc1)
        q, k, v, u, gv = _in_proj(h, w_in[l])
        attn = _latent_attention(_rope_2d(q), _rope_2d(k), v, cache_k[:, l], cache_v[:, l], attn_sink[l])
        xl = xl + ga1 * _merge(attn, u, gv, w_s[l], b_s[l], g_v[l], g_attn[l], g_mlp[l], w_out[l])
        h = _modulate(xl, norm2[l], sh2, sc2)
        xl = xl + ga2 * _expert_choice_moe(h, w_router[l], w_gate[l], w_up[l], w_down[l])

    y_prompt = _rmsnorm(xp, final_norm)
    y_sample = _rmsnorm(xl, final_norm)
    state_k = jnp.stack(new_k, axis=1)
    state_v = jnp.stack(new_v, axis=1)
    return (y_prompt, y_sample, state_k, state_v)
```

```python
import functools

import jax
import jax.numpy as jnp
from jax import lax
from jax.experimental import pallas as pl
from jax.experimental.pallas import tpu as pltpu

F32 = jnp.float32
BF16 = jnp.bfloat16
I32 = jnp.int32

D_MODEL = 2048
GRID_W = 64
HEAD_DIM = 128
N_Q_HEADS = 8
N_KV_HEADS = 2
Q_PER_KV = N_Q_HEADS // N_KV_HEADS
ATTN_WIDTH = N_Q_HEADS * HEAD_DIM
KV_WIDTH = N_KV_HEADS * HEAD_DIM
N_MLP_GROUPS = 8
MLP_WIDTH = 1024
Q_END = ATTN_WIDTH
K_END = Q_END + KV_WIDTH
V_END = K_END + KV_WIDTH
U_END = V_END + MLP_WIDTH
IN_WIDTH = U_END + MLP_WIDTH
CHUNK = 128
WINDOW = 128
N_EXPERTS = 16
EXPERT_FF = 2048
CAPACITY_FACTOR = 2
ROPE_BASE = 10000.0
EPS = 1e-6
NEG_INF = -1e30
ATTN_SCALE = HEAD_DIM ** -0.5

LANES = 128
MIX_TILE = 256
LAT_SPAN = MIX_TILE + 2 * WINDOW
PACK_HALF = D_MODEL // 2
PACK_WIDTH = PACK_HALF + LANES
HI_MASK = -65536
MOD_ROWS = 8


def _rms(x):
    return x * lax.rsqrt(jnp.mean(x * x, axis=-1, keepdims=True) + EPS)


def _adaln_body(c_ref, w_ref, b_ref, o_ref):
    s = jax.nn.silu(c_ref[...])
    o_ref[...] = jnp.dot(s.astype(BF16), w_ref[...].astype(BF16),
                         preferred_element_type=F32) + b_ref[...]


def _adaln(cvec, w_ada, b_ada):
    tn = 1024
    width = w_ada.shape[1]
    return pl.pallas_call(
        _adaln_body,
        grid=(width // tn,),
        in_specs=[pl.BlockSpec((MOD_ROWS, D_MODEL), lambda j: (0, 0)),
                  pl.BlockSpec((D_MODEL, tn), lambda j: (0, j)),
                  pl.BlockSpec((1, tn), lambda j: (0, j))],
        out_specs=pl.BlockSpec((MOD_ROWS, tn), lambda j: (0, j)),
        out_shape=jax.ShapeDtypeStruct((MOD_ROWS, width), F32),
        compiler_params=pltpu.CompilerParams(vmem_limit_bytes=40 * 2**20),
        name="adaln",
    )(cvec, w_ada, b_ada)


def _rope(x, cos, sin, n_heads):
    lane = lax.broadcasted_iota(I32, (x.shape[0], HEAD_DIM), 1)
    first = (lane & 32) == 0
    outs = []
    for h in range(n_heads):
        xh = x[:, h * HEAD_DIM:(h + 1) * HEAD_DIM]
        partner = jnp.where(first, pltpu.roll(xh, 96, 1), pltpu.roll(xh, 32, 1))
        outs.append(xh * cos + partner * sin)
    return jnp.concatenate(outs, axis=1)


def _inproj_body(*refs, latent, tiles_per_seq):
    if latent:
        x_ref, mod_ref, g_ref, w_ref, cos_ref, sin_ref, q_ref, k_ref, v_ref, u_ref, gv_ref = refs
    else:
        x_ref, mod_ref, g_ref, w_ref, q_ref, k_ref, v_ref, u_ref, gv_ref = refs
    i = pl.program_id(0)
    row = 1 + i // tiles_per_seq if latent else 0
    shift = mod_ref[pl.ds(row, 1), 0:D_MODEL]
    scale = mod_ref[pl.ds(row, 1), D_MODEL:2 * D_MODEL]
    h = _rms(x_ref[...]) * g_ref[...] * (1 + scale) + shift
    z = jnp.dot(h.astype(BF16), w_ref[...], preferred_element_type=F32)
    q = z[:, 0:Q_END]
    k = z[:, Q_END:K_END]
    if latent:
        cos = cos_ref[...]
        sin = sin_ref[...]
        q = _rope(q, cos, sin, N_Q_HEADS)
        k = _rope(k, cos, sin, N_KV_HEADS)
    q_ref[...] = q.astype(BF16)
    k_ref[...] = k
    v_ref[...] = z[:, K_END:V_END]
    u_ref[...] = z[:, V_END:U_END]
    gv_ref[...] = z[:, U_END:IN_WIDTH]


def _inproj(x2d, mod, norm1, w_in_bf16, rope_tables, seq_len):
    n = x2d.shape[0]
    tm = 256
    latent = rope_tables is not None
    tiles_per_seq = seq_len // tm
    in_specs = [pl.BlockSpec((tm, D_MODEL), lambda i: (i, 0)),
                pl.BlockSpec(mod.shape, lambda i: (0, 0)),
                pl.BlockSpec((1, D_MODEL), lambda i: (0, 0)),
                pl.BlockSpec((D_MODEL, IN_WIDTH), lambda i: (0, 0))]
    args = [x2d, mod, norm1, w_in_bf16]
    if latent:
        in_specs += [pl.BlockSpec((tm, HEAD_DIM), lambda i: (i % tiles_per_seq, 0))] * 2
        args += list(rope_tables)
    widths = (ATTN_WIDTH, KV_WIDTH, KV_WIDTH, MLP_WIDTH, MLP_WIDTH)
    dtypes = (BF16, F32, F32, F32, F32)
    return pl.pallas_call(
        functools.partial(_inproj_body, latent=latent, tiles_per_seq=tiles_per_seq),
        grid=(n // tm,),
        in_specs=in_specs,
        out_specs=[pl.BlockSpec((tm, w), lambda i: (i, 0)) for w in widths],
        out_shape=[jax.ShapeDtypeStruct((n, w), dt) for w, dt in zip(widths, dtypes)],
        compiler_params=pltpu.CompilerParams(vmem_limit_bytes=56 * 2**20),
        name="inproj_lat" if latent else "inproj_ctx",
    )(*args)


def _mix_body(*refs, latent, seq_len):
    if latent:
        (sink_ref, q_ref, k_ref, v_ref, ck_ref, cv_ref, u_ref, gv_ref, x_ref, mod_ref, ws_ref, bst_ref,
         gvn_ref, gattn_ref, gmlp_ref, wout_ref, norm2_ref, wr_ref, wrt_ref,
         x1_ref, pack_ref, afft_ref) = refs
    else:
        (sink_ref, q_ref, k_ref, v_ref, u_ref, gv_ref, x_ref, mod_ref, ws_ref, bst_ref,
         gvn_ref, gattn_ref, gmlp_ref, wout_ref, norm2_ref, wr_ref, wrt_ref,
         x1_ref, pack_ref, afft_ref) = refs
    b = pl.program_id(0)
    t = pl.program_id(1)
    row = 1 + b if latent else 0
    tq = MIX_TILE

    q = q_ref[...]
    if latent:
        ks = pl.multiple_of(jnp.clip(t * tq - WINDOW, 0, seq_len - LAT_SPAN), WINDOW)
        kcat = jnp.concatenate([ck_ref[0], k_ref[pl.ds(ks, LAT_SPAN), :]], axis=0).astype(BF16)
        vcat = jnp.concatenate([cv_ref[0], v_ref[pl.ds(ks, LAT_SPAN), :]], axis=0).astype(BF16)
        n_ctx = ck_ref.shape[1]
        n_keys = n_ctx + LAT_SPAN
        kidx = lax.broadcasted_iota(I32, (tq, n_keys), 1)
        qpos = t * tq + lax.broadcasted_iota(I32, (tq, n_keys), 0)
        kpos = ks + kidx - n_ctx
        mask = (kidx < n_ctx) | (jnp.abs(qpos - kpos) <= WINDOW)
    else:
        kcat = k_ref[...].astype(BF16)
        vcat = v_ref[...].astype(BF16)
        mask = None

    heads = []
    for h in range(N_Q_HEADS):
        kv = h // Q_PER_KV
        qh = q[:, h * HEAD_DIM:(h + 1) * HEAD_DIM]
        kh = kcat[:, kv * HEAD_DIM:(kv + 1) * HEAD_DIM]
        vh = vcat[:, kv * HEAD_DIM:(kv + 1) * HEAD_DIM]
        s = lax.dot_general(qh, kh, (((1,), (1,)), ((), ())), preferred_element_type=F32) * ATTN_SCALE
        if mask is not None:
            s = jnp.where(mask, s, NEG_INF)
        sink = sink_ref[h]
        m = jnp.maximum(jnp.max(s, axis=1, keepdims=True), sink)
        p = jnp.exp(s - m)
        den = jnp.sum(p, axis=1, keepdims=True) + jnp.exp(sink - m)
        heads.append(jnp.dot(p.astype(BF16), vh, preferred_element_type=F32) / den)
    attn = jnp.concatenate(heads, axis=1)

    sg = (_rms(jax.nn.gelu(gv_ref[...])) * gvn_ref[...]).astype(BF16)
    bst = bst_ref[...]
    chunks = []
    for c in range(tq // CHUNK):
        groups = []
        for g in range(N_MLP_GROUPS):
            blk = sg[c * CHUNK:(c + 1) * CHUNK, g * LANES:(g + 1) * LANES]
            groups.append(jnp.dot(ws_ref[g], blk, preferred_element_type=F32) + bst[:, g:g + 1])
        chunks.append(jnp.concatenate(groups, axis=1))
    mlp = jax.nn.gelu(u_ref[...]) * jnp.concatenate(chunks, axis=0)

    mixed = jnp.concatenate([_rms(attn) * gattn_ref[...], _rms(mlp) * gmlp_ref[...]], axis=1)
    out = jnp.dot(mixed.astype(BF16), wout_ref[...], preferred_element_type=F32)
    gate1 = mod_ref[pl.ds(row, 1), 2 * D_MODEL:3 * D_MODEL]
    x1 = x_ref[...] + gate1 * out
    x1_ref[...] = x1

    shift2 = mod_ref[pl.ds(row, 1), 3 * D_MODEL:4 * D_MODEL]
    scale2 = mod_ref[pl.ds(row, 1), 4 * D_MODEL:5 * D_MODEL]
    h2 = _rms(x1) * norm2_ref[...] * (1 + scale2) + shift2

    hh = h2.astype(BF16)
    hl = (h2 - hh.astype(F32)).astype(BF16)
    wr = wr_ref[...]
    wh = wr.astype(BF16)
    wl = (wr - wh.astype(F32)).astype(BF16)
    logits = (jnp.dot(hh, wh, preferred_element_type=F32) + jnp.dot(hl, wh, preferred_element_type=F32)
              + jnp.dot(hh, wl, preferred_element_type=F32))
    lane = lax.broadcasted_iota(I32, logits.shape, 1)
    logits = jnp.where(lane < N_EXPERTS, logits, NEG_INF)
    pe = jnp.exp(logits - jnp.max(logits, axis=1, keepdims=True))
    aff = pe / jnp.sum(pe, axis=1, keepdims=True)

    wrt = wrt_ref[...]
    wth = wrt.astype(BF16)
    wtl = (wrt - wth.astype(F32)).astype(BF16)
    nt = (((1,), (1,)), ((), ()))
    logits_t = (lax.dot_general(wth, hh, nt, preferred_element_type=F32)
                + lax.dot_general(wth, hl, nt, preferred_element_type=F32)
                + lax.dot_general(wtl, hh, nt, preferred_element_type=F32))
    pt = jnp.exp(logits_t - jnp.max(logits_t, axis=0, keepdims=True))
    afft_ref[...] = pt / jnp.sum(pt, axis=0, keepdims=True)

    lo = lax.bitcast_convert_type(hh[:, 0:PACK_HALF].astype(F32), I32)
    hi = lax.bitcast_convert_type(hh[:, PACK_HALF:D_MODEL].astype(F32), I32)
    pack_ref[:, 0:PACK_HALF] = lax.shift_right_logical(lo, 16) | (hi & HI_MASK)
    pack_ref[:, PACK_HALF:PACK_WIDTH] = lax.bitcast_convert_type(aff, I32)


def _mix(sink, q, k, v, cache, u, gv, x2d, mod, w_s_bf16, b_s_t, g_v, g_attn, g_mlp, w_out_bf16, norm2,
         w_router_pad, w_router_t, batch, seq_len):
    n = x2d.shape[0]
    latent = cache is not None
    tiles = seq_len // MIX_TILE
    tok = lambda b, t: (b * tiles + t, 0)
    const2 = lambda b, t: (0, 0)

    in_specs = [pl.BlockSpec(memory_space=pltpu.SMEM),
                pl.BlockSpec((MIX_TILE, ATTN_WIDTH), tok),
                pl.BlockSpec((seq_len, KV_WIDTH), lambda b, t: (b, 0)),
                pl.BlockSpec((seq_len, KV_WIDTH), lambda b, t: (b, 0))]
    args = [sink, q, k, v]
    if latent:
        ck, cv = cache
        in_specs += [pl.BlockSpec((1,) + ck.shape[1:], lambda b, t: (b, 0, 0))] * 2
        args += [ck, cv]
    in_specs += [pl.BlockSpec((MIX_TILE, MLP_WIDTH), tok),
                 pl.BlockSpec((MIX_TILE, MLP_WIDTH), tok),
                 pl.BlockSpec((MIX_TILE, D_MODEL), tok),
                 pl.BlockSpec(mod.shape, const2),
                 pl.BlockSpec(w_s_bf16.shape, lambda b, t: (0, 0, 0)),
                 pl.BlockSpec(b_s_t.shape, const2),
                 pl.BlockSpec((1, MLP_WIDTH), const2),
                 pl.BlockSpec((1, ATTN_WIDTH), const2),
                 pl.BlockSpec((1, MLP_WIDTH), const2),
                 pl.BlockSpec(w_out_bf16.shape, const2),
                 pl.BlockSpec((1, D_MODEL), const2),
                 pl.BlockSpec(w_router_pad.shape, const2),
                 pl.BlockSpec(w_router_t.shape, const2)]
    args += [u, gv, x2d, mod, w_s_bf16, b_s_t, g_v, g_attn, g_mlp, w_out_bf16, norm2, w_router_pad, w_router_t]
    return pl.pallas_call(
        functools.partial(_mix_body, latent=latent, seq_len=seq_len),
        grid=(batch, tiles),
        in_specs=in_specs,
        out_specs=[pl.BlockSpec((MIX_TILE, D_MODEL), tok),
                   pl.BlockSpec((MIX_TILE, PACK_WIDTH), tok),
                   pl.BlockSpec((N_EXPERTS, MIX_TILE), lambda b, t: (0, b * tiles + t))],
        out_shape=[jax.ShapeDtypeStruct((n, D_MODEL), F32),
                   jax.ShapeDtypeStruct((n, PACK_WIDTH), I32),
                   jax.ShapeDtypeStruct((N_EXPERTS, n), F32)],
        compiler_params=pltpu.CompilerParams(vmem_limit_bytes=56 * 2**20),
        name="mix_lat" if latent else "mix_ctx",
    )(*args)


def _lane_exclusive_count(flags_bf16, upper):
    width = upper.shape[0]
    carry = jnp.zeros((flags_bf16.shape[0], 1), F32)
    outs = []
    for j in range(flags_bf16.shape[1] // width):
        blk = flags_bf16[:, j * width:(j + 1) * width]
        outs.append(jnp.dot(blk, upper, preferred_element_type=F32) + carry)
        carry = carry + jnp.sum(blk.astype(F32), axis=1, keepdims=True)
    return jnp.concatenate(outs, axis=1)


def _route_body(afft_ref, dst_ref, cnt_ref, *, cap):
    bits = lax.bitcast_convert_type(afft_ref[...], I32)

    def bisect(i, thr):
        cand = thr | jnp.left_shift(jnp.int32(1), 30 - i)
        n_ge = jnp.sum((bits >= cand).astype(F32), axis=1, keepdims=True)
        return jnp.where(n_ge >= cap, cand, thr)

    thr = lax.fori_loop(0, 31, bisect, jnp.zeros((N_EXPERTS, 1), I32))

    width = 256
    r = lax.broadcasted_iota(I32, (width, width), 0)
    c = lax.broadcasted_iota(I32, (width, width), 1)
    upper = (r < c).astype(BF16)

    above = bits > thr
    tied = bits == thr
    need = cap - jnp.sum(above.astype(F32), axis=1, keepdims=True)
    tie_rank = _lane_exclusive_count(tied.astype(BF16), upper)
    sel = above | (tied & (tie_rank < need))
    sel_bf = sel.astype(BF16)
    pos = _lane_exclusive_count(sel_bf, upper).astype(I32)

    er = lax.broadcasted_iota(I32, (N_EXPERTS, N_EXPERTS), 0)
    ec = lax.broadcasted_iota(I32, (N_EXPERTS, N_EXPERTS), 1)
    rank = jnp.dot((ec < er).astype(BF16), sel_bf, preferred_element_type=F32).astype(I32)

    expert = lax.broadcasted_iota(I32, bits.shape, 0)
    flat = expert * cap + pos
    rows = [jnp.sum(jnp.where(sel & (rank == k), flat, 0), axis=0, keepdims=True) for k in range(N_EXPERTS)]
    dst_ref[...] = jnp.concatenate(rows, axis=0)
    cnt_ref[...] = jnp.sum(sel.astype(I32), axis=0, keepdims=True)


def _route(aff_t, cap):
    n = aff_t.shape[1]
    return pl.pallas_call(
        functools.partial(_route_body, cap=cap),
        out_shape=[jax.ShapeDtypeStruct((N_EXPERTS, n), I32), jax.ShapeDtypeStruct((1, n), I32)],
        compiler_params=pltpu.CompilerParams(vmem_limit_bytes=48 * 2**20),
        name="route",
    )(aff_t)


def _row_copy(src, src_row, dst, dst_row, sem):
    return pltpu.make_async_copy(src.at[pl.ds(src_row, 1)], dst.at[pl.ds(dst_row, 1)], sem)


def _dispatch_body(cnt_ref, dst_ref, src_ref, out_ref, sem, *, tb):
    base = pl.program_id(0) * tb

    def token(n, total):
        count = cnt_ref[0, n]

        def issue(k, carry):
            _row_copy(src_ref, base + n, out_ref, dst_ref[k, n], sem).start()
            return carry

        lax.fori_loop(0, count, issue, 0)
        return total + count

    total = lax.fori_loop(0, tb, token, 0)

    def drain(j, carry):
        _row_copy(src_ref, 0, out_ref, 0, sem).wait()
        return carry

    lax.fori_loop(0, total, drain, 0)


def _dispatch(cnt, dst, packed, cap):
    n = packed.shape[0]
    tb = 512
    return pl.pallas_call(
        functools.partial(_dispatch_body, tb=tb),
        grid=(n // tb,),
        in_specs=[pl.BlockSpec((1, tb), lambda i: (0, i), memory_space=pltpu.SMEM),
                  pl.BlockSpec((N_EXPERTS, tb), lambda i: (0, i), memory_space=pltpu.SMEM),
                  pl.BlockSpec(memory_space=pl.ANY)],
        out_specs=pl.BlockSpec(memory_space=pl.ANY),
        out_shape=jax.ShapeDtypeStruct((N_EXPERTS * cap, PACK_WIDTH), I32),
        scratch_shapes=[pltpu.SemaphoreType.DMA(())],
        compiler_params=pltpu.CompilerParams(has_side_effects=True),
        name="dispatch",
    )(cnt, dst, packed)


FF_TILE = 256
N_FF_TILES = EXPERT_FF // FF_TILE
OUT_TILE = 256
N_OUT_TILES = D_MODEL // OUT_TILE


def _expert_body(xc_ref, xl_ref, wg_ref, wu_ref, wd_ref, oc_ref, ol_ref, xb_ref, hid_ref, gate_ref, *, cap_c):
    e = pl.program_id(0)
    s = pl.program_id(1)

    @pl.when(s == 0)
    def _unpack():
        for ref, lo_row, hi_row in ((xc_ref, 0, cap_c), (xl_ref, cap_c, xb_ref.shape[0])):
            words = ref[:, 0:PACK_HALF]
            xb_ref[lo_row:hi_row, 0:PACK_HALF] = lax.bitcast_convert_type(
                jnp.left_shift(words, 16), F32).astype(BF16)
            xb_ref[lo_row:hi_row, PACK_HALF:D_MODEL] = lax.bitcast_convert_type(
                words & HI_MASK, F32).astype(BF16)
            aff = lax.bitcast_convert_type(ref[:, PACK_HALF:PACK_WIDTH], F32)
            lane = lax.broadcasted_iota(I32, aff.shape, 1)
            gate_ref[lo_row:hi_row, :] = jnp.sum(jnp.where(lane == e, aff, 0.0), axis=1, keepdims=True)

    @pl.when(s < N_FF_TILES)
    def _up():
        x = xb_ref[...]
        a = jnp.dot(x, wg_ref[0].astype(BF16), preferred_element_type=F32)
        b = jnp.dot(x, wu_ref[0].astype(BF16), preferred_element_type=F32)
        hid_ref[s] = (jax.nn.silu(a) * b).astype(BF16)

    @pl.when(s >= N_FF_TILES)
    def _down():
        wd = wd_ref[0].astype(BF16)
        acc = jnp.dot(hid_ref[0], wd[0:FF_TILE, :], preferred_element_type=F32)
        for j in range(1, N_FF_TILES):
            acc = acc + jnp.dot(hid_ref[j], wd[j * FF_TILE:(j + 1) * FF_TILE, :], preferred_element_type=F32)
        out = acc * gate_ref[...]
        oc_ref[...] = out[0:cap_c, :]
        ol_ref[...] = out[cap_c:, :]


def _experts(xe_c, xe_l, w_gate, w_up, w_down, cap_c, cap_l):
    cap = cap_c + cap_l
    up_idx = lambda e, s: (e, 0, jnp.minimum(s, N_FF_TILES - 1))
    down_idx = lambda e, s: (e, 0, jnp.maximum(s - N_FF_TILES, 0))
    out_idx = lambda e, s: (e, jnp.maximum(s - N_FF_TILES, 0))
    return pl.pallas_call(
        functools.partial(_expert_body, cap_c=cap_c),
        grid=(N_EXPERTS, N_FF_TILES + N_OUT_TILES),
        in_specs=[pl.BlockSpec((cap_c, PACK_WIDTH), lambda e, s: (e, 0)),
                  pl.BlockSpec((cap_l, PACK_WIDTH), lambda e, s: (e, 0)),
                  pl.BlockSpec((1, D_MODEL, FF_TILE), up_idx),
                  pl.BlockSpec((1, D_MODEL, FF_TILE), up_idx),
                  pl.BlockSpec((1, EXPERT_FF, OUT_TILE), down_idx)],
        out_specs=[pl.BlockSpec((cap_c, OUT_TILE), out_idx),
                   pl.BlockSpec((cap_l, OUT_TILE), out_idx)],
        out_shape=[jax.ShapeDtypeStruct((N_EXPERTS * cap_c, D_MODEL), F32),
                   jax.ShapeDtypeStruct((N_EXPERTS * cap_l, D_MODEL), F32)],
        scratch_shapes=[pltpu.VMEM((cap, D_MODEL), BF16),
                        pltpu.VMEM((N_FF_TILES, cap, FF_TILE), BF16),
                        pltpu.VMEM((cap, 1), F32)],
        compiler_params=pltpu.CompilerParams(vmem_limit_bytes=60 * 2**20),
        name="experts",
    )(xe_c, xe_l, w_gate, w_up, w_down)


COMBINE_TILE = 128


def _combine_body(cnt_ref, dst_ref, cntv_ref, eo_ref, x1_ref, mod_ref, fn_ref, y_ref, buf, sem, *,
                  latent, tiles_per_seq):
    i = pl.program_id(0)
    tb = COMBINE_TILE

    @pl.when(i == 0)
    def _init():
        buf[...] = jnp.zeros(buf.shape, F32)

    def token(n, carry):
        total, kmax = carry
        count = cnt_ref[0, n]

        def issue(k, c):
            pltpu.make_async_copy(eo_ref.at[pl.ds(dst_ref[k, n], 1)], buf.at[k, pl.ds(n, 1)], sem).start()
            return c

        lax.fori_loop(0, count, issue, 0)
        return total + count, jnp.maximum(kmax, count)

    total, kmax = lax.fori_loop(0, tb, token, (jnp.int32(0), jnp.int32(0)))

    def drain(j, c):
        pltpu.make_async_copy(eo_ref.at[pl.ds(0, 1)], buf.at[0, pl.ds(0, 1)], sem).wait()
        return c

    lax.fori_loop(0, total, drain, 0)

    counts = cntv_ref[...]

    def accumulate(k, acc):
        return acc + jnp.where(counts > k, buf[k], 0.0)

    moe = lax.fori_loop(0, kmax, accumulate, jnp.zeros((tb, D_MODEL), F32))
    row = 1 + i // tiles_per_seq if latent else 0
    gate2 = mod_ref[pl.ds(row, 1), 5 * D_MODEL:6 * D_MODEL]
    y_ref[...] = _rms(x1_ref[...] + gate2 * moe) * fn_ref[...]


def _combine(cnt, dst, cnt_col, expert_out, x1, mod, final_norm, latent, seq_len):
    n = x1.shape[0]
    tb = COMBINE_TILE
    return pl.pallas_call(
        functools.partial(_combine_body, latent=latent, tiles_per_seq=seq_len // tb),
        grid=(n // tb,),
        in_specs=[pl.BlockSpec((1, tb), lambda i: (0, i), memory_space=pltpu.SMEM),
                  pl.BlockSpec((N_EXPERTS, tb), lambda i: (0, i), memory_space=pltpu.SMEM),
                  pl.BlockSpec((tb, 1), lambda i: (i, 0)),
                  pl.BlockSpec(memory_space=pl.ANY),
                  pl.BlockSpec((tb, D_MODEL), lambda i: (i, 0)),
                  pl.BlockSpec(mod.shape, lambda i: (0, 0)),
                  pl.BlockSpec((1, D_MODEL), lambda i: (0, 0))],
        out_specs=pl.BlockSpec((tb, D_MODEL), lambda i: (i, 0)),
        out_shape=jax.ShapeDtypeStruct((n, D_MODEL), F32),
        scratch_shapes=[pltpu.VMEM((N_EXPERTS, tb, D_MODEL), F32), pltpu.SemaphoreType.DMA(())],
        compiler_params=pltpu.CompilerParams(vmem_limit_bytes=40 * 2**20),
        name="combine_lat" if latent else "combine_ctx",
    )(cnt, dst, cnt_col, expert_out, x1, mod, final_norm)


def _rope_tables(seq_len):
    pos = jnp.arange(seq_len)
    half = HEAD_DIM // 2
    freqs = ROPE_BASE ** (-jnp.arange(0, half, 2, dtype=F32) / half)
    ang_r = (pos // GRID_W).astype(F32)[:, None] * freqs[None, :]
    ang_c = (pos % GRID_W).astype(F32)[:, None] * freqs[None, :]
    cos = jnp.concatenate([jnp.cos(ang_r)] * 2 + [jnp.cos(ang_c)] * 2, axis=1)
    sin = jnp.concatenate([-jnp.sin(ang_r), jnp.sin(ang_r), -jnp.sin(ang_c), jnp.sin(ang_c)], axis=1)
    return cos, sin


def kernel(x_prompt, x_sample, cache_k, cache_v, c, c_ctx, w_ada, b_ada, norm1, w_in, attn_sink, w_s, b_s, g_v,
           g_attn, g_mlp, w_out, norm2, w_router, w_gate, w_up, w_down, final_norm):
    depth = w_ada.shape[0]
    assert depth == 1, "single-layer trunk"
    batch, seq, _ = x_prompt.shape
    dec_batch, dec_seq, _ = x_sample.shape
    n_ctx = batch * seq
    n_lat = dec_batch * dec_seq
    cap_c = CAPACITY_FACTOR * n_ctx // N_EXPERTS
    cap_l = CAPACITY_FACTOR * n_lat // N_EXPERTS
    l = 0

    cvec = jnp.concatenate([c_ctx[None, :], c, jnp.zeros((MOD_ROWS - 1 - dec_batch, D_MODEL), F32)], axis=0)
    mod = _adaln(cvec, w_ada[l], b_ada[l][None, :])

    w_in_b = w_in[l].astype(BF16)
    w_out_b = w_out[l].astype(BF16)
    w_s_b = w_s[l].astype(BF16)
    b_s_t = jnp.transpose(b_s[l])
    w_router_pad = jnp.pad(w_router[l], ((0, 0), (0, LANES - N_EXPERTS)))
    w_router_t = jnp.transpose(w_router[l])
    row = lambda a: a.reshape(1, -1)
    shared = (mod, w_s_b, b_s_t, row(g_v[l]), row(g_attn[l]), row(g_mlp[l]), w_out_b, row(norm2[l]),
              w_router_pad, w_router_t)

    xp = x_prompt.reshape(n_ctx, D_MODEL)
    xl = x_sample.reshape(n_lat, D_MODEL)

    q, k, v, u, gv = _inproj(xp, mod, row(norm1[l]), w_in_b, None, seq)
    x1_c, pack_c, afft_c = _mix(attn_sink[l], q, k, v, None, u, gv, xp, *shared, batch, seq)
    state_k = k.reshape(batch, 1, seq, N_KV_HEADS, HEAD_DIM)
    state_v = v.reshape(batch, 1, seq, N_KV_HEADS, HEAD_DIM)

    q, k, v, u, gv = _inproj(xl, mod, row(norm1[l]), w_in_b, _rope_tables(dec_seq), dec_seq)
    cache = (cache_k[:, l].reshape(dec_batch, -1, KV_WIDTH), cache_v[:, l].reshape(dec_batch, -1, KV_WIDTH))
    x1_l, pack_l, afft_l = _mix(attn_sink[l], q, k, v, cache, u, gv, xl, *shared, dec_batch, dec_seq)

    dst_c, cnt_c = _route(afft_c, cap_c)
    dst_l, cnt_l = _route(afft_l, cap_l)
    xe_c = _dispatch(cnt_c, dst_c, pack_c, cap_c)
    xe_l = _dispatch(cnt_l, dst_l, pack_l, cap_l)
    eo_c, eo_l = _experts(xe_c, xe_l, w_gate[l], w_up[l], w_down[l], cap_c, cap_l)
    fn = row(final_norm)
    y_c = _combine(cnt_c, dst_c, cnt_c.reshape(n_ctx, 1), eo_c, x1_c, mod, fn, False, seq)
    y_l = _combine(cnt_l, dst_l, cnt_l.reshape(n_lat, 1), eo_l, x1_l, mod, fn, True, dec_seq)

    return (y_c.reshape(batch, seq, D_MODEL), y_l.reshape(dec_batch, dec_seq, D_MODEL), state_k, state_v)
```

```python
import functools

import jax
import jax.numpy as jnp
from jax import lax
from jax.experimental import pallas as pl
from jax.experimental.pallas import tpu as pltpu

F32 = jnp.float32
BF16 = jnp.bfloat16
I32 = jnp.int32

D_MODEL = 2048
GRID_W = 64
HEAD_DIM = 128
N_Q_HEADS = 8
N_KV_HEADS = 2
Q_PER_KV = N_Q_HEADS // N_KV_HEADS
ATTN_WIDTH = N_Q_HEADS * HEAD_DIM
KV_WIDTH = N_KV_HEADS * HEAD_DIM
N_MLP_GROUPS = 8
MLP_WIDTH = 1024
Q_END = ATTN_WIDTH
K_END = Q_END + KV_WIDTH
V_END = K_END + KV_WIDTH
U_END = V_END + MLP_WIDTH
IN_WIDTH = U_END + MLP_WIDTH
CHUNK = 128
WINDOW = 128
N_EXPERTS = 16
EXPERT_FF = 2048
CAPACITY_FACTOR = 2
ROPE_BASE = 10000.0
EPS = 1e-6
NEG_INF = -1e30
ATTN_SCALE = HEAD_DIM ** -0.5

LANES = 128
MIX_TILE = 256
LAT_SPAN = MIX_TILE + 2 * WINDOW
ROW_TILES = D_MODEL // LANES
MOD_ROWS = 8


def _rms(x):
    return x * lax.rsqrt(jnp.mean(x * x, axis=-1, keepdims=True) + EPS)


def _adaln_body(c_ref, w_ref, b_ref, o_ref):
    s = jax.nn.silu(c_ref[...])
    o_ref[...] = jnp.dot(s.astype(BF16), w_ref[...].astype(BF16),
                         preferred_element_type=F32) + b_ref[...]


def _adaln(cvec, w_ada, b_ada):
    tn = 1024
    width = w_ada.shape[1]
    return pl.pallas_call(
        _adaln_body,
        grid=(width // tn,),
        in_specs=[pl.BlockSpec((MOD_ROWS, D_MODEL), lambda j: (0, 0)),
                  pl.BlockSpec((D_MODEL, tn), lambda j: (0, j)),
                  pl.BlockSpec((1, tn), lambda j: (0, j))],
        out_specs=pl.BlockSpec((MOD_ROWS, tn), lambda j: (0, j)),
        out_shape=jax.ShapeDtypeStruct((MOD_ROWS, width), F32),
        compiler_params=pltpu.CompilerParams(vmem_limit_bytes=40 * 2**20),
        name="adaln",
    )(cvec, w_ada, b_ada)


def _rope(x, cos, sin, n_heads):
    lane = lax.broadcasted_iota(I32, (x.shape[0], HEAD_DIM), 1)
    first = (lane & 32) == 0
    outs = []
    for h in range(n_heads):
        xh = x[:, h * HEAD_DIM:(h + 1) * HEAD_DIM]
        partner = jnp.where(first, pltpu.roll(xh, 96, 1), pltpu.roll(xh, 32, 1))
        outs.append(xh * cos + partner * sin)
    return jnp.concatenate(outs, axis=1)


def _inproj_body(*refs, latent, tiles_per_seq):
    if latent:
        x_ref, mod_ref, g_ref, w_ref, cos_ref, sin_ref, q_ref, k_ref, v_ref, u_ref, gv_ref = refs
    else:
        x_ref, mod_ref, g_ref, w_ref, q_ref, k_ref, v_ref, u_ref, gv_ref = refs
    i = pl.program_id(0)
    row = 1 + i // tiles_per_seq if latent else 0
    shift = mod_ref[pl.ds(row, 1), 0:D_MODEL]
    scale = mod_ref[pl.ds(row, 1), D_MODEL:2 * D_MODEL]
    h = _rms(x_ref[...]) * g_ref[...] * (1 + scale) + shift
    z = jnp.dot(h.astype(BF16), w_ref[...], preferred_element_type=F32)
    q = z[:, 0:Q_END]
    k = z[:, Q_END:K_END]
    if latent:
        cos = cos_ref[...]
        sin = sin_ref[...]
        q = _rope(q, cos, sin, N_Q_HEADS)
        k = _rope(k, cos, sin, N_KV_HEADS)
    q_ref[...] = q.astype(BF16)
    k_ref[...] = k
    v_ref[...] = z[:, K_END:V_END]
    u_ref[...] = z[:, V_END:U_END]
    gv_ref[...] = z[:, U_END:IN_WIDTH]


def _inproj(x2d, mod, norm1, w_in_bf16, rope_tables, seq_len):
    n = x2d.shape[0]
    tm = 256
    latent = rope_tables is not None
    tiles_per_seq = seq_len // tm
    in_specs = [pl.BlockSpec((tm, D_MODEL), lambda i: (i, 0)),
                pl.BlockSpec(mod.shape, lambda i: (0, 0)),
                pl.BlockSpec((1, D_MODEL), lambda i: (0, 0)),
                pl.BlockSpec((D_MODEL, IN_WIDTH), lambda i: (0, 0))]
    args = [x2d, mod, norm1, w_in_bf16]
    if latent:
        in_specs += [pl.BlockSpec((tm, HEAD_DIM), lambda i: (i % tiles_per_seq, 0))] * 2
        args += list(rope_tables)
    widths = (ATTN_WIDTH, KV_WIDTH, KV_WIDTH, MLP_WIDTH, MLP_WIDTH)
    dtypes = (BF16, F32, F32, F32, F32)
    return pl.pallas_call(
        functools.partial(_inproj_body, latent=latent, tiles_per_seq=tiles_per_seq),
        grid=(n // tm,),
        in_specs=in_specs,
        out_specs=[pl.BlockSpec((tm, w), lambda i: (i, 0)) for w in widths],
        out_shape=[jax.ShapeDtypeStruct((n, w), dt) for w, dt in zip(widths, dtypes)],
        compiler_params=pltpu.CompilerParams(vmem_limit_bytes=56 * 2**20),
        name="inproj_lat" if latent else "inproj_ctx",
    )(*args)


def _mix_body(*refs, latent, seq_len):
    if latent:
        (sink_ref, q_ref, k_ref, v_ref, ck_ref, cv_ref, u_ref, gv_ref, x_ref, mod_ref, ws_ref, bst_ref,
         gvn_ref, gattn_ref, gmlp_ref, wout_ref, norm2_ref, wrt_ref,
         x1_ref, h2_ref, afft_ref) = refs
    else:
        (sink_ref, q_ref, k_ref, v_ref, u_ref, gv_ref, x_ref, mod_ref, ws_ref, bst_ref,
         gvn_ref, gattn_ref, gmlp_ref, wout_ref, norm2_ref, wrt_ref,
         x1_ref, h2_ref, afft_ref) = refs
    b = pl.program_id(0)
    t = pl.program_id(1)
    row = 1 + b if latent else 0
    tq = MIX_TILE

    q = q_ref[...]
    if latent:
        ks = pl.multiple_of(jnp.clip(t * tq - WINDOW, 0, seq_len - LAT_SPAN), WINDOW)
        kcat = jnp.concatenate([ck_ref[0], k_ref[pl.ds(ks, LAT_SPAN), :]], axis=0).astype(BF16)
        vcat = jnp.concatenate([cv_ref[0], v_ref[pl.ds(ks, LAT_SPAN), :]], axis=0).astype(BF16)
        n_ctx = ck_ref.shape[1]
        n_keys = n_ctx + LAT_SPAN
        kidx = lax.broadcasted_iota(I32, (tq, n_keys), 1)
        qpos = t * tq + lax.broadcasted_iota(I32, (tq, n_keys), 0)
        kpos = ks + kidx - n_ctx
        mask = (kidx < n_ctx) | (jnp.abs(qpos - kpos) <= WINDOW)
    else:
        kcat = k_ref[...].astype(BF16)
        vcat = v_ref[...].astype(BF16)
        mask = None

    heads = []
    for h in range(N_Q_HEADS):
        kv = h // Q_PER_KV
        qh = q[:, h * HEAD_DIM:(h + 1) * HEAD_DIM]
        kh = kcat[:, kv * HEAD_DIM:(kv + 1) * HEAD_DIM]
        vh = vcat[:, kv * HEAD_DIM:(kv + 1) * HEAD_DIM]
        s = lax.dot_general(qh, kh, (((1,), (1,)), ((), ())), preferred_element_type=F32) * ATTN_SCALE
        if mask is not None:
            s = jnp.where(mask, s, NEG_INF)
        sink = sink_ref[h]
        m = jnp.maximum(jnp.max(s, axis=1, keepdims=True), sink)
        p = jnp.exp(s - m)
        den = jnp.sum(p, axis=1, keepdims=True) + jnp.exp(sink - m)
        heads.append(jnp.dot(p.astype(BF16), vh, preferred_element_type=F32) / den)
    attn = jnp.concatenate(heads, axis=1)

    sg = (_rms(jax.nn.gelu(gv_ref[...])) * gvn_ref[...]).astype(BF16)
    bst = bst_ref[...]
    chunks = []
    for c in range(tq // CHUNK):
        groups = []
        for g in range(N_MLP_GROUPS):
            blk = sg[c * CHUNK:(c + 1) * CHUNK, g * LANES:(g + 1) * LANES]
            groups.append(jnp.dot(ws_ref[g], blk, preferred_element_type=F32) + bst[:, g:g + 1])
        chunks.append(jnp.concatenate(groups, axis=1))
    mlp = jax.nn.gelu(u_ref[...]) * jnp.concatenate(chunks, axis=0)

    mixed = jnp.concatenate([_rms(attn) * gattn_ref[...], _rms(mlp) * gmlp_ref[...]], axis=1)
    out = jnp.dot(mixed.astype(BF16), wout_ref[...], preferred_element_type=F32)
    gate1 = mod_ref[pl.ds(row, 1), 2 * D_MODEL:3 * D_MODEL]
    x1 = x_ref[...] + gate1 * out
    x1_ref[...] = x1

    shift2 = mod_ref[pl.ds(row, 1), 3 * D_MODEL:4 * D_MODEL]
    scale2 = mod_ref[pl.ds(row, 1), 4 * D_MODEL:5 * D_MODEL]
    h2 = _rms(x1) * norm2_ref[...] * (1 + scale2) + shift2

    hh = h2.astype(BF16)
    hl = (h2 - hh.astype(F32)).astype(BF16)
    wrt = wrt_ref[...]
    wth = wrt.astype(BF16)
    wtl = (wrt - wth.astype(F32)).astype(BF16)
    nt = (((1,), (1,)), ((), ()))
    logits_t = (lax.dot_general(wth, hh, nt, preferred_element_type=F32)
                + lax.dot_general(wth, hl, nt, preferred_element_type=F32)
                + lax.dot_general(wtl, hh, nt, preferred_element_type=F32))
    pt = jnp.exp(logits_t - jnp.max(logits_t, axis=0, keepdims=True))
    afft_ref[...] = pt / jnp.sum(pt, axis=0, keepdims=True)

    h2_ref[...] = hh.astype(F32).reshape(tq, ROW_TILES, LANES).astype(BF16)


def _mix(sink, q, k, v, cache, u, gv, x2d, mod, w_s_bf16, b_s_t, g_v, g_attn, g_mlp, w_out_bf16, norm2,
         w_router_t, batch, seq_len):
    n = x2d.shape[0]
    latent = cache is not None
    tiles = seq_len // MIX_TILE
    tok = lambda b, t: (b * tiles + t, 0)
    const2 = lambda b, t: (0, 0)

    in_specs = [pl.BlockSpec(memory_space=pltpu.SMEM),
                pl.BlockSpec((MIX_TILE, ATTN_WIDTH), tok),
                pl.BlockSpec((seq_len, KV_WIDTH), lambda b, t: (b, 0)),
                pl.BlockSpec((seq_len, KV_WIDTH), lambda b, t: (b, 0))]
    args = [sink, q, k, v]
    if latent:
        ck, cv = cache
        in_specs += [pl.BlockSpec((1,) + ck.shape[1:], lambda b, t: (b, 0, 0))] * 2
        args += [ck, cv]
    in_specs += [pl.BlockSpec((MIX_TILE, MLP_WIDTH), tok),
                 pl.BlockSpec((MIX_TILE, MLP_WIDTH), tok),
                 pl.BlockSpec((MIX_TILE, D_MODEL), tok),
                 pl.BlockSpec(mod.shape, const2),
                 pl.BlockSpec(w_s_bf16.shape, lambda b, t: (0, 0, 0)),
                 pl.BlockSpec(b_s_t.shape, const2),
                 pl.BlockSpec((1, MLP_WIDTH), const2),
                 pl.BlockSpec((1, ATTN_WIDTH), const2),
                 pl.BlockSpec((1, MLP_WIDTH), const2),
                 pl.BlockSpec(w_out_bf16.shape, const2),
                 pl.BlockSpec((1, D_MODEL), const2),
                 pl.BlockSpec(w_router_t.shape, const2)]
    args += [u, gv, x2d, mod, w_s_bf16, b_s_t, g_v, g_attn, g_mlp, w_out_bf16, norm2, w_router_t]
    return pl.pallas_call(
        functools.partial(_mix_body, latent=latent, seq_len=seq_len),
        grid=(batch, tiles),
        in_specs=in_specs,
        out_specs=[pl.BlockSpec((MIX_TILE, D_MODEL), tok),
                   pl.BlockSpec((MIX_TILE, ROW_TILES, LANES), lambda b, t: (b * tiles + t, 0, 0)),
                   pl.BlockSpec((N_EXPERTS, MIX_TILE), lambda b, t: (0, b * tiles + t))],
        out_shape=[jax.ShapeDtypeStruct((n, D_MODEL), F32),
                   jax.ShapeDtypeStruct((n, ROW_TILES, LANES), BF16),
                   jax.ShapeDtypeStruct((N_EXPERTS, n), F32)],
        compiler_params=pltpu.CompilerParams(vmem_limit_bytes=56 * 2**20),
        name="mix_lat" if latent else "mix_ctx",
    )(*args)


def _lane_exclusive_count(flags_bf16, upper):
    width = upper.shape[0]
    carry = jnp.zeros((flags_bf16.shape[0], 1), F32)
    outs = []
    for j in range(flags_bf16.shape[1] // width):
        blk = flags_bf16[:, j * width:(j + 1) * width]
        outs.append(jnp.dot(blk, upper, preferred_element_type=F32) + carry)
        carry = carry + jnp.sum(blk.astype(F32), axis=1, keepdims=True)
    return jnp.concatenate(outs, axis=1)


def _route_body(afft_ref, src_ref, gate_ref, cnt_ref, idxt_ref, *, cap):
    aff = afft_ref[...]
    bits = lax.bitcast_convert_type(aff, I32)

    def bisect(i, thr):
        cand = thr | jnp.left_shift(jnp.int32(1), 30 - i)
        n_ge = jnp.sum((bits >= cand).astype(F32), axis=1, keepdims=True)
        return jnp.where(n_ge >= cap, cand, thr)

    thr = lax.fori_loop(0, 31, bisect, jnp.zeros((N_EXPERTS, 1), I32))

    width = 256
    r = lax.broadcasted_iota(I32, (width, width), 0)
    c = lax.broadcasted_iota(I32, (width, width), 1)
    upper = (r < c).astype(BF16)

    above = bits > thr
    tied = bits == thr
    need = cap - jnp.sum(above.astype(F32), axis=1, keepdims=True)
    tie_rank = _lane_exclusive_count(tied.astype(BF16), upper)
    sel = above | (tied & (tie_rank < need))
    sel_bf = sel.astype(BF16)
    pos = _lane_exclusive_count(sel_bf, upper).astype(I32)

    er = lax.broadcasted_iota(I32, (N_EXPERTS, N_EXPERTS), 0)
    ec = lax.broadcasted_iota(I32, (N_EXPERTS, N_EXPERTS), 1)
    rank = jnp.dot((ec < er).astype(BF16), sel_bf, preferred_element_type=F32).astype(I32)

    expert = lax.broadcasted_iota(I32, bits.shape, 0)
    flat = expert * cap + pos
    picks = [sel & (rank == k) for k in range(N_EXPERTS)]
    src_ref[...] = jnp.concatenate(
        [jnp.sum(jnp.where(p, flat, 0), axis=0, keepdims=True) for p in picks], axis=0)
    gate_ref[...] = jnp.concatenate(
        [jnp.sum(jnp.where(p, aff, 0.0), axis=0, keepdims=True) for p in picks], axis=0)
    cnt_ref[...] = jnp.sum(sel.astype(I32), axis=0, keepdims=True)

    incl = pos + sel.astype(I32)
    for e in range(N_EXPERTS):
        row = incl[e:e + 1, :]

        def slots(jb, carry):
            j = jb * 8 + lax.broadcasted_iota(I32, (8, 1), 0)
            n_le = jnp.sum((row <= j).astype(F32), axis=1, keepdims=True)
            idxt_ref[pl.ds(pl.multiple_of(jb * 8, 8), 8), e:e + 1] = n_le.astype(I32)
            return carry

        lax.fori_loop(0, cap // 8, slots, 0)


def _route(aff_t, cap):
    n = aff_t.shape[1]
    return pl.pallas_call(
        functools.partial(_route_body, cap=cap),
        out_shape=[jax.ShapeDtypeStruct((N_EXPERTS, n), I32), jax.ShapeDtypeStruct((N_EXPERTS, n), F32),
                   jax.ShapeDtypeStruct((1, n), I32), jax.ShapeDtypeStruct((cap, N_EXPERTS), I32)],
        compiler_params=pltpu.CompilerParams(vmem_limit_bytes=48 * 2**20),
        name="route",
    )(aff_t)


FF_TILE = 256
N_FF_TILES = EXPERT_FF // FF_TILE
OUT_TILE = 256
N_OUT_TILES = D_MODEL // OUT_TILE
EXPERT_STEPS = N_FF_TILES + N_OUT_TILES


def _expert_body(idxc_ref, idxl_ref, hc_ref, hl_ref, wg_ref, wu_ref, wd_ref, oc_ref, ol_ref,
                 gbuf, xb_ref, hid_ref, sem, *, cap_c, cap_l):
    e = pl.program_id(0)
    s = pl.program_id(1)
    cap = cap_c + cap_l

    def issue(expert, step):
        slot = expert % 2
        for idx_ref, src, per_step, base in ((idxc_ref, hc_ref, cap_c // EXPERT_STEPS, 0),
                                             (idxl_ref, hl_ref, cap_l // EXPERT_STEPS, cap_c)):
            for j in range(per_step):
                r = step * per_step + j
                pltpu.make_async_copy(src.at[idx_ref[expert, r]], gbuf.at[slot, base + r], sem.at[slot]).start()

    @pl.when((e == 0) & (s == 0))
    def _first():
        def body(step, carry):
            issue(0, step)
            return carry
        lax.fori_loop(0, EXPERT_STEPS, body, 0)

    @pl.when(e + 1 < N_EXPERTS)
    def _prefetch():
        issue(e + 1, s)

    @pl.when(s == 0)
    def _unpack():
        slot = e % 2
        pltpu.make_async_copy(hc_ref.at[pl.ds(0, cap)], gbuf.at[slot], sem.at[slot]).wait()
        xb_ref[...] = gbuf[slot].astype(F32).reshape(cap, D_MODEL).astype(BF16)

    @pl.when(s < N_FF_TILES)
    def _up():
        x = xb_ref[...]
        a = jnp.dot(x, wg_ref[0].astype(BF16), preferred_element_type=F32)
        b = jnp.dot(x, wu_ref[0].astype(BF16), preferred_element_type=F32)
        hid_ref[s] = (jax.nn.silu(a) * b).astype(BF16)

    @pl.when(s >= N_FF_TILES)
    def _down():
        wd = wd_ref[0].astype(BF16)
        acc = jnp.dot(hid_ref[0], wd[0:FF_TILE, :], preferred_element_type=F32)
        for j in range(1, N_FF_TILES):
            acc = acc + jnp.dot(hid_ref[j], wd[j * FF_TILE:(j + 1) * FF_TILE, :], preferred_element_type=F32)
        oc_ref[...] = acc[0:cap_c, :]
        ol_ref[...] = acc[cap_c:, :]


def _experts(idx_c, idx_l, h2_c, h2_l, w_gate, w_up, w_down):
    cap_c = idx_c.shape[1]
    cap_l = idx_l.shape[1]
    cap = cap_c + cap_l
    up_idx = lambda e, s, ic, il: (e, 0, jnp.minimum(s, N_FF_TILES - 1))
    down_idx = lambda e, s, ic, il: (e, 0, jnp.maximum(s - N_FF_TILES, 0))
    out_idx = lambda e, s, ic, il: (e, jnp.maximum(s - N_FF_TILES, 0))
    return pl.pallas_call(
        functools.partial(_expert_body, cap_c=cap_c, cap_l=cap_l),
        grid_spec=pltpu.PrefetchScalarGridSpec(
            num_scalar_prefetch=2,
            grid=(N_EXPERTS, EXPERT_STEPS),
            in_specs=[pl.BlockSpec(memory_space=pl.ANY),
                      pl.BlockSpec(memory_space=pl.ANY),
                      pl.BlockSpec((1, D_MODEL, FF_TILE), up_idx),
                      pl.BlockSpec((1, D_MODEL, FF_TILE), up_idx),
                      pl.BlockSpec((1, EXPERT_FF, OUT_TILE), down_idx)],
            out_specs=[pl.BlockSpec((cap_c, OUT_TILE), out_idx),
                       pl.BlockSpec((cap_l, OUT_TILE), out_idx)],
            scratch_shapes=[pltpu.VMEM((2, cap, ROW_TILES, LANES), BF16),
                            pltpu.VMEM((cap, D_MODEL), BF16),
                            pltpu.VMEM((N_FF_TILES, cap, FF_TILE), BF16),
                            pltpu.SemaphoreType.DMA((2,))]),
        out_shape=[jax.ShapeDtypeStruct((N_EXPERTS * cap_c, D_MODEL), F32),
                   jax.ShapeDtypeStruct((N_EXPERTS * cap_l, D_MODEL), F32)],
        compiler_params=pltpu.CompilerParams(vmem_limit_bytes=60 * 2**20),
        name="experts",
    )(idx_c, idx_l, h2_c, h2_l, w_gate, w_up, w_down)


COMBINE_TILE = 128


def _combine_body(cnt_ref, src_ref, cntv_ref, gate_ref, eo_ref, x1_ref, mod_ref, fn_ref, y_ref, buf, sem, *,
                  latent, tiles_per_seq):
    i = pl.program_id(0)
    tb = COMBINE_TILE

    @pl.when(i == 0)
    def _init():
        buf[...] = jnp.zeros(buf.shape, F32)

    def row_copy(src_row, k, n):
        return pltpu.make_async_copy(eo_ref.at[src_row], buf.at[k, n], sem)

    def token(n, carry):
        total, kmax = carry
        count = cnt_ref[0, n]

        def issue(k, c):
            row_copy(src_ref[k, n], k, n).start()
            return c

        lax.fori_loop(0, count, issue, 0)
        return total + count, jnp.maximum(kmax, count)

    total, kmax = lax.fori_loop(0, tb, token, (jnp.int32(0), jnp.int32(0)))

    def drain(j, c):
        row_copy(0, 0, 0).wait()
        return c

    lax.fori_loop(0, total, drain, 0)

    counts = cntv_ref[...]
    gates = gate_ref[...]
    lane = lax.broadcasted_iota(I32, gates.shape, 1)

    def accumulate(k, acc):
        g = jnp.sum(jnp.where(lane == k, gates, 0.0), axis=1, keepdims=True)
        return acc + jnp.where(counts > k, g * buf[k].reshape(tb, D_MODEL), 0.0)

    moe = lax.fori_loop(0, kmax, accumulate, jnp.zeros((tb, D_MODEL), F32))
    row = 1 + i // tiles_per_seq if latent else 0
    gate2 = mod_ref[pl.ds(row, 1), 5 * D_MODEL:6 * D_MODEL]
    y_ref[...] = _rms(x1_ref[...] + gate2 * moe) * fn_ref[...]


def _combine(cnt, src, cnt_col, gates, expert_out, x1, mod, final_norm, latent, seq_len):
    n = x1.shape[0]
    tb = COMBINE_TILE
    return pl.pallas_call(
        functools.partial(_combine_body, latent=latent, tiles_per_seq=seq_len // tb),
        grid=(n // tb,),
        in_specs=[pl.BlockSpec((1, tb), lambda i: (0, i), memory_space=pltpu.SMEM),
                  pl.BlockSpec((N_EXPERTS, tb), lambda i: (0, i), memory_space=pltpu.SMEM),
                  pl.BlockSpec((tb, 1), lambda i: (i, 0)),
                  pl.BlockSpec((tb, N_EXPERTS), lambda i: (i, 0)),
                  pl.BlockSpec(memory_space=pl.ANY),
                  pl.BlockSpec((tb, D_MODEL), lambda i: (i, 0)),
                  pl.BlockSpec(mod.shape, lambda i: (0, 0)),
                  pl.BlockSpec((1, D_MODEL), lambda i: (0, 0))],
        out_specs=pl.BlockSpec((tb, D_MODEL), lambda i: (i, 0)),
        out_shape=jax.ShapeDtypeStruct((n, D_MODEL), F32),
        scratch_shapes=[pltpu.VMEM((N_EXPERTS, tb, ROW_TILES, LANES), F32), pltpu.SemaphoreType.DMA(())],
        compiler_params=pltpu.CompilerParams(vmem_limit_bytes=40 * 2**20),
        name="combine_lat" if latent else "combine_ctx",
    )(cnt, src, cnt_col, gates, expert_out, x1, mod, final_norm)


def _rope_tables(seq_len):
    pos = jnp.arange(seq_len)
    half = HEAD_DIM // 2
    freqs = ROPE_BASE ** (-jnp.arange(0, half, 2, dtype=F32) / half)
    ang_r = (pos // GRID_W).astype(F32)[:, None] * freqs[None, :]
    ang_c = (pos % GRID_W).astype(F32)[:, None] * freqs[None, :]
    cos = jnp.concatenate([jnp.cos(ang_r)] * 2 + [jnp.cos(ang_c)] * 2, axis=1)
    sin = jnp.concatenate([-jnp.sin(ang_r), jnp.sin(ang_r), -jnp.sin(ang_c), jnp.sin(ang_c)], axis=1)
    return cos, sin


def kernel(x_prompt, x_sample, cache_k, cache_v, c, c_ctx, w_ada, b_ada, norm1, w_in, attn_sink, w_s, b_s, g_v,
           g_attn, g_mlp, w_out, norm2, w_router, w_gate, w_up, w_down, final_norm):
    depth = w_ada.shape[0]
    assert depth == 1, "single-layer trunk"
    batch, seq, _ = x_prompt.shape
    dec_batch, dec_seq, _ = x_sample.shape
    n_ctx = batch * seq
    n_lat = dec_batch * dec_seq
    cap_c = CAPACITY_FACTOR * n_ctx // N_EXPERTS
    cap_l = CAPACITY_FACTOR * n_lat // N_EXPERTS
    l = 0

    cvec = jnp.concatenate([c_ctx[None, :], c, jnp.zeros((MOD_ROWS - 1 - dec_batch, D_MODEL), F32)], axis=0)
    mod = _adaln(cvec, w_ada[l], b_ada[l][None, :])

    w_in_b = w_in[l].astype(BF16)
    w_out_b = w_out[l].astype(BF16)
    w_s_b = w_s[l].astype(BF16)
    b_s_t = jnp.transpose(b_s[l])
    w_router_t = jnp.transpose(w_router[l])
    row = lambda a: a.reshape(1, -1)
    shared = (mod, w_s_b, b_s_t, row(g_v[l]), row(g_attn[l]), row(g_mlp[l]), w_out_b, row(norm2[l]), w_router_t)

    xp = x_prompt.reshape(n_ctx, D_MODEL)
    xl = x_sample.reshape(n_lat, D_MODEL)

    q, k, v, u, gv = _inproj(xp, mod, row(norm1[l]), w_in_b, None, seq)
    x1_c, h2_c, afft_c = _mix(attn_sink[l], q, k, v, None, u, gv, xp, *shared, batch, seq)
    state_k = k.reshape(batch, 1, seq, N_KV_HEADS, HEAD_DIM)
    state_v = v.reshape(batch, 1, seq, N_KV_HEADS, HEAD_DIM)

    q, k, v, u, gv = _inproj(xl, mod, row(norm1[l]), w_in_b, _rope_tables(dec_seq), dec_seq)
    cache = (cache_k[:, l].reshape(dec_batch, -1, KV_WIDTH), cache_v[:, l].reshape(dec_batch, -1, KV_WIDTH))
    x1_l, h2_l, afft_l = _mix(attn_sink[l], q, k, v, cache, u, gv, xl, *shared, dec_batch, dec_seq)

    src_c, gate_c, cnt_c, idxt_c = _route(afft_c, cap_c)
    src_l, gate_l, cnt_l, idxt_l = _route(afft_l, cap_l)
    eo_c, eo_l = _experts(jnp.transpose(idxt_c), jnp.transpose(idxt_l), h2_c, h2_l, w_gate[l], w_up[l], w_down[l])
    fn = row(final_norm)
    slabs = lambda eo: eo.reshape(eo.shape[0], ROW_TILES, LANES)
    y_c = _combine(cnt_c, src_c, cnt_c.reshape(n_ctx, 1), jnp.transpose(gate_c), slabs(eo_c), x1_c, mod, fn,
                   False, seq)
    y_l = _combine(cnt_l, src_l, cnt_l.reshape(n_lat, 1), jnp.transpose(gate_l), slabs(eo_l), x1_l, mod, fn,
                   True, dec_seq)

    return (y_c.reshape(batch, seq, D_MODEL), y_l.reshape(dec_batch, dec_seq, D_MODEL), state_k, state_v)
```

```python
import functools

import jax
import jax.numpy as jnp
from jax import lax
from jax.experimental import pallas as pl
from jax.experimental.pallas import tpu as pltpu

F32 = jnp.float32
BF16 = jnp.bfloat16
I32 = jnp.int32

D_MODEL = 2048
GRID_W = 64
HEAD_DIM = 128
N_Q_HEADS = 8
N_KV_HEADS = 2
Q_PER_KV = N_Q_HEADS // N_KV_HEADS
ATTN_WIDTH = N_Q_HEADS * HEAD_DIM
KV_WIDTH = N_KV_HEADS * HEAD_DIM
N_MLP_GROUPS = 8
MLP_WIDTH = 1024
Q_END = ATTN_WIDTH
K_END = Q_END + KV_WIDTH
V_END = K_END + KV_WIDTH
U_END = V_END + MLP_WIDTH
IN_WIDTH = U_END + MLP_WIDTH
CHUNK = 128
WINDOW = 128
N_EXPERTS = 16
EXPERT_FF = 2048
CAPACITY_FACTOR = 2
ROPE_BASE = 10000.0
EPS = 1e-6
NEG_INF = -1e30
ATTN_SCALE = HEAD_DIM ** -0.5

LANES = 128
MIX_TILE = 256
LAT_SPAN = MIX_TILE + 2 * WINDOW
ROW_TILES = D_MODEL // LANES
MOD_ROWS = 8


def _rms(x):
    return x * lax.rsqrt(jnp.mean(x * x, axis=-1, keepdims=True) + EPS)


def _adaln_body(c_ref, w_ref, b_ref, o_ref):
    s = jax.nn.silu(c_ref[...])
    o_ref[...] = jnp.dot(s.astype(BF16), w_ref[...].astype(BF16),
                         preferred_element_type=F32) + b_ref[...]


def _adaln(cvec, w_ada, b_ada):
    tn = 1024
    width = w_ada.shape[1]
    return pl.pallas_call(
        _adaln_body,
        grid=(width // tn,),
        in_specs=[pl.BlockSpec((MOD_ROWS, D_MODEL), lambda j: (0, 0)),
                  pl.BlockSpec((D_MODEL, tn), lambda j: (0, j)),
                  pl.BlockSpec((1, tn), lambda j: (0, j))],
        out_specs=pl.BlockSpec((MOD_ROWS, tn), lambda j: (0, j)),
        out_shape=jax.ShapeDtypeStruct((MOD_ROWS, width), F32),
        compiler_params=pltpu.CompilerParams(vmem_limit_bytes=40 * 2**20),
        name="adaln",
    )(cvec, w_ada, b_ada)


def _rope(x, cos, sin, n_heads):
    lane = lax.broadcasted_iota(I32, (x.shape[0], HEAD_DIM), 1)
    first = (lane & 32) == 0
    outs = []
    for h in range(n_heads):
        xh = x[:, h * HEAD_DIM:(h + 1) * HEAD_DIM]
        partner = jnp.where(first, pltpu.roll(xh, 96, 1), pltpu.roll(xh, 32, 1))
        outs.append(xh * cos + partner * sin)
    return jnp.concatenate(outs, axis=1)


def _inproj_body(*refs, latent, tiles_per_seq):
    if latent:
        x_ref, mod_ref, g_ref, w_ref, cos_ref, sin_ref, q_ref, k_ref, v_ref, u_ref, gv_ref = refs
    else:
        x_ref, mod_ref, g_ref, w_ref, q_ref, k_ref, v_ref, u_ref, gv_ref = refs
    i = pl.program_id(0)
    row = 1 + i // tiles_per_seq if latent else 0
    shift = mod_ref[pl.ds(row, 1), 0:D_MODEL]
    scale = mod_ref[pl.ds(row, 1), D_MODEL:2 * D_MODEL]
    h = _rms(x_ref[...]) * g_ref[...] * (1 + scale) + shift
    z = jnp.dot(h.astype(BF16), w_ref[...], preferred_element_type=F32)
    q = z[:, 0:Q_END]
    k = z[:, Q_END:K_END]
    if latent:
        cos = cos_ref[...]
        sin = sin_ref[...]
        q = _rope(q, cos, sin, N_Q_HEADS)
        k = _rope(k, cos, sin, N_KV_HEADS)
    q_ref[...] = q.astype(BF16)
    k_ref[...] = k
    v_ref[...] = z[:, K_END:V_END]
    u_ref[...] = z[:, V_END:U_END]
    gv_ref[...] = z[:, U_END:IN_WIDTH]


def _inproj(x2d, mod, norm1, w_in_bf16, rope_tables, seq_len):
    n = x2d.shape[0]
    tm = 256
    latent = rope_tables is not None
    tiles_per_seq = seq_len // tm
    in_specs = [pl.BlockSpec((tm, D_MODEL), lambda i: (i, 0)),
                pl.BlockSpec(mod.shape, lambda i: (0, 0)),
                pl.BlockSpec((1, D_MODEL), lambda i: (0, 0)),
                pl.BlockSpec((D_MODEL, IN_WIDTH), lambda i: (0, 0))]
    args = [x2d, mod, norm1, w_in_bf16]
    if latent:
        in_specs += [pl.BlockSpec((tm, HEAD_DIM), lambda i: (i % tiles_per_seq, 0))] * 2
        args += list(rope_tables)
    widths = (ATTN_WIDTH, KV_WIDTH, KV_WIDTH, MLP_WIDTH, MLP_WIDTH)
    dtypes = (BF16, F32, F32, F32, F32)
    return pl.pallas_call(
        functools.partial(_inproj_body, latent=latent, tiles_per_seq=tiles_per_seq),
        grid=(n // tm,),
        in_specs=in_specs,
        out_specs=[pl.BlockSpec((tm, w), lambda i: (i, 0)) for w in widths],
        out_shape=[jax.ShapeDtypeStruct((n, w), dt) for w, dt in zip(widths, dtypes)],
        compiler_params=pltpu.CompilerParams(vmem_limit_bytes=56 * 2**20),
        name="inproj_lat" if latent else "inproj_ctx",
    )(*args)


def _mix_body(*refs, latent, seq_len):
    if latent:
        (sink_ref, q_ref, k_ref, v_ref, ck_ref, cv_ref, u_ref, gv_ref, x_ref, mod_ref, ws_ref, bst_ref,
         gvn_ref, gattn_ref, gmlp_ref, wout_ref, norm2_ref, wrt_ref,
         x1_ref, h2_ref, afft_ref) = refs
    else:
        (sink_ref, q_ref, k_ref, v_ref, u_ref, gv_ref, x_ref, mod_ref, ws_ref, bst_ref,
         gvn_ref, gattn_ref, gmlp_ref, wout_ref, norm2_ref, wrt_ref,
         x1_ref, h2_ref, afft_ref) = refs
    b = pl.program_id(0)
    t = pl.program_id(1)
    row = 1 + b if latent else 0
    tq = MIX_TILE

    q = q_ref[...]
    if latent:
        ks = pl.multiple_of(jnp.clip(t * tq - WINDOW, 0, seq_len - LAT_SPAN), WINDOW)
        kcat = jnp.concatenate([ck_ref[0], k_ref[pl.ds(ks, LAT_SPAN), :]], axis=0).astype(BF16)
        vcat = jnp.concatenate([cv_ref[0], v_ref[pl.ds(ks, LAT_SPAN), :]], axis=0).astype(BF16)
        n_ctx = ck_ref.shape[1]
        n_keys = n_ctx + LAT_SPAN
        kidx = lax.broadcasted_iota(I32, (tq, n_keys), 1)
        qpos = t * tq + lax.broadcasted_iota(I32, (tq, n_keys), 0)
        kpos = ks + kidx - n_ctx
        mask = (kidx < n_ctx) | (jnp.abs(qpos - kpos) <= WINDOW)
    else:
        kcat = k_ref[...].astype(BF16)
        vcat = v_ref[...].astype(BF16)
        mask = None

    heads = []
    for h in range(N_Q_HEADS):
        kv = h // Q_PER_KV
        qh = q[:, h * HEAD_DIM:(h + 1) * HEAD_DIM]
        kh = kcat[:, kv * HEAD_DIM:(kv + 1) * HEAD_DIM]
        vh = vcat[:, kv * HEAD_DIM:(kv + 1) * HEAD_DIM]
        s = lax.dot_general(qh, kh, (((1,), (1,)), ((), ())), preferred_element_type=F32) * ATTN_SCALE
        if mask is not None:
            s = jnp.where(mask, s, NEG_INF)
        sink = sink_ref[h]
        m = jnp.maximum(jnp.max(s, axis=1, keepdims=True), sink)
        p = jnp.exp(s - m)
        den = jnp.sum(p, axis=1, keepdims=True) + jnp.exp(sink - m)
        heads.append(jnp.dot(p.astype(BF16), vh, preferred_element_type=F32) / den)
    attn = jnp.concatenate(heads, axis=1)

    sg = (_rms(jax.nn.gelu(gv_ref[...])) * gvn_ref[...]).astype(BF16)
    bst = bst_ref[...]
    chunks = []
    for c in range(tq // CHUNK):
        groups = []
        for g in range(N_MLP_GROUPS):
            blk = sg[c * CHUNK:(c + 1) * CHUNK, g * LANES:(g + 1) * LANES]
            groups.append(jnp.dot(ws_ref[g], blk, preferred_element_type=F32) + bst[:, g:g + 1])
        chunks.append(jnp.concatenate(groups, axis=1))
    mlp = jax.nn.gelu(u_ref[...]) * jnp.concatenate(chunks, axis=0)

    mixed = jnp.concatenate([_rms(attn) * gattn_ref[...], _rms(mlp) * gmlp_ref[...]], axis=1)
    out = jnp.dot(mixed.astype(BF16), wout_ref[...], preferred_element_type=F32)
    gate1 = mod_ref[pl.ds(row, 1), 2 * D_MODEL:3 * D_MODEL]
    x1 = x_ref[...] + gate1 * out
    x1_ref[...] = x1

    shift2 = mod_ref[pl.ds(row, 1), 3 * D_MODEL:4 * D_MODEL]
    scale2 = mod_ref[pl.ds(row, 1), 4 * D_MODEL:5 * D_MODEL]
    h2 = _rms(x1) * norm2_ref[...] * (1 + scale2) + shift2

    hh = h2.astype(BF16)
    hl = (h2 - hh.astype(F32)).astype(BF16)
    wrt = wrt_ref[...]
    wth = wrt.astype(BF16)
    wtl = (wrt - wth.astype(F32)).astype(BF16)
    nt = (((1,), (1,)), ((), ()))
    logits_t = (lax.dot_general(wth, hh, nt, preferred_element_type=F32)
                + lax.dot_general(wth, hl, nt, preferred_element_type=F32)
                + lax.dot_general(wtl, hh, nt, preferred_element_type=F32))
    pt = jnp.exp(logits_t - jnp.max(logits_t, axis=0, keepdims=True))
    afft_ref[...] = pt / jnp.sum(pt, axis=0, keepdims=True)

    h2_ref[...] = hh.astype(F32).reshape(tq, ROW_TILES, LANES).astype(BF16)


def _mix(sink, q, k, v, cache, u, gv, x2d, mod, w_s_bf16, b_s_t, g_v, g_attn, g_mlp, w_out_bf16, norm2,
         w_router_t, batch, seq_len):
    n = x2d.shape[0]
    latent = cache is not None
    tiles = seq_len // MIX_TILE
    tok = lambda b, t: (b * tiles + t, 0)
    const2 = lambda b, t: (0, 0)

    in_specs = [pl.BlockSpec(memory_space=pltpu.SMEM),
                pl.BlockSpec((MIX_TILE, ATTN_WIDTH), tok),
                pl.BlockSpec((seq_len, KV_WIDTH), lambda b, t: (b, 0)),
                pl.BlockSpec((seq_len, KV_WIDTH), lambda b, t: (b, 0))]
    args = [sink, q, k, v]
    if latent:
        ck, cv = cache
        in_specs += [pl.BlockSpec((1,) + ck.shape[1:], lambda b, t: (b, 0, 0))] * 2
        args += [ck, cv]
    in_specs += [pl.BlockSpec((MIX_TILE, MLP_WIDTH), tok),
                 pl.BlockSpec((MIX_TILE, MLP_WIDTH), tok),
                 pl.BlockSpec((MIX_TILE, D_MODEL), tok),
                 pl.BlockSpec(mod.shape, const2),
                 pl.BlockSpec(w_s_bf16.shape, lambda b, t: (0, 0, 0)),
                 pl.BlockSpec(b_s_t.shape, const2),
                 pl.BlockSpec((1, MLP_WIDTH), const2),
                 pl.BlockSpec((1, ATTN_WIDTH), const2),
                 pl.BlockSpec((1, MLP_WIDTH), const2),
                 pl.BlockSpec(w_out_bf16.shape, const2),
                 pl.BlockSpec((1, D_MODEL), const2),
                 pl.BlockSpec(w_router_t.shape, const2)]
    args += [u, gv, x2d, mod, w_s_bf16, b_s_t, g_v, g_attn, g_mlp, w_out_bf16, norm2, w_router_t]
    return pl.pallas_call(
        functools.partial(_mix_body, latent=latent, seq_len=seq_len),
        grid=(batch, tiles),
        in_specs=in_specs,
        out_specs=[pl.BlockSpec((MIX_TILE, D_MODEL), tok),
                   pl.BlockSpec((MIX_TILE, ROW_TILES, LANES), lambda b, t: (b * tiles + t, 0, 0)),
                   pl.BlockSpec((N_EXPERTS, MIX_TILE), lambda b, t: (0, b * tiles + t))],
        out_shape=[jax.ShapeDtypeStruct((n, D_MODEL), F32),
                   jax.ShapeDtypeStruct((n, ROW_TILES, LANES), BF16),
                   jax.ShapeDtypeStruct((N_EXPERTS, n), F32)],
        compiler_params=pltpu.CompilerParams(vmem_limit_bytes=56 * 2**20),
        name="mix_lat" if latent else "mix_ctx",
    )(*args)


def _lane_exclusive_count(flags_bf16, upper):
    width = upper.shape[0]
    carry = jnp.zeros((flags_bf16.shape[0], 1), F32)
    outs = []
    for j in range(flags_bf16.shape[1] // width):
        blk = flags_bf16[:, j * width:(j + 1) * width]
        outs.append(jnp.dot(blk, upper, preferred_element_type=F32) + carry)
        carry = carry + jnp.sum(blk.astype(F32), axis=1, keepdims=True)
    return jnp.concatenate(outs, axis=1)


def _route_body(afft_ref, src_ref, gate_ref, cnt_ref, idxt_ref, *, cap):
    aff = afft_ref[...]

    def bisect(i, thr):
        cand = thr | jnp.left_shift(jnp.int32(1), 30 - i)
        n_ge = jnp.sum((aff >= lax.bitcast_convert_type(cand, F32)).astype(F32), axis=1, keepdims=True)
        return jnp.where(n_ge >= cap, cand, thr)

    thr = lax.fori_loop(0, 31, bisect, jnp.zeros((N_EXPERTS, 1), I32))

    width = 256
    r = lax.broadcasted_iota(I32, (width, width), 0)
    c = lax.broadcasted_iota(I32, (width, width), 1)
    upper = (r < c).astype(BF16)

    above = aff >= lax.bitcast_convert_type(thr + 1, F32)
    tied = (aff >= lax.bitcast_convert_type(thr, F32)) & jnp.logical_not(above)
    need = cap - jnp.sum(above.astype(F32), axis=1, keepdims=True)
    tie_rank = _lane_exclusive_count(tied.astype(BF16), upper)
    sel = above | (tied & (tie_rank < need))
    sel_bf = sel.astype(BF16)
    pos = _lane_exclusive_count(sel_bf, upper).astype(I32)

    er = lax.broadcasted_iota(I32, (N_EXPERTS, N_EXPERTS), 0)
    ec = lax.broadcasted_iota(I32, (N_EXPERTS, N_EXPERTS), 1)
    rank = jnp.dot((ec < er).astype(BF16), sel_bf, preferred_element_type=F32).astype(I32)

    expert = lax.broadcasted_iota(I32, aff.shape, 0)
    flat = expert * cap + pos
    picks = [sel & (rank == k) for k in range(N_EXPERTS)]
    src_ref[...] = jnp.concatenate(
        [jnp.sum(jnp.where(p, flat, 0), axis=0, keepdims=True) for p in picks], axis=0)
    gate_ref[...] = jnp.concatenate(
        [jnp.sum(jnp.where(p, aff, 0.0), axis=0, keepdims=True) for p in picks], axis=0)
    cnt_ref[...] = jnp.sum(sel.astype(I32), axis=0, keepdims=True)

    incl = pos + sel.astype(I32)
    rows_per_step = 32
    for e in range(N_EXPERTS):
        row = incl[e:e + 1, :]

        def slots(jb, carry):
            j0 = pl.multiple_of(jb * rows_per_step, rows_per_step)
            j = j0 + lax.broadcasted_iota(I32, (rows_per_step, 1), 0)
            n_le = jnp.sum((row <= j).astype(F32), axis=1, keepdims=True)
            idxt_ref[pl.ds(j0, rows_per_step), e:e + 1] = n_le.astype(I32)
            return carry

        lax.fori_loop(0, cap // rows_per_step, slots, 0)


def _route(aff_t, cap):
    n = aff_t.shape[1]
    return pl.pallas_call(
        functools.partial(_route_body, cap=cap),
        out_shape=[jax.ShapeDtypeStruct((N_EXPERTS, n), I32), jax.ShapeDtypeStruct((N_EXPERTS, n), F32),
                   jax.ShapeDtypeStruct((1, n), I32), jax.ShapeDtypeStruct((cap, N_EXPERTS), I32)],
        compiler_params=pltpu.CompilerParams(vmem_limit_bytes=48 * 2**20),
        name="route",
    )(aff_t)


FF_TILE = 256
N_FF_TILES = EXPERT_FF // FF_TILE
OUT_TILE = 256
N_OUT_TILES = D_MODEL // OUT_TILE
EXPERT_STEPS = N_FF_TILES + N_OUT_TILES


def _expert_body(idxc_ref, idxl_ref, hc_ref, hl_ref, wg_ref, wu_ref, wd_ref, oc_ref, ol_ref,
                 gbuf, xb_ref, hid_ref, sem, *, cap_c, cap_l):
    e = pl.program_id(0)
    s = pl.program_id(1)
    cap = cap_c + cap_l

    def issue(expert, step):
        slot = expert % 2
        for idx_ref, src, per_step, base in ((idxc_ref, hc_ref, cap_c // EXPERT_STEPS, 0),
                                             (idxl_ref, hl_ref, cap_l // EXPERT_STEPS, cap_c)):
            for j in range(per_step):
                r = step * per_step + j
                pltpu.make_async_copy(src.at[idx_ref[expert, r]], gbuf.at[slot, base + r], sem.at[slot]).start()

    @pl.when((e == 0) & (s == 0))
    def _first():
        def body(step, carry):
            issue(0, step)
            return carry
        lax.fori_loop(0, EXPERT_STEPS, body, 0)

    @pl.when(e + 1 < N_EXPERTS)
    def _prefetch():
        issue(e + 1, s)

    @pl.when(s == 0)
    def _unpack():
        slot = e % 2
        pltpu.make_async_copy(hc_ref.at[pl.ds(0, cap)], gbuf.at[slot], sem.at[slot]).wait()
        xb_ref[...] = gbuf[slot].astype(F32).reshape(cap, D_MODEL).astype(BF16)

    @pl.when(s < N_FF_TILES)
    def _up():
        x = xb_ref[...]
        a = jnp.dot(x, wg_ref[0].astype(BF16), preferred_element_type=F32)
        b = jnp.dot(x, wu_ref[0].astype(BF16), preferred_element_type=F32)
        hid_ref[s] = (jax.nn.silu(a) * b).astype(BF16)

    @pl.when(s >= N_FF_TILES)
    def _down():
        wd = wd_ref[0].astype(BF16)
        acc = jnp.dot(hid_ref[0], wd[0:FF_TILE, :], preferred_element_type=F32)
        for j in range(1, N_FF_TILES):
            acc = acc + jnp.dot(hid_ref[j], wd[j * FF_TILE:(j + 1) * FF_TILE, :], preferred_element_type=F32)
        oc_ref[...] = acc[0:cap_c, :]
        ol_ref[...] = acc[cap_c:, :]


def _experts(idx_c, idx_l, h2_c, h2_l, w_gate, w_up, w_down):
    cap_c = idx_c.shape[1]
    cap_l = idx_l.shape[1]
    cap = cap_c + cap_l
    up_idx = lambda e, s, ic, il: (e, 0, jnp.minimum(s, N_FF_TILES - 1))
    down_idx = lambda e, s, ic, il: (e, 0, jnp.maximum(s - N_FF_TILES, 0))
    out_idx = lambda e, s, ic, il: (e, jnp.maximum(s - N_FF_TILES, 0))
    return pl.pallas_call(
        functools.partial(_expert_body, cap_c=cap_c, cap_l=cap_l),
        grid_spec=pltpu.PrefetchScalarGridSpec(
            num_scalar_prefetch=2,
            grid=(N_EXPERTS, EXPERT_STEPS),
            in_specs=[pl.BlockSpec(memory_space=pl.ANY),
                      pl.BlockSpec(memory_space=pl.ANY),
                      pl.BlockSpec((1, D_MODEL, FF_TILE), up_idx),
                      pl.BlockSpec((1, D_MODEL, FF_TILE), up_idx),
                      pl.BlockSpec((1, EXPERT_FF, OUT_TILE), down_idx)],
            out_specs=[pl.BlockSpec((cap_c, OUT_TILE), out_idx),
                       pl.BlockSpec((cap_l, OUT_TILE), out_idx)],
            scratch_shapes=[pltpu.VMEM((2, cap, ROW_TILES, LANES), BF16),
                            pltpu.VMEM((cap, D_MODEL), BF16),
                            pltpu.VMEM((N_FF_TILES, cap, FF_TILE), BF16),
                            pltpu.SemaphoreType.DMA((2,))]),
        out_shape=[jax.ShapeDtypeStruct((N_EXPERTS * cap_c, D_MODEL), F32),
                   jax.ShapeDtypeStruct((N_EXPERTS * cap_l, D_MODEL), F32)],
        compiler_params=pltpu.CompilerParams(vmem_limit_bytes=60 * 2**20),
        name="experts",
    )(idx_c, idx_l, h2_c, h2_l, w_gate, w_up, w_down)


COMBINE_TILE = 128


COMBINE_ROWS = N_EXPERTS * COMBINE_TILE


def _combine_body(cnt_nx, src_nx, cnt_cu, src_cu, gate_cu, eo_ref, x1_ref, mod_ref, fn_ref, y_ref,
                  buf, acc_ref, total_ref, sem, *, latent, tiles_per_seq, n_blocks):
    i = pl.program_id(0)
    tb = COMBINE_TILE
    slot = i % 2

    def gather(cnt_ref, src_ref, to_slot):
        def token(n, off):
            count = cnt_ref[0, n]

            def issue(k, c):
                pltpu.make_async_copy(eo_ref.at[src_ref[k, n]], buf.at[to_slot, off + k], sem.at[to_slot]).start()
                return c

            lax.fori_loop(0, count, issue, 0)
            return off + count

        total_ref[to_slot] = lax.fori_loop(0, tb, token, jnp.int32(0))

    @pl.when(i == 0)
    def _first():
        gather(cnt_cu, src_cu, 0)

    @pl.when(i + 1 < n_blocks)
    def _prefetch():
        gather(cnt_nx, src_nx, 1 - slot)

    total = total_ref[slot]
    for bit in range(COMBINE_ROWS.bit_length()):
        rows = 1 << bit

        @pl.when((total >> bit) & 1 == 1)
        def _wait():
            pltpu.make_async_copy(eo_ref.at[pl.ds(0, rows)], buf.at[slot, pl.ds(0, rows)], sem.at[slot]).wait()

    def token(n, off):
        count = cnt_cu[0, n]

        def add(k, acc):
            return acc + gate_cu[k, n] * buf[slot, off + k]

        acc_ref[n] = lax.fori_loop(0, count, add, jnp.zeros((ROW_TILES, LANES), F32))
        return off + count

    lax.fori_loop(0, tb, token, jnp.int32(0))
    moe = acc_ref[...].reshape(tb, D_MODEL)
    row = 1 + i // tiles_per_seq if latent else 0
    gate2 = mod_ref[pl.ds(row, 1), 5 * D_MODEL:6 * D_MODEL]
    y_ref[...] = _rms(x1_ref[...] + gate2 * moe) * fn_ref[...]


def _combine(cnt, src, gates, expert_out, x1, mod, final_norm, latent, seq_len):
    n = x1.shape[0]
    tb = COMBINE_TILE
    n_blocks = n // tb
    cur = lambda i: (0, i)
    nxt = lambda i: (0, jnp.minimum(i + 1, n_blocks - 1))
    smem = lambda rows, idx: pl.BlockSpec((rows, tb), idx, memory_space=pltpu.SMEM)
    return pl.pallas_call(
        functools.partial(_combine_body, latent=latent, tiles_per_seq=seq_len // tb, n_blocks=n_blocks),
        grid=(n_blocks,),
        in_specs=[smem(1, nxt), smem(N_EXPERTS, nxt), smem(1, cur), smem(N_EXPERTS, cur), smem(N_EXPERTS, cur),
                  pl.BlockSpec(memory_space=pl.ANY),
                  pl.BlockSpec((tb, D_MODEL), lambda i: (i, 0)),
                  pl.BlockSpec(mod.shape, lambda i: (0, 0)),
                  pl.BlockSpec((1, D_MODEL), lambda i: (0, 0))],
        out_specs=pl.BlockSpec((tb, D_MODEL), lambda i: (i, 0)),
        out_shape=jax.ShapeDtypeStruct((n, D_MODEL), F32),
        scratch_shapes=[pltpu.VMEM((2, COMBINE_ROWS, ROW_TILES, LANES), F32),
                        pltpu.VMEM((tb, ROW_TILES, LANES), F32),
                        pltpu.SMEM((2,), I32),
                        pltpu.SemaphoreType.DMA((2,))],
        compiler_params=pltpu.CompilerParams(vmem_limit_bytes=52 * 2**20),
        name="combine_lat" if latent else "combine_ctx",
    )(cnt, src, cnt, src, gates, expert_out, x1, mod, final_norm)


def _rope_tables(seq_len):
    pos = jnp.arange(seq_len)
    half = HEAD_DIM // 2
    freqs = ROPE_BASE ** (-jnp.arange(0, half, 2, dtype=F32) / half)
    ang_r = (pos // GRID_W).astype(F32)[:, None] * freqs[None, :]
    ang_c = (pos % GRID_W).astype(F32)[:, None] * freqs[None, :]
    cos = jnp.concatenate([jnp.cos(ang_r)] * 2 + [jnp.cos(ang_c)] * 2, axis=1)
    sin = jnp.concatenate([-jnp.sin(ang_r), jnp.sin(ang_r), -jnp.sin(ang_c), jnp.sin(ang_c)], axis=1)
    return cos, sin


def kernel(x_prompt, x_sample, cache_k, cache_v, c, c_ctx, w_ada, b_ada, norm1, w_in, attn_sink, w_s, b_s, g_v,
           g_attn, g_mlp, w_out, norm2, w_router, w_gate, w_up, w_down, final_norm):
    depth = w_ada.shape[0]
    assert depth == 1, "single-layer trunk"
    batch, seq, _ = x_prompt.shape
    dec_batch, dec_seq, _ = x_sample.shape
    n_ctx = batch * seq
    n_lat = dec_batch * dec_seq
    cap_c = CAPACITY_FACTOR * n_ctx // N_EXPERTS
    cap_l = CAPACITY_FACTOR * n_lat // N_EXPERTS
    l = 0

    cvec = jnp.concatenate([c_ctx[None, :], c, jnp.zeros((MOD_ROWS - 1 - dec_batch, D_MODEL), F32)], axis=0)
    mod = _adaln(cvec, w_ada[l], b_ada[l][None, :])

    w_in_b = w_in[l].astype(BF16)
    w_out_b = w_out[l].astype(BF16)
    w_s_b = w_s[l].astype(BF16)
    b_s_t = jnp.transpose(b_s[l])
    w_router_t = jnp.transpose(w_router[l])
    row = lambda a: a.reshape(1, -1)
    shared = (mod, w_s_b, b_s_t, row(g_v[l]), row(g_attn[l]), row(g_mlp[l]), w_out_b, row(norm2[l]), w_router_t)

    xp = x_prompt.reshape(n_ctx, D_MODEL)
    xl = x_sample.reshape(n_lat, D_MODEL)

    q, k, v, u, gv = _inproj(xp, mod, row(norm1[l]), w_in_b, None, seq)
    x1_c, h2_c, afft_c = _mix(attn_sink[l], q, k, v, None, u, gv, xp, *shared, batch, seq)
    state_k = k.reshape(batch, 1, seq, N_KV_HEADS, HEAD_DIM)
    state_v = v.reshape(batch, 1, seq, N_KV_HEADS, HEAD_DIM)

    q, k, v, u, gv = _inproj(xl, mod, row(norm1[l]), w_in_b, _rope_tables(dec_seq), dec_seq)
    cache = (cache_k[:, l].reshape(dec_batch, -1, KV_WIDTH), cache_v[:, l].reshape(dec_batch, -1, KV_WIDTH))
    x1_l, h2_l, afft_l = _mix(attn_sink[l], q, k, v, cache, u, gv, xl, *shared, dec_batch, dec_seq)

    src_c, gate_c, cnt_c, idxt_c = _route(afft_c, cap_c)
    src_l, gate_l, cnt_l, idxt_l = _route(afft_l, cap_l)
    eo_c, eo_l = _experts(jnp.transpose(idxt_c), jnp.transpose(idxt_l), h2_c, h2_l, w_gate[l], w_up[l], w_down[l])
    fn = row(final_norm)
    slabs = lambda eo: eo.reshape(eo.shape[0], ROW_TILES, LANES)
    y_c = _combine(cnt_c, src_c, gate_c, slabs(eo_c), x1_c, mod, fn, False, seq)
    y_l = _combine(cnt_l, src_l, gate_l, slabs(eo_l), x1_l, mod, fn, True, dec_seq)

    return (y_c.reshape(batch, seq, D_MODEL), y_l.reshape(dec_batch, dec_seq, D_MODEL), state_k, state_v)
```

```python
import functools

import jax
import jax.numpy as jnp
from jax import lax
from jax.experimental import pallas as pl
from jax.experimental.pallas import tpu as pltpu

F32 = jnp.float32
BF16 = jnp.bfloat16
I32 = jnp.int32

D_MODEL = 2048
GRID_W = 64
HEAD_DIM = 128
N_Q_HEADS = 8
N_KV_HEADS = 2
Q_PER_KV = N_Q_HEADS // N_KV_HEADS
ATTN_WIDTH = N_Q_HEADS * HEAD_DIM
KV_WIDTH = N_KV_HEADS * HEAD_DIM
N_MLP_GROUPS = 8
MLP_WIDTH = 1024
Q_END = ATTN_WIDTH
K_END = Q_END + KV_WIDTH
V_END = K_END + KV_WIDTH
U_END = V_END + MLP_WIDTH
IN_WIDTH = U_END + MLP_WIDTH
CHUNK = 128
WINDOW = 128
N_EXPERTS = 16
EXPERT_FF = 2048
CAPACITY_FACTOR = 2
ROPE_BASE = 10000.0
EPS = 1e-6
NEG_INF = -1e30
ATTN_SCALE = HEAD_DIM ** -0.5

LANES = 128
MIX_TILE = 256
LAT_SPAN = MIX_TILE + 2 * WINDOW
ROW_TILES = D_MODEL // LANES
MOD_ROWS = 8


def _rms(x):
    return x * lax.rsqrt(jnp.mean(x * x, axis=-1, keepdims=True) + EPS)


def _adaln_body(c_ref, w_ref, b_ref, o_ref):
    s = jax.nn.silu(c_ref[...])
    o_ref[...] = jnp.dot(s.astype(BF16), w_ref[...].astype(BF16),
                         preferred_element_type=F32) + b_ref[...]


def _adaln(cvec, w_ada, b_ada):
    tn = 1024
    width = w_ada.shape[1]
    return pl.pallas_call(
        _adaln_body,
        grid=(width // tn,),
        in_specs=[pl.BlockSpec((MOD_ROWS, D_MODEL), lambda j: (0, 0)),
                  pl.BlockSpec((D_MODEL, tn), lambda j: (0, j)),
                  pl.BlockSpec((1, tn), lambda j: (0, j))],
        out_specs=pl.BlockSpec((MOD_ROWS, tn), lambda j: (0, j)),
        out_shape=jax.ShapeDtypeStruct((MOD_ROWS, width), F32),
        compiler_params=pltpu.CompilerParams(vmem_limit_bytes=40 * 2**20),
        name="adaln",
    )(cvec, w_ada, b_ada)


def _rope(x, cos, sin, n_heads):
    lane = lax.broadcasted_iota(I32, (x.shape[0], HEAD_DIM), 1)
    first = (lane & 32) == 0
    outs = []
    for h in range(n_heads):
        xh = x[:, h * HEAD_DIM:(h + 1) * HEAD_DIM]
        partner = jnp.where(first, pltpu.roll(xh, 96, 1), pltpu.roll(xh, 32, 1))
        outs.append(xh * cos + partner * sin)
    return jnp.concatenate(outs, axis=1)


def _inproj_body(*refs, latent, tiles_per_seq):
    if latent:
        x_ref, mod_ref, g_ref, w_ref, cos_ref, sin_ref, q_ref, k_ref, v_ref, u_ref, gv_ref = refs
    else:
        x_ref, mod_ref, g_ref, w_ref, q_ref, k_ref, v_ref, u_ref, gv_ref = refs
    i = pl.program_id(0)
    row = 1 + i // tiles_per_seq if latent else 0
    shift = mod_ref[pl.ds(row, 1), 0:D_MODEL]
    scale = mod_ref[pl.ds(row, 1), D_MODEL:2 * D_MODEL]
    h = _rms(x_ref[...]) * g_ref[...] * (1 + scale) + shift
    z = jnp.dot(h.astype(BF16), w_ref[...], preferred_element_type=F32)
    q = z[:, 0:Q_END]
    k = z[:, Q_END:K_END]
    if latent:
        cos = cos_ref[...]
        sin = sin_ref[...]
        q = _rope(q, cos, sin, N_Q_HEADS)
        k = _rope(k, cos, sin, N_KV_HEADS)
    q_ref[...] = q.astype(BF16)
    k_ref[...] = k
    v_ref[...] = z[:, K_END:V_END]
    u_ref[...] = z[:, V_END:U_END]
    gv_ref[...] = z[:, U_END:IN_WIDTH]


def _inproj(x2d, mod, norm1, w_in_bf16, rope_tables, seq_len):
    n = x2d.shape[0]
    tm = 256
    latent = rope_tables is not None
    tiles_per_seq = seq_len // tm
    in_specs = [pl.BlockSpec((tm, D_MODEL), lambda i: (i, 0)),
                pl.BlockSpec(mod.shape, lambda i: (0, 0)),
                pl.BlockSpec((1, D_MODEL), lambda i: (0, 0)),
                pl.BlockSpec((D_MODEL, IN_WIDTH), lambda i: (0, 0))]
    args = [x2d, mod, norm1, w_in_bf16]
    if latent:
        in_specs += [pl.BlockSpec((tm, HEAD_DIM), lambda i: (i % tiles_per_seq, 0))] * 2
        args += list(rope_tables)
    widths = (ATTN_WIDTH, KV_WIDTH, KV_WIDTH, MLP_WIDTH, MLP_WIDTH)
    dtypes = (BF16, F32, F32, F32, F32)
    return pl.pallas_call(
        functools.partial(_inproj_body, latent=latent, tiles_per_seq=tiles_per_seq),
        grid=(n // tm,),
        in_specs=in_specs,
        out_specs=[pl.BlockSpec((tm, w), lambda i: (i, 0)) for w in widths],
        out_shape=[jax.ShapeDtypeStruct((n, w), dt) for w, dt in zip(widths, dtypes)],
        compiler_params=pltpu.CompilerParams(vmem_limit_bytes=56 * 2**20),
        name="inproj_lat" if latent else "inproj_ctx",
    )(*args)


def _mix_body(*refs, latent, seq_len):
    if latent:
        (sink_ref, q_ref, k_ref, v_ref, ck_ref, cv_ref, u_ref, gv_ref, x_ref, mod_ref, ws_ref, bst_ref,
         gvn_ref, gattn_ref, gmlp_ref, wout_ref, norm2_ref, wrt_ref,
         x1_ref, h2_ref, afft_ref) = refs
    else:
        (sink_ref, q_ref, k_ref, v_ref, u_ref, gv_ref, x_ref, mod_ref, ws_ref, bst_ref,
         gvn_ref, gattn_ref, gmlp_ref, wout_ref, norm2_ref, wrt_ref,
         x1_ref, h2_ref, afft_ref) = refs
    b = pl.program_id(0)
    t = pl.program_id(1)
    row = 1 + b if latent else 0
    tq = MIX_TILE

    q = q_ref[...]
    if latent:
        ks = pl.multiple_of(jnp.clip(t * tq - WINDOW, 0, seq_len - LAT_SPAN), WINDOW)
        kcat = jnp.concatenate([ck_ref[0], k_ref[pl.ds(ks, LAT_SPAN), :]], axis=0).astype(BF16)
        vcat = jnp.concatenate([cv_ref[0], v_ref[pl.ds(ks, LAT_SPAN), :]], axis=0).astype(BF16)
        n_ctx = ck_ref.shape[1]
        n_keys = n_ctx + LAT_SPAN
        kidx = lax.broadcasted_iota(I32, (tq, n_keys), 1)
        qpos = t * tq + lax.broadcasted_iota(I32, (tq, n_keys), 0)
        kpos = ks + kidx - n_ctx
        mask = (kidx < n_ctx) | (jnp.abs(qpos - kpos) <= WINDOW)
    else:
        kcat = k_ref[...].astype(BF16)
        vcat = v_ref[...].astype(BF16)
        mask = None

    heads = []
    for h in range(N_Q_HEADS):
        kv = h // Q_PER_KV
        qh = q[:, h * HEAD_DIM:(h + 1) * HEAD_DIM]
        kh = kcat[:, kv * HEAD_DIM:(kv + 1) * HEAD_DIM]
        vh = vcat[:, kv * HEAD_DIM:(kv + 1) * HEAD_DIM]
        s = lax.dot_general(qh, kh, (((1,), (1,)), ((), ())), preferred_element_type=F32) * ATTN_SCALE
        if mask is not None:
            s = jnp.where(mask, s, NEG_INF)
        sink = sink_ref[h]
        m = jnp.maximum(jnp.max(s, axis=1, keepdims=True), sink)
        p = jnp.exp(s - m)
        den = jnp.sum(p, axis=1, keepdims=True) + jnp.exp(sink - m)
        heads.append(jnp.dot(p.astype(BF16), vh, preferred_element_type=F32) / den)
    attn = jnp.concatenate(heads, axis=1)

    sg = (_rms(jax.nn.gelu(gv_ref[...])) * gvn_ref[...]).astype(BF16)
    bst = bst_ref[...]
    chunks = []
    for c in range(tq // CHUNK):
        groups = []
        for g in range(N_MLP_GROUPS):
            blk = sg[c * CHUNK:(c + 1) * CHUNK, g * LANES:(g + 1) * LANES]
            groups.append(jnp.dot(ws_ref[g], blk, preferred_element_type=F32) + bst[:, g:g + 1])
        chunks.append(jnp.concatenate(groups, axis=1))
    mlp = jax.nn.gelu(u_ref[...]) * jnp.concatenate(chunks, axis=0)

    mixed = jnp.concatenate([_rms(attn) * gattn_ref[...], _rms(mlp) * gmlp_ref[...]], axis=1)
    out = jnp.dot(mixed.astype(BF16), wout_ref[...], preferred_element_type=F32)
    gate1 = mod_ref[pl.ds(row, 1), 2 * D_MODEL:3 * D_MODEL]
    x1 = x_ref[...] + gate1 * out
    x1_ref[...] = x1

    shift2 = mod_ref[pl.ds(row, 1), 3 * D_MODEL:4 * D_MODEL]
    scale2 = mod_ref[pl.ds(row, 1), 4 * D_MODEL:5 * D_MODEL]
    h2 = _rms(x1) * norm2_ref[...] * (1 + scale2) + shift2

    hh = h2.astype(BF16)
    hl = (h2 - hh.astype(F32)).astype(BF16)
    wrt = wrt_ref[...]
    wth = wrt.astype(BF16)
    wtl = (wrt - wth.astype(F32)).astype(BF16)
    nt = (((1,), (1,)), ((), ()))
    logits_t = (lax.dot_general(wth, hh, nt, preferred_element_type=F32)
                + lax.dot_general(wth, hl, nt, preferred_element_type=F32)
                + lax.dot_general(wtl, hh, nt, preferred_element_type=F32))
    pt = jnp.exp(logits_t - jnp.max(logits_t, axis=0, keepdims=True))
    afft_ref[...] = pt / jnp.sum(pt, axis=0, keepdims=True)

    h2_ref[...] = hh.astype(F32).reshape(tq, ROW_TILES, LANES).astype(BF16)


def _mix(sink, q, k, v, cache, u, gv, x2d, mod, w_s_bf16, b_s_t, g_v, g_attn, g_mlp, w_out_bf16, norm2,
         w_router_t, batch, seq_len):
    n = x2d.shape[0]
    latent = cache is not None
    tiles = seq_len // MIX_TILE
    tok = lambda b, t: (b * tiles + t, 0)
    const2 = lambda b, t: (0, 0)

    in_specs = [pl.BlockSpec(memory_space=pltpu.SMEM),
                pl.BlockSpec((MIX_TILE, ATTN_WIDTH), tok),
                pl.BlockSpec((seq_len, KV_WIDTH), lambda b, t: (b, 0)),
                pl.BlockSpec((seq_len, KV_WIDTH), lambda b, t: (b, 0))]
    args = [sink, q, k, v]
    if latent:
        ck, cv = cache
        in_specs += [pl.BlockSpec((1,) + ck.shape[1:], lambda b, t: (b, 0, 0))] * 2
        args += [ck, cv]
    in_specs += [pl.BlockSpec((MIX_TILE, MLP_WIDTH), tok),
                 pl.BlockSpec((MIX_TILE, MLP_WIDTH), tok),
                 pl.BlockSpec((MIX_TILE, D_MODEL), tok),
                 pl.BlockSpec(mod.shape, const2),
                 pl.BlockSpec(w_s_bf16.shape, lambda b, t: (0, 0, 0)),
                 pl.BlockSpec(b_s_t.shape, const2),
                 pl.BlockSpec((1, MLP_WIDTH), const2),
                 pl.BlockSpec((1, ATTN_WIDTH), const2),
                 pl.BlockSpec((1, MLP_WIDTH), const2),
                 pl.BlockSpec(w_out_bf16.shape, const2),
                 pl.BlockSpec((1, D_MODEL), const2),
                 pl.BlockSpec(w_router_t.shape, const2)]
    args += [u, gv, x2d, mod, w_s_bf16, b_s_t, g_v, g_attn, g_mlp, w_out_bf16, norm2, w_router_t]
    return pl.pallas_call(
        functools.partial(_mix_body, latent=latent, seq_len=seq_len),
        grid=(batch, tiles),
        in_specs=in_specs,
        out_specs=[pl.BlockSpec((MIX_TILE, D_MODEL), tok),
                   pl.BlockSpec((MIX_TILE, ROW_TILES, LANES), lambda b, t: (b * tiles + t, 0, 0)),
                   pl.BlockSpec((N_EXPERTS, MIX_TILE), lambda b, t: (0, b * tiles + t))],
        out_shape=[jax.ShapeDtypeStruct((n, D_MODEL), F32),
                   jax.ShapeDtypeStruct((n, ROW_TILES, LANES), BF16),
                   jax.ShapeDtypeStruct((N_EXPERTS, n), F32)],
        compiler_params=pltpu.CompilerParams(vmem_limit_bytes=56 * 2**20),
        name="mix_lat" if latent else "mix_ctx",
    )(*args)


def _lane_exclusive_count(flags_bf16, upper):
    width = upper.shape[0]
    carry = jnp.zeros((flags_bf16.shape[0], 1), F32)
    outs = []
    for j in range(flags_bf16.shape[1] // width):
        blk = flags_bf16[:, j * width:(j + 1) * width]
        outs.append(jnp.dot(blk, upper, preferred_element_type=F32) + carry)
        carry = carry + jnp.sum(blk.astype(F32), axis=1, keepdims=True)
    return jnp.concatenate(outs, axis=1)


def _route_body(afft_ref, pos_ref, gate_ref, idxt_ref, *, cap):
    aff = afft_ref[...]

    def bisect(i, thr):
        cand = thr | jnp.left_shift(jnp.int32(1), 30 - i)
        n_ge = jnp.sum((aff >= lax.bitcast_convert_type(cand, F32)).astype(F32), axis=1, keepdims=True)
        return jnp.where(n_ge >= cap, cand, thr)

    thr = lax.fori_loop(0, 31, bisect, jnp.zeros((N_EXPERTS, 1), I32))

    width = 256
    r = lax.broadcasted_iota(I32, (width, width), 0)
    c = lax.broadcasted_iota(I32, (width, width), 1)
    upper = (r < c).astype(BF16)

    above = aff >= lax.bitcast_convert_type(thr + 1, F32)
    tied = (aff >= lax.bitcast_convert_type(thr, F32)) & jnp.logical_not(above)
    need = cap - jnp.sum(above.astype(F32), axis=1, keepdims=True)
    tie_rank = _lane_exclusive_count(tied.astype(BF16), upper)
    sel = above | (tied & (tie_rank < need))
    pos = _lane_exclusive_count(sel.astype(BF16), upper).astype(I32)
    pos_ref[...] = pos
    gate_ref[...] = jnp.where(sel, aff, 0.0)

    incl = pos + sel.astype(I32)
    rows_per_step = 32
    for e in range(N_EXPERTS):
        row = incl[e:e + 1, :]

        def slots(jb, carry):
            j0 = pl.multiple_of(jb * rows_per_step, rows_per_step)
            j = j0 + lax.broadcasted_iota(I32, (rows_per_step, 1), 0)
            n_le = jnp.sum((row <= j).astype(F32), axis=1, keepdims=True)
            idxt_ref[pl.ds(j0, rows_per_step), e:e + 1] = n_le.astype(I32)
            return carry

        lax.fori_loop(0, cap // rows_per_step, slots, 0)


def _route(aff_t, cap):
    n = aff_t.shape[1]
    return pl.pallas_call(
        functools.partial(_route_body, cap=cap),
        out_shape=[jax.ShapeDtypeStruct((N_EXPERTS, n), I32), jax.ShapeDtypeStruct((N_EXPERTS, n), F32),
                   jax.ShapeDtypeStruct((cap, N_EXPERTS), I32)],
        compiler_params=pltpu.CompilerParams(vmem_limit_bytes=48 * 2**20),
        name="route",
    )(aff_t)


FF_TILE = 256
N_FF_TILES = EXPERT_FF // FF_TILE
OUT_TILE = 256
N_OUT_TILES = D_MODEL // OUT_TILE
EXPERT_STEPS = N_FF_TILES + N_OUT_TILES


def _expert_body(idxc_ref, idxl_ref, hc_ref, hl_ref, wg_ref, wu_ref, wd_ref, oc_ref, ol_ref,
                 gbuf, xb_ref, hid_ref, sem, *, cap_c, cap_l):
    e = pl.program_id(0)
    s = pl.program_id(1)
    cap = cap_c + cap_l

    def issue(expert, step):
        slot = expert % 2
        for idx_ref, src, per_step, base in ((idxc_ref, hc_ref, cap_c // EXPERT_STEPS, 0),
                                             (idxl_ref, hl_ref, cap_l // EXPERT_STEPS, cap_c)):
            for j in range(per_step):
                r = step * per_step + j
                pltpu.make_async_copy(src.at[idx_ref[expert, r]], gbuf.at[slot, base + r], sem.at[slot]).start()

    @pl.when((e == 0) & (s == 0))
    def _first():
        def body(step, carry):
            issue(0, step)
            return carry
        lax.fori_loop(0, EXPERT_STEPS, body, 0)

    @pl.when(e + 1 < N_EXPERTS)
    def _prefetch():
        issue(e + 1, s)

    @pl.when(s == 0)
    def _unpack():
        slot = e % 2
        pltpu.make_async_copy(hc_ref.at[pl.ds(0, cap)], gbuf.at[slot], sem.at[slot]).wait()
        xb_ref[...] = gbuf[slot].astype(F32).reshape(cap, D_MODEL).astype(BF16)

    @pl.when(s < N_FF_TILES)
    def _up():
        x = xb_ref[...]
        a = jnp.dot(x, wg_ref[0].astype(BF16), preferred_element_type=F32)
        b = jnp.dot(x, wu_ref[0].astype(BF16), preferred_element_type=F32)
        hid_ref[s] = (jax.nn.silu(a) * b).astype(BF16)

    @pl.when(s >= N_FF_TILES)
    def _down():
        wd = wd_ref[0].astype(BF16)
        acc = jnp.dot(hid_ref[0], wd[0:FF_TILE, :], preferred_element_type=F32)
        for j in range(1, N_FF_TILES):
            acc = acc + jnp.dot(hid_ref[j], wd[j * FF_TILE:(j + 1) * FF_TILE, :], preferred_element_type=F32)
        oc_ref[...] = acc[0:cap_c, :]
        ol_ref[...] = acc[cap_c:, :]


def _experts(idx_c, idx_l, h2_c, h2_l, w_gate, w_up, w_down):
    cap_c = idx_c.shape[1]
    cap_l = idx_l.shape[1]
    cap = cap_c + cap_l
    up_idx = lambda e, s, ic, il: (e, 0, jnp.minimum(s, N_FF_TILES - 1))
    down_idx = lambda e, s, ic, il: (e, 0, jnp.maximum(s - N_FF_TILES, 0))
    out_idx = lambda e, s, ic, il: (e, jnp.maximum(s - N_FF_TILES, 0))
    return pl.pallas_call(
        functools.partial(_expert_body, cap_c=cap_c, cap_l=cap_l),
        grid_spec=pltpu.PrefetchScalarGridSpec(
            num_scalar_prefetch=2,
            grid=(N_EXPERTS, EXPERT_STEPS),
            in_specs=[pl.BlockSpec(memory_space=pl.ANY),
                      pl.BlockSpec(memory_space=pl.ANY),
                      pl.BlockSpec((1, D_MODEL, FF_TILE), up_idx),
                      pl.BlockSpec((1, D_MODEL, FF_TILE), up_idx),
                      pl.BlockSpec((1, EXPERT_FF, OUT_TILE), down_idx)],
            out_specs=[pl.BlockSpec((cap_c, OUT_TILE), out_idx),
                       pl.BlockSpec((cap_l, OUT_TILE), out_idx)],
            scratch_shapes=[pltpu.VMEM((2, cap, ROW_TILES, LANES), BF16),
                            pltpu.VMEM((cap, D_MODEL), BF16),
                            pltpu.VMEM((N_FF_TILES, cap, FF_TILE), BF16),
                            pltpu.SemaphoreType.DMA((2,))]),
        out_shape=[jax.ShapeDtypeStruct((N_EXPERTS * cap_c, D_MODEL), F32),
                   jax.ShapeDtypeStruct((N_EXPERTS * cap_l, D_MODEL), F32)],
        compiler_params=pltpu.CompilerParams(vmem_limit_bytes=60 * 2**20),
        name="experts",
    )(idx_c, idx_l, h2_c, h2_l, w_gate, w_up, w_down)


COMBINE_TILE = 128


SUBLANES = 8
CHUNK_ROWS = 40
CHUNK_K = N_EXPERTS * CHUNK_ROWS


def _split_bf16(x):
    hi = x.astype(BF16)
    return hi, (x - hi.astype(F32)).astype(BF16)


def _combine_body(bs_ref, pos_ref, gate_ref, eo_ref, x1_ref, mod_ref, fn_ref, y_ref, rows, sem, *,
                  cap, n_blocks, latent, tiles_per_seq):
    i = pl.program_id(0)
    tb = COMBINE_TILE
    slot = i % 2

    def window_base(e, blk):
        return (bs_ref[e, blk] // SUBLANES) * SUBLANES

    def window_start(e, blk, p):
        return jnp.minimum(window_base(e, blk) + p * CHUNK_ROWS, cap - CHUNK_ROWS)

    def fetch(blk, p, to_slot):
        for e in range(N_EXPERTS):
            start = pl.multiple_of(e * cap + window_start(e, blk, p), SUBLANES)
            pltpu.make_async_copy(eo_ref.at[pl.ds(start, CHUNK_ROWS)],
                                  rows.at[to_slot, pl.ds(e * CHUNK_ROWS, CHUNK_ROWS)], sem.at[to_slot]).start()

    def wait(to_slot):
        pltpu.make_async_copy(eo_ref.at[pl.ds(0, CHUNK_K)], rows.at[to_slot], sem.at[to_slot]).wait()

    @pl.when(i == 0)
    def _first():
        fetch(0, 0, 0)

    @pl.when(i + 1 < n_blocks)
    def _prefetch():
        fetch(i + 1, 0, 1 - slot)

    lane = lax.broadcasted_iota(I32, (1, N_EXPERTS), 1)
    col = lax.broadcasted_iota(I32, (1, CHUNK_K), 1)
    col_row = (col % CHUNK_ROWS).astype(F32)
    expand = (lax.broadcasted_iota(I32, (N_EXPERTS, CHUNK_K), 0) == col // CHUNK_ROWS).astype(BF16)
    widen = lambda v: jnp.dot(v, expand, preferred_element_type=F32)

    pos = pos_ref[...]
    g_hi, g_lo = _split_bf16(gate_ref[...])
    gate_wide = widen(g_hi) + widen(g_lo)

    def one_pass(p, acc):
        base = jnp.zeros((1, N_EXPERTS), I32)
        start = jnp.zeros((1, N_EXPERTS), I32)
        for e in range(N_EXPERTS):
            base = jnp.where(lane == e, window_base(e, i), base)
            start = jnp.where(lane == e, window_start(e, i, p), start)
        rel = pos - base
        mine = (rel >= p * CHUNK_ROWS) & (rel < (p + 1) * CHUNK_ROWS)
        window_col = jnp.where(mine, pos - start, -1).astype(F32).astype(BF16)
        s_mat = jnp.where(widen(window_col) == col_row, gate_wide, 0.0)
        s_hi, s_lo = _split_bf16(s_mat)
        r_hi, r_lo = _split_bf16(rows[slot])
        return (acc + jnp.dot(s_hi, r_hi, preferred_element_type=F32)
                + jnp.dot(s_hi, r_lo, preferred_element_type=F32)
                + jnp.dot(s_lo, r_hi, preferred_element_type=F32))

    wait(slot)
    moe = one_pass(0, jnp.zeros((tb, D_MODEL), F32))

    n_pass = jnp.int32(1)
    for e in range(N_EXPERTS):
        span = bs_ref[e, i + 1] - window_base(e, i)
        n_pass = jnp.maximum(n_pass, (span + CHUNK_ROWS - 1) // CHUNK_ROWS)

    def extra(p, acc):
        fetch(i, p, slot)
        wait(slot)
        return one_pass(p, acc)

    moe = lax.fori_loop(1, n_pass, extra, moe)
    row = 1 + i // tiles_per_seq if latent else 0
    gate2 = mod_ref[pl.ds(row, 1), 5 * D_MODEL:6 * D_MODEL]
    y_ref[...] = _rms(x1_ref[...] + gate2 * moe) * fn_ref[...]


def _combine(block_start, pos_t, gate_t, expert_out, x1, mod, final_norm, cap, latent, seq_len):
    n = x1.shape[0]
    tb = COMBINE_TILE
    n_blocks = n // tb
    return pl.pallas_call(
        functools.partial(_combine_body, cap=cap, n_blocks=n_blocks, latent=latent, tiles_per_seq=seq_len // tb),
        grid_spec=pltpu.PrefetchScalarGridSpec(
            num_scalar_prefetch=1,
            grid=(n_blocks,),
            in_specs=[pl.BlockSpec((tb, N_EXPERTS), lambda i, bs: (i, 0)),
                      pl.BlockSpec((tb, N_EXPERTS), lambda i, bs: (i, 0)),
                      pl.BlockSpec(memory_space=pl.ANY),
                      pl.BlockSpec((tb, D_MODEL), lambda i, bs: (i, 0)),
                      pl.BlockSpec(mod.shape, lambda i, bs: (0, 0)),
                      pl.BlockSpec((1, D_MODEL), lambda i, bs: (0, 0))],
            out_specs=pl.BlockSpec((tb, D_MODEL), lambda i, bs: (i, 0)),
            scratch_shapes=[pltpu.VMEM((2, CHUNK_K, D_MODEL), F32), pltpu.SemaphoreType.DMA((2,))]),
        out_shape=jax.ShapeDtypeStruct((n, D_MODEL), F32),
        compiler_params=pltpu.CompilerParams(vmem_limit_bytes=48 * 2**20),
        name="combine_lat" if latent else "combine_ctx",
    )(block_start, pos_t, gate_t, expert_out, x1, mod, final_norm)


def _rope_tables(seq_len):
    pos = jnp.arange(seq_len)
    half = HEAD_DIM // 2
    freqs = ROPE_BASE ** (-jnp.arange(0, half, 2, dtype=F32) / half)
    ang_r = (pos // GRID_W).astype(F32)[:, None] * freqs[None, :]
    ang_c = (pos % GRID_W).astype(F32)[:, None] * freqs[None, :]
    cos = jnp.concatenate([jnp.cos(ang_r)] * 2 + [jnp.cos(ang_c)] * 2, axis=1)
    sin = jnp.concatenate([-jnp.sin(ang_r), jnp.sin(ang_r), -jnp.sin(ang_c), jnp.sin(ang_c)], axis=1)
    return cos, sin


def kernel(x_prompt, x_sample, cache_k, cache_v, c, c_ctx, w_ada, b_ada, norm1, w_in, attn_sink, w_s, b_s, g_v,
           g_attn, g_mlp, w_out, norm2, w_router, w_gate, w_up, w_down, final_norm):
    depth = w_ada.shape[0]
    assert depth == 1, "single-layer trunk"
    batch, seq, _ = x_prompt.shape
    dec_batch, dec_seq, _ = x_sample.shape
    n_ctx = batch * seq
    n_lat = dec_batch * dec_seq
    cap_c = CAPACITY_FACTOR * n_ctx // N_EXPERTS
    cap_l = CAPACITY_FACTOR * n_lat // N_EXPERTS
    l = 0

    cvec = jnp.concatenate([c_ctx[None, :], c, jnp.zeros((MOD_ROWS - 1 - dec_batch, D_MODEL), F32)], axis=0)
    mod = _adaln(cvec, w_ada[l], b_ada[l][None, :])

    w_in_b = w_in[l].astype(BF16)
    w_out_b = w_out[l].astype(BF16)
    w_s_b = w_s[l].astype(BF16)
    b_s_t = jnp.transpose(b_s[l])
    w_router_t = jnp.transpose(w_router[l])
    row = lambda a: a.reshape(1, -1)
    shared = (mod, w_s_b, b_s_t, row(g_v[l]), row(g_attn[l]), row(g_mlp[l]), w_out_b, row(norm2[l]), w_router_t)

    xp = x_prompt.reshape(n_ctx, D_MODEL)
    xl = x_sample.reshape(n_lat, D_MODEL)

    q, k, v, u, gv = _inproj(xp, mod, row(norm1[l]), w_in_b, None, seq)
    x1_c, h2_c, afft_c = _mix(attn_sink[l], q, k, v, None, u, gv, xp, *shared, batch, seq)
    state_k = k.reshape(batch, 1, seq, N_KV_HEADS, HEAD_DIM)
    state_v = v.reshape(batch, 1, seq, N_KV_HEADS, HEAD_DIM)

    q, k, v, u, gv = _inproj(xl, mod, row(norm1[l]), w_in_b, _rope_tables(dec_seq), dec_seq)
    cache = (cache_k[:, l].reshape(dec_batch, -1, KV_WIDTH), cache_v[:, l].reshape(dec_batch, -1, KV_WIDTH))
    x1_l, h2_l, afft_l = _mix(attn_sink[l], q, k, v, cache, u, gv, xl, *shared, dec_batch, dec_seq)

    pos_c, gate_c, idxt_c = _route(afft_c, cap_c)
    pos_l, gate_l, idxt_l = _route(afft_l, cap_l)
    eo_c, eo_l = _experts(jnp.transpose(idxt_c), jnp.transpose(idxt_l), h2_c, h2_l, w_gate[l], w_up[l], w_down[l])
    fn = row(final_norm)

    def block_starts(pos, cap):
        return jnp.concatenate([pos[:, ::COMBINE_TILE], jnp.full((N_EXPERTS, 1), cap, I32)], axis=1)

    y_c = _combine(block_starts(pos_c, cap_c), jnp.transpose(pos_c), jnp.transpose(gate_c), eo_c, x1_c, mod, fn,
                   cap_c, False, seq)
    y_l = _combine(block_starts(pos_l, cap_l), jnp.transpose(pos_l), jnp.transpose(gate_l), eo_l, x1_l, mod, fn,
                   cap_l, True, dec_seq)

    return (y_c.reshape(batch, seq, D_MODEL), y_l.reshape(dec_batch, dec_seq, D_MODEL), state_k, state_v)
```

```python
import functools

import jax
import jax.numpy as jnp
from jax import lax
from jax.experimental import pallas as pl
from jax.experimental.pallas import tpu as pltpu

F32 = jnp.float32
BF16 = jnp.bfloat16
I32 = jnp.int32

D_MODEL = 2048
GRID_W = 64
HEAD_DIM = 128
N_Q_HEADS = 8
N_KV_HEADS = 2
Q_PER_KV = N_Q_HEADS // N_KV_HEADS
ATTN_WIDTH = N_Q_HEADS * HEAD_DIM
KV_WIDTH = N_KV_HEADS * HEAD_DIM
N_MLP_GROUPS = 8
MLP_WIDTH = 1024
Q_END = ATTN_WIDTH
K_END = Q_END + KV_WIDTH
V_END = K_END + KV_WIDTH
U_END = V_END + MLP_WIDTH
IN_WIDTH = U_END + MLP_WIDTH
CHUNK = 128
WINDOW = 128
N_EXPERTS = 16
EXPERT_FF = 2048
CAPACITY_FACTOR = 2
ROPE_BASE = 10000.0
EPS = 1e-6
NEG_INF = -1e30
ATTN_SCALE = HEAD_DIM ** -0.5

LANES = 128
MIX_TILE = 256
LAT_SPAN = MIX_TILE + 2 * WINDOW
ROW_TILES = D_MODEL // LANES
MOD_ROWS = 8


def _rms(x):
    return x * lax.rsqrt(jnp.mean(x * x, axis=-1, keepdims=True) + EPS)


def _adaln_body(c_ref, w_ref, b_ref, o_ref):
    s = jax.nn.silu(c_ref[...])
    o_ref[...] = jnp.dot(s.astype(BF16), w_ref[...].astype(BF16),
                         preferred_element_type=F32) + b_ref[...]


def _adaln(cvec, w_ada, b_ada):
    tn = 1024
    width = w_ada.shape[1]
    return pl.pallas_call(
        _adaln_body,
        grid=(width // tn,),
        in_specs=[pl.BlockSpec((MOD_ROWS, D_MODEL), lambda j: (0, 0)),
                  pl.BlockSpec((D_MODEL, tn), lambda j: (0, j)),
                  pl.BlockSpec((1, tn), lambda j: (0, j))],
        out_specs=pl.BlockSpec((MOD_ROWS, tn), lambda j: (0, j)),
        out_shape=jax.ShapeDtypeStruct((MOD_ROWS, width), F32),
        compiler_params=pltpu.CompilerParams(vmem_limit_bytes=40 * 2**20),
        name="adaln",
    )(cvec, w_ada, b_ada)


def _rope(x, cos, sin, n_heads):
    lane = lax.broadcasted_iota(I32, (x.shape[0], HEAD_DIM), 1)
    first = (lane & 32) == 0
    outs = []
    for h in range(n_heads):
        xh = x[:, h * HEAD_DIM:(h + 1) * HEAD_DIM]
        partner = jnp.where(first, pltpu.roll(xh, 96, 1), pltpu.roll(xh, 32, 1))
        outs.append(xh * cos + partner * sin)
    return jnp.concatenate(outs, axis=1)


def _inproj_body(*refs, latent, tiles_per_seq):
    if latent:
        x_ref, mod_ref, g_ref, w_ref, cos_ref, sin_ref, q_ref, k_ref, v_ref, u_ref, gv_ref = refs
    else:
        x_ref, mod_ref, g_ref, w_ref, q_ref, k_ref, v_ref, u_ref, gv_ref = refs
    i = pl.program_id(0)
    row = 1 + i // tiles_per_seq if latent else 0
    shift = mod_ref[pl.ds(row, 1), 0:D_MODEL]
    scale = mod_ref[pl.ds(row, 1), D_MODEL:2 * D_MODEL]
    h = _rms(x_ref[...]) * g_ref[...] * (1 + scale) + shift
    z = jnp.dot(h.astype(BF16), w_ref[...], preferred_element_type=F32)
    q = z[:, 0:Q_END]
    k = z[:, Q_END:K_END]
    if latent:
        cos = cos_ref[...]
        sin = sin_ref[...]
        q = _rope(q, cos, sin, N_Q_HEADS)
        k = _rope(k, cos, sin, N_KV_HEADS)
    q_ref[...] = q.astype(BF16)
    k_ref[...] = k
    v_ref[...] = z[:, K_END:V_END]
    u_ref[...] = z[:, V_END:U_END]
    gv_ref[...] = z[:, U_END:IN_WIDTH]


def _inproj(x2d, mod, norm1, w_in_bf16, rope_tables, seq_len):
    n = x2d.shape[0]
    tm = 256
    latent = rope_tables is not None
    tiles_per_seq = seq_len // tm
    in_specs = [pl.BlockSpec((tm, D_MODEL), lambda i: (i, 0)),
                pl.BlockSpec(mod.shape, lambda i: (0, 0)),
                pl.BlockSpec((1, D_MODEL), lambda i: (0, 0)),
                pl.BlockSpec((D_MODEL, IN_WIDTH), lambda i: (0, 0))]
    args = [x2d, mod, norm1, w_in_bf16]
    if latent:
        in_specs += [pl.BlockSpec((tm, HEAD_DIM), lambda i: (i % tiles_per_seq, 0))] * 2
        args += list(rope_tables)
    widths = (ATTN_WIDTH, KV_WIDTH, KV_WIDTH, MLP_WIDTH, MLP_WIDTH)
    dtypes = (BF16, F32, F32, F32, F32)
    return pl.pallas_call(
        functools.partial(_inproj_body, latent=latent, tiles_per_seq=tiles_per_seq),
        grid=(n // tm,),
        in_specs=in_specs,
        out_specs=[pl.BlockSpec((tm, w), lambda i: (i, 0)) for w in widths],
        out_shape=[jax.ShapeDtypeStruct((n, w), dt) for w, dt in zip(widths, dtypes)],
        compiler_params=pltpu.CompilerParams(vmem_limit_bytes=56 * 2**20),
        name="inproj_lat" if latent else "inproj_ctx",
    )(*args)


def _mix_body(*refs, latent, seq_len):
    if latent:
        (sink_ref, q_ref, k_ref, v_ref, ck_ref, cv_ref, u_ref, gv_ref, x_ref, mod_ref, ws_ref, bst_ref,
         gvn_ref, gattn_ref, gmlp_ref, wout_ref, norm2_ref, wrt_ref,
         x1_ref, h2_ref, afft_ref) = refs
    else:
        (sink_ref, q_ref, k_ref, v_ref, u_ref, gv_ref, x_ref, mod_ref, ws_ref, bst_ref,
         gvn_ref, gattn_ref, gmlp_ref, wout_ref, norm2_ref, wrt_ref,
         x1_ref, h2_ref, afft_ref) = refs
    b = pl.program_id(0)
    t = pl.program_id(1)
    row = 1 + b if latent else 0
    tq = MIX_TILE

    q = q_ref[...]
    if latent:
        ks = pl.multiple_of(jnp.clip(t * tq - WINDOW, 0, seq_len - LAT_SPAN), WINDOW)
        kcat = jnp.concatenate([ck_ref[0], k_ref[pl.ds(ks, LAT_SPAN), :]], axis=0).astype(BF16)
        vcat = jnp.concatenate([cv_ref[0], v_ref[pl.ds(ks, LAT_SPAN), :]], axis=0).astype(BF16)
        n_ctx = ck_ref.shape[1]
        n_keys = n_ctx + LAT_SPAN
        kidx = lax.broadcasted_iota(I32, (tq, n_keys), 1)
        qpos = t * tq + lax.broadcasted_iota(I32, (tq, n_keys), 0)
        kpos = ks + kidx - n_ctx
        mask = (kidx < n_ctx) | (jnp.abs(qpos - kpos) <= WINDOW)
    else:
        kcat = k_ref[...].astype(BF16)
        vcat = v_ref[...].astype(BF16)
        mask = None

    heads = []
    for h in range(N_Q_HEADS):
        kv = h // Q_PER_KV
        qh = q[:, h * HEAD_DIM:(h + 1) * HEAD_DIM]
        kh = kcat[:, kv * HEAD_DIM:(kv + 1) * HEAD_DIM]
        vh = vcat[:, kv * HEAD_DIM:(kv + 1) * HEAD_DIM]
        s = lax.dot_general(qh, kh, (((1,), (1,)), ((), ())), preferred_element_type=F32) * ATTN_SCALE
        if mask is not None:
            s = jnp.where(mask, s, NEG_INF)
        sink = sink_ref[h]
        m = jnp.maximum(jnp.max(s, axis=1, keepdims=True), sink)
        p = jnp.exp(s - m)
        den = jnp.sum(p, axis=1, keepdims=True) + jnp.exp(sink - m)
        heads.append(jnp.dot(p.astype(BF16), vh, preferred_element_type=F32) / den)
    attn = jnp.concatenate(heads, axis=1)

    sg = (_rms(jax.nn.gelu(gv_ref[...])) * gvn_ref[...]).astype(BF16)
    bst = bst_ref[...]
    chunks = []
    for c in range(tq // CHUNK):
        groups = []
        for g in range(N_MLP_GROUPS):
            blk = sg[c * CHUNK:(c + 1) * CHUNK, g * LANES:(g + 1) * LANES]
            groups.append(jnp.dot(ws_ref[g], blk, preferred_element_type=F32) + bst[:, g:g + 1])
        chunks.append(jnp.concatenate(groups, axis=1))
    mlp = jax.nn.gelu(u_ref[...]) * jnp.concatenate(chunks, axis=0)

    mixed = jnp.concatenate([_rms(attn) * gattn_ref[...], _rms(mlp) * gmlp_ref[...]], axis=1)
    out = jnp.dot(mixed.astype(BF16), wout_ref[...], preferred_element_type=F32)
    gate1 = mod_ref[pl.ds(row, 1), 2 * D_MODEL:3 * D_MODEL]
    x1 = x_ref[...] + gate1 * out
    x1_ref[...] = x1

    shift2 = mod_ref[pl.ds(row, 1), 3 * D_MODEL:4 * D_MODEL]
    scale2 = mod_ref[pl.ds(row, 1), 4 * D_MODEL:5 * D_MODEL]
    h2 = _rms(x1) * norm2_ref[...] * (1 + scale2) + shift2

    hh = h2.astype(BF16)
    hl = (h2 - hh.astype(F32)).astype(BF16)
    wrt = wrt_ref[...]
    wth = wrt.astype(BF16)
    wtl = (wrt - wth.astype(F32)).astype(BF16)
    nt = (((1,), (1,)), ((), ()))
    logits_t = (lax.dot_general(wth, hh, nt, preferred_element_type=F32)
                + lax.dot_general(wth, hl, nt, preferred_element_type=F32)
                + lax.dot_general(wtl, hh, nt, preferred_element_type=F32))
    pt = jnp.exp(logits_t - jnp.max(logits_t, axis=0, keepdims=True))
    afft_ref[...] = pt / jnp.sum(pt, axis=0, keepdims=True)

    h2_ref[...] = hh.astype(F32).reshape(tq, ROW_TILES, LANES).astype(BF16)


def _mix(sink, q, k, v, cache, u, gv, x2d, mod, w_s_bf16, b_s_t, g_v, g_attn, g_mlp, w_out_bf16, norm2,
         w_router_t, batch, seq_len):
    n = x2d.shape[0]
    latent = cache is not None
    tiles = seq_len // MIX_TILE
    tok = lambda b, t: (b * tiles + t, 0)
    const2 = lambda b, t: (0, 0)

    in_specs = [pl.BlockSpec(memory_space=pltpu.SMEM),
                pl.BlockSpec((MIX_TILE, ATTN_WIDTH), tok),
                pl.BlockSpec((seq_len, KV_WIDTH), lambda b, t: (b, 0)),
                pl.BlockSpec((seq_len, KV_WIDTH), lambda b, t: (b, 0))]
    args = [sink, q, k, v]
    if latent:
        ck, cv = cache
        in_specs += [pl.BlockSpec((1,) + ck.shape[1:], lambda b, t: (b, 0, 0))] * 2
        args += [ck, cv]
    in_specs += [pl.BlockSpec((MIX_TILE, MLP_WIDTH), tok),
                 pl.BlockSpec((MIX_TILE, MLP_WIDTH), tok),
                 pl.BlockSpec((MIX_TILE, D_MODEL), tok),
                 pl.BlockSpec(mod.shape, const2),
                 pl.BlockSpec(w_s_bf16.shape, lambda b, t: (0, 0, 0)),
                 pl.BlockSpec(b_s_t.shape, const2),
                 pl.BlockSpec((1, MLP_WIDTH), const2),
                 pl.BlockSpec((1, ATTN_WIDTH), const2),
                 pl.BlockSpec((1, MLP_WIDTH), const2),
                 pl.BlockSpec(w_out_bf16.shape, const2),
                 pl.BlockSpec((1, D_MODEL), const2),
                 pl.BlockSpec(w_router_t.shape, const2)]
    args += [u, gv, x2d, mod, w_s_bf16, b_s_t, g_v, g_attn, g_mlp, w_out_bf16, norm2, w_router_t]
    return pl.pallas_call(
        functools.partial(_mix_body, latent=latent, seq_len=seq_len),
        grid=(batch, tiles),
        in_specs=in_specs,
        out_specs=[pl.BlockSpec((MIX_TILE, D_MODEL), tok),
                   pl.BlockSpec((MIX_TILE, ROW_TILES, LANES), lambda b, t: (b * tiles + t, 0, 0)),
                   pl.BlockSpec((N_EXPERTS, MIX_TILE), lambda b, t: (0, b * tiles + t))],
        out_shape=[jax.ShapeDtypeStruct((n, D_MODEL), F32),
                   jax.ShapeDtypeStruct((n, ROW_TILES, LANES), BF16),
                   jax.ShapeDtypeStruct((N_EXPERTS, n), F32)],
        compiler_params=pltpu.CompilerParams(vmem_limit_bytes=56 * 2**20),
        name="mix_lat" if latent else "mix_ctx",
    )(*args)


def _exclusive_counts(flags, upper, lower):
    n_e, nb, _ = flags.shape
    f = flags.astype(BF16)
    local = jnp.dot(f.reshape(n_e * nb, LANES), upper, preferred_element_type=F32).reshape(n_e, nb, LANES)
    totals = jnp.broadcast_to(jnp.sum(flags, axis=2, keepdims=True), flags.shape).astype(BF16)
    before = jnp.stack([jnp.dot(lower, totals[e], preferred_element_type=F32) for e in range(n_e)], axis=0)
    return local, before


def _route_body(aff_ref, pos_ref, gate_ref, idx_ref, local_ref, start_ref, total_ref, *, cap):
    aff = aff_ref[...]
    n_e, nb, _ = aff.shape
    reduce_tokens = lambda x: jnp.sum(jnp.sum(x, axis=2, keepdims=True), axis=1, keepdims=True)

    def bisect(i, thr):
        cand = thr | jnp.left_shift(jnp.int32(1), 30 - i)
        n_ge = reduce_tokens((aff >= lax.bitcast_convert_type(cand, F32)).astype(F32))
        return jnp.where(n_ge >= cap, cand, thr)

    thr = lax.fori_loop(0, 31, bisect, jnp.zeros((n_e, 1, 1), I32))

    upper = (lax.broadcasted_iota(I32, (LANES, LANES), 0) < lax.broadcasted_iota(I32, (LANES, LANES), 1)).astype(BF16)
    lower = (lax.broadcasted_iota(I32, (nb, nb), 1) < lax.broadcasted_iota(I32, (nb, nb), 0)).astype(BF16)

    above = aff >= lax.bitcast_convert_type(thr + 1, F32)
    tied = (aff >= lax.bitcast_convert_type(thr, F32)) & jnp.logical_not(above)
    need = cap - reduce_tokens(above.astype(F32))
    tie_local, tie_before = _exclusive_counts(tied.astype(F32), upper, lower)
    sel = above | (tied & (tie_local + tie_before < need))
    sel_f = sel.astype(F32)

    local, before = _exclusive_counts(sel_f, upper, lower)
    pos_ref[...] = (local + before).astype(I32)
    gate_ref[...] = jnp.where(sel, aff, 0.0)

    local_ref[...] = (local + sel_f).astype(BF16)
    start_ref[...] = before[:, :, 0:1]
    total_ref[...] = jnp.sum(sel_f, axis=2, keepdims=True)
    slot = lax.broadcasted_iota(I32, (1, cap), 1).astype(F32)
    block_id = lax.broadcasted_iota(I32, (nb, 1), 0).astype(F32)

    def lists(e, carry):
        start = start_ref[e]
        end = start + total_ref[e]
        blk = jnp.sum((end <= slot).astype(F32), axis=0, keepdims=True)
        onehot = block_id == blk
        rank = slot - jnp.sum(jnp.where(onehot, start, 0.0), axis=0, keepdims=True)
        counts = lax.dot_general(local_ref[e], onehot.astype(BF16), (((0,), (0,)), ((), ())),
                                 preferred_element_type=F32)
        tok = jnp.sum((counts <= rank).astype(F32), axis=0, keepdims=True)
        idx_ref[pl.ds(e, 1), :] = (blk * LANES + tok).astype(I32)
        return carry

    lax.fori_loop(0, n_e, lists, 0)


def _route(aff_blocks, cap):
    n_e, nb, _ = aff_blocks.shape
    return pl.pallas_call(
        functools.partial(_route_body, cap=cap),
        out_shape=[jax.ShapeDtypeStruct(aff_blocks.shape, I32), jax.ShapeDtypeStruct(aff_blocks.shape, F32),
                   jax.ShapeDtypeStruct((n_e, cap), I32)],
        scratch_shapes=[pltpu.VMEM(aff_blocks.shape, BF16), pltpu.VMEM((n_e, nb, 1), F32),
                        pltpu.VMEM((n_e, nb, 1), F32)],
        compiler_params=pltpu.CompilerParams(vmem_limit_bytes=48 * 2**20),
        name="route",
    )(aff_blocks)


FF_TILE = 256
N_FF_TILES = EXPERT_FF // FF_TILE
OUT_TILE = 256
N_OUT_TILES = D_MODEL // OUT_TILE
EXPERT_STEPS = N_FF_TILES + N_OUT_TILES


def _expert_body(idxc_ref, idxl_ref, hc_ref, hl_ref, wg_ref, wu_ref, wd_ref, oc_ref, ol_ref,
                 gbuf, xb_ref, hid_ref, sem, *, cap_c, cap_l):
    e = pl.program_id(0)
    s = pl.program_id(1)
    cap = cap_c + cap_l
    cur = e % 2
    nxt = 1 - cur
    next_expert = (e + 1) % N_EXPERTS

    def issue(expert, to_slot, part, parts):
        for idx_ref, src, n_rows, base in ((idxc_ref, hc_ref, cap_c, 0), (idxl_ref, hl_ref, cap_l, cap_c)):
            per = n_rows // parts
            for j in range(per):
                r = part * per + j
                pltpu.make_async_copy(src.at[idx_ref[expert, r]], gbuf.at[to_slot, base + r],
                                      sem.at[to_slot]).start()

    def arrived(slot):
        pltpu.make_async_copy(hc_ref.at[pl.ds(0, cap)], gbuf.at[slot], sem.at[slot]).wait()

    def unpack(slot, part, parts):
        per = cap // parts
        r0 = pl.multiple_of(part * per, per)
        xb_ref[slot, pl.ds(r0, per), :] = gbuf[slot, pl.ds(r0, per)].astype(F32).reshape(per, D_MODEL).astype(BF16)

    @pl.when((e == 0) & (s == 0))
    def _first():
        def gather(part, carry):
            issue(0, 0, part, N_FF_TILES)
            return carry
        lax.fori_loop(0, N_FF_TILES, gather, 0)
        arrived(0)

        def relayout(part, carry):
            unpack(0, part, N_OUT_TILES)
            return carry
        lax.fori_loop(0, N_OUT_TILES, relayout, 0)

    @pl.when(s < N_FF_TILES)
    def _up():
        issue(next_expert, nxt, s, N_FF_TILES)
        x = xb_ref[cur]
        a = jnp.dot(x, wg_ref[0].astype(BF16), preferred_element_type=F32)
        b = jnp.dot(x, wu_ref[0].astype(BF16), preferred_element_type=F32)
        hid_ref[s] = (jax.nn.silu(a) * b).astype(BF16)

    @pl.when(s == N_FF_TILES)
    def _arrived():
        arrived(nxt)

    @pl.when(s >= N_FF_TILES)
    def _down():
        unpack(nxt, s - N_FF_TILES, N_OUT_TILES)
        wd = wd_ref[0].astype(BF16)
        acc = jnp.dot(hid_ref[0], wd[0:FF_TILE, :], preferred_element_type=F32)
        for j in range(1, N_FF_TILES):
            acc = acc + jnp.dot(hid_ref[j], wd[j * FF_TILE:(j + 1) * FF_TILE, :], preferred_element_type=F32)
        oc_ref[...] = acc[0:cap_c, :]
        ol_ref[...] = acc[cap_c:, :]


def _experts(idx_c, idx_l, h2_c, h2_l, w_gate, w_up, w_down):
    cap_c = idx_c.shape[1]
    cap_l = idx_l.shape[1]
    cap = cap_c + cap_l
    up_idx = lambda e, s, ic, il: (e, 0, jnp.minimum(s, N_FF_TILES - 1))
    down_idx = lambda e, s, ic, il: (e, 0, jnp.maximum(s - N_FF_TILES, 0))
    out_idx = lambda e, s, ic, il: (e, jnp.maximum(s - N_FF_TILES, 0))
    return pl.pallas_call(
        functools.partial(_expert_body, cap_c=cap_c, cap_l=cap_l),
        grid_spec=pltpu.PrefetchScalarGridSpec(
            num_scalar_prefetch=2,
            grid=(N_EXPERTS, EXPERT_STEPS),
            in_specs=[pl.BlockSpec(memory_space=pl.ANY),
                      pl.BlockSpec(memory_space=pl.ANY),
                      pl.BlockSpec((1, D_MODEL, FF_TILE), up_idx),
                      pl.BlockSpec((1, D_MODEL, FF_TILE), up_idx),
                      pl.BlockSpec((1, EXPERT_FF, OUT_TILE), down_idx)],
            out_specs=[pl.BlockSpec((cap_c, OUT_TILE), out_idx),
                       pl.BlockSpec((cap_l, OUT_TILE), out_idx)],
            scratch_shapes=[pltpu.VMEM((2, cap, ROW_TILES, LANES), BF16),
                            pltpu.VMEM((2, cap, D_MODEL), BF16),
                            pltpu.VMEM((N_FF_TILES, cap, FF_TILE), BF16),
                            pltpu.SemaphoreType.DMA((2,))]),
        out_shape=[jax.ShapeDtypeStruct((N_EXPERTS * cap_c, D_MODEL), F32),
                   jax.ShapeDtypeStruct((N_EXPERTS * cap_l, D_MODEL), F32)],
        compiler_params=pltpu.CompilerParams(vmem_limit_bytes=60 * 2**20),
        name="experts",
    )(idx_c, idx_l, h2_c, h2_l, w_gate, w_up, w_down)


COMBINE_TILE = 128


SUBLANES = 8
WINDOW_ROWS_CTX = 40
WINDOW_ROWS_LAT = 72


def _split_bf16(x):
    hi = x.astype(BF16)
    return hi, (x - hi.astype(F32)).astype(BF16)


def _combine_body(bs_ref, pos_ref, gate_ref, eo_ref, x1_ref, mod_ref, fn_ref, y_ref, rows, sem, *,
                  cap, win, n_blocks, latent, tiles_per_seq):
    i = pl.program_id(0)
    tb = COMBINE_TILE
    slot = i % 2

    def window_base(e, blk):
        return (bs_ref[e, blk] // SUBLANES) * SUBLANES

    def window_start(e, blk, p):
        return jnp.minimum(window_base(e, blk) + p * win, cap - win)

    def fetch(blk, p, to_slot):
        for e in range(N_EXPERTS):
            start = pl.multiple_of(e * cap + window_start(e, blk, p), SUBLANES)
            pltpu.make_async_copy(eo_ref.at[pl.ds(start, win)],
                                  rows.at[to_slot, pl.ds(e * win, win)], sem.at[to_slot]).start()

    def wait(to_slot):
        pltpu.make_async_copy(eo_ref.at[pl.ds(0, (N_EXPERTS * win))], rows.at[to_slot], sem.at[to_slot]).wait()

    @pl.when(i == 0)
    def _first():
        fetch(0, 0, 0)

    @pl.when(i + 1 < n_blocks)
    def _prefetch():
        fetch(i + 1, 0, 1 - slot)

    lane = lax.broadcasted_iota(I32, (1, N_EXPERTS), 1)
    col = lax.broadcasted_iota(I32, (1, (N_EXPERTS * win)), 1)
    col_row = (col % win).astype(F32)
    expand = (lax.broadcasted_iota(I32, (N_EXPERTS, (N_EXPERTS * win)), 0) == col // win).astype(BF16)
    widen = lambda v: jnp.dot(v, expand, preferred_element_type=F32)

    pos = pos_ref[...]
    g_hi, g_lo = _split_bf16(gate_ref[...])
    gate_wide = widen(g_hi) + widen(g_lo)

    def one_pass(p, acc):
        base = jnp.zeros((1, N_EXPERTS), I32)
        start = jnp.zeros((1, N_EXPERTS), I32)
        for e in range(N_EXPERTS):
            base = jnp.where(lane == e, window_base(e, i), base)
            start = jnp.where(lane == e, window_start(e, i, p), start)
        rel = pos - base
        mine = (rel >= p * win) & (rel < (p + 1) * win)
        window_col = jnp.where(mine, pos - start, -1).astype(F32).astype(BF16)
        s_mat = jnp.where(widen(window_col) == col_row, gate_wide, 0.0)
        s_hi, s_lo = _split_bf16(s_mat)
        r_hi, r_lo = _split_bf16(rows[slot])
        return (acc + jnp.dot(s_hi, r_hi, preferred_element_type=F32)
                + jnp.dot(s_hi, r_lo, preferred_element_type=F32)
                + jnp.dot(s_lo, r_hi, preferred_element_type=F32))

    wait(slot)
    moe = one_pass(0, jnp.zeros((tb, D_MODEL), F32))

    n_pass = jnp.int32(1)
    for e in range(N_EXPERTS):
        span = bs_ref[e, i + 1] - window_base(e, i)
        n_pass = jnp.maximum(n_pass, (span + win - 1) // win)

    def extra(p, acc):
        fetch(i, p, slot)
        wait(slot)
        return one_pass(p, acc)

    moe = lax.fori_loop(1, n_pass, extra, moe)
    row = 1 + i // tiles_per_seq if latent else 0
    gate2 = mod_ref[pl.ds(row, 1), 5 * D_MODEL:6 * D_MODEL]
    y_ref[...] = _rms(x1_ref[...] + gate2 * moe) * fn_ref[...]


def _combine(block_start, pos_t, gate_t, expert_out, x1, mod, final_norm, cap, latent, seq_len):
    n = x1.shape[0]
    tb = COMBINE_TILE
    n_blocks = n // tb
    win = WINDOW_ROWS_LAT if latent else WINDOW_ROWS_CTX
    return pl.pallas_call(
        functools.partial(_combine_body, cap=cap, win=win, n_blocks=n_blocks, latent=latent,
                          tiles_per_seq=seq_len // tb),
        grid_spec=pltpu.PrefetchScalarGridSpec(
            num_scalar_prefetch=1,
            grid=(n_blocks,),
            in_specs=[pl.BlockSpec((tb, N_EXPERTS), lambda i, bs: (i, 0)),
                      pl.BlockSpec((tb, N_EXPERTS), lambda i, bs: (i, 0)),
                      pl.BlockSpec(memory_space=pl.ANY),
                      pl.BlockSpec((tb, D_MODEL), lambda i, bs: (i, 0)),
                      pl.BlockSpec(mod.shape, lambda i, bs: (0, 0)),
                      pl.BlockSpec((1, D_MODEL), lambda i, bs: (0, 0))],
            out_specs=pl.BlockSpec((tb, D_MODEL), lambda i, bs: (i, 0)),
            scratch_shapes=[pltpu.VMEM((2, N_EXPERTS * win, D_MODEL), F32), pltpu.SemaphoreType.DMA((2,))]),
        out_shape=jax.ShapeDtypeStruct((n, D_MODEL), F32),
        compiler_params=pltpu.CompilerParams(vmem_limit_bytes=48 * 2**20),
        name="combine_lat" if latent else "combine_ctx",
    )(block_start, pos_t, gate_t, expert_out, x1, mod, final_norm)


def _rope_tables(seq_len):
    pos = jnp.arange(seq_len)
    half = HEAD_DIM // 2
    freqs = ROPE_BASE ** (-jnp.arange(0, half, 2, dtype=F32) / half)
    ang_r = (pos // GRID_W).astype(F32)[:, None] * freqs[None, :]
    ang_c = (pos % GRID_W).astype(F32)[:, None] * freqs[None, :]
    cos = jnp.concatenate([jnp.cos(ang_r)] * 2 + [jnp.cos(ang_c)] * 2, axis=1)
    sin = jnp.concatenate([-jnp.sin(ang_r), jnp.sin(ang_r), -jnp.sin(ang_c), jnp.sin(ang_c)], axis=1)
    return cos, sin


def kernel(x_prompt, x_sample, cache_k, cache_v, c, c_ctx, w_ada, b_ada, norm1, w_in, attn_sink, w_s, b_s, g_v,
           g_attn, g_mlp, w_out, norm2, w_router, w_gate, w_up, w_down, final_norm):
    depth = w_ada.shape[0]
    assert depth == 1, "single-layer trunk"
    batch, seq, _ = x_prompt.shape
    dec_batch, dec_seq, _ = x_sample.shape
    n_ctx = batch * seq
    n_lat = dec_batch * dec_seq
    cap_c = CAPACITY_FACTOR * n_ctx // N_EXPERTS
    cap_l = CAPACITY_FACTOR * n_lat // N_EXPERTS
    l = 0

    cvec = jnp.concatenate([c_ctx[None, :], c, jnp.zeros((MOD_ROWS - 1 - dec_batch, D_MODEL), F32)], axis=0)
    mod = _adaln(cvec, w_ada[l], b_ada[l][None, :])

    w_in_b = w_in[l].astype(BF16)
    w_out_b = w_out[l].astype(BF16)
    w_s_b = w_s[l].astype(BF16)
    b_s_t = jnp.transpose(b_s[l])
    w_router_t = jnp.transpose(w_router[l])
    row = lambda a: a.reshape(1, -1)
    shared = (mod, w_s_b, b_s_t, row(g_v[l]), row(g_attn[l]), row(g_mlp[l]), w_out_b, row(norm2[l]), w_router_t)

    xp = x_prompt.reshape(n_ctx, D_MODEL)
    xl = x_sample.reshape(n_lat, D_MODEL)

    q, k, v, u, gv = _inproj(xp, mod, row(norm1[l]), w_in_b, None, seq)
    x1_c, h2_c, afft_c = _mix(attn_sink[l], q, k, v, None, u, gv, xp, *shared, batch, seq)
    state_k = k.reshape(batch, 1, seq, N_KV_HEADS, HEAD_DIM)
    state_v = v.reshape(batch, 1, seq, N_KV_HEADS, HEAD_DIM)

    q, k, v, u, gv = _inproj(xl, mod, row(norm1[l]), w_in_b, _rope_tables(dec_seq), dec_seq)
    cache = (cache_k[:, l].reshape(dec_batch, -1, KV_WIDTH), cache_v[:, l].reshape(dec_batch, -1, KV_WIDTH))
    x1_l, h2_l, afft_l = _mix(attn_sink[l], q, k, v, cache, u, gv, xl, *shared, dec_batch, dec_seq)

    def route(aff_t, cap):
        n = aff_t.shape[1]
        pos, gate, idx = _route(aff_t.reshape(N_EXPERTS, n // LANES, LANES), cap)
        return pos.reshape(N_EXPERTS, n), gate.reshape(N_EXPERTS, n), idx

    pos_c, gate_c, idx_c = route(afft_c, cap_c)
    pos_l, gate_l, idx_l = route(afft_l, cap_l)
    eo_c, eo_l = _experts(idx_c, idx_l, h2_c, h2_l, w_gate[l], w_up[l], w_down[l])
    fn = row(final_norm)

    def block_starts(pos, cap):
        return jnp.concatenate([pos[:, ::COMBINE_TILE], jnp.full((N_EXPERTS, 1), cap, I32)], axis=1)

    y_c = _combine(block_starts(pos_c, cap_c), jnp.transpose(pos_c), jnp.transpose(gate_c), eo_c, x1_c, mod, fn,
                   cap_c, False, seq)
    y_l = _combine(block_starts(pos_l, cap_l), jnp.transpose(pos_l), jnp.transpose(gate_l), eo_l, x1_l, mod, fn,
                   cap_l, True, dec_seq)

    return (y_c.reshape(batch, seq, D_MODEL), y_l.reshape(dec_batch, dec_seq, D_MODEL), state_k, state_v)
```

```python
import functools

import jax
import jax.numpy as jnp
from jax import lax
from jax.experimental import pallas as pl
from jax.experimental.pallas import tpu as pltpu

F32 = jnp.float32
BF16 = jnp.bfloat16
I32 = jnp.int32

D_MODEL = 2048
GRID_W = 64
HEAD_DIM = 128
N_Q_HEADS = 8
N_KV_HEADS = 2
Q_PER_KV = N_Q_HEADS // N_KV_HEADS
ATTN_WIDTH = N_Q_HEADS * HEAD_DIM
KV_WIDTH = N_KV_HEADS * HEAD_DIM
N_MLP_GROUPS = 8
MLP_WIDTH = 1024
Q_END = ATTN_WIDTH
K_END = Q_END + KV_WIDTH
V_END = K_END + KV_WIDTH
U_END = V_END + MLP_WIDTH
IN_WIDTH = U_END + MLP_WIDTH
CHUNK = 128
WINDOW = 128
N_EXPERTS = 16
EXPERT_FF = 2048
CAPACITY_FACTOR = 2
ROPE_BASE = 10000.0
EPS = 1e-6
NEG_INF = -1e30
ATTN_SCALE = HEAD_DIM ** -0.5

LANES = 128
MIX_TILE = 256
MIX_SUB = 2
LAT_SPAN = MIX_TILE + 2 * WINDOW
ROW_TILES = D_MODEL // LANES
MOD_ROWS = 8


def _rms(x):
    return x * lax.rsqrt(jnp.mean(x * x, axis=-1, keepdims=True) + EPS)


def _adaln_body(c_ref, w_ref, b_ref, o_ref):
    s = jax.nn.silu(c_ref[...])
    o_ref[...] = jnp.dot(s.astype(BF16), w_ref[...].astype(BF16),
                         preferred_element_type=F32) + b_ref[...]


def _adaln(cvec, w_ada, b_ada):
    tn = 1024
    width = w_ada.shape[1]
    return pl.pallas_call(
        _adaln_body,
        grid=(width // tn,),
        in_specs=[pl.BlockSpec((MOD_ROWS, D_MODEL), lambda j: (0, 0)),
                  pl.BlockSpec((D_MODEL, tn), lambda j: (0, j)),
                  pl.BlockSpec((1, tn), lambda j: (0, j))],
        out_specs=pl.BlockSpec((MOD_ROWS, tn), lambda j: (0, j)),
        out_shape=jax.ShapeDtypeStruct((MOD_ROWS, width), F32),
        compiler_params=pltpu.CompilerParams(vmem_limit_bytes=40 * 2**20),
        name="adaln",
    )(cvec, w_ada, b_ada)


def _rope(x, cos, sin, n_heads):
    lane = lax.broadcasted_iota(I32, (x.shape[0], HEAD_DIM), 1)
    first = (lane & 32) == 0
    outs = []
    for h in range(n_heads):
        xh = x[:, h * HEAD_DIM:(h + 1) * HEAD_DIM]
        partner = jnp.where(first, pltpu.roll(xh, 96, 1), pltpu.roll(xh, 32, 1))
        outs.append(xh * cos + partner * sin)
    return jnp.concatenate(outs, axis=1)


def _inproj_body(*refs, latent, tiles_per_seq):
    if latent:
        x_ref, mod_ref, g_ref, w_ref, cos_ref, sin_ref, q_ref, k_ref, v_ref, u_ref, gv_ref = refs
    else:
        x_ref, mod_ref, g_ref, w_ref, q_ref, k_ref, v_ref, u_ref, gv_ref = refs
    i = pl.program_id(0)
    row = 1 + i // tiles_per_seq if latent else 0
    shift = mod_ref[pl.ds(row, 1), 0:D_MODEL]
    scale = mod_ref[pl.ds(row, 1), D_MODEL:2 * D_MODEL]
    h = _rms(x_ref[...]) * g_ref[...] * (1 + scale) + shift
    z = jnp.dot(h.astype(BF16), w_ref[...], preferred_element_type=F32)
    q = z[:, 0:Q_END]
    k = z[:, Q_END:K_END]
    if latent:
        cos = cos_ref[...]
        sin = sin_ref[...]
        q = _rope(q, cos, sin, N_Q_HEADS)
        k = _rope(k, cos, sin, N_KV_HEADS)
    q_ref[...] = q.astype(BF16)
    k_ref[...] = k
    v_ref[...] = z[:, K_END:V_END]
    u_ref[...] = z[:, V_END:U_END]
    gv_ref[...] = z[:, U_END:IN_WIDTH]


def _inproj(x2d, mod, norm1, w_in_bf16, rope_tables, seq_len):
    n = x2d.shape[0]
    tm = 256
    latent = rope_tables is not None
    tiles_per_seq = seq_len // tm
    in_specs = [pl.BlockSpec((tm, D_MODEL), lambda i: (i, 0)),
                pl.BlockSpec(mod.shape, lambda i: (0, 0)),
                pl.BlockSpec((1, D_MODEL), lambda i: (0, 0)),
                pl.BlockSpec((D_MODEL, IN_WIDTH), lambda i: (0, 0))]
    args = [x2d, mod, norm1, w_in_bf16]
    if latent:
        in_specs += [pl.BlockSpec((tm, HEAD_DIM), lambda i: (i % tiles_per_seq, 0))] * 2
        args += list(rope_tables)
    widths = (ATTN_WIDTH, KV_WIDTH, KV_WIDTH, MLP_WIDTH, MLP_WIDTH)
    dtypes = (BF16, F32, F32, F32, F32)
    return pl.pallas_call(
        functools.partial(_inproj_body, latent=latent, tiles_per_seq=tiles_per_seq),
        grid=(n // tm,),
        in_specs=in_specs,
        out_specs=[pl.BlockSpec((tm, w), lambda i: (i, 0)) for w in widths],
        out_shape=[jax.ShapeDtypeStruct((n, w), dt) for w, dt in zip(widths, dtypes)],
        compiler_params=pltpu.CompilerParams(vmem_limit_bytes=56 * 2**20),
        name="inproj_lat" if latent else "inproj_ctx",
    )(*args)


def _mix_body(*refs, latent, seq_len):
    if latent:
        (sink_ref, q_ref, k_ref, v_ref, ck_ref, cv_ref, u_ref, gv_ref, x_ref, mod_ref, ws_ref, bst_ref,
         gvn_ref, gattn_ref, gmlp_ref, wout_ref, norm2_ref, wrt_ref,
         x1_ref, h2_ref, afft_ref) = refs
    else:
        (sink_ref, q_ref, k_ref, v_ref, u_ref, gv_ref, x_ref, mod_ref, ws_ref, bst_ref,
         gvn_ref, gattn_ref, gmlp_ref, wout_ref, norm2_ref, wrt_ref,
         x1_ref, h2_ref, afft_ref) = refs
    b = pl.program_id(0)
    t = pl.program_id(1)
    row = 1 + b if latent else 0
    for sub in range(MIX_SUB):
        _mix_tile(sub, t * MIX_SUB + sub, row, latent, seq_len, sink_ref, q_ref, k_ref, v_ref,
                  (ck_ref, cv_ref) if latent else None, u_ref, gv_ref, x_ref, mod_ref, ws_ref, bst_ref,
                  gvn_ref, gattn_ref, gmlp_ref, wout_ref, norm2_ref, wrt_ref, x1_ref, h2_ref, afft_ref)


def _mix_tile(sub, tile, row, latent, seq_len, sink_ref, q_ref, k_ref, v_ref, cache_refs, u_ref, gv_ref, x_ref,
              mod_ref, ws_ref, bst_ref, gvn_ref, gattn_ref, gmlp_ref, wout_ref, norm2_ref, wrt_ref,
              x1_ref, h2_ref, afft_ref):
    tq = MIX_TILE
    rows = slice(sub * tq, (sub + 1) * tq)

    q = q_ref[rows, :]
    if latent:
        ck_ref, cv_ref = cache_refs
        ks = pl.multiple_of(jnp.clip(tile * tq - WINDOW, 0, seq_len - LAT_SPAN), WINDOW)
        kcat = jnp.concatenate([ck_ref[0], k_ref[pl.ds(ks, LAT_SPAN), :]], axis=0).astype(BF16)
        vcat = jnp.concatenate([cv_ref[0], v_ref[pl.ds(ks, LAT_SPAN), :]], axis=0).astype(BF16)
        n_ctx = ck_ref.shape[1]
        n_keys = n_ctx + LAT_SPAN
        kidx = lax.broadcasted_iota(I32, (tq, n_keys), 1)
        qpos = tile * tq + lax.broadcasted_iota(I32, (tq, n_keys), 0)
        kpos = ks + kidx - n_ctx
        mask = (kidx < n_ctx) | (jnp.abs(qpos - kpos) <= WINDOW)
    else:
        kcat = k_ref[rows, :].astype(BF16)
        vcat = v_ref[rows, :].astype(BF16)
        mask = None

    heads = []
    for h in range(N_Q_HEADS):
        kv = h // Q_PER_KV
        qh = q[:, h * HEAD_DIM:(h + 1) * HEAD_DIM]
        kh = kcat[:, kv * HEAD_DIM:(kv + 1) * HEAD_DIM]
        vh = vcat[:, kv * HEAD_DIM:(kv + 1) * HEAD_DIM]
        s = lax.dot_general(qh, kh, (((1,), (1,)), ((), ())), preferred_element_type=F32) * ATTN_SCALE
        if mask is not None:
            s = jnp.where(mask, s, NEG_INF)
        sink = sink_ref[h]
        m = jnp.maximum(jnp.max(s, axis=1, keepdims=True), sink)
        p = jnp.exp(s - m)
        den = jnp.sum(p, axis=1, keepdims=True) + jnp.exp(sink - m)
        heads.append(jnp.dot(p.astype(BF16), vh, preferred_element_type=F32) / den)
    attn = jnp.concatenate(heads, axis=1)

    sg = (_rms(jax.nn.gelu(gv_ref[rows, :])) * gvn_ref[...]).astype(BF16)
    bst = bst_ref[...]
    chunks = []
    for c in range(tq // CHUNK):
        groups = []
        for g in range(N_MLP_GROUPS):
            blk = sg[c * CHUNK:(c + 1) * CHUNK, g * LANES:(g + 1) * LANES]
            groups.append(jnp.dot(ws_ref[g], blk, preferred_element_type=F32) + bst[:, g:g + 1])
        chunks.append(jnp.concatenate(groups, axis=1))
    mlp = jax.nn.gelu(u_ref[rows, :]) * jnp.concatenate(chunks, axis=0)

    mixed = jnp.concatenate([_rms(attn) * gattn_ref[...], _rms(mlp) * gmlp_ref[...]], axis=1)
    out = jnp.dot(mixed.astype(BF16), wout_ref[...], preferred_element_type=F32)
    gate1 = mod_ref[pl.ds(row, 1), 2 * D_MODEL:3 * D_MODEL]
    x1 = x_ref[rows, :] + gate1 * out
    x1_ref[rows, :] = x1

    shift2 = mod_ref[pl.ds(row, 1), 3 * D_MODEL:4 * D_MODEL]
    scale2 = mod_ref[pl.ds(row, 1), 4 * D_MODEL:5 * D_MODEL]
    h2 = _rms(x1) * norm2_ref[...] * (1 + scale2) + shift2

    hh = h2.astype(BF16)
    hl = (h2 - hh.astype(F32)).astype(BF16)
    wrt = wrt_ref[...]
    wth = wrt.astype(BF16)
    wtl = (wrt - wth.astype(F32)).astype(BF16)
    nt = (((1,), (1,)), ((), ()))
    logits_t = (lax.dot_general(wth, hh, nt, preferred_element_type=F32)
                + lax.dot_general(wth, hl, nt, preferred_element_type=F32)
                + lax.dot_general(wtl, hh, nt, preferred_element_type=F32))
    pt = jnp.exp(logits_t - jnp.max(logits_t, axis=0, keepdims=True))
    afft_ref[:, rows] = pt / jnp.sum(pt, axis=0, keepdims=True)

    h2_ref[rows, :] = hh


def _mix(sink, q, k, v, cache, u, gv, x2d, mod, w_s_bf16, b_s_t, g_v, g_attn, g_mlp, w_out_bf16, norm2,
         w_router_t, batch, seq_len):
    n = x2d.shape[0]
    latent = cache is not None
    block = MIX_SUB * MIX_TILE
    if latent:
        grid = (batch, seq_len // block)
        kv_rows = seq_len
    else:
        assert seq_len == MIX_TILE
        grid = (batch // MIX_SUB, 1)
        kv_rows = block
    tiles = grid[1]
    tok = lambda b, t: (b * tiles + t, 0)
    const2 = lambda b, t: (0, 0)

    in_specs = [pl.BlockSpec(memory_space=pltpu.SMEM),
                pl.BlockSpec((block, ATTN_WIDTH), tok),
                pl.BlockSpec((kv_rows, KV_WIDTH), lambda b, t: (b, 0)),
                pl.BlockSpec((kv_rows, KV_WIDTH), lambda b, t: (b, 0))]
    args = [sink, q, k, v]
    if latent:
        ck, cv = cache
        in_specs += [pl.BlockSpec((1,) + ck.shape[1:], lambda b, t: (b, 0, 0))] * 2
        args += [ck, cv]
    in_specs += [pl.BlockSpec((block, MLP_WIDTH), tok),
                 pl.BlockSpec((block, MLP_WIDTH), tok),
                 pl.BlockSpec((block, D_MODEL), tok),
                 pl.BlockSpec(mod.shape, const2),
                 pl.BlockSpec(w_s_bf16.shape, lambda b, t: (0, 0, 0)),
                 pl.BlockSpec(b_s_t.shape, const2),
                 pl.BlockSpec((1, MLP_WIDTH), const2),
                 pl.BlockSpec((1, ATTN_WIDTH), const2),
                 pl.BlockSpec((1, MLP_WIDTH), const2),
                 pl.BlockSpec(w_out_bf16.shape, const2),
                 pl.BlockSpec((1, D_MODEL), const2),
                 pl.BlockSpec(w_router_t.shape, const2)]
    args += [u, gv, x2d, mod, w_s_bf16, b_s_t, g_v, g_attn, g_mlp, w_out_bf16, norm2, w_router_t]
    return pl.pallas_call(
        functools.partial(_mix_body, latent=latent, seq_len=seq_len),
        grid=grid,
        in_specs=in_specs,
        out_specs=[pl.BlockSpec((block, D_MODEL), tok),
                   pl.BlockSpec((block, D_MODEL), tok),
                   pl.BlockSpec((N_EXPERTS, block), lambda b, t: (0, b * tiles + t))],
        out_shape=[jax.ShapeDtypeStruct((n, D_MODEL), F32),
                   jax.ShapeDtypeStruct((n, D_MODEL), BF16),
                   jax.ShapeDtypeStruct((N_EXPERTS, n), F32)],
        compiler_params=pltpu.CompilerParams(vmem_limit_bytes=56 * 2**20),
        name="mix_lat" if latent else "mix_ctx",
    )(*args)


def _exclusive_counts(flags, upper, lower):
    n_e, nb, _ = flags.shape
    f = flags.astype(BF16)
    local = jnp.dot(f.reshape(n_e * nb, LANES), upper, preferred_element_type=F32).reshape(n_e, nb, LANES)
    totals = jnp.broadcast_to(jnp.sum(flags, axis=2, keepdims=True), flags.shape).astype(BF16)
    before = jnp.stack([jnp.dot(lower, totals[e], preferred_element_type=F32) for e in range(n_e)], axis=0)
    return local, before


def _route_body(aff_ref, pos_ref, gate_ref, idx_ref, local_ref, start_ref, total_ref, *, cap):
    aff = aff_ref[...]
    n_e, nb, _ = aff.shape
    reduce_tokens = lambda x: jnp.sum(jnp.sum(x, axis=2, keepdims=True), axis=1, keepdims=True)

    def bisect(i, thr):
        cand = thr | jnp.left_shift(jnp.int32(1), 30 - i)
        n_ge = reduce_tokens((aff >= lax.bitcast_convert_type(cand, F32)).astype(F32))
        return jnp.where(n_ge >= cap, cand, thr)

    thr = lax.fori_loop(0, 31, bisect, jnp.zeros((n_e, 1, 1), I32))

    upper = (lax.broadcasted_iota(I32, (LANES, LANES), 0) < lax.broadcasted_iota(I32, (LANES, LANES), 1)).astype(BF16)
    lower = (lax.broadcasted_iota(I32, (nb, nb), 1) < lax.broadcasted_iota(I32, (nb, nb), 0)).astype(BF16)

    above = aff >= lax.bitcast_convert_type(thr + 1, F32)
    tied = (aff >= lax.bitcast_convert_type(thr, F32)) & jnp.logical_not(above)
    need = cap - reduce_tokens(above.astype(F32))
    tie_local, tie_before = _exclusive_counts(tied.astype(F32), upper, lower)
    sel = above | (tied & (tie_local + tie_before < need))
    sel_f = sel.astype(F32)

    local, before = _exclusive_counts(sel_f, upper, lower)
    pos_ref[...] = (local + before).astype(I32)
    gate_ref[...] = jnp.where(sel, aff, 0.0)

    local_ref[...] = (local + sel_f).astype(BF16)
    start_ref[...] = before[:, :, 0:1]
    total_ref[...] = jnp.sum(sel_f, axis=2, keepdims=True)
    slot = lax.broadcasted_iota(I32, (1, cap), 1).astype(F32)
    block_id = lax.broadcasted_iota(I32, (nb, 1), 0).astype(F32)

    def lists(e, carry):
        start = start_ref[e]
        end = start + total_ref[e]
        blk = jnp.sum((end <= slot).astype(F32), axis=0, keepdims=True)
        onehot = block_id == blk
        rank = slot - jnp.sum(jnp.where(onehot, start, 0.0), axis=0, keepdims=True)
        counts = lax.dot_general(local_ref[e], onehot.astype(BF16), (((0,), (0,)), ((), ())),
                                 preferred_element_type=F32)
        tok = jnp.sum((counts <= rank).astype(F32), axis=0, keepdims=True)
        idx_ref[pl.ds(e, 1), :] = (blk * LANES + tok).astype(I32)
        return carry

    lax.fori_loop(0, n_e, lists, 0)


def _route(aff_blocks, cap):
    n_e, nb, _ = aff_blocks.shape
    return pl.pallas_call(
        functools.partial(_route_body, cap=cap),
        out_shape=[jax.ShapeDtypeStruct(aff_blocks.shape, I32), jax.ShapeDtypeStruct(aff_blocks.shape, F32),
                   jax.ShapeDtypeStruct((n_e, cap), I32)],
        scratch_shapes=[pltpu.VMEM(aff_blocks.shape, BF16), pltpu.VMEM((n_e, nb, 1), F32),
                        pltpu.VMEM((n_e, nb, 1), F32)],
        compiler_params=pltpu.CompilerParams(vmem_limit_bytes=48 * 2**20),
        name="route",
    )(aff_blocks)


FF_TILE = 256
N_FF_TILES = EXPERT_FF // FF_TILE
OUT_TILE = 256
N_OUT_TILES = D_MODEL // OUT_TILE
EXPERT_STEPS = N_FF_TILES + N_OUT_TILES


def _expert_body(idxc_ref, idxl_ref, hc_ref, hl_ref, wg_ref, wu_ref, wd_ref, oc_ref, ol_ref,
                 gbuf, xb_ref, hid_ref, sem, *, cap_c, cap_l):
    e = pl.program_id(0)
    s = pl.program_id(1)
    cap = cap_c + cap_l
    cur = e % 2
    nxt = 1 - cur
    next_expert = (e + 1) % N_EXPERTS

    def issue(expert, to_slot, part, parts):
        for idx_ref, src, n_rows, base in ((idxc_ref, hc_ref, cap_c, 0), (idxl_ref, hl_ref, cap_l, cap_c)):
            per = n_rows // parts
            for j in range(per):
                r = part * per + j
                pltpu.make_async_copy(src.at[idx_ref[expert * n_rows + r]], gbuf.at[to_slot, base + r],
                                      sem.at[to_slot]).start()

    def arrived(slot):
        pltpu.make_async_copy(hc_ref.at[pl.ds(0, cap)], gbuf.at[slot], sem.at[slot]).wait()

    def unpack(slot, part, parts):
        per = cap // parts
        r0 = pl.multiple_of(part * per, per)
        xb_ref[slot, pl.ds(r0, per), :] = gbuf[slot, pl.ds(r0, per)].astype(F32).reshape(per, D_MODEL).astype(BF16)

    @pl.when((e == 0) & (s == 0))
    def _first():
        def gather(part, carry):
            issue(0, 0, part, N_FF_TILES)
            return carry
        lax.fori_loop(0, N_FF_TILES, gather, 0)
        arrived(0)

        def relayout(part, carry):
            unpack(0, part, N_OUT_TILES)
            return carry
        lax.fori_loop(0, N_OUT_TILES, relayout, 0)

    @pl.when(s < N_FF_TILES)
    def _up():
        issue(next_expert, nxt, s, N_FF_TILES)
        x = xb_ref[cur]
        a = jnp.dot(x, wg_ref[0].astype(BF16), preferred_element_type=F32)
        b = jnp.dot(x, wu_ref[0].astype(BF16), preferred_element_type=F32)
        hid_ref[s] = (jax.nn.silu(a) * b).astype(BF16)

    @pl.when(s == N_FF_TILES)
    def _arrived():
        arrived(nxt)

    @pl.when(s >= N_FF_TILES)
    def _down():
        unpack(nxt, s - N_FF_TILES, N_OUT_TILES)
        wd = wd_ref[0].astype(BF16)
        acc = jnp.dot(hid_ref[0], wd[0:FF_TILE, :], preferred_element_type=F32)
        for j in range(1, N_FF_TILES):
            acc = acc + jnp.dot(hid_ref[j], wd[j * FF_TILE:(j + 1) * FF_TILE, :], preferred_element_type=F32)
        oc_ref[...] = acc[0:cap_c, :]
        ol_ref[...] = acc[cap_c:, :]


def _experts(idx_c, idx_l, h2_c, h2_l, w_gate, w_up, w_down):
    cap_c = idx_c.shape[1]
    cap_l = idx_l.shape[1]
    cap = cap_c + cap_l
    idx_c = idx_c.reshape(-1)
    idx_l = idx_l.reshape(-1)
    up_idx = lambda e, s, ic, il: (e, 0, jnp.minimum(s, N_FF_TILES - 1))
    down_idx = lambda e, s, ic, il: (e, 0, jnp.maximum(s - N_FF_TILES, 0))
    out_idx = lambda e, s, ic, il: (e, jnp.maximum(s - N_FF_TILES, 0))
    return pl.pallas_call(
        functools.partial(_expert_body, cap_c=cap_c, cap_l=cap_l),
        grid_spec=pltpu.PrefetchScalarGridSpec(
            num_scalar_prefetch=2,
            grid=(N_EXPERTS, EXPERT_STEPS),
            in_specs=[pl.BlockSpec(memory_space=pl.ANY),
                      pl.BlockSpec(memory_space=pl.ANY),
                      pl.BlockSpec((1, D_MODEL, FF_TILE), up_idx),
                      pl.BlockSpec((1, D_MODEL, FF_TILE), up_idx),
                      pl.BlockSpec((1, EXPERT_FF, OUT_TILE), down_idx)],
            out_specs=[pl.BlockSpec((cap_c, OUT_TILE), out_idx),
                       pl.BlockSpec((cap_l, OUT_TILE), out_idx)],
            scratch_shapes=[pltpu.VMEM((2, cap, ROW_TILES, LANES), BF16),
                            pltpu.VMEM((2, cap, D_MODEL), BF16),
                            pltpu.VMEM((N_FF_TILES, cap, FF_TILE), BF16),
                            pltpu.SemaphoreType.DMA((2,))]),
        out_shape=[jax.ShapeDtypeStruct((N_EXPERTS * cap_c, D_MODEL), F32),
                   jax.ShapeDtypeStruct((N_EXPERTS * cap_l, D_MODEL), F32)],
        compiler_params=pltpu.CompilerParams(vmem_limit_bytes=60 * 2**20),
        name="experts",
    )(idx_c, idx_l, h2_c, h2_l, w_gate, w_up, w_down)


COMBINE_TILE = 128


SUBLANES = 8
WINDOW_ROWS_CTX = 40
WINDOW_ROWS_LAT = 72


def _split_bf16(x):
    hi = x.astype(BF16)
    return hi, (x - hi.astype(F32)).astype(BF16)


def _combine_body(bs_ref, pos_ref, gate_ref, eo_ref, x1_ref, mod_ref, fn_ref, y_ref, rows, sem, *,
                  cap, win, n_blocks, latent, tiles_per_seq):
    i = pl.program_id(0)
    tb = COMBINE_TILE
    slot = i % 2

    def window_base(e, blk):
        return (bs_ref[e, blk] // SUBLANES) * SUBLANES

    def window_start(e, blk, p):
        return jnp.minimum(window_base(e, blk) + p * win, cap - win)

    def fetch(blk, p, to_slot):
        for e in range(N_EXPERTS):
            start = pl.multiple_of(e * cap + window_start(e, blk, p), SUBLANES)
            pltpu.make_async_copy(eo_ref.at[pl.ds(start, win)],
                                  rows.at[to_slot, pl.ds(e * win, win)], sem.at[to_slot]).start()

    def wait(to_slot):
        pltpu.make_async_copy(eo_ref.at[pl.ds(0, (N_EXPERTS * win))], rows.at[to_slot], sem.at[to_slot]).wait()

    @pl.when(i == 0)
    def _first():
        fetch(0, 0, 0)

    @pl.when(i + 1 < n_blocks)
    def _prefetch():
        fetch(i + 1, 0, 1 - slot)

    lane = lax.broadcasted_iota(I32, (1, N_EXPERTS), 1)
    col = lax.broadcasted_iota(I32, (1, (N_EXPERTS * win)), 1)
    col_row = (col % win).astype(F32)
    expand = (lax.broadcasted_iota(I32, (N_EXPERTS, (N_EXPERTS * win)), 0) == col // win).astype(BF16)
    widen = lambda v: jnp.dot(v, expand, preferred_element_type=F32)

    pos = pos_ref[...]
    g_hi, g_lo = _split_bf16(gate_ref[...])
    gate_wide = widen(g_hi) + widen(g_lo)

    def one_pass(p, acc):
        base = jnp.zeros((1, N_EXPERTS), I32)
        start = jnp.zeros((1, N_EXPERTS), I32)
        for e in range(N_EXPERTS):
            base = jnp.where(lane == e, window_base(e, i), base)
            start = jnp.where(lane == e, window_start(e, i, p), start)
        rel = pos - base
        mine = (rel >= p * win) & (rel < (p + 1) * win)
        window_col = jnp.where(mine, pos - start, -1).astype(F32).astype(BF16)
        s_mat = jnp.where(widen(window_col) == col_row, gate_wide, 0.0)
        s_hi, s_lo = _split_bf16(s_mat)
        r_hi, r_lo = _split_bf16(rows[slot])
        return (acc + jnp.dot(s_hi, r_hi, preferred_element_type=F32)
                + jnp.dot(s_hi, r_lo, preferred_element_type=F32)
                + jnp.dot(s_lo, r_hi, preferred_element_type=F32))

    wait(slot)
    moe = one_pass(0, jnp.zeros((tb, D_MODEL), F32))

    n_pass = jnp.int32(1)
    for e in range(N_EXPERTS):
        span = bs_ref[e, i + 1] - window_base(e, i)
        n_pass = jnp.maximum(n_pass, (span + win - 1) // win)

    def extra(p, acc):
        fetch(i, p, slot)
        wait(slot)
        return one_pass(p, acc)

    moe = lax.fori_loop(1, n_pass, extra, moe)
    row = 1 + i // tiles_per_seq if latent else 0
    gate2 = mod_ref[pl.ds(row, 1), 5 * D_MODEL:6 * D_MODEL]
    y_ref[...] = _rms(x1_ref[...] + gate2 * moe) * fn_ref[...]


def _combine(block_start, pos_t, gate_t, expert_out, x1, mod, final_norm, cap, latent, seq_len):
    n = x1.shape[0]
    tb = COMBINE_TILE
    n_blocks = n // tb
    win = WINDOW_ROWS_LAT if latent else WINDOW_ROWS_CTX
    return pl.pallas_call(
        functools.partial(_combine_body, cap=cap, win=win, n_blocks=n_blocks, latent=latent,
                          tiles_per_seq=seq_len // tb),
        grid_spec=pltpu.PrefetchScalarGridSpec(
            num_scalar_prefetch=1,
            grid=(n_blocks,),
            in_specs=[pl.BlockSpec((tb, N_EXPERTS), lambda i, bs: (i, 0)),
                      pl.BlockSpec((tb, N_EXPERTS), lambda i, bs: (i, 0)),
                      pl.BlockSpec(memory_space=pl.ANY),
                      pl.BlockSpec((tb, D_MODEL), lambda i, bs: (i, 0)),
                      pl.BlockSpec(mod.shape, lambda i, bs: (0, 0)),
                      pl.BlockSpec((1, D_MODEL), lambda i, bs: (0, 0))],
            out_specs=pl.BlockSpec((tb, D_MODEL), lambda i, bs: (i, 0)),
            scratch_shapes=[pltpu.VMEM((2, N_EXPERTS * win, D_MODEL), F32), pltpu.SemaphoreType.DMA((2,))]),
        out_shape=jax.ShapeDtypeStruct((n, D_MODEL), F32),
        compiler_params=pltpu.CompilerParams(vmem_limit_bytes=48 * 2**20),
        name="combine_lat" if latent else "combine_ctx",
    )(block_start, pos_t, gate_t, expert_out, x1, mod, final_norm)


def _rope_tables(seq_len):
    pos = jnp.arange(seq_len)
    half = HEAD_DIM // 2
    freqs = ROPE_BASE ** (-jnp.arange(0, half, 2, dtype=F32) / half)
    ang_r = (pos // GRID_W).astype(F32)[:, None] * freqs[None, :]
    ang_c = (pos % GRID_W).astype(F32)[:, None] * freqs[None, :]
    cos = jnp.concatenate([jnp.cos(ang_r)] * 2 + [jnp.cos(ang_c)] * 2, axis=1)
    sin = jnp.concatenate([-jnp.sin(ang_r), jnp.sin(ang_r), -jnp.sin(ang_c), jnp.sin(ang_c)], axis=1)
    return cos, sin


def kernel(x_prompt, x_sample, cache_k, cache_v, c, c_ctx, w_ada, b_ada, norm1, w_in, attn_sink, w_s, b_s, g_v,
           g_attn, g_mlp, w_out, norm2, w_router, w_gate, w_up, w_down, final_norm):
    depth = w_ada.shape[0]
    assert depth == 1, "single-layer trunk"
    batch, seq, _ = x_prompt.shape
    dec_batch, dec_seq, _ = x_sample.shape
    n_ctx = batch * seq
    n_lat = dec_batch * dec_seq
    cap_c = CAPACITY_FACTOR * n_ctx // N_EXPERTS
    cap_l = CAPACITY_FACTOR * n_lat // N_EXPERTS
    l = 0

    cvec = jnp.concatenate([c_ctx[None, :], c, jnp.zeros((MOD_ROWS - 1 - dec_batch, D_MODEL), F32)], axis=0)
    mod = _adaln(cvec, w_ada[l], b_ada[l][None, :])

    w_in_b = w_in[l].astype(BF16)
    w_out_b = w_out[l].astype(BF16)
    w_s_b = w_s[l].astype(BF16)
    b_s_t = jnp.transpose(b_s[l])
    w_router_t = jnp.transpose(w_router[l])
    row = lambda a: a.reshape(1, -1)
    shared = (mod, w_s_b, b_s_t, row(g_v[l]), row(g_attn[l]), row(g_mlp[l]), w_out_b, row(norm2[l]), w_router_t)

    xp = x_prompt.reshape(n_ctx, D_MODEL)
    xl = x_sample.reshape(n_lat, D_MODEL)

    q, k, v, u, gv = _inproj(xp, mod, row(norm1[l]), w_in_b, None, seq)
    x1_c, h2_c, afft_c = _mix(attn_sink[l], q, k, v, None, u, gv, xp, *shared, batch, seq)
    state_k = k.reshape(batch, 1, seq, N_KV_HEADS, HEAD_DIM)
    state_v = v.reshape(batch, 1, seq, N_KV_HEADS, HEAD_DIM)

    q, k, v, u, gv = _inproj(xl, mod, row(norm1[l]), w_in_b, _rope_tables(dec_seq), dec_seq)
    cache = (cache_k[:, l].reshape(dec_batch, -1, KV_WIDTH), cache_v[:, l].reshape(dec_batch, -1, KV_WIDTH))
    x1_l, h2_l, afft_l = _mix(attn_sink[l], q, k, v, cache, u, gv, xl, *shared, dec_batch, dec_seq)

    def route(aff_t, cap):
        n = aff_t.shape[1]
        pos, gate, idx = _route(aff_t.reshape(N_EXPERTS, n // LANES, LANES), cap)
        return pos.reshape(N_EXPERTS, n), gate.reshape(N_EXPERTS, n), idx

    pos_c, gate_c, idx_c = route(afft_c, cap_c)
    pos_l, gate_l, idx_l = route(afft_l, cap_l)
    slabs = lambda h: h.reshape(h.shape[0], ROW_TILES, LANES)
    eo_c, eo_l = _experts(idx_c, idx_l, slabs(h2_c), slabs(h2_l), w_gate[l], w_up[l], w_down[l])
    fn = row(final_norm)

    def block_starts(pos, cap):
        return jnp.concatenate([pos[:, ::COMBINE_TILE], jnp.full((N_EXPERTS, 1), cap, I32)], axis=1)

    y_c = _combine(block_starts(pos_c, cap_c), jnp.transpose(pos_c), jnp.transpose(gate_c), eo_c, x1_c, mod, fn,
                   cap_c, False, seq)
    y_l = _combine(block_starts(pos_l, cap_l), jnp.transpose(pos_l), jnp.transpose(gate_l), eo_l, x1_l, mod, fn,
                   cap_l, True, dec_seq)

    return (y_c.reshape(batch, seq, D_MODEL), y_l.reshape(dec_batch, dec_seq, D_MODEL), state_k, state_v)
```

```python
import functools

import jax
import jax.numpy as jnp
from jax import lax
from jax.experimental import pallas as pl
from jax.experimental.pallas import tpu as pltpu

F32 = jnp.float32
BF16 = jnp.bfloat16
I32 = jnp.int32

D_MODEL = 2048
GRID_W = 64
HEAD_DIM = 128
N_Q_HEADS = 8
N_KV_HEADS = 2
Q_PER_KV = N_Q_HEADS // N_KV_HEADS
ATTN_WIDTH = N_Q_HEADS * HEAD_DIM
KV_WIDTH = N_KV_HEADS * HEAD_DIM
N_MLP_GROUPS = 8
MLP_WIDTH = 1024
Q_END = ATTN_WIDTH
K_END = Q_END + KV_WIDTH
V_END = K_END + KV_WIDTH
U_END = V_END + MLP_WIDTH
IN_WIDTH = U_END + MLP_WIDTH
CHUNK = 128
WINDOW = 128
N_EXPERTS = 16
EXPERT_FF = 2048
CAPACITY_FACTOR = 2
ROPE_BASE = 10000.0
EPS = 1e-6
NEG_INF = -1e30
ATTN_SCALE = HEAD_DIM ** -0.5

LANES = 128
MIX_TILE = 256
MIX_SUB = 2
LAT_SPAN = MIX_TILE + 2 * WINDOW
ROW_TILES = D_MODEL // LANES
MOD_ROWS = 8


def _rms(x):
    return x * lax.rsqrt(jnp.mean(x * x, axis=-1, keepdims=True) + EPS)


def _split_bf16_f32(x):
    hi = x.astype(BF16)
    return hi, x - hi.astype(F32)


def _split_bf16(x):
    hi, rest = _split_bf16_f32(x)
    return hi, rest.astype(BF16)


def _adaln_body(c_ref, w_ref, b_ref, o_ref):
    s = jax.nn.silu(c_ref[...])
    o_ref[...] = jnp.dot(s.astype(BF16), w_ref[...].astype(BF16),
                         preferred_element_type=F32) + b_ref[...]


def _adaln(cvec, w_ada, b_ada):
    tn = 1024
    width = w_ada.shape[1]
    return pl.pallas_call(
        _adaln_body,
        grid=(width // tn,),
        in_specs=[pl.BlockSpec((MOD_ROWS, D_MODEL), lambda j: (0, 0)),
                  pl.BlockSpec((D_MODEL, tn), lambda j: (0, j)),
                  pl.BlockSpec((1, tn), lambda j: (0, j))],
        out_specs=pl.BlockSpec((MOD_ROWS, tn), lambda j: (0, j)),
        out_shape=jax.ShapeDtypeStruct((MOD_ROWS, width), F32),
        compiler_params=pltpu.CompilerParams(vmem_limit_bytes=40 * 2**20),
        name="adaln",
    )(cvec, w_ada, b_ada)


def _rope(x, cos, sin, n_heads):
    lane = lax.broadcasted_iota(I32, (x.shape[0], HEAD_DIM), 1)
    first = (lane & 32) == 0
    outs = []
    for h in range(n_heads):
        xh = x[:, h * HEAD_DIM:(h + 1) * HEAD_DIM]
        partner = jnp.where(first, pltpu.roll(xh, 96, 1), pltpu.roll(xh, 32, 1))
        outs.append(xh * cos + partner * sin)
    return jnp.concatenate(outs, axis=1)


def _inproj_body(*refs, latent, tiles_per_seq):
    if latent:
        x_ref, mod_ref, g_ref, w_ref, cos_ref, sin_ref, q_ref, k_ref, v_ref, u_ref, gv_ref = refs
    else:
        x_ref, mod_ref, g_ref, w_ref, q_ref, k_ref, v_ref, u_ref, gv_ref = refs
    i = pl.program_id(0)
    row = 1 + i // tiles_per_seq if latent else 0
    shift = mod_ref[pl.ds(row, 1), 0:D_MODEL]
    scale = mod_ref[pl.ds(row, 1), D_MODEL:2 * D_MODEL]
    h = _rms(x_ref[...]) * g_ref[...] * (1 + scale) + shift
    z = jnp.dot(h.astype(BF16), w_ref[...], preferred_element_type=F32)
    q = z[:, 0:Q_END]
    k = z[:, Q_END:K_END]
    if latent:
        cos = cos_ref[...]
        sin = sin_ref[...]
        q = _rope(q, cos, sin, N_Q_HEADS)
        k = _rope(k, cos, sin, N_KV_HEADS)
    q_ref[...] = q.astype(BF16)
    k_ref[...] = k
    v_ref[...] = z[:, K_END:V_END]
    u_ref[...] = z[:, V_END:U_END]
    gv_ref[...] = z[:, U_END:IN_WIDTH]


def _inproj(x2d, mod, norm1, w_in_bf16, rope_tables, seq_len):
    n = x2d.shape[0]
    tm = 256
    latent = rope_tables is not None
    tiles_per_seq = seq_len // tm
    in_specs = [pl.BlockSpec((tm, D_MODEL), lambda i: (i, 0)),
                pl.BlockSpec(mod.shape, lambda i: (0, 0)),
                pl.BlockSpec((1, D_MODEL), lambda i: (0, 0)),
                pl.BlockSpec((D_MODEL, IN_WIDTH), lambda i: (0, 0))]
    args = [x2d, mod, norm1, w_in_bf16]
    if latent:
        in_specs += [pl.BlockSpec((tm, HEAD_DIM), lambda i: (i % tiles_per_seq, 0))] * 2
        args += list(rope_tables)
    widths = (ATTN_WIDTH, KV_WIDTH, KV_WIDTH, MLP_WIDTH, MLP_WIDTH)
    dtypes = (BF16, F32, F32, F32, F32)
    return pl.pallas_call(
        functools.partial(_inproj_body, latent=latent, tiles_per_seq=tiles_per_seq),
        grid=(n // tm,),
        in_specs=in_specs,
        out_specs=[pl.BlockSpec((tm, w), lambda i: (i, 0)) for w in widths],
        out_shape=[jax.ShapeDtypeStruct((n, w), dt) for w, dt in zip(widths, dtypes)],
        compiler_params=pltpu.CompilerParams(vmem_limit_bytes=56 * 2**20),
        name="inproj_lat" if latent else "inproj_ctx",
    )(*args)


def _mix_body(*refs, latent, seq_len):
    if latent:
        (sink_ref, q_ref, k_ref, v_ref, ck_ref, cv_ref, u_ref, gv_ref, x_ref, mod_ref, ws_ref, bst_ref,
         gvn_ref, gattn_ref, gmlp_ref, wout_ref, norm2_ref, wrt_ref,
         x1_ref, h2_ref, afft_ref) = refs
    else:
        (sink_ref, q_ref, k_ref, v_ref, u_ref, gv_ref, x_ref, mod_ref, ws_ref, bst_ref,
         gvn_ref, gattn_ref, gmlp_ref, wout_ref, norm2_ref, wrt_ref,
         x1_ref, h2_ref, afft_ref) = refs
    b = pl.program_id(0)
    t = pl.program_id(1)
    row = 1 + b if latent else 0
    for sub in range(MIX_SUB):
        _mix_tile(sub, t * MIX_SUB + sub, row, latent, seq_len, sink_ref, q_ref, k_ref, v_ref,
                  (ck_ref, cv_ref) if latent else None, u_ref, gv_ref, x_ref, mod_ref, ws_ref, bst_ref,
                  gvn_ref, gattn_ref, gmlp_ref, wout_ref, norm2_ref, wrt_ref, x1_ref, h2_ref, afft_ref)


def _mix_tile(sub, tile, row, latent, seq_len, sink_ref, q_ref, k_ref, v_ref, cache_refs, u_ref, gv_ref, x_ref,
              mod_ref, ws_ref, bst_ref, gvn_ref, gattn_ref, gmlp_ref, wout_ref, norm2_ref, wrt_ref,
              x1_ref, h2_ref, afft_ref):
    tq = MIX_TILE
    rows = slice(sub * tq, (sub + 1) * tq)

    q = q_ref[rows, :]
    if latent:
        ck_ref, cv_ref = cache_refs
        ks = pl.multiple_of(jnp.clip(tile * tq - WINDOW, 0, seq_len - LAT_SPAN), WINDOW)
        kcat = jnp.concatenate([ck_ref[0], k_ref[pl.ds(ks, LAT_SPAN), :]], axis=0).astype(BF16)
        vcat = jnp.concatenate([cv_ref[0], v_ref[pl.ds(ks, LAT_SPAN), :]], axis=0).astype(BF16)
        n_ctx = ck_ref.shape[1]
        n_keys = n_ctx + LAT_SPAN
        kidx = lax.broadcasted_iota(I32, (tq, n_keys), 1)
        qpos = tile * tq + lax.broadcasted_iota(I32, (tq, n_keys), 0)
        kpos = ks + kidx - n_ctx
        mask = (kidx < n_ctx) | (jnp.abs(qpos - kpos) <= WINDOW)
    else:
        kcat = k_ref[rows, :].astype(BF16)
        vcat = v_ref[rows, :].astype(BF16)
        mask = None

    heads = []
    for h in range(N_Q_HEADS):
        kv = h // Q_PER_KV
        qh = q[:, h * HEAD_DIM:(h + 1) * HEAD_DIM]
        kh = kcat[:, kv * HEAD_DIM:(kv + 1) * HEAD_DIM]
        vh = vcat[:, kv * HEAD_DIM:(kv + 1) * HEAD_DIM]
        s = lax.dot_general(qh, kh, (((1,), (1,)), ((), ())), preferred_element_type=F32) * ATTN_SCALE
        if mask is not None:
            s = jnp.where(mask, s, NEG_INF)
        sink = sink_ref[h]
        m = jnp.maximum(jnp.max(s, axis=1, keepdims=True), sink)
        p = jnp.exp(s - m)
        den = jnp.sum(p, axis=1, keepdims=True) + jnp.exp(sink - m)
        heads.append(jnp.dot(p.astype(BF16), vh, preferred_element_type=F32) / den)
    attn = jnp.concatenate(heads, axis=1)

    sg = (_rms(jax.nn.gelu(gv_ref[rows, :])) * gvn_ref[...]).astype(BF16)
    bst = bst_ref[...]
    chunks = []
    for c in range(tq // CHUNK):
        groups = []
        for g in range(N_MLP_GROUPS):
            blk = sg[c * CHUNK:(c + 1) * CHUNK, g * LANES:(g + 1) * LANES]
            groups.append(jnp.dot(ws_ref[g], blk, preferred_element_type=F32) + bst[:, g:g + 1])
        chunks.append(jnp.concatenate(groups, axis=1))
    mlp = jax.nn.gelu(u_ref[rows, :]) * jnp.concatenate(chunks, axis=0)

    mixed = jnp.concatenate([_rms(attn) * gattn_ref[...], _rms(mlp) * gmlp_ref[...]], axis=1)
    out = jnp.dot(mixed.astype(BF16), wout_ref[...], preferred_element_type=F32)
    gate1 = mod_ref[pl.ds(row, 1), 2 * D_MODEL:3 * D_MODEL]
    x1 = x_ref[rows, :] + gate1 * out
    x1_ref[rows, :] = x1

    shift2 = mod_ref[pl.ds(row, 1), 3 * D_MODEL:4 * D_MODEL]
    scale2 = mod_ref[pl.ds(row, 1), 4 * D_MODEL:5 * D_MODEL]
    h2 = _rms(x1) * norm2_ref[...] * (1 + scale2) + shift2

    hh = h2.astype(BF16)
    hl = (h2 - hh.astype(F32)).astype(BF16)
    wrt = wrt_ref[...]
    wth = wrt.astype(BF16)
    wtl = (wrt - wth.astype(F32)).astype(BF16)
    nt = (((1,), (1,)), ((), ()))
    logits_t = (lax.dot_general(wth, hh, nt, preferred_element_type=F32)
                + lax.dot_general(wth, hl, nt, preferred_element_type=F32)
                + lax.dot_general(wtl, hh, nt, preferred_element_type=F32))
    pt = jnp.exp(logits_t - jnp.max(logits_t, axis=0, keepdims=True))
    afft_ref[:, rows] = pt / jnp.sum(pt, axis=0, keepdims=True)

    h2_ref[rows, :] = hh


def _mix(sink, q, k, v, cache, u, gv, x2d, mod, w_s_bf16, b_s_t, g_v, g_attn, g_mlp, w_out_bf16, norm2,
         w_router_t, batch, seq_len):
    n = x2d.shape[0]
    latent = cache is not None
    block = MIX_SUB * MIX_TILE
    if latent:
        grid = (batch, seq_len // block)
        kv_rows = seq_len
    else:
        assert seq_len == MIX_TILE
        grid = (batch // MIX_SUB, 1)
        kv_rows = block
    tiles = grid[1]
    tok = lambda b, t: (b * tiles + t, 0)
    const2 = lambda b, t: (0, 0)

    in_specs = [pl.BlockSpec(memory_space=pltpu.SMEM),
                pl.BlockSpec((block, ATTN_WIDTH), tok),
                pl.BlockSpec((kv_rows, KV_WIDTH), lambda b, t: (b, 0)),
                pl.BlockSpec((kv_rows, KV_WIDTH), lambda b, t: (b, 0))]
    args = [sink, q, k, v]
    if latent:
        ck, cv = cache
        in_specs += [pl.BlockSpec((1,) + ck.shape[1:], lambda b, t: (b, 0, 0))] * 2
        args += [ck, cv]
    in_specs += [pl.BlockSpec((block, MLP_WIDTH), tok),
                 pl.BlockSpec((block, MLP_WIDTH), tok),
                 pl.BlockSpec((block, D_MODEL), tok),
                 pl.BlockSpec(mod.shape, const2),
                 pl.BlockSpec(w_s_bf16.shape, lambda b, t: (0, 0, 0)),
                 pl.BlockSpec(b_s_t.shape, const2),
                 pl.BlockSpec((1, MLP_WIDTH), const2),
                 pl.BlockSpec((1, ATTN_WIDTH), const2),
                 pl.BlockSpec((1, MLP_WIDTH), const2),
                 pl.BlockSpec(w_out_bf16.shape, const2),
                 pl.BlockSpec((1, D_MODEL), const2),
                 pl.BlockSpec(w_router_t.shape, const2)]
    args += [u, gv, x2d, mod, w_s_bf16, b_s_t, g_v, g_attn, g_mlp, w_out_bf16, norm2, w_router_t]
    return pl.pallas_call(
        functools.partial(_mix_body, latent=latent, seq_len=seq_len),
        grid=grid,
        in_specs=in_specs,
        out_specs=[pl.BlockSpec((block, D_MODEL), tok),
                   pl.BlockSpec((block, D_MODEL), tok),
                   pl.BlockSpec((N_EXPERTS, block), lambda b, t: (0, b * tiles + t))],
        out_shape=[jax.ShapeDtypeStruct((n, D_MODEL), F32),
                   jax.ShapeDtypeStruct((n, D_MODEL), BF16),
                   jax.ShapeDtypeStruct((N_EXPERTS, n), F32)],
        compiler_params=pltpu.CompilerParams(vmem_limit_bytes=56 * 2**20),
        name="mix_lat" if latent else "mix_ctx",
    )(*args)


def _exclusive_counts(flags, upper, lower):
    n_e, nb, _ = flags.shape
    f = flags.astype(BF16)
    local = jnp.dot(f.reshape(n_e * nb, LANES), upper, preferred_element_type=F32).reshape(n_e, nb, LANES)
    totals = jnp.broadcast_to(jnp.sum(flags, axis=2, keepdims=True), flags.shape).astype(BF16)
    before = jnp.stack([jnp.dot(lower, totals[e], preferred_element_type=F32) for e in range(n_e)], axis=0)
    return local, before


def _route_body(aff_ref, pos_ref, idx_ref, sgate_ref, start_ref, local_ref, gate_ref, total_ref, *, cap):
    aff = aff_ref[...]
    n_e, nb, _ = aff.shape
    reduce_tokens = lambda x: jnp.sum(jnp.sum(x, axis=2, keepdims=True), axis=1, keepdims=True)

    def bisect(i, thr):
        cand = thr | jnp.left_shift(jnp.int32(1), 30 - i)
        n_ge = reduce_tokens((aff >= lax.bitcast_convert_type(cand, F32)).astype(F32))
        return jnp.where(n_ge >= cap, cand, thr)

    thr = lax.fori_loop(0, 31, bisect, jnp.zeros((n_e, 1, 1), I32))

    upper = (lax.broadcasted_iota(I32, (LANES, LANES), 0) < lax.broadcasted_iota(I32, (LANES, LANES), 1)).astype(BF16)
    lower = (lax.broadcasted_iota(I32, (nb, nb), 1) < lax.broadcasted_iota(I32, (nb, nb), 0)).astype(BF16)

    above = aff >= lax.bitcast_convert_type(thr + 1, F32)
    tied = (aff >= lax.bitcast_convert_type(thr, F32)) & jnp.logical_not(above)
    need = cap - reduce_tokens(above.astype(F32))
    tie_local, tie_before = _exclusive_counts(tied.astype(F32), upper, lower)
    sel = above | (tied & (tie_local + tie_before < need))
    sel_f = sel.astype(F32)

    local, before = _exclusive_counts(sel_f, upper, lower)
    pos_ref[...] = jnp.where(sel, (local + before).astype(I32), -1)
    gate_ref[...] = aff

    local_ref[...] = (local + sel_f).astype(BF16)
    start_ref[...] = before[:, :, 0:1]
    total_ref[...] = jnp.sum(sel_f, axis=2, keepdims=True)
    slot = lax.broadcasted_iota(I32, (1, cap), 1).astype(F32)
    block_id = lax.broadcasted_iota(I32, (nb, 1), 0).astype(F32)
    lane_id = lax.broadcasted_iota(I32, (LANES, 1), 0).astype(F32)
    tn = (((0,), (0,)), ((), ()))

    def lists(e, carry):
        start = start_ref[e]
        end = start + total_ref[e]
        blk = jnp.sum((end <= slot).astype(F32), axis=0, keepdims=True)
        onehot = block_id == blk
        onehot_b = onehot.astype(BF16)
        rank = slot - jnp.sum(jnp.where(onehot, start, 0.0), axis=0, keepdims=True)
        counts = lax.dot_general(local_ref[e], onehot_b, tn, preferred_element_type=F32)
        tok = jnp.sum((counts <= rank).astype(F32), axis=0, keepdims=True)
        idx_ref[pl.ds(e, 1), :] = (blk * LANES + tok).astype(I32)
        g1, rest = _split_bf16_f32(gate_ref[e])
        g2, rest = _split_bf16_f32(rest)
        block_gates = sum(lax.dot_general(g, onehot_b, tn, preferred_element_type=F32)
                          for g in (g1, g2, rest.astype(BF16)))
        sgate_ref[pl.ds(e, 1), :] = jnp.sum(jnp.where(lane_id == tok, block_gates, 0.0), axis=0, keepdims=True)
        return carry

    lax.fori_loop(0, n_e, lists, 0)


def _route(aff_blocks, cap):
    n_e, nb, _ = aff_blocks.shape
    return pl.pallas_call(
        functools.partial(_route_body, cap=cap),
        out_shape=[jax.ShapeDtypeStruct(aff_blocks.shape, I32), jax.ShapeDtypeStruct((n_e, cap), I32),
                   jax.ShapeDtypeStruct((n_e, cap), F32), jax.ShapeDtypeStruct((n_e, nb, 1), F32)],
        scratch_shapes=[pltpu.VMEM(aff_blocks.shape, BF16), pltpu.VMEM(aff_blocks.shape, F32),
                        pltpu.VMEM((n_e, nb, 1), F32)],
        compiler_params=pltpu.CompilerParams(vmem_limit_bytes=48 * 2**20),
        name="route",
    )(aff_blocks)


FF_TILE = 256
N_FF_TILES = EXPERT_FF // FF_TILE
OUT_TILE = 256
N_OUT_TILES = D_MODEL // OUT_TILE
EXPERT_STEPS = N_FF_TILES + N_OUT_TILES


def _expert_body(idxc_ref, idxl_ref, hc_ref, hl_ref, gc_ref, gl_ref, wg_ref, wu_ref, wd_ref,
                 och_ref, ocl_ref, olh_ref, oll_ref, gbuf, xb_ref, hid_ref, sem, *, cap_c, cap_l):
    e = pl.program_id(0)
    s = pl.program_id(1)
    cap = cap_c + cap_l
    cur = e % 2
    nxt = 1 - cur
    next_expert = (e + 1) % N_EXPERTS

    def issue(expert, part, parts):
        for idx_ref, src, n_rows, base in ((idxc_ref, hc_ref, cap_c, 0), (idxl_ref, hl_ref, cap_l, cap_c)):
            per = n_rows // parts
            for j in range(per):
                r = part * per + j
                pltpu.make_async_copy(src.at[idx_ref[expert * n_rows + r]], gbuf.at[base + r], sem).start()

    def arrived():
        pltpu.make_async_copy(hc_ref.at[pl.ds(0, cap)], gbuf, sem).wait()

    def unpack(slot, part, parts):
        per = cap // parts
        r0 = pl.multiple_of(part * per, per)
        xb_ref[slot, pl.ds(r0, per), :] = gbuf[pl.ds(r0, per)].astype(F32).reshape(per, D_MODEL).astype(BF16)

    @pl.when((e == 0) & (s == 0))
    def _first():
        def gather(part, carry):
            issue(0, part, N_FF_TILES)
            return carry
        lax.fori_loop(0, N_FF_TILES, gather, 0)
        arrived()

        def relayout(part, carry):
            unpack(0, part, N_OUT_TILES)
            return carry
        lax.fori_loop(0, N_OUT_TILES, relayout, 0)

    @pl.when(s < N_FF_TILES)
    def _up():
        issue(next_expert, s, N_FF_TILES)
        x = xb_ref[cur]
        a = jnp.dot(x, wg_ref[0].astype(BF16), preferred_element_type=F32)
        b = jnp.dot(x, wu_ref[0].astype(BF16), preferred_element_type=F32)
        hid_ref[s] = (jax.nn.silu(a) * b).astype(BF16)

    @pl.when(s == N_FF_TILES)
    def _arrived():
        arrived()

    @pl.when(s >= N_FF_TILES)
    def _down():
        unpack(nxt, s - N_FF_TILES, N_OUT_TILES)
        wd = wd_ref[0].astype(BF16)
        acc = jnp.dot(hid_ref[0], wd[0:FF_TILE, :], preferred_element_type=F32)
        for j in range(1, N_FF_TILES):
            acc = acc + jnp.dot(hid_ref[j], wd[j * FF_TILE:(j + 1) * FF_TILE, :], preferred_element_type=F32)
        for g_ref, hi_ref, lo_ref, rows in ((gc_ref, och_ref, ocl_ref, slice(0, cap_c)),
                                            (gl_ref, olh_ref, oll_ref, slice(cap_c, cap))):
            gates = g_ref[...]
            lane = lax.broadcasted_iota(I32, gates.shape, 1)
            gate = jnp.sum(jnp.where(lane == e, gates, 0.0), axis=1, keepdims=True)
            hi_ref[...], lo_ref[...] = _split_bf16(acc[rows, :] * gate)


def _experts(idx_c, idx_l, h2_c, h2_l, sgate_c, sgate_l, w_gate, w_up, w_down):
    cap_c = idx_c.shape[1]
    cap_l = idx_l.shape[1]
    cap = cap_c + cap_l
    idx_c = idx_c.reshape(-1)
    idx_l = idx_l.reshape(-1)
    const = lambda e, s, ic, il: (0, 0)
    up_idx = lambda e, s, ic, il: (e, 0, jnp.minimum(s, N_FF_TILES - 1))
    down_idx = lambda e, s, ic, il: (e, 0, jnp.maximum(s - N_FF_TILES, 0))
    out_idx = lambda e, s, ic, il: (e, jnp.maximum(s - N_FF_TILES, 0))
    out_c = jax.ShapeDtypeStruct((N_EXPERTS * cap_c, D_MODEL), BF16)
    out_l = jax.ShapeDtypeStruct((N_EXPERTS * cap_l, D_MODEL), BF16)
    return pl.pallas_call(
        functools.partial(_expert_body, cap_c=cap_c, cap_l=cap_l),
        grid_spec=pltpu.PrefetchScalarGridSpec(
            num_scalar_prefetch=2,
            grid=(N_EXPERTS, EXPERT_STEPS),
            in_specs=[pl.BlockSpec(memory_space=pl.ANY),
                      pl.BlockSpec(memory_space=pl.ANY),
                      pl.BlockSpec(sgate_c.shape, const),
                      pl.BlockSpec(sgate_l.shape, const),
                      pl.BlockSpec((1, D_MODEL, FF_TILE), up_idx),
                      pl.BlockSpec((1, D_MODEL, FF_TILE), up_idx),
                      pl.BlockSpec((1, EXPERT_FF, OUT_TILE), down_idx)],
            out_specs=[pl.BlockSpec((cap_c, OUT_TILE), out_idx), pl.BlockSpec((cap_c, OUT_TILE), out_idx),
                       pl.BlockSpec((cap_l, OUT_TILE), out_idx), pl.BlockSpec((cap_l, OUT_TILE), out_idx)],
            scratch_shapes=[pltpu.VMEM((cap, ROW_TILES, LANES), BF16),
                            pltpu.VMEM((2, cap, D_MODEL), BF16),
                            pltpu.VMEM((N_FF_TILES, cap, FF_TILE), BF16),
                            pltpu.SemaphoreType.DMA(())]),
        out_shape=[out_c, out_c, out_l, out_l],
        compiler_params=pltpu.CompilerParams(vmem_limit_bytes=60 * 2**20),
        name="experts",
    )(idx_c, idx_l, h2_c, h2_l, sgate_c, sgate_l, w_gate, w_up, w_down)


COMBINE_TILE = 128


BF16_ROWS = 16
WINDOW_ROWS_CTX = 48
WINDOW_ROWS_LAT = 80


def _combine_body(bs_ref, pos_ref, hi_ref, lo_ref, x1_ref, mod_ref, fn_ref, y_ref, rows_hi, rows_lo, sem, *,
                  cap, win, n_blocks, latent, tiles_per_seq):
    i = pl.program_id(0)
    tb = COMBINE_TILE
    slot = i % 2

    def window_base(e, blk):
        return (bs_ref[e, blk] // BF16_ROWS) * BF16_ROWS

    def window_start(e, blk, p):
        return jnp.minimum(window_base(e, blk) + p * win, cap - win)

    def fetch(blk, p, to_slot):
        for e in range(N_EXPERTS):
            start = pl.multiple_of(e * cap + window_start(e, blk, p), BF16_ROWS)
            for src, dst in ((hi_ref, rows_hi), (lo_ref, rows_lo)):
                pltpu.make_async_copy(src.at[pl.ds(start, win)], dst.at[to_slot, pl.ds(e * win, win)],
                                      sem.at[to_slot]).start()

    def wait(to_slot):
        for src, dst in ((hi_ref, rows_hi), (lo_ref, rows_lo)):
            pltpu.make_async_copy(src.at[pl.ds(0, N_EXPERTS * win)], dst.at[to_slot], sem.at[to_slot]).wait()

    @pl.when(i == 0)
    def _first():
        fetch(0, 0, 0)

    @pl.when(i + 1 < n_blocks)
    def _prefetch():
        fetch(i + 1, 0, 1 - slot)

    lane = lax.broadcasted_iota(I32, (1, N_EXPERTS), 1)
    col = lax.broadcasted_iota(I32, (1, (N_EXPERTS * win)), 1)
    col_row = (col % win).astype(F32)
    expand = (lax.broadcasted_iota(I32, (N_EXPERTS, (N_EXPERTS * win)), 0) == col // win).astype(BF16)
    widen = lambda v: jnp.dot(v, expand, preferred_element_type=F32)

    pos = pos_ref[...]

    def one_pass(p, acc):
        base = jnp.zeros((1, N_EXPERTS), I32)
        start = jnp.zeros((1, N_EXPERTS), I32)
        for e in range(N_EXPERTS):
            base = jnp.where(lane == e, window_base(e, i), base)
            start = jnp.where(lane == e, window_start(e, i, p), start)
        rel = pos - base
        mine = (pos >= 0) & (rel >= p * win) & (rel < (p + 1) * win)
        window_col = jnp.where(mine, pos - start, -1).astype(F32).astype(BF16)
        select = (widen(window_col) == col_row).astype(BF16)
        return (acc + jnp.dot(select, rows_hi[slot], preferred_element_type=F32)
                + jnp.dot(select, rows_lo[slot], preferred_element_type=F32))

    wait(slot)
    moe = one_pass(0, jnp.zeros((tb, D_MODEL), F32))

    n_pass = jnp.int32(1)
    for e in range(N_EXPERTS):
        span = bs_ref[e, i + 1] - window_base(e, i)
        n_pass = jnp.maximum(n_pass, (span + win - 1) // win)

    def extra(p, acc):
        fetch(i, p, slot)
        wait(slot)
        return one_pass(p, acc)

    moe = lax.fori_loop(1, n_pass, extra, moe)
    row = 1 + i // tiles_per_seq if latent else 0
    gate2 = mod_ref[pl.ds(row, 1), 5 * D_MODEL:6 * D_MODEL]
    y_ref[...] = _rms(x1_ref[...] + gate2 * moe) * fn_ref[...]


def _combine(block_start, pos_t, out_hi, out_lo, x1, mod, final_norm, cap, latent, seq_len):
    n = x1.shape[0]
    tb = COMBINE_TILE
    n_blocks = n // tb
    win = WINDOW_ROWS_LAT if latent else WINDOW_ROWS_CTX
    return pl.pallas_call(
        functools.partial(_combine_body, cap=cap, win=win, n_blocks=n_blocks, latent=latent,
                          tiles_per_seq=seq_len // tb),
        grid_spec=pltpu.PrefetchScalarGridSpec(
            num_scalar_prefetch=1,
            grid=(n_blocks,),
            in_specs=[pl.BlockSpec((tb, N_EXPERTS), lambda i, bs: (i, 0)),
                      pl.BlockSpec(memory_space=pl.ANY),
                      pl.BlockSpec(memory_space=pl.ANY),
                      pl.BlockSpec((tb, D_MODEL), lambda i, bs: (i, 0)),
                      pl.BlockSpec(mod.shape, lambda i, bs: (0, 0)),
                      pl.BlockSpec((1, D_MODEL), lambda i, bs: (0, 0))],
            out_specs=pl.BlockSpec((tb, D_MODEL), lambda i, bs: (i, 0)),
            scratch_shapes=[pltpu.VMEM((2, N_EXPERTS * win, D_MODEL), BF16),
                            pltpu.VMEM((2, N_EXPERTS * win, D_MODEL), BF16),
                            pltpu.SemaphoreType.DMA((2,))]),
        out_shape=jax.ShapeDtypeStruct((n, D_MODEL), F32),
        compiler_params=pltpu.CompilerParams(vmem_limit_bytes=48 * 2**20),
        name="combine_lat" if latent else "combine_ctx",
    )(block_start, pos_t, out_hi, out_lo, x1, mod, final_norm)


def _rope_tables(seq_len):
    pos = jnp.arange(seq_len)
    half = HEAD_DIM // 2
    freqs = ROPE_BASE ** (-jnp.arange(0, half, 2, dtype=F32) / half)
    ang_r = (pos // GRID_W).astype(F32)[:, None] * freqs[None, :]
    ang_c = (pos % GRID_W).astype(F32)[:, None] * freqs[None, :]
    cos = jnp.concatenate([jnp.cos(ang_r)] * 2 + [jnp.cos(ang_c)] * 2, axis=1)
    sin = jnp.concatenate([-jnp.sin(ang_r), jnp.sin(ang_r), -jnp.sin(ang_c), jnp.sin(ang_c)], axis=1)
    return cos, sin


def kernel(x_prompt, x_sample, cache_k, cache_v, c, c_ctx, w_ada, b_ada, norm1, w_in, attn_sink, w_s, b_s, g_v,
           g_attn, g_mlp, w_out, norm2, w_router, w_gate, w_up, w_down, final_norm):
    depth = w_ada.shape[0]
    assert depth == 1, "single-layer trunk"
    batch, seq, _ = x_prompt.shape
    dec_batch, dec_seq, _ = x_sample.shape
    n_ctx = batch * seq
    n_lat = dec_batch * dec_seq
    cap_c = CAPACITY_FACTOR * n_ctx // N_EXPERTS
    cap_l = CAPACITY_FACTOR * n_lat // N_EXPERTS
    l = 0

    cvec = jnp.concatenate([c_ctx[None, :], c, jnp.zeros((MOD_ROWS - 1 - dec_batch, D_MODEL), F32)], axis=0)
    mod = _adaln(cvec, w_ada[l], b_ada[l][None, :])

    w_in_b = w_in[l].astype(BF16)
    w_out_b = w_out[l].astype(BF16)
    w_s_b = w_s[l].astype(BF16)
    b_s_t = jnp.transpose(b_s[l])
    w_router_t = jnp.transpose(w_router[l])
    row = lambda a: a.reshape(1, -1)
    shared = (mod, w_s_b, b_s_t, row(g_v[l]), row(g_attn[l]), row(g_mlp[l]), w_out_b, row(norm2[l]), w_router_t)

    xp = x_prompt.reshape(n_ctx, D_MODEL)
    xl = x_sample.reshape(n_lat, D_MODEL)

    q, k, v, u, gv = _inproj(xp, mod, row(norm1[l]), w_in_b, None, seq)
    x1_c, h2_c, afft_c = _mix(attn_sink[l], q, k, v, None, u, gv, xp, *shared, batch, seq)
    state_k = k.reshape(batch, 1, seq, N_KV_HEADS, HEAD_DIM)
    state_v = v.reshape(batch, 1, seq, N_KV_HEADS, HEAD_DIM)

    q, k, v, u, gv = _inproj(xl, mod, row(norm1[l]), w_in_b, _rope_tables(dec_seq), dec_seq)
    cache = (cache_k[:, l].reshape(dec_batch, -1, KV_WIDTH), cache_v[:, l].reshape(dec_batch, -1, KV_WIDTH))
    x1_l, h2_l, afft_l = _mix(attn_sink[l], q, k, v, cache, u, gv, xl, *shared, dec_batch, dec_seq)

    def route(aff_t, cap):
        n = aff_t.shape[1]
        pos, idx, slot_gate, start = _route(aff_t.reshape(N_EXPERTS, n // LANES, LANES), cap)
        starts = jnp.concatenate([start.reshape(N_EXPERTS, -1).astype(I32), jnp.full((N_EXPERTS, 1), cap, I32)],
                                 axis=1)
        return jnp.transpose(pos.reshape(N_EXPERTS, n)), idx, jnp.transpose(slot_gate), starts

    assert COMBINE_TILE == LANES
    pos_c, idx_c, sgate_c, starts_c = route(afft_c, cap_c)
    pos_l, idx_l, sgate_l, starts_l = route(afft_l, cap_l)
    slabs = lambda h: h.reshape(h.shape[0], ROW_TILES, LANES)
    och, ocl, olh, oll = _experts(idx_c, idx_l, slabs(h2_c), slabs(h2_l), sgate_c, sgate_l,
                                  w_gate[l], w_up[l], w_down[l])
    fn = row(final_norm)
    y_c = _combine(starts_c, pos_c, och, ocl, x1_c, mod, fn, cap_c, False, seq)
    y_l = _combine(starts_l, pos_l, olh, oll, x1_l, mod, fn, cap_l, True, dec_seq)

    return (y_c.reshape(batch, seq, D_MODEL), y_l.reshape(dec_batch, dec_seq, D_MODEL), state_k, state_v)
```

```python
import functools

import jax
import jax.numpy as jnp
from jax import lax
from jax.experimental import pallas as pl
from jax.experimental.pallas import tpu as pltpu

F32 = jnp.float32
BF16 = jnp.bfloat16
I32 = jnp.int32

D_MODEL = 2048
GRID_W = 64
HEAD_DIM = 128
N_Q_HEADS = 8
N_KV_HEADS = 2
Q_PER_KV = N_Q_HEADS // N_KV_HEADS
ATTN_WIDTH = N_Q_HEADS * HEAD_DIM
KV_WIDTH = N_KV_HEADS * HEAD_DIM
N_MLP_GROUPS = 8
MLP_WIDTH = 1024
Q_END = ATTN_WIDTH
K_END = Q_END + KV_WIDTH
V_END = K_END + KV_WIDTH
U_END = V_END + MLP_WIDTH
IN_WIDTH = U_END + MLP_WIDTH
CHUNK = 128
WINDOW = 128
N_EXPERTS = 16
EXPERT_FF = 2048
CAPACITY_FACTOR = 2
ROPE_BASE = 10000.0
EPS = 1e-6
NEG_INF = -1e30
ATTN_SCALE = HEAD_DIM ** -0.5

LANES = 128
MIX_TILE = 256
MIX_SUB = 2
LAT_SPAN = MIX_TILE + 2 * WINDOW
ROW_TILES = D_MODEL // LANES
MOD_ROWS = 8


def _rms(x):
    return x * lax.rsqrt(jnp.mean(x * x, axis=-1, keepdims=True) + EPS)


def _split_bf16_f32(x):
    hi = x.astype(BF16)
    return hi, x - hi.astype(F32)


def _split_bf16(x):
    hi, rest = _split_bf16_f32(x)
    return hi, rest.astype(BF16)


def _adaln_body(c_ref, w_ref, b_ref, o_ref):
    s = jax.nn.silu(c_ref[...])
    o_ref[...] = jnp.dot(s.astype(BF16), w_ref[...].astype(BF16),
                         preferred_element_type=F32) + b_ref[...]


def _adaln(cvec, w_ada, b_ada):
    tn = 1024
    width = w_ada.shape[1]
    return pl.pallas_call(
        _adaln_body,
        grid=(width // tn,),
        in_specs=[pl.BlockSpec((MOD_ROWS, D_MODEL), lambda j: (0, 0)),
                  pl.BlockSpec((D_MODEL, tn), lambda j: (0, j)),
                  pl.BlockSpec((1, tn), lambda j: (0, j))],
        out_specs=pl.BlockSpec((MOD_ROWS, tn), lambda j: (0, j)),
        out_shape=jax.ShapeDtypeStruct((MOD_ROWS, width), F32),
        compiler_params=pltpu.CompilerParams(vmem_limit_bytes=40 * 2**20),
        name="adaln",
    )(cvec, w_ada, b_ada)


def _rope(x, cos, sin, n_heads):
    lane = lax.broadcasted_iota(I32, (x.shape[0], HEAD_DIM), 1)
    first = (lane & 32) == 0
    outs = []
    for h in range(n_heads):
        xh = x[:, h * HEAD_DIM:(h + 1) * HEAD_DIM]
        partner = jnp.where(first, pltpu.roll(xh, 96, 1), pltpu.roll(xh, 32, 1))
        outs.append(xh * cos + partner * sin)
    return jnp.concatenate(outs, axis=1)


def _inproj_body(*refs, latent, tiles_per_seq):
    if latent:
        x_ref, mod_ref, g_ref, w_ref, cos_ref, sin_ref, q_ref, k_ref, v_ref, u_ref, gv_ref = refs
    else:
        x_ref, mod_ref, g_ref, w_ref, q_ref, k_ref, v_ref, u_ref, gv_ref = refs
    i = pl.program_id(0)
    row = 1 + i // tiles_per_seq if latent else 0
    shift = mod_ref[pl.ds(row, 1), 0:D_MODEL]
    scale = mod_ref[pl.ds(row, 1), D_MODEL:2 * D_MODEL]
    h = _rms(x_ref[...]) * g_ref[...] * (1 + scale) + shift
    z = jnp.dot(h.astype(BF16), w_ref[...], preferred_element_type=F32)
    q = z[:, 0:Q_END]
    k = z[:, Q_END:K_END]
    if latent:
        cos = cos_ref[...]
        sin = sin_ref[...]
        q = _rope(q, cos, sin, N_Q_HEADS)
        k = _rope(k, cos, sin, N_KV_HEADS)
    q_ref[...] = q.astype(BF16)
    k_ref[...] = k
    v_ref[...] = z[:, K_END:V_END]
    u_ref[...] = z[:, V_END:U_END]
    gv_ref[...] = z[:, U_END:IN_WIDTH]


def _inproj(x2d, mod, norm1, w_in_bf16, rope_tables, seq_len):
    n = x2d.shape[0]
    tm = 256
    latent = rope_tables is not None
    tiles_per_seq = seq_len // tm
    in_specs = [pl.BlockSpec((tm, D_MODEL), lambda i: (i, 0)),
                pl.BlockSpec(mod.shape, lambda i: (0, 0)),
                pl.BlockSpec((1, D_MODEL), lambda i: (0, 0)),
                pl.BlockSpec((D_MODEL, IN_WIDTH), lambda i: (0, 0))]
    args = [x2d, mod, norm1, w_in_bf16]
    if latent:
        in_specs += [pl.BlockSpec((tm, HEAD_DIM), lambda i: (i % tiles_per_seq, 0))] * 2
        args += list(rope_tables)
    widths = (ATTN_WIDTH, KV_WIDTH, KV_WIDTH, MLP_WIDTH, MLP_WIDTH)
    dtypes = (BF16, F32, F32, F32, F32)
    return pl.pallas_call(
        functools.partial(_inproj_body, latent=latent, tiles_per_seq=tiles_per_seq),
        grid=(n // tm,),
        in_specs=in_specs,
        out_specs=[pl.BlockSpec((tm, w), lambda i: (i, 0)) for w in widths],
        out_shape=[jax.ShapeDtypeStruct((n, w), dt) for w, dt in zip(widths, dtypes)],
        compiler_params=pltpu.CompilerParams(vmem_limit_bytes=56 * 2**20),
        name="inproj_lat" if latent else "inproj_ctx",
    )(*args)


def _mix_body(*refs, latent, seq_len):
    if latent:
        sink_ref, q_ref, k_ref, v_ref, ck_ref, cv_ref, u_ref, gv_ref, x_ref, mod_ref, *rest = refs
        proj_refs = (q_ref, k_ref, v_ref, ck_ref, cv_ref, u_ref, gv_ref)
    else:
        sink_ref, x_ref, mod_ref, norm1_ref, win_ref, *rest = refs
        proj_refs = (norm1_ref, win_ref) + tuple(rest[-2:])
        rest = rest[:-2]
    b = pl.program_id(0)
    t = pl.program_id(1)
    row = 1 + b if latent else 0
    for sub in range(MIX_SUB):
        _mix_tile(sub, t * MIX_SUB + sub, row, latent, seq_len, sink_ref, proj_refs, x_ref, mod_ref, *rest)


def _mix_tile(sub, tile, row, latent, seq_len, sink_ref, proj_refs, x_ref,
              mod_ref, ws_ref, bst_ref, gvn_ref, gattn_ref, gmlp_ref, wout_ref, norm2_ref, wrt_ref,
              x1_ref, h2_ref, afft_ref):
    tq = MIX_TILE
    rows = slice(sub * tq, (sub + 1) * tq)

    if latent:
        q_ref, k_ref, v_ref, ck_ref, cv_ref, u_ref, gv_ref = proj_refs
        q = q_ref[rows, :]
        u = u_ref[rows, :]
        gv = gv_ref[rows, :]
        ks = pl.multiple_of(jnp.clip(tile * tq - WINDOW, 0, seq_len - LAT_SPAN), WINDOW)
        kcat = jnp.concatenate([ck_ref[0], k_ref[pl.ds(ks, LAT_SPAN), :]], axis=0).astype(BF16)
        vcat = jnp.concatenate([cv_ref[0], v_ref[pl.ds(ks, LAT_SPAN), :]], axis=0).astype(BF16)
        n_ctx = ck_ref.shape[1]
        n_keys = n_ctx + LAT_SPAN
        kidx = lax.broadcasted_iota(I32, (tq, n_keys), 1)
        qpos = tile * tq + lax.broadcasted_iota(I32, (tq, n_keys), 0)
        kpos = ks + kidx - n_ctx
        mask = (kidx < n_ctx) | (jnp.abs(qpos - kpos) <= WINDOW)
    else:
        norm1_ref, win_ref, kout_ref, vout_ref = proj_refs
        shift1 = mod_ref[pl.ds(row, 1), 0:D_MODEL]
        scale1 = mod_ref[pl.ds(row, 1), D_MODEL:2 * D_MODEL]
        h1 = _rms(x_ref[rows, :]) * norm1_ref[...] * (1 + scale1) + shift1
        z = jnp.dot(h1.astype(BF16), win_ref[...], preferred_element_type=F32)
        q = z[:, 0:Q_END].astype(BF16)
        k = z[:, Q_END:K_END]
        v = z[:, K_END:V_END]
        u = z[:, V_END:U_END]
        gv = z[:, U_END:IN_WIDTH]
        kout_ref[rows, :] = k
        vout_ref[rows, :] = v
        kcat = k.astype(BF16)
        vcat = v.astype(BF16)
        mask = None

    heads = []
    for h in range(N_Q_HEADS):
        kv = h // Q_PER_KV
        qh = q[:, h * HEAD_DIM:(h + 1) * HEAD_DIM]
        kh = kcat[:, kv * HEAD_DIM:(kv + 1) * HEAD_DIM]
        vh = vcat[:, kv * HEAD_DIM:(kv + 1) * HEAD_DIM]
        s = lax.dot_general(qh, kh, (((1,), (1,)), ((), ())), preferred_element_type=F32) * ATTN_SCALE
        if mask is not None:
            s = jnp.where(mask, s, NEG_INF)
        sink = sink_ref[h]
        m = jnp.maximum(jnp.max(s, axis=1, keepdims=True), sink)
        p = jnp.exp(s - m)
        den = jnp.sum(p, axis=1, keepdims=True) + jnp.exp(sink - m)
        heads.append(jnp.dot(p.astype(BF16), vh, preferred_element_type=F32) / den)
    attn = jnp.concatenate(heads, axis=1)

    sg = (_rms(jax.nn.gelu(gv)) * gvn_ref[...]).astype(BF16)
    bst = bst_ref[...]
    chunks = []
    for c in range(tq // CHUNK):
        groups = []
        for g in range(N_MLP_GROUPS):
            blk = sg[c * CHUNK:(c + 1) * CHUNK, g * LANES:(g + 1) * LANES]
            groups.append(jnp.dot(ws_ref[g], blk, preferred_element_type=F32) + bst[:, g:g + 1])
        chunks.append(jnp.concatenate(groups, axis=1))
    mlp = jax.nn.gelu(u) * jnp.concatenate(chunks, axis=0)

    mixed = jnp.concatenate([_rms(attn) * gattn_ref[...], _rms(mlp) * gmlp_ref[...]], axis=1)
    out = jnp.dot(mixed.astype(BF16), wout_ref[...], preferred_element_type=F32)
    gate1 = mod_ref[pl.ds(row, 1), 2 * D_MODEL:3 * D_MODEL]
    x1 = x_ref[rows, :] + gate1 * out
    x1_ref[rows, :] = x1

    shift2 = mod_ref[pl.ds(row, 1), 3 * D_MODEL:4 * D_MODEL]
    scale2 = mod_ref[pl.ds(row, 1), 4 * D_MODEL:5 * D_MODEL]
    h2 = _rms(x1) * norm2_ref[...] * (1 + scale2) + shift2

    hh = h2.astype(BF16)
    hl = (h2 - hh.astype(F32)).astype(BF16)
    wrt = wrt_ref[...]
    wth = wrt.astype(BF16)
    wtl = (wrt - wth.astype(F32)).astype(BF16)
    nt = (((1,), (1,)), ((), ()))
    logits_t = (lax.dot_general(wth, hh, nt, preferred_element_type=F32)
                + lax.dot_general(wth, hl, nt, preferred_element_type=F32)
                + lax.dot_general(wtl, hh, nt, preferred_element_type=F32))
    pt = jnp.exp(logits_t - jnp.max(logits_t, axis=0, keepdims=True))
    afft_ref[:, rows] = pt / jnp.sum(pt, axis=0, keepdims=True)

    h2_ref[rows, :] = hh


def _mix(sink, projected, x2d, mod, norm1, w_in_bf16, w_s_bf16, b_s_t, g_v, g_attn, g_mlp, w_out_bf16, norm2,
         w_router_t, batch, seq_len):
    n = x2d.shape[0]
    latent = projected is not None
    block = MIX_SUB * MIX_TILE
    const2 = lambda b, t: (0, 0)
    resident = dict(pipeline_mode=pl.Buffered(1))
    if latent:
        grid = (batch, seq_len // block)
        tiles = grid[1]
        tok = lambda b, t: (b * tiles + t, 0)
        q, k, v, ck, cv, u, gv = projected
        seq = lambda b, t: (b, 0)
        in_specs = [pl.BlockSpec(memory_space=pltpu.SMEM),
                    pl.BlockSpec((block, ATTN_WIDTH), tok),
                    pl.BlockSpec((seq_len, KV_WIDTH), seq),
                    pl.BlockSpec((seq_len, KV_WIDTH), seq),
                    pl.BlockSpec((1,) + ck.shape[1:], lambda b, t: (b, 0, 0)),
                    pl.BlockSpec((1,) + cv.shape[1:], lambda b, t: (b, 0, 0)),
                    pl.BlockSpec((block, MLP_WIDTH), tok),
                    pl.BlockSpec((block, MLP_WIDTH), tok),
                    pl.BlockSpec((block, D_MODEL), tok),
                    pl.BlockSpec(mod.shape, const2)]
        args = [sink, q, k, v, ck, cv, u, gv, x2d, mod]
    else:
        assert seq_len == MIX_TILE
        grid = (batch // MIX_SUB, 1)
        tiles = 1
        tok = lambda b, t: (b, 0)
        in_specs = [pl.BlockSpec(memory_space=pltpu.SMEM),
                    pl.BlockSpec((block, D_MODEL), tok),
                    pl.BlockSpec(mod.shape, const2),
                    pl.BlockSpec((1, D_MODEL), const2),
                    pl.BlockSpec(w_in_bf16.shape, const2, **resident)]
        args = [sink, x2d, mod, norm1, w_in_bf16]
    in_specs += [pl.BlockSpec(w_s_bf16.shape, lambda b, t: (0, 0, 0)),
                 pl.BlockSpec(b_s_t.shape, const2),
                 pl.BlockSpec((1, MLP_WIDTH), const2),
                 pl.BlockSpec((1, ATTN_WIDTH), const2),
                 pl.BlockSpec((1, MLP_WIDTH), const2),
                 pl.BlockSpec(w_out_bf16.shape, const2, **resident),
                 pl.BlockSpec((1, D_MODEL), const2),
                 pl.BlockSpec(w_router_t.shape, const2)]
    args += [w_s_bf16, b_s_t, g_v, g_attn, g_mlp, w_out_bf16, norm2, w_router_t]
    out_specs = [pl.BlockSpec((block, D_MODEL), tok),
                 pl.BlockSpec((block, D_MODEL), tok),
                 pl.BlockSpec((N_EXPERTS, block), lambda b, t: (0, b * tiles + t))]
    out_shape = [jax.ShapeDtypeStruct((n, D_MODEL), F32),
                 jax.ShapeDtypeStruct((n, D_MODEL), BF16),
                 jax.ShapeDtypeStruct((N_EXPERTS, n), F32)]
    if not latent:
        out_specs += [pl.BlockSpec((block, KV_WIDTH), tok)] * 2
        out_shape += [jax.ShapeDtypeStruct((n, KV_WIDTH), F32)] * 2
    return pl.pallas_call(
        functools.partial(_mix_body, latent=latent, seq_len=seq_len),
        grid=grid,
        in_specs=in_specs,
        out_specs=out_specs,
        out_shape=out_shape,
        compiler_params=pltpu.CompilerParams(vmem_limit_bytes=58 * 2**20),
        name="mix_lat" if latent else "mix_ctx",
    )(*args)


def _exclusive_counts(flags, upper, lower):
    n_e, nb, _ = flags.shape
    f = flags.astype(BF16)
    local = jnp.dot(f.reshape(n_e * nb, LANES), upper, preferred_element_type=F32).reshape(n_e, nb, LANES)
    totals = jnp.broadcast_to(jnp.sum(flags, axis=2, keepdims=True), flags.shape).astype(BF16)
    before = jnp.stack([jnp.dot(lower, totals[e], preferred_element_type=F32) for e in range(n_e)], axis=0)
    return local, before


def _route_body(aff_ref, pos_ref, idx_ref, sgate_ref, start_ref, local_ref, gate_ref, total_ref, *, cap):
    aff = aff_ref[...]
    n_e, nb, _ = aff.shape
    reduce_tokens = lambda x: jnp.sum(jnp.sum(x, axis=2, keepdims=True), axis=1, keepdims=True)

    def bisect(i, thr):
        cand = thr | jnp.left_shift(jnp.int32(1), 30 - i)
        n_ge = reduce_tokens((aff >= lax.bitcast_convert_type(cand, F32)).astype(F32))
        return jnp.where(n_ge >= cap, cand, thr)

    thr = lax.fori_loop(0, 31, bisect, jnp.zeros((n_e, 1, 1), I32))

    upper = (lax.broadcasted_iota(I32, (LANES, LANES), 0) < lax.broadcasted_iota(I32, (LANES, LANES), 1)).astype(BF16)
    lower = (lax.broadcasted_iota(I32, (nb, nb), 1) < lax.broadcasted_iota(I32, (nb, nb), 0)).astype(BF16)

    above = aff >= lax.bitcast_convert_type(thr + 1, F32)
    tied = (aff >= lax.bitcast_convert_type(thr, F32)) & jnp.logical_not(above)
    need = cap - reduce_tokens(above.astype(F32))
    tie_local, tie_before = _exclusive_counts(tied.astype(F32), upper, lower)
    sel = above | (tied & (tie_local + tie_before < need))
    sel_f = sel.astype(F32)

    local, before = _exclusive_counts(sel_f, upper, lower)
    pos_ref[...] = jnp.where(sel, (local + before).astype(I32), -1)
    gate_ref[...] = aff

    local_ref[...] = (local + sel_f).astype(BF16)
    start_ref[...] = before[:, :, 0:1]
    total_ref[...] = jnp.sum(sel_f, axis=2, keepdims=True)
    slot = lax.broadcasted_iota(I32, (1, cap), 1).astype(F32)
    block_id = lax.broadcasted_iota(I32, (nb, 1), 0).astype(F32)
    lane_id = lax.broadcasted_iota(I32, (LANES, 1), 0).astype(F32)
    tn = (((0,), (0,)), ((), ()))

    def lists(e, carry):
        start = start_ref[e]
        end = start + total_ref[e]
        blk = jnp.sum((end <= slot).astype(F32), axis=0, keepdims=True)
        onehot = block_id == blk
        onehot_b = onehot.astype(BF16)
        rank = slot - jnp.sum(jnp.where(onehot, start, 0.0), axis=0, keepdims=True)
        counts = lax.dot_general(local_ref[e], onehot_b, tn, preferred_element_type=F32)
        tok = jnp.sum((counts <= rank).astype(F32), axis=0, keepdims=True)
        idx_ref[pl.ds(e, 1), :] = (blk * LANES + tok).astype(I32)
        g1, rest = _split_bf16_f32(gate_ref[e])
        g2, rest = _split_bf16_f32(rest)
        block_gates = sum(lax.dot_general(g, onehot_b, tn, preferred_element_type=F32)
                          for g in (g1, g2, rest.astype(BF16)))
        sgate_ref[pl.ds(e, 1), :] = jnp.sum(jnp.where(lane_id == tok, block_gates, 0.0), axis=0, keepdims=True)
        return carry

    lax.fori_loop(0, n_e, lists, 0)


def _route(aff_blocks, cap):
    n_e, nb, _ = aff_blocks.shape
    return pl.pallas_call(
        functools.partial(_route_body, cap=cap),
        out_shape=[jax.ShapeDtypeStruct(aff_blocks.shape, I32), jax.ShapeDtypeStruct((n_e, cap), I32),
                   jax.ShapeDtypeStruct((n_e, cap), F32), jax.ShapeDtypeStruct((n_e, nb, 1), F32)],
        scratch_shapes=[pltpu.VMEM(aff_blocks.shape, BF16), pltpu.VMEM(aff_blocks.shape, F32),
                        pltpu.VMEM((n_e, nb, 1), F32)],
        compiler_params=pltpu.CompilerParams(vmem_limit_bytes=48 * 2**20),
        name="route",
    )(aff_blocks)


FF_TILE = 256
N_FF_TILES = EXPERT_FF // FF_TILE
OUT_TILE = 256
N_OUT_TILES = D_MODEL // OUT_TILE
EXPERT_STEPS = N_FF_TILES + N_OUT_TILES


def _expert_body(idxc_ref, idxl_ref, hc_ref, hl_ref, gc_ref, gl_ref, wg_ref, wu_ref, wd_ref,
                 och_ref, ocl_ref, olh_ref, oll_ref, gbuf, xb_ref, hid_ref, sem, *, cap_c, cap_l):
    e = pl.program_id(0)
    s = pl.program_id(1)
    cap = cap_c + cap_l
    cur = e % 2
    nxt = 1 - cur
    next_expert = (e + 1) % N_EXPERTS

    def issue(expert, part, parts):
        for idx_ref, src, n_rows, base in ((idxc_ref, hc_ref, cap_c, 0), (idxl_ref, hl_ref, cap_l, cap_c)):
            per = n_rows // parts
            for j in range(per):
                r = part * per + j
                pltpu.make_async_copy(src.at[idx_ref[expert * n_rows + r]], gbuf.at[base + r], sem).start()

    def arrived():
        pltpu.make_async_copy(hc_ref.at[pl.ds(0, cap)], gbuf, sem).wait()

    def unpack(slot, part, parts):
        per = cap // parts
        r0 = pl.multiple_of(part * per, per)
        xb_ref[slot, pl.ds(r0, per), :] = gbuf[pl.ds(r0, per)].astype(F32).reshape(per, D_MODEL).astype(BF16)

    @pl.when((e == 0) & (s == 0))
    def _first():
        def gather(part, carry):
            issue(0, part, N_FF_TILES)
            return carry
        lax.fori_loop(0, N_FF_TILES, gather, 0)
        arrived()

        def relayout(part, carry):
            unpack(0, part, N_OUT_TILES)
            return carry
        lax.fori_loop(0, N_OUT_TILES, relayout, 0)

    @pl.when(s < N_FF_TILES)
    def _up():
        issue(next_expert, s, N_FF_TILES)
        x = xb_ref[cur]
        a = jnp.dot(x, wg_ref[0].astype(BF16), preferred_element_type=F32)
        b = jnp.dot(x, wu_ref[0].astype(BF16), preferred_element_type=F32)
        hid_ref[s] = (jax.nn.silu(a) * b).astype(BF16)

    @pl.when(s == N_FF_TILES)
    def _arrived():
        arrived()

    @pl.when(s >= N_FF_TILES)
    def _down():
        unpack(nxt, s - N_FF_TILES, N_OUT_TILES)
        wd = wd_ref[0].astype(BF16)
        acc = jnp.dot(hid_ref[0], wd[0:FF_TILE, :], preferred_element_type=F32)
        for j in range(1, N_FF_TILES):
            acc = acc + jnp.dot(hid_ref[j], wd[j * FF_TILE:(j + 1) * FF_TILE, :], preferred_element_type=F32)
        for g_ref, hi_ref, lo_ref, rows in ((gc_ref, och_ref, ocl_ref, slice(0, cap_c)),
                                            (gl_ref, olh_ref, oll_ref, slice(cap_c, cap))):
            gates = g_ref[...]
            lane = lax.broadcasted_iota(I32, gates.shape, 1)
            gate = jnp.sum(jnp.where(lane == e, gates, 0.0), axis=1, keepdims=True)
            hi_ref[...], lo_ref[...] = _split_bf16(acc[rows, :] * gate)


def _experts(idx_c, idx_l, h2_c, h2_l, sgate_c, sgate_l, w_gate, w_up, w_down):
    cap_c = idx_c.shape[1]
    cap_l = idx_l.shape[1]
    cap = cap_c + cap_l
    idx_c = idx_c.reshape(-1)
    idx_l = idx_l.reshape(-1)
    const = lambda e, s, ic, il: (0, 0)
    up_idx = lambda e, s, ic, il: (e, 0, jnp.minimum(s, N_FF_TILES - 1))
    down_idx = lambda e, s, ic, il: (e, 0, jnp.maximum(s - N_FF_TILES, 0))
    out_idx = lambda e, s, ic, il: (e, jnp.maximum(s - N_FF_TILES, 0))
    out_c = jax.ShapeDtypeStruct((N_EXPERTS * cap_c, D_MODEL), BF16)
    out_l = jax.ShapeDtypeStruct((N_EXPERTS * cap_l, D_MODEL), BF16)
    return pl.pallas_call(
        functools.partial(_expert_body, cap_c=cap_c, cap_l=cap_l),
        grid_spec=pltpu.PrefetchScalarGridSpec(
            num_scalar_prefetch=2,
            grid=(N_EXPERTS, EXPERT_STEPS),
            in_specs=[pl.BlockSpec(memory_space=pl.ANY),
                      pl.BlockSpec(memory_space=pl.ANY),
                      pl.BlockSpec(sgate_c.shape, const),
                      pl.BlockSpec(sgate_l.shape, const),
                      pl.BlockSpec((1, D_MODEL, FF_TILE), up_idx),
                      pl.BlockSpec((1, D_MODEL, FF_TILE), up_idx),
                      pl.BlockSpec((1, EXPERT_FF, OUT_TILE), down_idx)],
            out_specs=[pl.BlockSpec((cap_c, OUT_TILE), out_idx), pl.BlockSpec((cap_c, OUT_TILE), out_idx),
                       pl.BlockSpec((cap_l, OUT_TILE), out_idx), pl.BlockSpec((cap_l, OUT_TILE), out_idx)],
            scratch_shapes=[pltpu.VMEM((cap, ROW_TILES, LANES), BF16),
                            pltpu.VMEM((2, cap, D_MODEL), BF16),
                            pltpu.VMEM((N_FF_TILES, cap, FF_TILE), BF16),
                            pltpu.SemaphoreType.DMA(())]),
        out_shape=[out_c, out_c, out_l, out_l],
        compiler_params=pltpu.CompilerParams(vmem_limit_bytes=60 * 2**20),
        name="experts",
    )(idx_c, idx_l, h2_c, h2_l, sgate_c, sgate_l, w_gate, w_up, w_down)


COMBINE_TILE = 128


BF16_ROWS = 16
WINDOW_ROWS_CTX = 48
WINDOW_ROWS_LAT = 80


def _combine_body(bs_ref, pos_ref, hi_ref, lo_ref, x1_ref, mod_ref, fn_ref, y_ref, rows_hi, rows_lo, sem, *,
                  cap, win, n_blocks, latent, tiles_per_seq):
    i = pl.program_id(0)
    tb = COMBINE_TILE
    slot = i % 2

    def window_base(e, blk):
        return (bs_ref[e, blk] // BF16_ROWS) * BF16_ROWS

    def window_start(e, blk, p):
        return jnp.minimum(window_base(e, blk) + p * win, cap - win)

    def fetch(blk, p, to_slot):
        for e in range(N_EXPERTS):
            start = pl.multiple_of(e * cap + window_start(e, blk, p), BF16_ROWS)
            for src, dst in ((hi_ref, rows_hi), (lo_ref, rows_lo)):
                pltpu.make_async_copy(src.at[pl.ds(start, win)], dst.at[to_slot, pl.ds(e * win, win)],
                                      sem.at[to_slot]).start()

    def wait(to_slot):
        for src, dst in ((hi_ref, rows_hi), (lo_ref, rows_lo)):
            pltpu.make_async_copy(src.at[pl.ds(0, N_EXPERTS * win)], dst.at[to_slot], sem.at[to_slot]).wait()

    @pl.when(i == 0)
    def _first():
        fetch(0, 0, 0)

    @pl.when(i + 1 < n_blocks)
    def _prefetch():
        fetch(i + 1, 0, 1 - slot)

    lane = lax.broadcasted_iota(I32, (1, N_EXPERTS), 1)
    col = lax.broadcasted_iota(I32, (1, (N_EXPERTS * win)), 1)
    col_row = (col % win).astype(F32)
    expand = (lax.broadcasted_iota(I32, (N_EXPERTS, (N_EXPERTS * win)), 0) == col // win).astype(BF16)
    widen = lambda v: jnp.dot(v, expand, preferred_element_type=F32)

    pos = pos_ref[...]

    def one_pass(p, acc):
        base = jnp.zeros((1, N_EXPERTS), I32)
        start = jnp.zeros((1, N_EXPERTS), I32)
        for e in range(N_EXPERTS):
            base = jnp.where(lane == e, window_base(e, i), base)
            start = jnp.where(lane == e, window_start(e, i, p), start)
        rel = pos - base
        mine = (pos >= 0) & (rel >= p * win) & (rel < (p + 1) * win)
        window_col = jnp.where(mine, pos - start, -1).astype(F32).astype(BF16)
        select = (widen(window_col) == col_row).astype(BF16)
        return (acc + jnp.dot(select, rows_hi[slot], preferred_element_type=F32)
                + jnp.dot(select, rows_lo[slot], preferred_element_type=F32))

    wait(slot)
    moe = one_pass(0, jnp.zeros((tb, D_MODEL), F32))

    n_pass = jnp.int32(1)
    for e in range(N_EXPERTS):
        span = bs_ref[e, i + 1] - window_base(e, i)
        n_pass = jnp.maximum(n_pass, (span + win - 1) // win)

    def extra(p, acc):
        fetch(i, p, slot)
        wait(slot)
        return one_pass(p, acc)

    moe = lax.fori_loop(1, n_pass, extra, moe)
    row = 1 + i // tiles_per_seq if latent else 0
    gate2 = mod_ref[pl.ds(row, 1), 5 * D_MODEL:6 * D_MODEL]
    y_ref[...] = _rms(x1_ref[...] + gate2 * moe) * fn_ref[...]


def _combine(block_start, pos_t, out_hi, out_lo, x1, mod, final_norm, cap, latent, seq_len):
    n = x1.shape[0]
    tb = COMBINE_TILE
    n_blocks = n // tb
    win = WINDOW_ROWS_LAT if latent else WINDOW_ROWS_CTX
    return pl.pallas_call(
        functools.partial(_combine_body, cap=cap, win=win, n_blocks=n_blocks, latent=latent,
                          tiles_per_seq=seq_len // tb),
        grid_spec=pltpu.PrefetchScalarGridSpec(
            num_scalar_prefetch=1,
            grid=(n_blocks,),
            in_specs=[pl.BlockSpec((tb, N_EXPERTS), lambda i, bs: (i, 0)),
                      pl.BlockSpec(memory_space=pl.ANY),
                      pl.BlockSpec(memory_space=pl.ANY),
                      pl.BlockSpec((tb, D_MODEL), lambda i, bs: (i, 0)),
                      pl.BlockSpec(mod.shape, lambda i, bs: (0, 0)),
                      pl.BlockSpec((1, D_MODEL), lambda i, bs: (0, 0))],
            out_specs=pl.BlockSpec((tb, D_MODEL), lambda i, bs: (i, 0)),
            scratch_shapes=[pltpu.VMEM((2, N_EXPERTS * win, D_MODEL), BF16),
                            pltpu.VMEM((2, N_EXPERTS * win, D_MODEL), BF16),
                            pltpu.SemaphoreType.DMA((2,))]),
        out_shape=jax.ShapeDtypeStruct((n, D_MODEL), F32),
        compiler_params=pltpu.CompilerParams(vmem_limit_bytes=48 * 2**20),
        name="combine_lat" if latent else "combine_ctx",
    )(block_start, pos_t, out_hi, out_lo, x1, mod, final_norm)


def _rope_tables(seq_len):
    pos = jnp.arange(seq_len)
    half = HEAD_DIM // 2
    freqs = ROPE_BASE ** (-jnp.arange(0, half, 2, dtype=F32) / half)
    ang_r = (pos // GRID_W).astype(F32)[:, None] * freqs[None, :]
    ang_c = (pos % GRID_W).astype(F32)[:, None] * freqs[None, :]
    cos = jnp.concatenate([jnp.cos(ang_r)] * 2 + [jnp.cos(ang_c)] * 2, axis=1)
    sin = jnp.concatenate([-jnp.sin(ang_r), jnp.sin(ang_r), -jnp.sin(ang_c), jnp.sin(ang_c)], axis=1)
    return cos, sin


def kernel(x_prompt, x_sample, cache_k, cache_v, c, c_ctx, w_ada, b_ada, norm1, w_in, attn_sink, w_s, b_s, g_v,
           g_attn, g_mlp, w_out, norm2, w_router, w_gate, w_up, w_down, final_norm):
    depth = w_ada.shape[0]
    assert depth == 1, "single-layer trunk"
    batch, seq, _ = x_prompt.shape
    dec_batch, dec_seq, _ = x_sample.shape
    n_ctx = batch * seq
    n_lat = dec_batch * dec_seq
    cap_c = CAPACITY_FACTOR * n_ctx // N_EXPERTS
    cap_l = CAPACITY_FACTOR * n_lat // N_EXPERTS
    l = 0

    cvec = jnp.concatenate([c_ctx[None, :], c, jnp.zeros((MOD_ROWS - 1 - dec_batch, D_MODEL), F32)], axis=0)
    mod = _adaln(cvec, w_ada[l], b_ada[l][None, :])

    w_in_b = w_in[l].astype(BF16)
    w_out_b = w_out[l].astype(BF16)
    w_s_b = w_s[l].astype(BF16)
    b_s_t = jnp.transpose(b_s[l])
    w_router_t = jnp.transpose(w_router[l])
    row = lambda a: a.reshape(1, -1)
    shared = (mod, row(norm1[l]), w_in_b, w_s_b, b_s_t, row(g_v[l]), row(g_attn[l]), row(g_mlp[l]), w_out_b,
              row(norm2[l]), w_router_t)

    xp = x_prompt.reshape(n_ctx, D_MODEL)
    xl = x_sample.reshape(n_lat, D_MODEL)

    x1_c, h2_c, afft_c, k, v = _mix(attn_sink[l], None, xp, *shared, batch, seq)
    state_k = k.reshape(batch, 1, seq, N_KV_HEADS, HEAD_DIM)
    state_v = v.reshape(batch, 1, seq, N_KV_HEADS, HEAD_DIM)

    q, k, v, u, gv = _inproj(xl, mod, row(norm1[l]), w_in_b, _rope_tables(dec_seq), dec_seq)
    projected = (q, k, v, cache_k[:, l].reshape(dec_batch, -1, KV_WIDTH),
                 cache_v[:, l].reshape(dec_batch, -1, KV_WIDTH), u, gv)
    x1_l, h2_l, afft_l = _mix(attn_sink[l], projected, xl, *shared, dec_batch, dec_seq)

    def route(aff_t, cap):
        n = aff_t.shape[1]
        pos, idx, slot_gate, start = _route(aff_t.reshape(N_EXPERTS, n // LANES, LANES), cap)
        starts = jnp.concatenate([start.reshape(N_EXPERTS, -1).astype(I32), jnp.full((N_EXPERTS, 1), cap, I32)],
                                 axis=1)
        return jnp.transpose(pos.reshape(N_EXPERTS, n)), idx, jnp.transpose(slot_gate), starts

    assert COMBINE_TILE == LANES
    pos_c, idx_c, sgate_c, starts_c = route(afft_c, cap_c)
    pos_l, idx_l, sgate_l, starts_l = route(afft_l, cap_l)
    slabs = lambda h: h.reshape(h.shape[0], ROW_TILES, LANES)
    och, ocl, olh, oll = _experts(idx_c, idx_l, slabs(h2_c), slabs(h2_l), sgate_c, sgate_l,
                                  w_gate[l], w_up[l], w_down[l])
    fn = row(final_norm)
    y_c = _combine(starts_c, pos_c, och, ocl, x1_c, mod, fn, cap_c, False, seq)
    y_l = _combine(starts_l, pos_l, olh, oll, x1_l, mod, fn, cap_l, True, dec_seq)

    return (y_c.reshape(batch, seq, D_MODEL), y_l.reshape(dec_batch, dec_seq, D_MODEL), state_k, state_v)
```

```python
import functools

import jax
import jax.numpy as jnp
from jax import lax
from jax.experimental import pallas as pl
from jax.experimental.pallas import tpu as pltpu

F32 = jnp.float32
BF16 = jnp.bfloat16
I32 = jnp.int32

D_MODEL = 2048
GRID_W = 64
HEAD_DIM = 128
N_Q_HEADS = 8
N_KV_HEADS = 2
Q_PER_KV = N_Q_HEADS // N_KV_HEADS
ATTN_WIDTH = N_Q_HEADS * HEAD_DIM
KV_WIDTH = N_KV_HEADS * HEAD_DIM
N_MLP_GROUPS = 8
MLP_WIDTH = 1024
Q_END = ATTN_WIDTH
K_END = Q_END + KV_WIDTH
V_END = K_END + KV_WIDTH
U_END = V_END + MLP_WIDTH
IN_WIDTH = U_END + MLP_WIDTH
CHUNK = 128
WINDOW = 128
N_EXPERTS = 16
EXPERT_FF = 2048
CAPACITY_FACTOR = 2
ROPE_BASE = 10000.0
EPS = 1e-6
NEG_INF = -1e30
ATTN_SCALE = HEAD_DIM ** -0.5

LANES = 128
MIX_TILE = 256
MIX_SUB = 2
LAT_SPAN = MIX_TILE + 2 * WINDOW
ROW_TILES = D_MODEL // LANES
MOD_ROWS = 8


def _rms(x):
    return x * lax.rsqrt(jnp.mean(x * x, axis=-1, keepdims=True) + EPS)


def _split_bf16_f32(x):
    hi = x.astype(BF16)
    return hi, x - hi.astype(F32)


def _split_bf16(x):
    hi, rest = _split_bf16_f32(x)
    return hi, rest.astype(BF16)


def _adaln_body(c_ref, w_ref, b_ref, o_ref):
    s = jax.nn.silu(c_ref[...])
    o_ref[...] = jnp.dot(s.astype(BF16), w_ref[...].astype(BF16),
                         preferred_element_type=F32) + b_ref[...]


def _adaln(cvec, w_ada, b_ada):
    tn = 1024
    width = w_ada.shape[1]
    return pl.pallas_call(
        _adaln_body,
        grid=(width // tn,),
        in_specs=[pl.BlockSpec((MOD_ROWS, D_MODEL), lambda j: (0, 0)),
                  pl.BlockSpec((D_MODEL, tn), lambda j: (0, j)),
                  pl.BlockSpec((1, tn), lambda j: (0, j))],
        out_specs=pl.BlockSpec((MOD_ROWS, tn), lambda j: (0, j)),
        out_shape=jax.ShapeDtypeStruct((MOD_ROWS, width), F32),
        compiler_params=pltpu.CompilerParams(vmem_limit_bytes=40 * 2**20),
        name="adaln",
    )(cvec, w_ada, b_ada)


def _rope(x, cos, sin, n_heads):
    lane = lax.broadcasted_iota(I32, (x.shape[0], HEAD_DIM), 1)
    first = (lane & 32) == 0
    outs = []
    for h in range(n_heads):
        xh = x[:, h * HEAD_DIM:(h + 1) * HEAD_DIM]
        partner = jnp.where(first, pltpu.roll(xh, 96, 1), pltpu.roll(xh, 32, 1))
        outs.append(xh * cos + partner * sin)
    return jnp.concatenate(outs, axis=1)


def _inproj_body(*refs, latent, tiles_per_seq):
    if latent:
        x_ref, mod_ref, g_ref, w_ref, cos_ref, sin_ref, q_ref, k_ref, v_ref, u_ref, gv_ref = refs
    else:
        x_ref, mod_ref, g_ref, w_ref, q_ref, k_ref, v_ref, u_ref, gv_ref = refs
    i = pl.program_id(0)
    row = 1 + i // tiles_per_seq if latent else 0
    shift = mod_ref[pl.ds(row, 1), 0:D_MODEL]
    scale = mod_ref[pl.ds(row, 1), D_MODEL:2 * D_MODEL]
    h = _rms(x_ref[...]) * g_ref[...] * (1 + scale) + shift
    z = jnp.dot(h.astype(BF16), w_ref[...], preferred_element_type=F32)
    q = z[:, 0:Q_END]
    k = z[:, Q_END:K_END]
    if latent:
        cos = cos_ref[...]
        sin = sin_ref[...]
        q = _rope(q, cos, sin, N_Q_HEADS)
        k = _rope(k, cos, sin, N_KV_HEADS)
    q_ref[...] = q.astype(BF16)
    k_ref[...] = k
    v_ref[...] = z[:, K_END:V_END]
    u_ref[...] = z[:, V_END:U_END]
    gv_ref[...] = z[:, U_END:IN_WIDTH]


def _inproj(x2d, mod, norm1, w_in_bf16, rope_tables, seq_len):
    n = x2d.shape[0]
    tm = 256
    latent = rope_tables is not None
    tiles_per_seq = seq_len // tm
    in_specs = [pl.BlockSpec((tm, D_MODEL), lambda i: (i, 0)),
                pl.BlockSpec(mod.shape, lambda i: (0, 0)),
                pl.BlockSpec((1, D_MODEL), lambda i: (0, 0)),
                pl.BlockSpec((D_MODEL, IN_WIDTH), lambda i: (0, 0))]
    args = [x2d, mod, norm1, w_in_bf16]
    if latent:
        in_specs += [pl.BlockSpec((tm, HEAD_DIM), lambda i: (i % tiles_per_seq, 0))] * 2
        args += list(rope_tables)
    widths = (ATTN_WIDTH, KV_WIDTH, KV_WIDTH, MLP_WIDTH, MLP_WIDTH)
    dtypes = (BF16, F32, F32, F32, F32)
    return pl.pallas_call(
        functools.partial(_inproj_body, latent=latent, tiles_per_seq=tiles_per_seq),
        grid=(n // tm,),
        in_specs=in_specs,
        out_specs=[pl.BlockSpec((tm, w), lambda i: (i, 0)) for w in widths],
        out_shape=[jax.ShapeDtypeStruct((n, w), dt) for w, dt in zip(widths, dtypes)],
        compiler_params=pltpu.CompilerParams(vmem_limit_bytes=56 * 2**20),
        name="inproj_lat" if latent else "inproj_ctx",
    )(*args)


def _mix_body(*refs, latent, seq_len):
    if latent:
        sink_ref, q_ref, k_ref, v_ref, ck_ref, cv_ref, u_ref, gv_ref, x_ref, mod_ref, *rest = refs
        proj_refs = (q_ref, k_ref, v_ref, ck_ref, cv_ref, u_ref, gv_ref)
    else:
        sink_ref, x_ref, mod_ref, norm1_ref, win_ref, *rest = refs
        proj_refs = (norm1_ref, win_ref) + tuple(rest[-2:])
        rest = rest[:-2]
    b = pl.program_id(0)
    t = pl.program_id(1)
    row = 1 + b if latent else 0
    for sub in range(MIX_SUB):
        _mix_tile(sub, t * MIX_SUB + sub, row, latent, seq_len, sink_ref, proj_refs, x_ref, mod_ref, *rest)


def _mix_tile(sub, tile, row, latent, seq_len, sink_ref, proj_refs, x_ref,
              mod_ref, ws_ref, bst_ref, gvn_ref, gattn_ref, gmlp_ref, wout_ref, norm2_ref, wrt_ref,
              x1_ref, h2_ref, afft_ref):
    tq = MIX_TILE
    rows = slice(sub * tq, (sub + 1) * tq)

    if latent:
        q_ref, k_ref, v_ref, ck_ref, cv_ref, u_ref, gv_ref = proj_refs
        q = q_ref[rows, :]
        u = u_ref[rows, :]
        gv = gv_ref[rows, :]
        ks = pl.multiple_of(jnp.clip(tile * tq - WINDOW, 0, seq_len - LAT_SPAN), WINDOW)
        kcat = jnp.concatenate([ck_ref[0], k_ref[pl.ds(ks, LAT_SPAN), :]], axis=0).astype(BF16)
        vcat = jnp.concatenate([cv_ref[0], v_ref[pl.ds(ks, LAT_SPAN), :]], axis=0).astype(BF16)
        n_ctx = ck_ref.shape[1]
        n_keys = n_ctx + LAT_SPAN
        kidx = lax.broadcasted_iota(I32, (tq, n_keys), 1)
        qpos = tile * tq + lax.broadcasted_iota(I32, (tq, n_keys), 0)
        kpos = ks + kidx - n_ctx
        mask = (kidx < n_ctx) | (jnp.abs(qpos - kpos) <= WINDOW)
    else:
        norm1_ref, win_ref, kout_ref, vout_ref = proj_refs
        shift1 = mod_ref[pl.ds(row, 1), 0:D_MODEL]
        scale1 = mod_ref[pl.ds(row, 1), D_MODEL:2 * D_MODEL]
        h1 = _rms(x_ref[rows, :]) * norm1_ref[...] * (1 + scale1) + shift1
        z = jnp.dot(h1.astype(BF16), win_ref[...], preferred_element_type=F32)
        q = z[:, 0:Q_END].astype(BF16)
        k = z[:, Q_END:K_END]
        v = z[:, K_END:V_END]
        u = z[:, V_END:U_END]
        gv = z[:, U_END:IN_WIDTH]
        kout_ref[rows, :] = k
        vout_ref[rows, :] = v
        kcat = k.astype(BF16)
        vcat = v.astype(BF16)
        mask = None

    heads = []
    for h in range(N_Q_HEADS):
        kv = h // Q_PER_KV
        qh = q[:, h * HEAD_DIM:(h + 1) * HEAD_DIM]
        kh = kcat[:, kv * HEAD_DIM:(kv + 1) * HEAD_DIM]
        vh = vcat[:, kv * HEAD_DIM:(kv + 1) * HEAD_DIM]
        s = lax.dot_general(qh, kh, (((1,), (1,)), ((), ())), preferred_element_type=F32) * ATTN_SCALE
        if mask is not None:
            s = jnp.where(mask, s, NEG_INF)
        sink = sink_ref[h]
        m = jnp.maximum(jnp.max(s, axis=1, keepdims=True), sink)
        p = jnp.exp(s - m)
        den = jnp.sum(p, axis=1, keepdims=True) + jnp.exp(sink - m)
        heads.append(jnp.dot(p.astype(BF16), vh, preferred_element_type=F32) / den)
    attn = jnp.concatenate(heads, axis=1)

    sg = (_rms(jax.nn.gelu(gv)) * gvn_ref[...]).astype(BF16)
    bst = bst_ref[...]
    chunks = []
    for c in range(tq // CHUNK):
        groups = []
        for g in range(N_MLP_GROUPS):
            blk = sg[c * CHUNK:(c + 1) * CHUNK, g * LANES:(g + 1) * LANES]
            groups.append(jnp.dot(ws_ref[g], blk, preferred_element_type=F32) + bst[:, g:g + 1])
        chunks.append(jnp.concatenate(groups, axis=1))
    mlp = jax.nn.gelu(u) * jnp.concatenate(chunks, axis=0)

    mixed = jnp.concatenate([_rms(attn) * gattn_ref[...], _rms(mlp) * gmlp_ref[...]], axis=1)
    out = jnp.dot(mixed.astype(BF16), wout_ref[...], preferred_element_type=F32)
    gate1 = mod_ref[pl.ds(row, 1), 2 * D_MODEL:3 * D_MODEL]
    x1 = x_ref[rows, :] + gate1 * out
    x1_ref[rows, :] = x1

    shift2 = mod_ref[pl.ds(row, 1), 3 * D_MODEL:4 * D_MODEL]
    scale2 = mod_ref[pl.ds(row, 1), 4 * D_MODEL:5 * D_MODEL]
    h2 = _rms(x1) * norm2_ref[...] * (1 + scale2) + shift2

    hh = h2.astype(BF16)
    hl = (h2 - hh.astype(F32)).astype(BF16)
    wrt = wrt_ref[...]
    wth = wrt.astype(BF16)
    wtl = (wrt - wth.astype(F32)).astype(BF16)
    nt = (((1,), (1,)), ((), ()))
    logits_t = (lax.dot_general(wth, hh, nt, preferred_element_type=F32)
                + lax.dot_general(wth, hl, nt, preferred_element_type=F32)
                + lax.dot_general(wtl, hh, nt, preferred_element_type=F32))
    pt = jnp.exp(logits_t - jnp.max(logits_t, axis=0, keepdims=True))
    afft_ref[:, rows] = pt / jnp.sum(pt, axis=0, keepdims=True)

    h2_ref[rows, :] = hh


def _mix(sink, projected, x2d, mod, norm1, w_in_bf16, w_s_bf16, b_s_t, g_v, g_attn, g_mlp, w_out_bf16, norm2,
         w_router_t, batch, seq_len):
    n = x2d.shape[0]
    latent = projected is not None
    block = MIX_SUB * MIX_TILE
    const2 = lambda b, t: (0, 0)
    resident = dict(pipeline_mode=pl.Buffered(1))
    if latent:
        grid = (batch, seq_len // block)
        tiles = grid[1]
        tok = lambda b, t: (b * tiles + t, 0)
        q, k, v, ck, cv, u, gv = projected
        seq = lambda b, t: (b, 0)
        in_specs = [pl.BlockSpec(memory_space=pltpu.SMEM),
                    pl.BlockSpec((block, ATTN_WIDTH), tok),
                    pl.BlockSpec((seq_len, KV_WIDTH), seq),
                    pl.BlockSpec((seq_len, KV_WIDTH), seq),
                    pl.BlockSpec((1,) + ck.shape[1:], lambda b, t: (b, 0, 0)),
                    pl.BlockSpec((1,) + cv.shape[1:], lambda b, t: (b, 0, 0)),
                    pl.BlockSpec((block, MLP_WIDTH), tok),
                    pl.BlockSpec((block, MLP_WIDTH), tok),
                    pl.BlockSpec((block, D_MODEL), tok),
                    pl.BlockSpec(mod.shape, const2)]
        args = [sink, q, k, v, ck, cv, u, gv, x2d, mod]
    else:
        assert seq_len == MIX_TILE
        grid = (batch // MIX_SUB, 1)
        tiles = 1
        tok = lambda b, t: (b, 0)
        in_specs = [pl.BlockSpec(memory_space=pltpu.SMEM),
                    pl.BlockSpec((block, D_MODEL), tok),
                    pl.BlockSpec(mod.shape, const2),
                    pl.BlockSpec((1, D_MODEL), const2),
                    pl.BlockSpec(w_in_bf16.shape, const2, **resident)]
        args = [sink, x2d, mod, norm1, w_in_bf16]
    in_specs += [pl.BlockSpec(w_s_bf16.shape, lambda b, t: (0, 0, 0)),
                 pl.BlockSpec(b_s_t.shape, const2),
                 pl.BlockSpec((1, MLP_WIDTH), const2),
                 pl.BlockSpec((1, ATTN_WIDTH), const2),
                 pl.BlockSpec((1, MLP_WIDTH), const2),
                 pl.BlockSpec(w_out_bf16.shape, const2, **resident),
                 pl.BlockSpec((1, D_MODEL), const2),
                 pl.BlockSpec(w_router_t.shape, const2)]
    args += [w_s_bf16, b_s_t, g_v, g_attn, g_mlp, w_out_bf16, norm2, w_router_t]
    out_specs = [pl.BlockSpec((block, D_MODEL), tok),
                 pl.BlockSpec((block, D_MODEL), tok),
                 pl.BlockSpec((N_EXPERTS, block), lambda b, t: (0, b * tiles + t))]
    out_shape = [jax.ShapeDtypeStruct((n, D_MODEL), F32),
                 jax.ShapeDtypeStruct((n, D_MODEL), BF16),
                 jax.ShapeDtypeStruct((N_EXPERTS, n), F32)]
    if not latent:
        out_specs += [pl.BlockSpec((block, KV_WIDTH), tok)] * 2
        out_shape += [jax.ShapeDtypeStruct((n, KV_WIDTH), F32)] * 2
    return pl.pallas_call(
        functools.partial(_mix_body, latent=latent, seq_len=seq_len),
        grid=grid,
        in_specs=in_specs,
        out_specs=out_specs,
        out_shape=out_shape,
        compiler_params=pltpu.CompilerParams(vmem_limit_bytes=58 * 2**20),
        name="mix_lat" if latent else "mix_ctx",
    )(*args)


def _exclusive_counts(flags, upper, lower):
    n_e, nb, _ = flags.shape
    f = flags.astype(BF16)
    local = jnp.dot(f.reshape(n_e * nb, LANES), upper, preferred_element_type=F32).reshape(n_e, nb, LANES)
    totals = jnp.broadcast_to(jnp.sum(flags, axis=2, keepdims=True), flags.shape).astype(BF16)
    before = jnp.stack([jnp.dot(lower, totals[e], preferred_element_type=F32) for e in range(n_e)], axis=0)
    return local, before


def _route_body(aff_ref, pos_ref, idx_ref, sgate_ref, start_ref, local_ref, gate_ref, total_ref, *, cap):
    aff = aff_ref[...]
    n_e, nb, _ = aff.shape
    reduce_tokens = lambda x: jnp.sum(jnp.sum(x, axis=2, keepdims=True), axis=1, keepdims=True)

    def bisect(i, thr):
        cand = thr | jnp.left_shift(jnp.int32(1), 30 - i)
        n_ge = reduce_tokens((aff >= lax.bitcast_convert_type(cand, F32)).astype(F32))
        return jnp.where(n_ge >= cap, cand, thr)

    thr = lax.fori_loop(0, 31, bisect, jnp.zeros((n_e, 1, 1), I32))

    upper = (lax.broadcasted_iota(I32, (LANES, LANES), 0) < lax.broadcasted_iota(I32, (LANES, LANES), 1)).astype(BF16)
    lower = (lax.broadcasted_iota(I32, (nb, nb), 1) < lax.broadcasted_iota(I32, (nb, nb), 0)).astype(BF16)

    above = aff >= lax.bitcast_convert_type(thr + 1, F32)
    tied = (aff >= lax.bitcast_convert_type(thr, F32)) & jnp.logical_not(above)
    need = cap - reduce_tokens(above.astype(F32))
    tie_local, tie_before = _exclusive_counts(tied.astype(F32), upper, lower)
    sel = above | (tied & (tie_local + tie_before < need))
    sel_f = sel.astype(F32)

    local, before = _exclusive_counts(sel_f, upper, lower)
    pos_ref[...] = jnp.where(sel, (local + before).astype(I32), -1)
    gate_ref[...] = aff

    local_ref[...] = (local + sel_f).astype(BF16)
    start_ref[...] = before[:, :, 0:1]
    total_ref[...] = jnp.sum(sel_f, axis=2, keepdims=True)
    slot = lax.broadcasted_iota(I32, (1, cap), 1).astype(F32)
    block_id = lax.broadcasted_iota(I32, (nb, 1), 0).astype(F32)
    lane_id = lax.broadcasted_iota(I32, (LANES, 1), 0).astype(F32)
    tn = (((0,), (0,)), ((), ()))

    def lists(e, carry):
        start = start_ref[e]
        end = start + total_ref[e]
        blk = jnp.sum((end <= slot).astype(F32), axis=0, keepdims=True)
        onehot = block_id == blk
        onehot_b = onehot.astype(BF16)
        rank = slot - jnp.sum(jnp.where(onehot, start, 0.0), axis=0, keepdims=True)
        counts = lax.dot_general(local_ref[e], onehot_b, tn, preferred_element_type=F32)
        tok = jnp.sum((counts <= rank).astype(F32), axis=0, keepdims=True)
        idx_ref[pl.ds(e, 1), :] = (blk * LANES + tok).astype(I32)
        g1, rest = _split_bf16_f32(gate_ref[e])
        g2, rest = _split_bf16_f32(rest)
        block_gates = sum(lax.dot_general(g, onehot_b, tn, preferred_element_type=F32)
                          for g in (g1, g2, rest.astype(BF16)))
        sgate_ref[pl.ds(e, 1), :] = jnp.sum(jnp.where(lane_id == tok, block_gates, 0.0), axis=0, keepdims=True)
        return carry

    lax.fori_loop(0, n_e, lists, 0)


def _route(aff_blocks, cap):
    n_e, nb, _ = aff_blocks.shape
    return pl.pallas_call(
        functools.partial(_route_body, cap=cap),
        out_shape=[jax.ShapeDtypeStruct(aff_blocks.shape, I32), jax.ShapeDtypeStruct((n_e, cap), I32),
                   jax.ShapeDtypeStruct((n_e, cap), F32), jax.ShapeDtypeStruct((n_e, nb, 1), F32)],
        scratch_shapes=[pltpu.VMEM(aff_blocks.shape, BF16), pltpu.VMEM(aff_blocks.shape, F32),
                        pltpu.VMEM((n_e, nb, 1), F32)],
        compiler_params=pltpu.CompilerParams(vmem_limit_bytes=48 * 2**20),
        name="route",
    )(aff_blocks)


FF_TILE = 256
N_FF_TILES = EXPERT_FF // FF_TILE
OUT_TILE = 256
N_OUT_TILES = D_MODEL // OUT_TILE
EXPERT_STEPS = N_FF_TILES + N_OUT_TILES


def _expert_body(idxc_ref, idxl_ref, hc_ref, hl_ref, gc_ref, gl_ref, wg_ref, wu_ref, wd_ref,
                 oc_ref, ol_ref, gbuf, xb_ref, hid_ref, sem, *, cap_c, cap_l):
    e = pl.program_id(0)
    s = pl.program_id(1)
    cap = cap_c + cap_l
    cur = e % 2
    nxt = 1 - cur
    next_expert = (e + 1) % N_EXPERTS

    def issue(expert, part, parts):
        for idx_ref, src, n_rows, base in ((idxc_ref, hc_ref, cap_c, 0), (idxl_ref, hl_ref, cap_l, cap_c)):
            per = n_rows // parts
            for j in range(per):
                r = part * per + j
                pltpu.make_async_copy(src.at[idx_ref[expert * n_rows + r]], gbuf.at[base + r], sem).start()

    def arrived():
        pltpu.make_async_copy(hc_ref.at[pl.ds(0, cap)], gbuf, sem).wait()

    def unpack(slot, part, parts):
        per = cap // parts
        r0 = pl.multiple_of(part * per, per)
        xb_ref[slot, pl.ds(r0, per), :] = gbuf[pl.ds(r0, per)].astype(F32).reshape(per, D_MODEL).astype(BF16)

    @pl.when((e == 0) & (s == 0))
    def _first():
        def gather(part, carry):
            issue(0, part, N_FF_TILES)
            return carry
        lax.fori_loop(0, N_FF_TILES, gather, 0)
        arrived()

        def relayout(part, carry):
            unpack(0, part, N_OUT_TILES)
            return carry
        lax.fori_loop(0, N_OUT_TILES, relayout, 0)

    @pl.when(s < N_FF_TILES)
    def _up():
        issue(next_expert, s, N_FF_TILES)
        x = xb_ref[cur]
        a = jnp.dot(x, wg_ref[0].astype(BF16), preferred_element_type=F32)
        b = jnp.dot(x, wu_ref[0].astype(BF16), preferred_element_type=F32)
        hid_ref[s] = (jax.nn.silu(a) * b).astype(BF16)

    @pl.when(s == N_FF_TILES)
    def _arrived():
        arrived()

    @pl.when(s >= N_FF_TILES)
    def _down():
        unpack(nxt, s - N_FF_TILES, N_OUT_TILES)
        wd = wd_ref[0].astype(BF16)
        acc = jnp.dot(hid_ref[0], wd[0:FF_TILE, :], preferred_element_type=F32)
        for j in range(1, N_FF_TILES):
            acc = acc + jnp.dot(hid_ref[j], wd[j * FF_TILE:(j + 1) * FF_TILE, :], preferred_element_type=F32)
        for g_ref, o_ref, rows in ((gc_ref, oc_ref, slice(0, cap_c)), (gl_ref, ol_ref, slice(cap_c, cap))):
            gates = g_ref[...]
            lane = lax.broadcasted_iota(I32, gates.shape, 1)
            gate = jnp.sum(jnp.where(lane == e, gates, 0.0), axis=1, keepdims=True)
            o_ref[...] = acc[rows, :] * gate


def _experts(idx_c, idx_l, h2_c, h2_l, sgate_c, sgate_l, w_gate, w_up, w_down):
    cap_c = idx_c.shape[1]
    cap_l = idx_l.shape[1]
    cap = cap_c + cap_l
    idx_c = idx_c.reshape(-1)
    idx_l = idx_l.reshape(-1)
    const = lambda e, s, ic, il: (0, 0)
    up_idx = lambda e, s, ic, il: (e, 0, jnp.minimum(s, N_FF_TILES - 1))
    down_idx = lambda e, s, ic, il: (e, 0, jnp.maximum(s - N_FF_TILES, 0))
    out_idx = lambda e, s, ic, il: (e, jnp.maximum(s - N_FF_TILES, 0))
    out_c = jax.ShapeDtypeStruct((N_EXPERTS * cap_c, D_MODEL), F32)
    out_l = jax.ShapeDtypeStruct((N_EXPERTS * cap_l, D_MODEL), F32)
    return pl.pallas_call(
        functools.partial(_expert_body, cap_c=cap_c, cap_l=cap_l),
        grid_spec=pltpu.PrefetchScalarGridSpec(
            num_scalar_prefetch=2,
            grid=(N_EXPERTS, EXPERT_STEPS),
            in_specs=[pl.BlockSpec(memory_space=pl.ANY),
                      pl.BlockSpec(memory_space=pl.ANY),
                      pl.BlockSpec(sgate_c.shape, const),
                      pl.BlockSpec(sgate_l.shape, const),
                      pl.BlockSpec((1, D_MODEL, FF_TILE), up_idx),
                      pl.BlockSpec((1, D_MODEL, FF_TILE), up_idx),
                      pl.BlockSpec((1, EXPERT_FF, OUT_TILE), down_idx)],
            out_specs=[pl.BlockSpec((cap_c, OUT_TILE), out_idx), pl.BlockSpec((cap_l, OUT_TILE), out_idx)],
            scratch_shapes=[pltpu.VMEM((cap, ROW_TILES, LANES), BF16),
                            pltpu.VMEM((2, cap, D_MODEL), BF16),
                            pltpu.VMEM((N_FF_TILES, cap, FF_TILE), BF16),
                            pltpu.SemaphoreType.DMA(())]),
        out_shape=[out_c, out_l],
        compiler_params=pltpu.CompilerParams(vmem_limit_bytes=60 * 2**20),
        name="experts",
    )(idx_c, idx_l, h2_c, h2_l, sgate_c, sgate_l, w_gate, w_up, w_down)


SUBLANES = 8
COMBINE_CTX = (256, 56)
COMBINE_LAT = (128, 72)


def _combine_body(bs_ref, pos_ref, eo_ref, x1_ref, mod_ref, fn_ref, y_ref, rows, sem, *,
                  tb, cap, win, n_blocks, latent, tiles_per_seq):
    i = pl.program_id(0)
    slot = i % 2

    def window_base(e, blk):
        return (bs_ref[e, blk] // SUBLANES) * SUBLANES

    def window_start(e, blk, p):
        return jnp.minimum(window_base(e, blk) + p * win, cap - win)

    def fetch(blk, p, to_slot):
        for e in range(N_EXPERTS):
            start = pl.multiple_of(e * cap + window_start(e, blk, p), SUBLANES)
            pltpu.make_async_copy(eo_ref.at[pl.ds(start, win)], rows.at[to_slot, pl.ds(e * win, win)],
                                  sem.at[to_slot]).start()

    def wait(to_slot):
        pltpu.make_async_copy(eo_ref.at[pl.ds(0, N_EXPERTS * win)], rows.at[to_slot], sem.at[to_slot]).wait()

    @pl.when(i == 0)
    def _first():
        fetch(0, 0, 0)

    @pl.when(i + 1 < n_blocks)
    def _prefetch():
        fetch(i + 1, 0, 1 - slot)

    lane = lax.broadcasted_iota(I32, (1, N_EXPERTS), 1)
    col = lax.broadcasted_iota(I32, (1, (N_EXPERTS * win)), 1)
    col_row = (col % win).astype(F32)
    expand = (lax.broadcasted_iota(I32, (N_EXPERTS, (N_EXPERTS * win)), 0) == col // win).astype(BF16)
    widen = lambda v: jnp.dot(v, expand, preferred_element_type=F32)

    pos = pos_ref[...]

    def one_pass(p, acc):
        base = jnp.zeros((1, N_EXPERTS), I32)
        start = jnp.zeros((1, N_EXPERTS), I32)
        for e in range(N_EXPERTS):
            base = jnp.where(lane == e, window_base(e, i), base)
            start = jnp.where(lane == e, window_start(e, i, p), start)
        rel = pos - base
        mine = (pos >= 0) & (rel >= p * win) & (rel < (p + 1) * win)
        window_col = jnp.where(mine, pos - start, -1).astype(F32).astype(BF16)
        select = (widen(window_col) == col_row).astype(BF16)
        r_hi, r_lo = _split_bf16(rows[slot])
        return (acc + jnp.dot(select, r_hi, preferred_element_type=F32)
                + jnp.dot(select, r_lo, preferred_element_type=F32))

    wait(slot)
    moe = one_pass(0, jnp.zeros((tb, D_MODEL), F32))

    n_pass = jnp.int32(1)
    for e in range(N_EXPERTS):
        span = bs_ref[e, i + 1] - window_base(e, i)
        n_pass = jnp.maximum(n_pass, (span + win - 1) // win)

    def extra(p, acc):
        fetch(i, p, slot)
        wait(slot)
        return one_pass(p, acc)

    moe = lax.fori_loop(1, n_pass, extra, moe)
    row = 1 + i // tiles_per_seq if latent else 0
    gate2 = mod_ref[pl.ds(row, 1), 5 * D_MODEL:6 * D_MODEL]
    y_ref[...] = _rms(x1_ref[...] + gate2 * moe) * fn_ref[...]


def _combine(block_start, pos_t, expert_out, x1, mod, final_norm, cap, latent, seq_len):
    n = x1.shape[0]
    tb, win = COMBINE_LAT if latent else COMBINE_CTX
    n_blocks = n // tb
    return pl.pallas_call(
        functools.partial(_combine_body, tb=tb, cap=cap, win=win, n_blocks=n_blocks, latent=latent,
                          tiles_per_seq=seq_len // tb),
        grid_spec=pltpu.PrefetchScalarGridSpec(
            num_scalar_prefetch=1,
            grid=(n_blocks,),
            in_specs=[pl.BlockSpec((tb, N_EXPERTS), lambda i, bs: (i, 0)),
                      pl.BlockSpec(memory_space=pl.ANY),
                      pl.BlockSpec((tb, D_MODEL), lambda i, bs: (i, 0)),
                      pl.BlockSpec(mod.shape, lambda i, bs: (0, 0)),
                      pl.BlockSpec((1, D_MODEL), lambda i, bs: (0, 0))],
            out_specs=pl.BlockSpec((tb, D_MODEL), lambda i, bs: (i, 0)),
            scratch_shapes=[pltpu.VMEM((2, N_EXPERTS * win, D_MODEL), F32), pltpu.SemaphoreType.DMA((2,))]),
        out_shape=jax.ShapeDtypeStruct((n, D_MODEL), F32),
        compiler_params=pltpu.CompilerParams(vmem_limit_bytes=52 * 2**20),
        name="combine_lat" if latent else "combine_ctx",
    )(block_start, pos_t, expert_out, x1, mod, final_norm)


def _rope_tables(seq_len):
    pos = jnp.arange(seq_len)
    half = HEAD_DIM // 2
    freqs = ROPE_BASE ** (-jnp.arange(0, half, 2, dtype=F32) / half)
    ang_r = (pos // GRID_W).astype(F32)[:, None] * freqs[None, :]
    ang_c = (pos % GRID_W).astype(F32)[:, None] * freqs[None, :]
    cos = jnp.concatenate([jnp.cos(ang_r)] * 2 + [jnp.cos(ang_c)] * 2, axis=1)
    sin = jnp.concatenate([-jnp.sin(ang_r), jnp.sin(ang_r), -jnp.sin(ang_c), jnp.sin(ang_c)], axis=1)
    return cos, sin


def kernel(x_prompt, x_sample, cache_k, cache_v, c, c_ctx, w_ada, b_ada, norm1, w_in, attn_sink, w_s, b_s, g_v,
           g_attn, g_mlp, w_out, norm2, w_router, w_gate, w_up, w_down, final_norm):
    depth = w_ada.shape[0]
    assert depth == 1, "single-layer trunk"
    batch, seq, _ = x_prompt.shape
    dec_batch, dec_seq, _ = x_sample.shape
    n_ctx = batch * seq
    n_lat = dec_batch * dec_seq
    cap_c = CAPACITY_FACTOR * n_ctx // N_EXPERTS
    cap_l = CAPACITY_FACTOR * n_lat // N_EXPERTS
    l = 0

    cvec = jnp.concatenate([c_ctx[None, :], c, jnp.zeros((MOD_ROWS - 1 - dec_batch, D_MODEL), F32)], axis=0)
    mod = _adaln(cvec, w_ada[l], b_ada[l][None, :])

    w_in_b = w_in[l].astype(BF16)
    w_out_b = w_out[l].astype(BF16)
    w_s_b = w_s[l].astype(BF16)
    b_s_t = jnp.transpose(b_s[l])
    w_router_t = jnp.transpose(w_router[l])
    row = lambda a: a.reshape(1, -1)
    shared = (mod, row(norm1[l]), w_in_b, w_s_b, b_s_t, row(g_v[l]), row(g_attn[l]), row(g_mlp[l]), w_out_b,
              row(norm2[l]), w_router_t)

    xp = x_prompt.reshape(n_ctx, D_MODEL)
    xl = x_sample.reshape(n_lat, D_MODEL)

    x1_c, h2_c, afft_c, k, v = _mix(attn_sink[l], None, xp, *shared, batch, seq)
    state_k = k.reshape(batch, 1, seq, N_KV_HEADS, HEAD_DIM)
    state_v = v.reshape(batch, 1, seq, N_KV_HEADS, HEAD_DIM)

    q, k, v, u, gv = _inproj(xl, mod, row(norm1[l]), w_in_b, _rope_tables(dec_seq), dec_seq)
    projected = (q, k, v, cache_k[:, l].reshape(dec_batch, -1, KV_WIDTH),
                 cache_v[:, l].reshape(dec_batch, -1, KV_WIDTH), u, gv)
    x1_l, h2_l, afft_l = _mix(attn_sink[l], projected, xl, *shared, dec_batch, dec_seq)

    def route(aff_t, cap, combine_tile):
        n = aff_t.shape[1]
        pos, idx, slot_gate, start = _route(aff_t.reshape(N_EXPERTS, n // LANES, LANES), cap)
        start = start.reshape(N_EXPERTS, -1)[:, ::combine_tile // LANES].astype(I32)
        starts = jnp.concatenate([start, jnp.full((N_EXPERTS, 1), cap, I32)], axis=1)
        return jnp.transpose(pos.reshape(N_EXPERTS, n)), idx, jnp.transpose(slot_gate), starts

    pos_c, idx_c, sgate_c, starts_c = route(afft_c, cap_c, COMBINE_CTX[0])
    pos_l, idx_l, sgate_l, starts_l = route(afft_l, cap_l, COMBINE_LAT[0])
    slabs = lambda h: h.reshape(h.shape[0], ROW_TILES, LANES)
    eo_c, eo_l = _experts(idx_c, idx_l, slabs(h2_c), slabs(h2_l), sgate_c, sgate_l, w_gate[l], w_up[l], w_down[l])
    fn = row(final_norm)
    y_c = _combine(starts_c, pos_c, eo_c, x1_c, mod, fn, cap_c, False, seq)
    y_l = _combine(starts_l, pos_l, eo_l, x1_l, mod, fn, cap_l, True, dec_seq)

    return (y_c.reshape(batch, seq, D_MODEL), y_l.reshape(dec_batch, dec_seq, D_MODEL), state_k, state_v)
```

```python
import functools

import jax
import jax.numpy as jnp
from jax import lax
from jax.experimental import pallas as pl
from jax.experimental.pallas import tpu as pltpu

F32 = jnp.float32
BF16 = jnp.bfloat16
I32 = jnp.int32

D_MODEL = 2048
GRID_W = 64
HEAD_DIM = 128
N_Q_HEADS = 8
N_KV_HEADS = 2
Q_PER_KV = N_Q_HEADS // N_KV_HEADS
ATTN_WIDTH = N_Q_HEADS * HEAD_DIM
KV_WIDTH = N_KV_HEADS * HEAD_DIM
N_MLP_GROUPS = 8
MLP_WIDTH = 1024
Q_END = ATTN_WIDTH
K_END = Q_END + KV_WIDTH
V_END = K_END + KV_WIDTH
U_END = V_END + MLP_WIDTH
IN_WIDTH = U_END + MLP_WIDTH
CHUNK = 128
WINDOW = 128
N_EXPERTS = 16
EXPERT_FF = 2048
CAPACITY_FACTOR = 2
ROPE_BASE = 10000.0
EPS = 1e-6
NEG_INF = -1e30
ATTN_SCALE = HEAD_DIM ** -0.5

LANES = 128
MIX_TILE = 256
MIX_SUB = 2
LAT_SPAN = MIX_TILE + 2 * WINDOW
ROW_TILES = D_MODEL // LANES
MOD_ROWS = 8


def _rms(x):
    return x * lax.rsqrt(jnp.mean(x * x, axis=-1, keepdims=True) + EPS)


def _split_bf16_f32(x):
    hi = x.astype(BF16)
    return hi, x - hi.astype(F32)


def _split_bf16(x):
    hi, rest = _split_bf16_f32(x)
    return hi, rest.astype(BF16)


def _adaln_body(c_ref, w_ref, b_ref, o_ref):
    s = jax.nn.silu(c_ref[...])
    o_ref[...] = jnp.dot(s.astype(BF16), w_ref[...].astype(BF16),
                         preferred_element_type=F32) + b_ref[...]


def _adaln(cvec, w_ada, b_ada):
    tn = 1024
    width = w_ada.shape[1]
    return pl.pallas_call(
        _adaln_body,
        grid=(width // tn,),
        in_specs=[pl.BlockSpec((MOD_ROWS, D_MODEL), lambda j: (0, 0)),
                  pl.BlockSpec((D_MODEL, tn), lambda j: (0, j)),
                  pl.BlockSpec((1, tn), lambda j: (0, j))],
        out_specs=pl.BlockSpec((MOD_ROWS, tn), lambda j: (0, j)),
        out_shape=jax.ShapeDtypeStruct((MOD_ROWS, width), F32),
        compiler_params=pltpu.CompilerParams(vmem_limit_bytes=40 * 2**20),
        name="adaln",
    )(cvec, w_ada, b_ada)


def _rope(x, cos, sin, n_heads):
    lane = lax.broadcasted_iota(I32, (x.shape[0], HEAD_DIM), 1)
    first = (lane & 32) == 0
    outs = []
    for h in range(n_heads):
        xh = x[:, h * HEAD_DIM:(h + 1) * HEAD_DIM]
        partner = jnp.where(first, pltpu.roll(xh, 96, 1), pltpu.roll(xh, 32, 1))
        outs.append(xh * cos + partner * sin)
    return jnp.concatenate(outs, axis=1)


def _inproj_body(*refs, latent, tiles_per_seq):
    if latent:
        x_ref, mod_ref, g_ref, w_ref, cos_ref, sin_ref, q_ref, k_ref, v_ref, u_ref, gv_ref = refs
    else:
        x_ref, mod_ref, g_ref, w_ref, q_ref, k_ref, v_ref, u_ref, gv_ref = refs
    i = pl.program_id(0)
    row = 1 + i // tiles_per_seq if latent else 0
    shift = mod_ref[pl.ds(row, 1), 0:D_MODEL]
    scale = mod_ref[pl.ds(row, 1), D_MODEL:2 * D_MODEL]
    h = _rms(x_ref[...]) * g_ref[...] * (1 + scale) + shift
    z = jnp.dot(h.astype(BF16), w_ref[...], preferred_element_type=F32)
    q = z[:, 0:Q_END]
    k = z[:, Q_END:K_END]
    if latent:
        cos = cos_ref[...]
        sin = sin_ref[...]
        q = _rope(q, cos, sin, N_Q_HEADS)
        k = _rope(k, cos, sin, N_KV_HEADS)
    q_ref[...] = q.astype(BF16)
    k_ref[...] = k
    v_ref[...] = z[:, K_END:V_END]
    u_ref[...] = z[:, V_END:U_END]
    gv_ref[...] = z[:, U_END:IN_WIDTH]


def _inproj(x2d, mod, norm1, w_in_bf16, rope_tables, seq_len):
    n = x2d.shape[0]
    tm = 256
    latent = rope_tables is not None
    tiles_per_seq = seq_len // tm
    in_specs = [pl.BlockSpec((tm, D_MODEL), lambda i: (i, 0)),
                pl.BlockSpec(mod.shape, lambda i: (0, 0)),
                pl.BlockSpec((1, D_MODEL), lambda i: (0, 0)),
                pl.BlockSpec((D_MODEL, IN_WIDTH), lambda i: (0, 0))]
    args = [x2d, mod, norm1, w_in_bf16]
    if latent:
        in_specs += [pl.BlockSpec((tm, HEAD_DIM), lambda i: (i % tiles_per_seq, 0))] * 2
        args += list(rope_tables)
    widths = (ATTN_WIDTH, KV_WIDTH, KV_WIDTH, MLP_WIDTH, MLP_WIDTH)
    dtypes = (BF16, F32, F32, F32, F32)
    return pl.pallas_call(
        functools.partial(_inproj_body, latent=latent, tiles_per_seq=tiles_per_seq),
        grid=(n // tm,),
        in_specs=in_specs,
        out_specs=[pl.BlockSpec((tm, w), lambda i: (i, 0)) for w in widths],
        out_shape=[jax.ShapeDtypeStruct((n, w), dt) for w, dt in zip(widths, dtypes)],
        compiler_params=pltpu.CompilerParams(vmem_limit_bytes=56 * 2**20),
        name="inproj_lat" if latent else "inproj_ctx",
    )(*args)


def _mix_body(*refs, latent, seq_len):
    if latent:
        sink_ref, q_ref, k_ref, v_ref, ck_ref, cv_ref, u_ref, gv_ref, x_ref, mod_ref, *rest = refs
        proj_refs = (q_ref, k_ref, v_ref, ck_ref, cv_ref, u_ref, gv_ref)
    else:
        sink_ref, x_ref, mod_ref, norm1_ref, win_ref, *rest = refs
        proj_refs = (norm1_ref, win_ref) + tuple(rest[-2:])
        rest = rest[:-2]
    b = pl.program_id(0)
    t = pl.program_id(1)
    row = 1 + b if latent else 0
    for sub in range(MIX_SUB):
        _mix_tile(sub, t * MIX_SUB + sub, row, latent, seq_len, sink_ref, proj_refs, x_ref, mod_ref, *rest)


def _mix_tile(sub, tile, row, latent, seq_len, sink_ref, proj_refs, x_ref,
              mod_ref, ws_ref, bst_ref, gvn_ref, gattn_ref, gmlp_ref, wout_ref, norm2_ref, wrt_ref,
              x1_ref, h2_ref, afft_ref):
    tq = MIX_TILE
    rows = slice(sub * tq, (sub + 1) * tq)

    if latent:
        q_ref, k_ref, v_ref, ck_ref, cv_ref, u_ref, gv_ref = proj_refs
        q = q_ref[rows, :]
        u = u_ref[rows, :]
        gv = gv_ref[rows, :]
        ks = pl.multiple_of(jnp.clip(tile * tq - WINDOW, 0, seq_len - LAT_SPAN), WINDOW)
        kcat = jnp.concatenate([ck_ref[0], k_ref[pl.ds(ks, LAT_SPAN), :]], axis=0).astype(BF16)
        vcat = jnp.concatenate([cv_ref[0], v_ref[pl.ds(ks, LAT_SPAN), :]], axis=0).astype(BF16)
        n_ctx = ck_ref.shape[1]
        n_keys = n_ctx + LAT_SPAN
        kidx = lax.broadcasted_iota(I32, (tq, n_keys), 1)
        qpos = tile * tq + lax.broadcasted_iota(I32, (tq, n_keys), 0)
        kpos = ks + kidx - n_ctx
        mask = (kidx < n_ctx) | (jnp.abs(qpos - kpos) <= WINDOW)
    else:
        norm1_ref, win_ref, kout_ref, vout_ref = proj_refs
        shift1 = mod_ref[pl.ds(row, 1), 0:D_MODEL]
        scale1 = mod_ref[pl.ds(row, 1), D_MODEL:2 * D_MODEL]
        h1 = _rms(x_ref[rows, :]) * norm1_ref[...] * (1 + scale1) + shift1
        z = jnp.dot(h1.astype(BF16), win_ref[...], preferred_element_type=F32)
        q = z[:, 0:Q_END].astype(BF16)
        k = z[:, Q_END:K_END]
        v = z[:, K_END:V_END]
        u = z[:, V_END:U_END]
        gv = z[:, U_END:IN_WIDTH]
        kout_ref[rows, :] = k
        vout_ref[rows, :] = v
        kcat = k.astype(BF16)
        vcat = v.astype(BF16)
        mask = None

    heads = []
    for h in range(N_Q_HEADS):
        kv = h // Q_PER_KV
        qh = q[:, h * HEAD_DIM:(h + 1) * HEAD_DIM]
        kh = kcat[:, kv * HEAD_DIM:(kv + 1) * HEAD_DIM]
        vh = vcat[:, kv * HEAD_DIM:(kv + 1) * HEAD_DIM]
        s = lax.dot_general(qh, kh, (((1,), (1,)), ((), ())), preferred_element_type=F32) * ATTN_SCALE
        if mask is not None:
            s = jnp.where(mask, s, NEG_INF)
        sink = sink_ref[h]
        m = jnp.maximum(jnp.max(s, axis=1, keepdims=True), sink)
        p = jnp.exp(s - m)
        den = jnp.sum(p, axis=1, keepdims=True) + jnp.exp(sink - m)
        heads.append(jnp.dot(p.astype(BF16), vh, preferred_element_type=F32) / den)
    attn = jnp.concatenate(heads, axis=1)

    sg = (_rms(jax.nn.gelu(gv)) * gvn_ref[...]).astype(BF16)
    bst = bst_ref[...]
    chunks = []
    for c in range(tq // CHUNK):
        groups = []
        for g in range(N_MLP_GROUPS):
            blk = sg[c * CHUNK:(c + 1) * CHUNK, g * LANES:(g + 1) * LANES]
            groups.append(jnp.dot(ws_ref[g], blk, preferred_element_type=F32) + bst[:, g:g + 1])
        chunks.append(jnp.concatenate(groups, axis=1))
    mlp = jax.nn.gelu(u) * jnp.concatenate(chunks, axis=0)

    mixed = jnp.concatenate([_rms(attn) * gattn_ref[...], _rms(mlp) * gmlp_ref[...]], axis=1)
    out = jnp.dot(mixed.astype(BF16), wout_ref[...], preferred_element_type=F32)
    gate1 = mod_ref[pl.ds(row, 1), 2 * D_MODEL:3 * D_MODEL]
    x1 = x_ref[rows, :] + gate1 * out
    x1_ref[rows, :] = x1

    shift2 = mod_ref[pl.ds(row, 1), 3 * D_MODEL:4 * D_MODEL]
    scale2 = mod_ref[pl.ds(row, 1), 4 * D_MODEL:5 * D_MODEL]
    h2 = _rms(x1) * norm2_ref[...] * (1 + scale2) + shift2

    hh = h2.astype(BF16)
    hl = (h2 - hh.astype(F32)).astype(BF16)
    wrt = wrt_ref[...]
    wth = wrt.astype(BF16)
    wtl = (wrt - wth.astype(F32)).astype(BF16)
    nt = (((1,), (1,)), ((), ()))
    logits_t = (lax.dot_general(wth, hh, nt, preferred_element_type=F32)
                + lax.dot_general(wth, hl, nt, preferred_element_type=F32)
                + lax.dot_general(wtl, hh, nt, preferred_element_type=F32))
    pt = jnp.exp(logits_t - jnp.max(logits_t, axis=0, keepdims=True))
    afft_ref[:, rows] = pt / jnp.sum(pt, axis=0, keepdims=True)

    h2_ref[rows] = hh.reshape(tq, ROW_TILES, LANES)


def _mix(sink, projected, x2d, mod, norm1, w_in_bf16, w_s_bf16, b_s_t, g_v, g_attn, g_mlp, w_out_bf16, norm2,
         w_router_t, batch, seq_len):
    n = x2d.shape[0]
    latent = projected is not None
    block = MIX_SUB * MIX_TILE
    const2 = lambda b, t: (0, 0)
    resident = dict(pipeline_mode=pl.Buffered(1))
    if latent:
        grid = (batch, seq_len // block)
        tiles = grid[1]
        tok = lambda b, t: (b * tiles + t, 0)
        q, k, v, ck, cv, u, gv = projected
        seq = lambda b, t: (b, 0)
        in_specs = [pl.BlockSpec(memory_space=pltpu.SMEM),
                    pl.BlockSpec((block, ATTN_WIDTH), tok),
                    pl.BlockSpec((seq_len, KV_WIDTH), seq),
                    pl.BlockSpec((seq_len, KV_WIDTH), seq),
                    pl.BlockSpec((1,) + ck.shape[1:], lambda b, t: (b, 0, 0)),
                    pl.BlockSpec((1,) + cv.shape[1:], lambda b, t: (b, 0, 0)),
                    pl.BlockSpec((block, MLP_WIDTH), tok),
                    pl.BlockSpec((block, MLP_WIDTH), tok),
                    pl.BlockSpec((block, D_MODEL), tok),
                    pl.BlockSpec(mod.shape, const2)]
        args = [sink, q, k, v, ck, cv, u, gv, x2d, mod]
    else:
        assert seq_len == MIX_TILE
        grid = (batch // MIX_SUB, 1)
        tiles = 1
        tok = lambda b, t: (b, 0)
        in_specs = [pl.BlockSpec(memory_space=pltpu.SMEM),
                    pl.BlockSpec((block, D_MODEL), tok),
                    pl.BlockSpec(mod.shape, const2),
                    pl.BlockSpec((1, D_MODEL), const2),
                    pl.BlockSpec(w_in_bf16.shape, const2, **resident)]
        args = [sink, x2d, mod, norm1, w_in_bf16]
    in_specs += [pl.BlockSpec(w_s_bf16.shape, lambda b, t: (0, 0, 0)),
                 pl.BlockSpec(b_s_t.shape, const2),
                 pl.BlockSpec((1, MLP_WIDTH), const2),
                 pl.BlockSpec((1, ATTN_WIDTH), const2),
                 pl.BlockSpec((1, MLP_WIDTH), const2),
                 pl.BlockSpec(w_out_bf16.shape, const2, **resident),
                 pl.BlockSpec((1, D_MODEL), const2),
                 pl.BlockSpec(w_router_t.shape, const2)]
    args += [w_s_bf16, b_s_t, g_v, g_attn, g_mlp, w_out_bf16, norm2, w_router_t]
    out_specs = [pl.BlockSpec((block, D_MODEL), tok),
                 pl.BlockSpec((block, ROW_TILES, LANES), lambda b, t: tok(b, t) + (0,)),
                 pl.BlockSpec((N_EXPERTS, block), lambda b, t: (0, b * tiles + t))]
    out_shape = [jax.ShapeDtypeStruct((n, D_MODEL), F32),
                 jax.ShapeDtypeStruct((n, ROW_TILES, LANES), BF16),
                 jax.ShapeDtypeStruct((N_EXPERTS, n), F32)]
    if not latent:
        out_specs += [pl.BlockSpec((block, KV_WIDTH), tok)] * 2
        out_shape += [jax.ShapeDtypeStruct((n, KV_WIDTH), F32)] * 2
    return pl.pallas_call(
        functools.partial(_mix_body, latent=latent, seq_len=seq_len),
        grid=grid,
        in_specs=in_specs,
        out_specs=out_specs,
        out_shape=out_shape,
        compiler_params=pltpu.CompilerParams(vmem_limit_bytes=58 * 2**20),
        name="mix_lat" if latent else "mix_ctx",
    )(*args)


def _exclusive_counts(flags, upper, lower):
    n_e, nb, _ = flags.shape
    f = flags.astype(BF16)
    local = jnp.dot(f.reshape(n_e * nb, LANES), upper, preferred_element_type=F32).reshape(n_e, nb, LANES)
    totals = jnp.broadcast_to(jnp.sum(flags, axis=2, keepdims=True), flags.shape).astype(BF16)
    before = jnp.stack([jnp.dot(lower, totals[e], preferred_element_type=F32) for e in range(n_e)], axis=0)
    return local, before


def _route_body(aff_ref, pos_ref, idx_ref, sgate_ref, start_ref, local_ref, gate_ref, total_ref, *, cap):
    aff = aff_ref[...]
    n_e, nb, _ = aff.shape
    reduce_tokens = lambda x: jnp.sum(jnp.sum(x, axis=2, keepdims=True), axis=1, keepdims=True)

    def bisect(i, thr):
        cand = thr | jnp.left_shift(jnp.int32(1), 30 - i)
        n_ge = reduce_tokens((aff >= lax.bitcast_convert_type(cand, F32)).astype(F32))
        return jnp.where(n_ge >= cap, cand, thr)

    thr = lax.fori_loop(0, 31, bisect, jnp.zeros((n_e, 1, 1), I32))

    upper = (lax.broadcasted_iota(I32, (LANES, LANES), 0) < lax.broadcasted_iota(I32, (LANES, LANES), 1)).astype(BF16)
    lower = (lax.broadcasted_iota(I32, (nb, nb), 1) < lax.broadcasted_iota(I32, (nb, nb), 0)).astype(BF16)

    above = aff >= lax.bitcast_convert_type(thr + 1, F32)
    tied = (aff >= lax.bitcast_convert_type(thr, F32)) & jnp.logical_not(above)
    need = cap - reduce_tokens(above.astype(F32))
    tie_local, tie_before = _exclusive_counts(tied.astype(F32), upper, lower)
    sel = above | (tied & (tie_local + tie_before < need))
    sel_f = sel.astype(F32)

    local, before = _exclusive_counts(sel_f, upper, lower)
    pos_ref[...] = jnp.where(sel, (local + before).astype(I32), -1)
    gate_ref[...] = aff

    local_ref[...] = (local + sel_f).astype(BF16)
    start_ref[...] = before[:, :, 0:1]
    total_ref[...] = jnp.sum(sel_f, axis=2, keepdims=True)
    slot = lax.broadcasted_iota(I32, (1, cap), 1).astype(F32)
    block_id = lax.broadcasted_iota(I32, (nb, 1), 0).astype(F32)
    lane_id = lax.broadcasted_iota(I32, (LANES, 1), 0).astype(F32)
    tn = (((0,), (0,)), ((), ()))

    def lists(e, carry):
        start = start_ref[e]
        end = start + total_ref[e]
        blk = jnp.sum((end <= slot).astype(F32), axis=0, keepdims=True)
        onehot = block_id == blk
        onehot_b = onehot.astype(BF16)
        rank = slot - jnp.sum(jnp.where(onehot, start, 0.0), axis=0, keepdims=True)
        counts = lax.dot_general(local_ref[e], onehot_b, tn, preferred_element_type=F32)
        tok = jnp.sum((counts <= rank).astype(F32), axis=0, keepdims=True)
        idx_ref[pl.ds(e, 1), :] = (blk * LANES + tok).astype(I32)
        g1, rest = _split_bf16_f32(gate_ref[e])
        g2, rest = _split_bf16_f32(rest)
        block_gates = sum(lax.dot_general(g, onehot_b, tn, preferred_element_type=F32)
                          for g in (g1, g2, rest.astype(BF16)))
        sgate_ref[pl.ds(e, 1), :] = jnp.sum(jnp.where(lane_id == tok, block_gates, 0.0), axis=0, keepdims=True)
        return carry

    lax.fori_loop(0, n_e, lists, 0)


def _route(aff_blocks, cap):
    n_e, nb, _ = aff_blocks.shape
    return pl.pallas_call(
        functools.partial(_route_body, cap=cap),
        out_shape=[jax.ShapeDtypeStruct(aff_blocks.shape, I32), jax.ShapeDtypeStruct((n_e, cap), I32),
                   jax.ShapeDtypeStruct((n_e, cap), F32), jax.ShapeDtypeStruct((n_e, nb, 1), F32)],
        scratch_shapes=[pltpu.VMEM(aff_blocks.shape, BF16), pltpu.VMEM(aff_blocks.shape, F32),
                        pltpu.VMEM((n_e, nb, 1), F32)],
        compiler_params=pltpu.CompilerParams(vmem_limit_bytes=48 * 2**20),
        name="route",
    )(aff_blocks)


FF_TILE = 256
N_FF_TILES = EXPERT_FF // FF_TILE
OUT_TILE = 256
N_OUT_TILES = D_MODEL // OUT_TILE
EXPERT_STEPS = N_FF_TILES + N_OUT_TILES


def _expert_body(idxc_ref, idxl_ref, hc_ref, hl_ref, gc_ref, gl_ref, wg_ref, wu_ref, wd_ref,
                 oc_ref, ol_ref, gbuf, xb_ref, hid_ref, sem, *, cap_c, cap_l):
    e = pl.program_id(0)
    s = pl.program_id(1)
    cap = cap_c + cap_l
    cur = e % 2
    nxt = 1 - cur
    next_expert = (e + 1) % N_EXPERTS

    def issue(expert, part, parts):
        for idx_ref, src, n_rows, base in ((idxc_ref, hc_ref, cap_c, 0), (idxl_ref, hl_ref, cap_l, cap_c)):
            per = n_rows // parts
            for j in range(per):
                r = part * per + j
                pltpu.make_async_copy(src.at[idx_ref[expert * n_rows + r]], gbuf.at[base + r], sem).start()

    def arrived():
        pltpu.make_async_copy(hc_ref.at[pl.ds(0, cap)], gbuf, sem).wait()

    def unpack(slot, part, parts):
        per = cap // parts
        r0 = pl.multiple_of(part * per, per)
        xb_ref[slot, pl.ds(r0, per), :] = gbuf[pl.ds(r0, per)].reshape(per, D_MODEL)

    @pl.when((e == 0) & (s == 0))
    def _first():
        def gather(part, carry):
            issue(0, part, N_FF_TILES)
            return carry
        lax.fori_loop(0, N_FF_TILES, gather, 0)
        arrived()

        def relayout(part, carry):
            unpack(0, part, N_OUT_TILES)
            return carry
        lax.fori_loop(0, N_OUT_TILES, relayout, 0)

    @pl.when(s < N_FF_TILES)
    def _up():
        issue(next_expert, s, N_FF_TILES)
        x = xb_ref[cur]
        a = jnp.dot(x, wg_ref[0].astype(BF16), preferred_element_type=F32)
        b = jnp.dot(x, wu_ref[0].astype(BF16), preferred_element_type=F32)
        hid_ref[s] = (jax.nn.silu(a) * b).astype(BF16)

    @pl.when(s == N_FF_TILES)
    def _arrived():
        arrived()

    @pl.when(s >= N_FF_TILES)
    def _down():
        unpack(nxt, s - N_FF_TILES, N_OUT_TILES)
        wd = wd_ref[0].astype(BF16)
        acc = jnp.dot(hid_ref[0], wd[0:FF_TILE, :], preferred_element_type=F32)
        for j in range(1, N_FF_TILES):
            acc = acc + jnp.dot(hid_ref[j], wd[j * FF_TILE:(j + 1) * FF_TILE, :], preferred_element_type=F32)
        for g_ref, o_ref, rows in ((gc_ref, oc_ref, slice(0, cap_c)), (gl_ref, ol_ref, slice(cap_c, cap))):
            gates = g_ref[...]
            lane = lax.broadcasted_iota(I32, gates.shape, 1)
            gate = jnp.sum(jnp.where(lane == e, gates, 0.0), axis=1, keepdims=True)
            o_ref[...] = acc[rows, :] * gate


def _experts(idx_c, idx_l, h2_c, h2_l, sgate_c, sgate_l, w_gate, w_up, w_down):
    cap_c = idx_c.shape[1]
    cap_l = idx_l.shape[1]
    cap = cap_c + cap_l
    idx_c = idx_c.reshape(-1)
    idx_l = idx_l.reshape(-1)
    const = lambda e, s, ic, il: (0, 0)
    up_idx = lambda e, s, ic, il: (e, 0, jnp.minimum(s, N_FF_TILES - 1))
    down_idx = lambda e, s, ic, il: (e, 0, jnp.maximum(s - N_FF_TILES, 0))
    out_idx = lambda e, s, ic, il: (e, jnp.maximum(s - N_FF_TILES, 0))
    out_c = jax.ShapeDtypeStruct((N_EXPERTS * cap_c, D_MODEL), F32)
    out_l = jax.ShapeDtypeStruct((N_EXPERTS * cap_l, D_MODEL), F32)
    return pl.pallas_call(
        functools.partial(_expert_body, cap_c=cap_c, cap_l=cap_l),
        grid_spec=pltpu.PrefetchScalarGridSpec(
            num_scalar_prefetch=2,
            grid=(N_EXPERTS, EXPERT_STEPS),
            in_specs=[pl.BlockSpec(memory_space=pl.ANY),
                      pl.BlockSpec(memory_space=pl.ANY),
                      pl.BlockSpec(sgate_c.shape, const),
                      pl.BlockSpec(sgate_l.shape, const),
                      pl.BlockSpec((1, D_MODEL, FF_TILE), up_idx),
                      pl.BlockSpec((1, D_MODEL, FF_TILE), up_idx),
                      pl.BlockSpec((1, EXPERT_FF, OUT_TILE), down_idx)],
            out_specs=[pl.BlockSpec((cap_c, OUT_TILE), out_idx), pl.BlockSpec((cap_l, OUT_TILE), out_idx)],
            scratch_shapes=[pltpu.VMEM((cap, ROW_TILES, LANES), BF16),
                            pltpu.VMEM((2, cap, D_MODEL), BF16),
                            pltpu.VMEM((N_FF_TILES, cap, FF_TILE), BF16),
                            pltpu.SemaphoreType.DMA(())]),
        out_shape=[out_c, out_l],
        compiler_params=pltpu.CompilerParams(vmem_limit_bytes=60 * 2**20),
        name="experts",
    )(idx_c, idx_l, h2_c, h2_l, sgate_c, sgate_l, w_gate, w_up, w_down)


SUBLANES = 8
COMBINE_CTX = (256, 56)
COMBINE_LAT = (128, 72)


def _combine_body(bs_ref, pos_ref, eo_ref, x1_ref, mod_ref, fn_ref, y_ref, rows, sem, *,
                  tb, cap, win, n_blocks, latent, tiles_per_seq):
    i = pl.program_id(0)
    slot = i % 2

    def window_base(e, blk):
        return (bs_ref[e, blk] // SUBLANES) * SUBLANES

    def window_start(e, blk, p):
        return jnp.minimum(window_base(e, blk) + p * win, cap - win)

    def fetch(blk, p, to_slot):
        for e in range(N_EXPERTS):
            start = pl.multiple_of(e * cap + window_start(e, blk, p), SUBLANES)
            pltpu.make_async_copy(eo_ref.at[pl.ds(start, win)], rows.at[to_slot, pl.ds(e * win, win)],
                                  sem.at[to_slot]).start()

    def wait(to_slot):
        pltpu.make_async_copy(eo_ref.at[pl.ds(0, N_EXPERTS * win)], rows.at[to_slot], sem.at[to_slot]).wait()

    @pl.when(i == 0)
    def _first():
        fetch(0, 0, 0)

    @pl.when(i + 1 < n_blocks)
    def _prefetch():
        fetch(i + 1, 0, 1 - slot)

    lane = lax.broadcasted_iota(I32, (1, N_EXPERTS), 1)
    col = lax.broadcasted_iota(I32, (1, (N_EXPERTS * win)), 1)
    col_row = (col % win).astype(F32)
    expand = (lax.broadcasted_iota(I32, (N_EXPERTS, (N_EXPERTS * win)), 0) == col // win).astype(BF16)
    widen = lambda v: jnp.dot(v, expand, preferred_element_type=F32)

    pos = pos_ref[...]

    def one_pass(p, acc):
        base = jnp.zeros((1, N_EXPERTS), I32)
        start = jnp.zeros((1, N_EXPERTS), I32)
        for e in range(N_EXPERTS):
            base = jnp.where(lane == e, window_base(e, i), base)
            start = jnp.where(lane == e, window_start(e, i, p), start)
        rel = pos - base
        mine = (pos >= 0) & (rel >= p * win) & (rel < (p + 1) * win)
        window_col = jnp.where(mine, pos - start, -1).astype(F32).astype(BF16)
        select = (widen(window_col) == col_row).astype(BF16)
        r_hi, r_lo = _split_bf16(rows[slot])
        return (acc + jnp.dot(select, r_hi, preferred_element_type=F32)
                + jnp.dot(select, r_lo, preferred_element_type=F32))

    wait(slot)
    moe = one_pass(0, jnp.zeros((tb, D_MODEL), F32))

    n_pass = jnp.int32(1)
    for e in range(N_EXPERTS):
        span = bs_ref[e, i + 1] - window_base(e, i)
        n_pass = jnp.maximum(n_pass, (span + win - 1) // win)

    def extra(p, acc):
        fetch(i, p, slot)
        wait(slot)
        return one_pass(p, acc)

    moe = lax.fori_loop(1, n_pass, extra, moe)
    row = 1 + i // tiles_per_seq if latent else 0
    gate2 = mod_ref[pl.ds(row, 1), 5 * D_MODEL:6 * D_MODEL]
    y_ref[...] = _rms(x1_ref[...] + gate2 * moe) * fn_ref[...]


def _combine(block_start, pos_t, expert_out, x1, mod, final_norm, cap, latent, seq_len):
    n = x1.shape[0]
    tb, win = COMBINE_LAT if latent else COMBINE_CTX
    n_blocks = n // tb
    return pl.pallas_call(
        functools.partial(_combine_body, tb=tb, cap=cap, win=win, n_blocks=n_blocks, latent=latent,
                          tiles_per_seq=seq_len // tb),
        grid_spec=pltpu.PrefetchScalarGridSpec(
            num_scalar_prefetch=1,
            grid=(n_blocks,),
            in_specs=[pl.BlockSpec((tb, N_EXPERTS), lambda i, bs: (i, 0)),
                      pl.BlockSpec(memory_space=pl.ANY),
                      pl.BlockSpec((tb, D_MODEL), lambda i, bs: (i, 0)),
                      pl.BlockSpec(mod.shape, lambda i, bs: (0, 0)),
                      pl.BlockSpec((1, D_MODEL), lambda i, bs: (0, 0))],
            out_specs=pl.BlockSpec((tb, D_MODEL), lambda i, bs: (i, 0)),
            scratch_shapes=[pltpu.VMEM((2, N_EXPERTS * win, D_MODEL), F32), pltpu.SemaphoreType.DMA((2,))]),
        out_shape=jax.ShapeDtypeStruct((n, D_MODEL), F32),
        compiler_params=pltpu.CompilerParams(vmem_limit_bytes=52 * 2**20),
        name="combine_lat" if latent else "combine_ctx",
    )(block_start, pos_t, expert_out, x1, mod, final_norm)


def _rope_tables(seq_len):
    pos = jnp.arange(seq_len)
    half = HEAD_DIM // 2
    freqs = ROPE_BASE ** (-jnp.arange(0, half, 2, dtype=F32) / half)
    ang_r = (pos // GRID_W).astype(F32)[:, None] * freqs[None, :]
    ang_c = (pos % GRID_W).astype(F32)[:, None] * freqs[None, :]
    cos = jnp.concatenate([jnp.cos(ang_r)] * 2 + [jnp.cos(ang_c)] * 2, axis=1)
    sin = jnp.concatenate([-jnp.sin(ang_r), jnp.sin(ang_r), -jnp.sin(ang_c), jnp.sin(ang_c)], axis=1)
    return cos, sin


def kernel(x_prompt, x_sample, cache_k, cache_v, c, c_ctx, w_ada, b_ada, norm1, w_in, attn_sink, w_s, b_s, g_v,
           g_attn, g_mlp, w_out, norm2, w_router, w_gate, w_up, w_down, final_norm):
    depth = w_ada.shape[0]
    assert depth == 1, "single-layer trunk"
    batch, seq, _ = x_prompt.shape
    dec_batch, dec_seq, _ = x_sample.shape
    n_ctx = batch * seq
    n_lat = dec_batch * dec_seq
    cap_c = CAPACITY_FACTOR * n_ctx // N_EXPERTS
    cap_l = CAPACITY_FACTOR * n_lat // N_EXPERTS
    l = 0

    cvec = jnp.concatenate([c_ctx[None, :], c, jnp.zeros((MOD_ROWS - 1 - dec_batch, D_MODEL), F32)], axis=0)
    mod = _adaln(cvec, w_ada[l], b_ada[l][None, :])

    w_in_b = w_in[l].astype(BF16)
    w_out_b = w_out[l].astype(BF16)
    w_s_b = w_s[l].astype(BF16)
    b_s_t = jnp.transpose(b_s[l])
    w_router_t = jnp.transpose(w_router[l])
    row = lambda a: a.reshape(1, -1)
    shared = (mod, row(norm1[l]), w_in_b, w_s_b, b_s_t, row(g_v[l]), row(g_attn[l]), row(g_mlp[l]), w_out_b,
              row(norm2[l]), w_router_t)

    xp = x_prompt.reshape(n_ctx, D_MODEL)
    xl = x_sample.reshape(n_lat, D_MODEL)

    x1_c, h2_c, afft_c, k, v = _mix(attn_sink[l], None, xp, *shared, batch, seq)
    state_k = k.reshape(batch, 1, seq, N_KV_HEADS, HEAD_DIM)
    state_v = v.reshape(batch, 1, seq, N_KV_HEADS, HEAD_DIM)

    q, k, v, u, gv = _inproj(xl, mod, row(norm1[l]), w_in_b, _rope_tables(dec_seq), dec_seq)
    projected = (q, k, v, cache_k[:, l].reshape(dec_batch, -1, KV_WIDTH),
                 cache_v[:, l].reshape(dec_batch, -1, KV_WIDTH), u, gv)
    x1_l, h2_l, afft_l = _mix(attn_sink[l], projected, xl, *shared, dec_batch, dec_seq)

    def route(aff_t, cap, combine_tile):
        n = aff_t.shape[1]
        pos, idx, slot_gate, start = _route(aff_t.reshape(N_EXPERTS, n // LANES, LANES), cap)
        start = start.reshape(N_EXPERTS, -1)[:, ::combine_tile // LANES].astype(I32)
        starts = jnp.concatenate([start, jnp.full((N_EXPERTS, 1), cap, I32)], axis=1)
        return jnp.transpose(pos.reshape(N_EXPERTS, n)), idx, jnp.transpose(slot_gate), starts

    pos_c, idx_c, sgate_c, starts_c = route(afft_c, cap_c, COMBINE_CTX[0])
    pos_l, idx_l, sgate_l, starts_l = route(afft_l, cap_l, COMBINE_LAT[0])
    eo_c, eo_l = _experts(idx_c, idx_l, h2_c, h2_l, sgate_c, sgate_l, w_gate[l], w_up[l], w_down[l])
    fn = row(final_norm)
    y_c = _combine(starts_c, pos_c, eo_c, x1_c, mod, fn, cap_c, False, seq)
    y_l = _combine(starts_l, pos_l, eo_l, x1_l, mod, fn, cap_l, True, dec_seq)

    return (y_c.reshape(batch, seq, D_MODEL), y_l.reshape(dec_batch, dec_seq, D_MODEL), state_k, state_v)
```

```python
import functools

import jax
import jax.numpy as jnp
from jax import lax
from jax.experimental import pallas as pl
from jax.experimental.pallas import tpu as pltpu

F32 = jnp.float32
BF16 = jnp.bfloat16
I32 = jnp.int32

D_MODEL = 2048
GRID_W = 64
HEAD_DIM = 128
N_Q_HEADS = 8
N_KV_HEADS = 2
Q_PER_KV = N_Q_HEADS // N_KV_HEADS
ATTN_WIDTH = N_Q_HEADS * HEAD_DIM
KV_WIDTH = N_KV_HEADS * HEAD_DIM
N_MLP_GROUPS = 8
MLP_WIDTH = 1024
Q_END = ATTN_WIDTH
K_END = Q_END + KV_WIDTH
V_END = K_END + KV_WIDTH
U_END = V_END + MLP_WIDTH
IN_WIDTH = U_END + MLP_WIDTH
CHUNK = 128
WINDOW = 128
N_EXPERTS = 16
EXPERT_FF = 2048
CAPACITY_FACTOR = 2
ROPE_BASE = 10000.0
EPS = 1e-6
NEG_INF = -1e30
ATTN_SCALE = HEAD_DIM ** -0.5

LANES = 128
MIX_TILE = 256
MIX_SUB = 2
LAT_SPAN = MIX_TILE + 2 * WINDOW
ROW_TILES = D_MODEL // LANES
MOD_ROWS = 8


def _rms(x):
    return x * lax.rsqrt(jnp.mean(x * x, axis=-1, keepdims=True) + EPS)


def _split_bf16_f32(x):
    hi = x.astype(BF16)
    return hi, x - hi.astype(F32)


def _split_bf16(x):
    hi, rest = _split_bf16_f32(x)
    return hi, rest.astype(BF16)


def _adaln_body(c_ref, w_ref, b_ref, o_ref):
    s = jax.nn.silu(c_ref[...])
    o_ref[...] = jnp.dot(s.astype(BF16), w_ref[...].astype(BF16),
                         preferred_element_type=F32) + b_ref[...]


def _adaln(cvec, w_ada, b_ada):
    tn = 1024
    width = w_ada.shape[1]
    return pl.pallas_call(
        _adaln_body,
        grid=(width // tn,),
        in_specs=[pl.BlockSpec((MOD_ROWS, D_MODEL), lambda j: (0, 0)),
                  pl.BlockSpec((D_MODEL, tn), lambda j: (0, j)),
                  pl.BlockSpec((1, tn), lambda j: (0, j))],
        out_specs=pl.BlockSpec((MOD_ROWS, tn), lambda j: (0, j)),
        out_shape=jax.ShapeDtypeStruct((MOD_ROWS, width), F32),
        compiler_params=pltpu.CompilerParams(vmem_limit_bytes=40 * 2**20),
        name="adaln",
    )(cvec, w_ada, b_ada)


def _rope(x, cos, sin, n_heads):
    lane = lax.broadcasted_iota(I32, (x.shape[0], HEAD_DIM), 1)
    first = (lane & 32) == 0
    outs = []
    for h in range(n_heads):
        xh = x[:, h * HEAD_DIM:(h + 1) * HEAD_DIM]
        partner = jnp.where(first, pltpu.roll(xh, 96, 1), pltpu.roll(xh, 32, 1))
        outs.append(xh * cos + partner * sin)
    return jnp.concatenate(outs, axis=1)


def _inproj_body(*refs, latent, tiles_per_seq):
    if latent:
        x_ref, mod_ref, g_ref, w_ref, cos_ref, sin_ref, q_ref, k_ref, v_ref, u_ref, gv_ref = refs
    else:
        x_ref, mod_ref, g_ref, w_ref, q_ref, k_ref, v_ref, u_ref, gv_ref = refs
    i = pl.program_id(0)
    row = 1 + i // tiles_per_seq if latent else 0
    shift = mod_ref[pl.ds(row, 1), 0:D_MODEL]
    scale = mod_ref[pl.ds(row, 1), D_MODEL:2 * D_MODEL]
    h = _rms(x_ref[...]) * g_ref[...] * (1 + scale) + shift
    z = jnp.dot(h.astype(BF16), w_ref[...], preferred_element_type=F32)
    q = z[:, 0:Q_END]
    k = z[:, Q_END:K_END]
    if latent:
        cos = cos_ref[...]
        sin = sin_ref[...]
        q = _rope(q, cos, sin, N_Q_HEADS)
        k = _rope(k, cos, sin, N_KV_HEADS)
    q_ref[...] = q.astype(BF16)
    k_ref[...] = k
    v_ref[...] = z[:, K_END:V_END]
    u_ref[...] = z[:, V_END:U_END]
    gv_ref[...] = z[:, U_END:IN_WIDTH]


def _inproj(x2d, mod, norm1, w_in_bf16, rope_tables, seq_len):
    n = x2d.shape[0]
    tm = 256
    latent = rope_tables is not None
    tiles_per_seq = seq_len // tm
    in_specs = [pl.BlockSpec((tm, D_MODEL), lambda i: (i, 0)),
                pl.BlockSpec(mod.shape, lambda i: (0, 0)),
                pl.BlockSpec((1, D_MODEL), lambda i: (0, 0)),
                pl.BlockSpec((D_MODEL, IN_WIDTH), lambda i: (0, 0))]
    args = [x2d, mod, norm1, w_in_bf16]
    if latent:
        in_specs += [pl.BlockSpec((tm, HEAD_DIM), lambda i: (i % tiles_per_seq, 0))] * 2
        args += list(rope_tables)
    widths = (ATTN_WIDTH, KV_WIDTH, KV_WIDTH, MLP_WIDTH, MLP_WIDTH)
    dtypes = (BF16, F32, F32, F32, F32)
    return pl.pallas_call(
        functools.partial(_inproj_body, latent=latent, tiles_per_seq=tiles_per_seq),
        grid=(n // tm,),
        in_specs=in_specs,
        out_specs=[pl.BlockSpec((tm, w), lambda i: (i, 0)) for w in widths],
        out_shape=[jax.ShapeDtypeStruct((n, w), dt) for w, dt in zip(widths, dtypes)],
        compiler_params=pltpu.CompilerParams(vmem_limit_bytes=56 * 2**20),
        name="inproj_lat" if latent else "inproj_ctx",
    )(*args)


def _mix_body(*refs, latent, seq_len):
    if latent:
        sink_ref, q_ref, k_ref, v_ref, ck_ref, cv_ref, u_ref, gv_ref, x_ref, mod_ref, *rest = refs
        proj_refs = (q_ref, k_ref, v_ref, ck_ref, cv_ref, u_ref, gv_ref)
    else:
        sink_ref, x_ref, mod_ref, norm1_ref, win_ref, *rest = refs
        proj_refs = (norm1_ref, win_ref) + tuple(rest[-2:])
        rest = rest[:-2]
    b = pl.program_id(0)
    t = pl.program_id(1)
    row = 1 + b if latent else 0
    for sub in range(MIX_SUB):
        _mix_tile(sub, t * MIX_SUB + sub, row, latent, seq_len, sink_ref, proj_refs, x_ref, mod_ref, *rest)


def _mix_tile(sub, tile, row, latent, seq_len, sink_ref, proj_refs, x_ref,
              mod_ref, ws_ref, bst_ref, gvn_ref, gattn_ref, gmlp_ref, wout_ref, norm2_ref, wrt_ref,
              x1_ref, h2_ref, afft_ref):
    tq = MIX_TILE
    rows = slice(sub * tq, (sub + 1) * tq)

    if latent:
        q_ref, k_ref, v_ref, ck_ref, cv_ref, u_ref, gv_ref = proj_refs
        q = q_ref[rows, :]
        u = u_ref[rows, :]
        gv = gv_ref[rows, :]
        ks = pl.multiple_of(jnp.clip(tile * tq - WINDOW, 0, seq_len - LAT_SPAN), WINDOW)
        kcat = jnp.concatenate([ck_ref[0], k_ref[pl.ds(ks, LAT_SPAN), :]], axis=0).astype(BF16)
        vcat = jnp.concatenate([cv_ref[0], v_ref[pl.ds(ks, LAT_SPAN), :]], axis=0).astype(BF16)
        n_ctx = ck_ref.shape[1]
        n_keys = n_ctx + LAT_SPAN
        kidx = lax.broadcasted_iota(I32, (tq, n_keys), 1)
        qpos = tile * tq + lax.broadcasted_iota(I32, (tq, n_keys), 0)
        kpos = ks + kidx - n_ctx
        mask = (kidx < n_ctx) | (jnp.abs(qpos - kpos) <= WINDOW)
    else:
        norm1_ref, win_ref, kout_ref, vout_ref = proj_refs
        shift1 = mod_ref[pl.ds(row, 1), 0:D_MODEL]
        scale1 = mod_ref[pl.ds(row, 1), D_MODEL:2 * D_MODEL]
        h1 = _rms(x_ref[rows, :]) * norm1_ref[...] * (1 + scale1) + shift1
        z = jnp.dot(h1.astype(BF16), win_ref[...], preferred_element_type=F32)
        q = z[:, 0:Q_END].astype(BF16)
        k = z[:, Q_END:K_END]
        v = z[:, K_END:V_END]
        u = z[:, V_END:U_END]
        gv = z[:, U_END:IN_WIDTH]
        kout_ref[rows, :] = k
        vout_ref[rows, :] = v
        kcat = k.astype(BF16)
        vcat = v.astype(BF16)
        mask = None

    heads = []
    for h in range(N_Q_HEADS):
        kv = h // Q_PER_KV
        qh = q[:, h * HEAD_DIM:(h + 1) * HEAD_DIM]
        kh = kcat[:, kv * HEAD_DIM:(kv + 1) * HEAD_DIM]
        vh = vcat[:, kv * HEAD_DIM:(kv + 1) * HEAD_DIM]
        s = lax.dot_general(qh, kh, (((1,), (1,)), ((), ())), preferred_element_type=F32) * ATTN_SCALE
        if mask is not None:
            s = jnp.where(mask, s, NEG_INF)
        sink = sink_ref[h]
        m = jnp.maximum(jnp.max(s, axis=1, keepdims=True), sink)
        p = jnp.exp(s - m)
        den = jnp.sum(p, axis=1, keepdims=True) + jnp.exp(sink - m)
        heads.append(jnp.dot(p.astype(BF16), vh, preferred_element_type=F32) / den)
    attn = jnp.concatenate(heads, axis=1)

    sg = (_rms(jax.nn.gelu(gv)) * gvn_ref[...]).astype(BF16)
    bst = bst_ref[...]
    chunks = []
    for c in range(tq // CHUNK):
        groups = []
        for g in range(N_MLP_GROUPS):
            blk = sg[c * CHUNK:(c + 1) * CHUNK, g * LANES:(g + 1) * LANES]
            groups.append(jnp.dot(ws_ref[g], blk, preferred_element_type=F32) + bst[:, g:g + 1])
        chunks.append(jnp.concatenate(groups, axis=1))
    mlp = jax.nn.gelu(u) * jnp.concatenate(chunks, axis=0)

    mixed = jnp.concatenate([_rms(attn) * gattn_ref[...], _rms(mlp) * gmlp_ref[...]], axis=1)
    out = jnp.dot(mixed.astype(BF16), wout_ref[...], preferred_element_type=F32)
    gate1 = mod_ref[pl.ds(row, 1), 2 * D_MODEL:3 * D_MODEL]
    x1 = x_ref[rows, :] + gate1 * out
    x1_ref[rows, :] = x1

    shift2 = mod_ref[pl.ds(row, 1), 3 * D_MODEL:4 * D_MODEL]
    scale2 = mod_ref[pl.ds(row, 1), 4 * D_MODEL:5 * D_MODEL]
    h2 = _rms(x1) * norm2_ref[...] * (1 + scale2) + shift2

    hh = h2.astype(BF16)
    hl = (h2 - hh.astype(F32)).astype(BF16)
    wrt = wrt_ref[...]
    wth = wrt.astype(BF16)
    wtl = (wrt - wth.astype(F32)).astype(BF16)
    nt = (((1,), (1,)), ((), ()))
    logits_t = (lax.dot_general(wth, hh, nt, preferred_element_type=F32)
                + lax.dot_general(wth, hl, nt, preferred_element_type=F32)
                + lax.dot_general(wtl, hh, nt, preferred_element_type=F32))
    pt = jnp.exp(logits_t - jnp.max(logits_t, axis=0, keepdims=True))
    afft_ref[:, rows] = pt / jnp.sum(pt, axis=0, keepdims=True)

    h2_ref[rows] = hh.reshape(tq, ROW_TILES, LANES)


def _mix(sink, projected, x2d, mod, norm1, w_in_bf16, w_s_bf16, b_s_t, g_v, g_attn, g_mlp, w_out_bf16, norm2,
         w_router_t, batch, seq_len):
    n = x2d.shape[0]
    latent = projected is not None
    block = MIX_SUB * MIX_TILE
    const2 = lambda b, t: (0, 0)
    resident = dict(pipeline_mode=pl.Buffered(1))
    if latent:
        grid = (batch, seq_len // block)
        tiles = grid[1]
        tok = lambda b, t: (b * tiles + t, 0)
        q, k, v, ck, cv, u, gv = projected
        seq = lambda b, t: (b, 0)
        in_specs = [pl.BlockSpec(memory_space=pltpu.SMEM),
                    pl.BlockSpec((block, ATTN_WIDTH), tok),
                    pl.BlockSpec((seq_len, KV_WIDTH), seq),
                    pl.BlockSpec((seq_len, KV_WIDTH), seq),
                    pl.BlockSpec((1,) + ck.shape[1:], lambda b, t: (b, 0, 0)),
                    pl.BlockSpec((1,) + cv.shape[1:], lambda b, t: (b, 0, 0)),
                    pl.BlockSpec((block, MLP_WIDTH), tok),
                    pl.BlockSpec((block, MLP_WIDTH), tok),
                    pl.BlockSpec((block, D_MODEL), tok),
                    pl.BlockSpec(mod.shape, const2)]
        args = [sink, q, k, v, ck, cv, u, gv, x2d, mod]
    else:
        assert seq_len == MIX_TILE
        grid = (batch // MIX_SUB, 1)
        tiles = 1
        tok = lambda b, t: (b, 0)
        in_specs = [pl.BlockSpec(memory_space=pltpu.SMEM),
                    pl.BlockSpec((block, D_MODEL), tok),
                    pl.BlockSpec(mod.shape, const2),
                    pl.BlockSpec((1, D_MODEL), const2),
                    pl.BlockSpec(w_in_bf16.shape, const2, **resident)]
        args = [sink, x2d, mod, norm1, w_in_bf16]
    in_specs += [pl.BlockSpec(w_s_bf16.shape, lambda b, t: (0, 0, 0)),
                 pl.BlockSpec(b_s_t.shape, const2),
                 pl.BlockSpec((1, MLP_WIDTH), const2),
                 pl.BlockSpec((1, ATTN_WIDTH), const2),
                 pl.BlockSpec((1, MLP_WIDTH), const2),
                 pl.BlockSpec(w_out_bf16.shape, const2, **resident),
                 pl.BlockSpec((1, D_MODEL), const2),
                 pl.BlockSpec(w_router_t.shape, const2)]
    args += [w_s_bf16, b_s_t, g_v, g_attn, g_mlp, w_out_bf16, norm2, w_router_t]
    out_specs = [pl.BlockSpec((block, D_MODEL), tok),
                 pl.BlockSpec((block, ROW_TILES, LANES), lambda b, t: tok(b, t) + (0,)),
                 pl.BlockSpec((N_EXPERTS, block), lambda b, t: (0, b * tiles + t))]
    out_shape = [jax.ShapeDtypeStruct((n, D_MODEL), F32),
                 jax.ShapeDtypeStruct((n, ROW_TILES, LANES), BF16),
                 jax.ShapeDtypeStruct((N_EXPERTS, n), F32)]
    if not latent:
        out_specs += [pl.BlockSpec((block, KV_WIDTH), tok)] * 2
        out_shape += [jax.ShapeDtypeStruct((n, KV_WIDTH), F32)] * 2
    return pl.pallas_call(
        functools.partial(_mix_body, latent=latent, seq_len=seq_len),
        grid=grid,
        in_specs=in_specs,
        out_specs=out_specs,
        out_shape=out_shape,
        compiler_params=pltpu.CompilerParams(vmem_limit_bytes=58 * 2**20),
        name="mix_lat" if latent else "mix_ctx",
    )(*args)


def _exclusive_counts(flags, upper, lower):
    n_e, nb, _ = flags.shape
    f = flags.astype(BF16)
    local = jnp.dot(f.reshape(n_e * nb, LANES), upper, preferred_element_type=F32).reshape(n_e, nb, LANES)
    totals = jnp.broadcast_to(jnp.sum(flags, axis=2, keepdims=True), flags.shape).astype(BF16)
    before = jnp.stack([jnp.dot(lower, totals[e], preferred_element_type=F32) for e in range(n_e)], axis=0)
    return local, before


def _route_body(aff_ref, pos_ref, idx_ref, sgate_ref, start_ref, local_ref, gate_ref, total_ref, *, cap):
    aff = aff_ref[...]
    n_e, nb, _ = aff.shape
    reduce_tokens = lambda x: jnp.sum(jnp.sum(x, axis=2, keepdims=True), axis=1, keepdims=True)

    def bisect(i, thr):
        cand = thr | jnp.left_shift(jnp.int32(1), 30 - i)
        n_ge = reduce_tokens((aff >= lax.bitcast_convert_type(cand, F32)).astype(F32))
        return jnp.where(n_ge >= cap, cand, thr)

    thr = lax.fori_loop(0, 31, bisect, jnp.zeros((n_e, 1, 1), I32))

    upper = (lax.broadcasted_iota(I32, (LANES, LANES), 0) < lax.broadcasted_iota(I32, (LANES, LANES), 1)).astype(BF16)
    lower = (lax.broadcasted_iota(I32, (nb, nb), 1) < lax.broadcasted_iota(I32, (nb, nb), 0)).astype(BF16)

    above = aff >= lax.bitcast_convert_type(thr + 1, F32)
    tied = (aff >= lax.bitcast_convert_type(thr, F32)) & jnp.logical_not(above)
    need = cap - reduce_tokens(above.astype(F32))
    tie_local, tie_before = _exclusive_counts(tied.astype(F32), upper, lower)
    sel = above | (tied & (tie_local + tie_before < need))
    sel_f = sel.astype(F32)

    local, before = _exclusive_counts(sel_f, upper, lower)
    pos_ref[...] = jnp.where(sel, (local + before).astype(I32), -1)
    gate_ref[...] = aff

    local_ref[...] = (local + sel_f).astype(BF16)
    start_ref[...] = before[:, :, 0:1]
    total_ref[...] = jnp.sum(sel_f, axis=2, keepdims=True)
    slot = lax.broadcasted_iota(I32, (1, cap), 1).astype(F32)
    block_id = lax.broadcasted_iota(I32, (nb, 1), 0).astype(F32)
    lane_id = lax.broadcasted_iota(I32, (LANES, 1), 0).astype(F32)
    tn = (((0,), (0,)), ((), ()))

    def lists(e, carry):
        start = start_ref[e]
        end = start + total_ref[e]
        blk = jnp.sum((end <= slot).astype(F32), axis=0, keepdims=True)
        onehot = block_id == blk
        onehot_b = onehot.astype(BF16)
        rank = slot - jnp.sum(jnp.where(onehot, start, 0.0), axis=0, keepdims=True)
        counts = lax.dot_general(local_ref[e], onehot_b, tn, preferred_element_type=F32)
        tok = jnp.sum((counts <= rank).astype(F32), axis=0, keepdims=True)
        idx_ref[pl.ds(e, 1), :] = (blk * LANES + tok).astype(I32)
        g1, rest = _split_bf16_f32(gate_ref[e])
        g2, rest = _split_bf16_f32(rest)
        block_gates = sum(lax.dot_general(g, onehot_b, tn, preferred_element_type=F32)
                          for g in (g1, g2, rest.astype(BF16)))
        sgate_ref[pl.ds(e, 1), :] = jnp.sum(jnp.where(lane_id == tok, block_gates, 0.0), axis=0, keepdims=True)
        return carry

    lax.fori_loop(0, n_e, lists, 0)


def _route(aff_blocks, cap):
    n_e, nb, _ = aff_blocks.shape
    return pl.pallas_call(
        functools.partial(_route_body, cap=cap),
        out_shape=[jax.ShapeDtypeStruct(aff_blocks.shape, I32), jax.ShapeDtypeStruct((n_e, cap), I32),
                   jax.ShapeDtypeStruct((n_e, cap), F32), jax.ShapeDtypeStruct((n_e, nb, 1), F32)],
        scratch_shapes=[pltpu.VMEM(aff_blocks.shape, BF16), pltpu.VMEM(aff_blocks.shape, F32),
                        pltpu.VMEM((n_e, nb, 1), F32)],
        compiler_params=pltpu.CompilerParams(vmem_limit_bytes=48 * 2**20),
        name="route",
    )(aff_blocks)


FF_TILE = 512
N_FF_TILES = EXPERT_FF // FF_TILE
OUT_TILE = 256
N_OUT_TILES = D_MODEL // OUT_TILE
EXPERT_STEPS = N_FF_TILES + N_OUT_TILES


def _expert_body(idxc_ref, idxl_ref, hc_ref, hl_ref, gc_ref, gl_ref, wg_ref, wu_ref, wd_ref,
                 oc_ref, ol_ref, gbuf, xb_ref, hid_ref, sem, *, cap_c, cap_l):
    e = pl.program_id(0)
    s = pl.program_id(1)
    cap = cap_c + cap_l
    cur = e % 2
    nxt = 1 - cur
    next_expert = (e + 1) % N_EXPERTS

    def issue(expert, part, parts):
        for idx_ref, src, n_rows, base in ((idxc_ref, hc_ref, cap_c, 0), (idxl_ref, hl_ref, cap_l, cap_c)):
            per = n_rows // parts
            for j in range(per):
                r = part * per + j
                pltpu.make_async_copy(src.at[idx_ref[expert * n_rows + r]], gbuf.at[base + r], sem).start()

    def arrived():
        pltpu.make_async_copy(hc_ref.at[pl.ds(0, cap)], gbuf, sem).wait()

    def unpack(slot, part, parts):
        per = cap // parts
        r0 = pl.multiple_of(part * per, per)
        xb_ref[slot, pl.ds(r0, per), :] = gbuf[pl.ds(r0, per)].reshape(per, D_MODEL)

    @pl.when((e == 0) & (s == 0))
    def _first():
        def gather(part, carry):
            issue(0, part, N_FF_TILES)
            return carry
        lax.fori_loop(0, N_FF_TILES, gather, 0)
        arrived()

        def relayout(part, carry):
            unpack(0, part, N_OUT_TILES)
            return carry
        lax.fori_loop(0, N_OUT_TILES, relayout, 0)

    @pl.when(s < N_FF_TILES)
    def _up():
        issue(next_expert, s, N_FF_TILES)
        x = xb_ref[cur]
        a = jnp.dot(x, wg_ref[0].astype(BF16), preferred_element_type=F32)
        b = jnp.dot(x, wu_ref[0].astype(BF16), preferred_element_type=F32)
        hid_ref[s] = (jax.nn.silu(a) * b).astype(BF16)

    @pl.when(s == N_FF_TILES)
    def _arrived():
        arrived()

    @pl.when(s >= N_FF_TILES)
    def _down():
        unpack(nxt, s - N_FF_TILES, N_OUT_TILES)
        wd = wd_ref[0].astype(BF16)
        acc = jnp.dot(hid_ref[0], wd[0:FF_TILE, :], preferred_element_type=F32)
        for j in range(1, N_FF_TILES):
            acc = acc + jnp.dot(hid_ref[j], wd[j * FF_TILE:(j + 1) * FF_TILE, :], preferred_element_type=F32)
        for g_ref, o_ref, rows in ((gc_ref, oc_ref, slice(0, cap_c)), (gl_ref, ol_ref, slice(cap_c, cap))):
            gates = g_ref[...]
            lane = lax.broadcasted_iota(I32, gates.shape, 1)
            gate = jnp.sum(jnp.where(lane == e, gates, 0.0), axis=1, keepdims=True)
            o_ref[...] = acc[rows, :] * gate


def _experts(idx_c, idx_l, h2_c, h2_l, sgate_c, sgate_l, w_gate, w_up, w_down):
    cap_c = idx_c.shape[1]
    cap_l = idx_l.shape[1]
    cap = cap_c + cap_l
    idx_c = idx_c.reshape(-1)
    idx_l = idx_l.reshape(-1)
    const = lambda e, s, ic, il: (0, 0)
    up_idx = lambda e, s, ic, il: (e, 0, jnp.minimum(s, N_FF_TILES - 1))
    down_idx = lambda e, s, ic, il: (e, 0, jnp.maximum(s - N_FF_TILES, 0))
    out_idx = lambda e, s, ic, il: (e, jnp.maximum(s - N_FF_TILES, 0))
    out_c = jax.ShapeDtypeStruct((N_EXPERTS * cap_c, D_MODEL), F32)
    out_l = jax.ShapeDtypeStruct((N_EXPERTS * cap_l, D_MODEL), F32)
    return pl.pallas_call(
        functools.partial(_expert_body, cap_c=cap_c, cap_l=cap_l),
        grid_spec=pltpu.PrefetchScalarGridSpec(
            num_scalar_prefetch=2,
            grid=(N_EXPERTS, EXPERT_STEPS),
            in_specs=[pl.BlockSpec(memory_space=pl.ANY),
                      pl.BlockSpec(memory_space=pl.ANY),
                      pl.BlockSpec(sgate_c.shape, const),
                      pl.BlockSpec(sgate_l.shape, const),
                      pl.BlockSpec((1, D_MODEL, FF_TILE), up_idx),
                      pl.BlockSpec((1, D_MODEL, FF_TILE), up_idx),
                      pl.BlockSpec((1, EXPERT_FF, OUT_TILE), down_idx)],
            out_specs=[pl.BlockSpec((cap_c, OUT_TILE), out_idx), pl.BlockSpec((cap_l, OUT_TILE), out_idx)],
            scratch_shapes=[pltpu.VMEM((cap, ROW_TILES, LANES), BF16),
                            pltpu.VMEM((2, cap, D_MODEL), BF16),
                            pltpu.VMEM((N_FF_TILES, cap, FF_TILE), BF16),
                            pltpu.SemaphoreType.DMA(())]),
        out_shape=[out_c, out_l],
        compiler_params=pltpu.CompilerParams(vmem_limit_bytes=60 * 2**20),
        name="experts",
    )(idx_c, idx_l, h2_c, h2_l, sgate_c, sgate_l, w_gate, w_up, w_down)


SUBLANES = 8
COMBINE_CTX = (256, 56)
COMBINE_LAT = (128, 72)


def _combine_body(bs_ref, pos_ref, eo_ref, x1_ref, mod_ref, fn_ref, y_ref, rows, sem, *,
                  tb, cap, win, n_blocks, latent, tiles_per_seq):
    i = pl.program_id(0)
    slot = i % 2

    def window_base(e, blk):
        return (bs_ref[e, blk] // SUBLANES) * SUBLANES

    def window_start(e, blk, p):
        return jnp.minimum(window_base(e, blk) + p * win, cap - win)

    def fetch(blk, p, to_slot):
        for e in range(N_EXPERTS):
            start = pl.multiple_of(e * cap + window_start(e, blk, p), SUBLANES)
            pltpu.make_async_copy(eo_ref.at[pl.ds(start, win)], rows.at[to_slot, pl.ds(e * win, win)],
                                  sem.at[to_slot]).start()

    def wait(to_slot):
        pltpu.make_async_copy(eo_ref.at[pl.ds(0, N_EXPERTS * win)], rows.at[to_slot], sem.at[to_slot]).wait()

    @pl.when(i == 0)
    def _first():
        fetch(0, 0, 0)

    @pl.when(i + 1 < n_blocks)
    def _prefetch():
        fetch(i + 1, 0, 1 - slot)

    lane = lax.broadcasted_iota(I32, (1, N_EXPERTS), 1)
    col = lax.broadcasted_iota(I32, (1, (N_EXPERTS * win)), 1)
    col_row = (col % win).astype(F32)
    expand = (lax.broadcasted_iota(I32, (N_EXPERTS, (N_EXPERTS * win)), 0) == col // win).astype(BF16)
    widen = lambda v: jnp.dot(v, expand, preferred_element_type=F32)

    pos = pos_ref[...]

    def one_pass(p, acc):
        base = jnp.zeros((1, N_EXPERTS), I32)
        start = jnp.zeros((1, N_EXPERTS), I32)
        for e in range(N_EXPERTS):
            base = jnp.where(lane == e, window_base(e, i), base)
            start = jnp.where(lane == e, window_start(e, i, p), start)
        rel = pos - base
        mine = (pos >= 0) & (rel >= p * win) & (rel < (p + 1) * win)
        window_col = jnp.where(mine, pos - start, -1).astype(F32).astype(BF16)
        select = (widen(window_col) == col_row).astype(BF16)
        r_hi, r_lo = _split_bf16(rows[slot])
        return (acc + jnp.dot(select, r_hi, preferred_element_type=F32)
                + jnp.dot(select, r_lo, preferred_element_type=F32))

    wait(slot)
    moe = one_pass(0, jnp.zeros((tb, D_MODEL), F32))

    n_pass = jnp.int32(1)
    for e in range(N_EXPERTS):
        span = bs_ref[e, i + 1] - window_base(e, i)
        n_pass = jnp.maximum(n_pass, (span + win - 1) // win)

    def extra(p, acc):
        fetch(i, p, slot)
        wait(slot)
        return one_pass(p, acc)

    moe = lax.fori_loop(1, n_pass, extra, moe)
    row = 1 + i // tiles_per_seq if latent else 0
    gate2 = mod_ref[pl.ds(row, 1), 5 * D_MODEL:6 * D_MODEL]
    y_ref[...] = _rms(x1_ref[...] + gate2 * moe) * fn_ref[...]


def _combine(block_start, pos_t, expert_out, x1, mod, final_norm, cap, latent, seq_len):
    n = x1.shape[0]
    tb, win = COMBINE_LAT if latent else COMBINE_CTX
    n_blocks = n // tb
    return pl.pallas_call(
        functools.partial(_combine_body, tb=tb, cap=cap, win=win, n_blocks=n_blocks, latent=latent,
                          tiles_per_seq=seq_len // tb),
        grid_spec=pltpu.PrefetchScalarGridSpec(
            num_scalar_prefetch=1,
            grid=(n_blocks,),
            in_specs=[pl.BlockSpec((tb, N_EXPERTS), lambda i, bs: (i, 0)),
                      pl.BlockSpec(memory_space=pl.ANY),
                      pl.BlockSpec((tb, D_MODEL), lambda i, bs: (i, 0)),
                      pl.BlockSpec(mod.shape, lambda i, bs: (0, 0)),
                      pl.BlockSpec((1, D_MODEL), lambda i, bs: (0, 0))],
            out_specs=pl.BlockSpec((tb, D_MODEL), lambda i, bs: (i, 0)),
            scratch_shapes=[pltpu.VMEM((2, N_EXPERTS * win, D_MODEL), F32), pltpu.SemaphoreType.DMA((2,))]),
        out_shape=jax.ShapeDtypeStruct((n, D_MODEL), F32),
        compiler_params=pltpu.CompilerParams(vmem_limit_bytes=52 * 2**20),
        name="combine_lat" if latent else "combine_ctx",
    )(block_start, pos_t, expert_out, x1, mod, final_norm)


def _rope_tables(seq_len):
    pos = jnp.arange(seq_len)
    half = HEAD_DIM // 2
    freqs = ROPE_BASE ** (-jnp.arange(0, half, 2, dtype=F32) / half)
    ang_r = (pos // GRID_W).astype(F32)[:, None] * freqs[None, :]
    ang_c = (pos % GRID_W).astype(F32)[:, None] * freqs[None, :]
    cos = jnp.concatenate([jnp.cos(ang_r)] * 2 + [jnp.cos(ang_c)] * 2, axis=1)
    sin = jnp.concatenate([-jnp.sin(ang_r), jnp.sin(ang_r), -jnp.sin(ang_c), jnp.sin(ang_c)], axis=1)
    return cos, sin


def kernel(x_prompt, x_sample, cache_k, cache_v, c, c_ctx, w_ada, b_ada, norm1, w_in, attn_sink, w_s, b_s, g_v,
           g_attn, g_mlp, w_out, norm2, w_router, w_gate, w_up, w_down, final_norm):
    depth = w_ada.shape[0]
    assert depth == 1, "single-layer trunk"
    batch, seq, _ = x_prompt.shape
    dec_batch, dec_seq, _ = x_sample.shape
    n_ctx = batch * seq
    n_lat = dec_batch * dec_seq
    cap_c = CAPACITY_FACTOR * n_ctx // N_EXPERTS
    cap_l = CAPACITY_FACTOR * n_lat // N_EXPERTS
    l = 0

    cvec = jnp.concatenate([c_ctx[None, :], c, jnp.zeros((MOD_ROWS - 1 - dec_batch, D_MODEL), F32)], axis=0)
    mod = _adaln(cvec, w_ada[l], b_ada[l][None, :])

    w_in_b = w_in[l].astype(BF16)
    w_out_b = w_out[l].astype(BF16)
    w_s_b = w_s[l].astype(BF16)
    b_s_t = jnp.transpose(b_s[l])
    w_router_t = jnp.transpose(w_router[l])
    row = lambda a: a.reshape(1, -1)
    shared = (mod, row(norm1[l]), w_in_b, w_s_b, b_s_t, row(g_v[l]), row(g_attn[l]), row(g_mlp[l]), w_out_b,
              row(norm2[l]), w_router_t)

    xp = x_prompt.reshape(n_ctx, D_MODEL)
    xl = x_sample.reshape(n_lat, D_MODEL)

    x1_c, h2_c, afft_c, k, v = _mix(attn_sink[l], None, xp, *shared, batch, seq)
    state_k = k.reshape(batch, 1, seq, N_KV_HEADS, HEAD_DIM)
    state_v = v.reshape(batch, 1, seq, N_KV_HEADS, HEAD_DIM)

    q, k, v, u, gv = _inproj(xl, mod, row(norm1[l]), w_in_b, _rope_tables(dec_seq), dec_seq)
    projected = (q, k, v, cache_k[:, l].reshape(dec_batch, -1, KV_WIDTH),
                 cache_v[:, l].reshape(dec_batch, -1, KV_WIDTH), u, gv)
    x1_l, h2_l, afft_l = _mix(attn_sink[l], projected, xl, *shared, dec_batch, dec_seq)

    def route(aff_t, cap, combine_tile):
        n = aff_t.shape[1]
        pos, idx, slot_gate, start = _route(aff_t.reshape(N_EXPERTS, n // LANES, LANES), cap)
        start = start.reshape(N_EXPERTS, -1)[:, ::combine_tile // LANES].astype(I32)
        starts = jnp.concatenate([start, jnp.full((N_EXPERTS, 1), cap, I32)], axis=1)
        return jnp.transpose(pos.reshape(N_EXPERTS, n)), idx, jnp.transpose(slot_gate), starts

    pos_c, idx_c, sgate_c, starts_c = route(afft_c, cap_c, COMBINE_CTX[0])
    pos_l, idx_l, sgate_l, starts_l = route(afft_l, cap_l, COMBINE_LAT[0])
    eo_c, eo_l = _experts(idx_c, idx_l, h2_c, h2_l, sgate_c, sgate_l, w_gate[l], w_up[l], w_down[l])
    fn = row(final_norm)
    y_c = _combine(starts_c, pos_c, eo_c, x1_c, mod, fn, cap_c, False, seq)
    y_l = _combine(starts_l, pos_l, eo_l, x1_l, mod, fn, cap_l, True, dec_seq)

    return (y_c.reshape(batch, seq, D_MODEL), y_l.reshape(dec_batch, dec_seq, D_MODEL), state_k, state_v)
```

```python
import functools

import jax
import jax.numpy as jnp
from jax import lax
from jax.experimental import pallas as pl
from jax.experimental.pallas import tpu as pltpu

F32 = jnp.float32
BF16 = jnp.bfloat16
I32 = jnp.int32

D_MODEL = 2048
GRID_W = 64
HEAD_DIM = 128
N_Q_HEADS = 8
N_KV_HEADS = 2
Q_PER_KV = N_Q_HEADS // N_KV_HEADS
ATTN_WIDTH = N_Q_HEADS * HEAD_DIM
KV_WIDTH = N_KV_HEADS * HEAD_DIM
N_MLP_GROUPS = 8
MLP_WIDTH = 1024
Q_END = ATTN_WIDTH
K_END = Q_END + KV_WIDTH
V_END = K_END + KV_WIDTH
U_END = V_END + MLP_WIDTH
IN_WIDTH = U_END + MLP_WIDTH
CHUNK = 128
WINDOW = 128
N_EXPERTS = 16
EXPERT_FF = 2048
CAPACITY_FACTOR = 2
ROPE_BASE = 10000.0
EPS = 1e-6
NEG_INF = -1e30
ATTN_SCALE = HEAD_DIM ** -0.5

LANES = 128
MIX_TILE = 256
MIX_SUB = 2
LAT_SPAN = MIX_TILE + 2 * WINDOW
ROW_TILES = D_MODEL // LANES
MOD_ROWS = 8


def _rms(x):
    return x * lax.rsqrt(jnp.mean(x * x, axis=-1, keepdims=True) + EPS)


def _split_bf16_f32(x):
    hi = x.astype(BF16)
    return hi, x - hi.astype(F32)


def _split_bf16(x):
    hi, rest = _split_bf16_f32(x)
    return hi, rest.astype(BF16)


def _adaln_body(c_ref, w_ref, b_ref, o_ref):
    s = jax.nn.silu(c_ref[...])
    o_ref[...] = jnp.dot(s.astype(BF16), w_ref[...].astype(BF16),
                         preferred_element_type=F32) + b_ref[...]


def _adaln(cvec, w_ada, b_ada):
    tn = 1024
    width = w_ada.shape[1]
    return pl.pallas_call(
        _adaln_body,
        grid=(width // tn,),
        in_specs=[pl.BlockSpec((MOD_ROWS, D_MODEL), lambda j: (0, 0)),
                  pl.BlockSpec((D_MODEL, tn), lambda j: (0, j)),
                  pl.BlockSpec((1, tn), lambda j: (0, j))],
        out_specs=pl.BlockSpec((MOD_ROWS, tn), lambda j: (0, j)),
        out_shape=jax.ShapeDtypeStruct((MOD_ROWS, width), F32),
        compiler_params=pltpu.CompilerParams(vmem_limit_bytes=40 * 2**20),
        name="adaln",
    )(cvec, w_ada, b_ada)


def _rope(x, cos, sin, n_heads):
    lane = lax.broadcasted_iota(I32, (x.shape[0], HEAD_DIM), 1)
    first = (lane & 32) == 0
    outs = []
    for h in range(n_heads):
        xh = x[:, h * HEAD_DIM:(h + 1) * HEAD_DIM]
        partner = jnp.where(first, pltpu.roll(xh, 96, 1), pltpu.roll(xh, 32, 1))
        outs.append(xh * cos + partner * sin)
    return jnp.concatenate(outs, axis=1)


def _inproj_body(*refs, latent, tiles_per_seq):
    if latent:
        x_ref, mod_ref, g_ref, w_ref, cos_ref, sin_ref, q_ref, k_ref, v_ref, u_ref, gv_ref = refs
    else:
        x_ref, mod_ref, g_ref, w_ref, q_ref, k_ref, v_ref, u_ref, gv_ref = refs
    i = pl.program_id(0)
    row = 1 + i // tiles_per_seq if latent else 0
    shift = mod_ref[pl.ds(row, 1), 0:D_MODEL]
    scale = mod_ref[pl.ds(row, 1), D_MODEL:2 * D_MODEL]
    h = _rms(x_ref[...]) * g_ref[...] * (1 + scale) + shift
    z = jnp.dot(h.astype(BF16), w_ref[...], preferred_element_type=F32)
    q = z[:, 0:Q_END]
    k = z[:, Q_END:K_END]
    if latent:
        cos = cos_ref[...]
        sin = sin_ref[...]
        q = _rope(q, cos, sin, N_Q_HEADS)
        k = _rope(k, cos, sin, N_KV_HEADS)
    q_ref[...] = q.astype(BF16)
    k_ref[...] = k
    v_ref[...] = z[:, K_END:V_END]
    u_ref[...] = z[:, V_END:U_END]
    gv_ref[...] = z[:, U_END:IN_WIDTH]


def _inproj(x2d, mod, norm1, w_in_bf16, rope_tables, seq_len):
    n = x2d.shape[0]
    tm = 256
    latent = rope_tables is not None
    tiles_per_seq = seq_len // tm
    in_specs = [pl.BlockSpec((tm, D_MODEL), lambda i: (i, 0)),
                pl.BlockSpec(mod.shape, lambda i: (0, 0)),
                pl.BlockSpec((1, D_MODEL), lambda i: (0, 0)),
                pl.BlockSpec((D_MODEL, IN_WIDTH), lambda i: (0, 0))]
    args = [x2d, mod, norm1, w_in_bf16]
    if latent:
        in_specs += [pl.BlockSpec((tm, HEAD_DIM), lambda i: (i % tiles_per_seq, 0))] * 2
        args += list(rope_tables)
    widths = (ATTN_WIDTH, KV_WIDTH, KV_WIDTH, MLP_WIDTH, MLP_WIDTH)
    dtypes = (BF16, F32, F32, F32, F32)
    return pl.pallas_call(
        functools.partial(_inproj_body, latent=latent, tiles_per_seq=tiles_per_seq),
        grid=(n // tm,),
        in_specs=in_specs,
        out_specs=[pl.BlockSpec((tm, w), lambda i: (i, 0)) for w in widths],
        out_shape=[jax.ShapeDtypeStruct((n, w), dt) for w, dt in zip(widths, dtypes)],
        compiler_params=pltpu.CompilerParams(vmem_limit_bytes=56 * 2**20),
        name="inproj_lat" if latent else "inproj_ctx",
    )(*args)


def _mix_body(*refs, latent, seq_len):
    if latent:
        sink_ref, q_ref, k_ref, v_ref, ck_ref, cv_ref, u_ref, gv_ref, x_ref, mod_ref, *rest = refs
        proj_refs = (q_ref, k_ref, v_ref, ck_ref, cv_ref, u_ref, gv_ref)
    else:
        sink_ref, x_ref, mod_ref, norm1_ref, win_ref, *rest = refs
        proj_refs = (norm1_ref, win_ref) + tuple(rest[-2:])
        rest = rest[:-2]
    b = pl.program_id(0)
    t = pl.program_id(1)
    row = 1 + b if latent else 0
    for sub in range(MIX_SUB):
        _mix_tile(sub, t * MIX_SUB + sub, row, latent, seq_len, sink_ref, proj_refs, x_ref, mod_ref, *rest)


def _mix_tile(sub, tile, row, latent, seq_len, sink_ref, proj_refs, x_ref,
              mod_ref, ws_ref, bst_ref, gvn_ref, gattn_ref, gmlp_ref, wout_ref, norm2_ref, wrt_ref,
              x1_ref, h2_ref, afft_ref):
    tq = MIX_TILE
    rows = slice(sub * tq, (sub + 1) * tq)

    if latent:
        q_ref, k_ref, v_ref, ck_ref, cv_ref, u_ref, gv_ref = proj_refs
        q = q_ref[rows, :]
        u = u_ref[rows, :]
        gv = gv_ref[rows, :]
        ks = pl.multiple_of(jnp.clip(tile * tq - WINDOW, 0, seq_len - LAT_SPAN), WINDOW)
        kcat = jnp.concatenate([ck_ref[0], k_ref[pl.ds(ks, LAT_SPAN), :]], axis=0).astype(BF16)
        vcat = jnp.concatenate([cv_ref[0], v_ref[pl.ds(ks, LAT_SPAN), :]], axis=0).astype(BF16)
        n_ctx = ck_ref.shape[1]
        n_keys = n_ctx + LAT_SPAN
        kidx = lax.broadcasted_iota(I32, (tq, n_keys), 1)
        qpos = tile * tq + lax.broadcasted_iota(I32, (tq, n_keys), 0)
        kpos = ks + kidx - n_ctx
        mask = (kidx < n_ctx) | (jnp.abs(qpos - kpos) <= WINDOW)
    else:
        norm1_ref, win_ref, kout_ref, vout_ref = proj_refs
        shift1 = mod_ref[pl.ds(row, 1), 0:D_MODEL]
        scale1 = mod_ref[pl.ds(row, 1), D_MODEL:2 * D_MODEL]
        h1 = _rms(x_ref[rows, :]) * norm1_ref[...] * (1 + scale1) + shift1
        z = jnp.dot(h1.astype(BF16), win_ref[...], preferred_element_type=F32)
        q = z[:, 0:Q_END].astype(BF16)
        k = z[:, Q_END:K_END]
        v = z[:, K_END:V_END]
        u = z[:, V_END:U_END]
        gv = z[:, U_END:IN_WIDTH]
        kout_ref[rows, :] = k
        vout_ref[rows, :] = v
        kcat = k.astype(BF16)
        vcat = v.astype(BF16)
        mask = None

    heads = []
    for h in range(N_Q_HEADS):
        kv = h // Q_PER_KV
        qh = q[:, h * HEAD_DIM:(h + 1) * HEAD_DIM]
        kh = kcat[:, kv * HEAD_DIM:(kv + 1) * HEAD_DIM]
        vh = vcat[:, kv * HEAD_DIM:(kv + 1) * HEAD_DIM]
        s = lax.dot_general(qh, kh, (((1,), (1,)), ((), ())), preferred_element_type=F32) * ATTN_SCALE
        if mask is not None:
            s = jnp.where(mask, s, NEG_INF)
        sink = sink_ref[h]
        m = jnp.maximum(jnp.max(s, axis=1, keepdims=True), sink)
        p = jnp.exp(s - m)
        den = jnp.sum(p, axis=1, keepdims=True) + jnp.exp(sink - m)
        heads.append(jnp.dot(p.astype(BF16), vh, preferred_element_type=F32) / den)
    attn = jnp.concatenate(heads, axis=1)

    sg = (_rms(jax.nn.gelu(gv)) * gvn_ref[...]).astype(BF16)
    bst = bst_ref[...]
    chunks = []
    for c in range(tq // CHUNK):
        groups = []
        for g in range(N_MLP_GROUPS):
            blk = sg[c * CHUNK:(c + 1) * CHUNK, g * LANES:(g + 1) * LANES]
            groups.append(jnp.dot(ws_ref[g], blk, preferred_element_type=F32) + bst[:, g:g + 1])
        chunks.append(jnp.concatenate(groups, axis=1))
    mlp = jax.nn.gelu(u) * jnp.concatenate(chunks, axis=0)

    mixed = jnp.concatenate([_rms(attn) * gattn_ref[...], _rms(mlp) * gmlp_ref[...]], axis=1)
    out = jnp.dot(mixed.astype(BF16), wout_ref[...], preferred_element_type=F32)
    gate1 = mod_ref[pl.ds(row, 1), 2 * D_MODEL:3 * D_MODEL]
    x1 = x_ref[rows, :] + gate1 * out
    x1_ref[rows, :] = x1

    shift2 = mod_ref[pl.ds(row, 1), 3 * D_MODEL:4 * D_MODEL]
    scale2 = mod_ref[pl.ds(row, 1), 4 * D_MODEL:5 * D_MODEL]
    h2 = _rms(x1) * norm2_ref[...] * (1 + scale2) + shift2

    hh = h2.astype(BF16)
    hl = (h2 - hh.astype(F32)).astype(BF16)
    wrt = wrt_ref[...]
    wth = wrt.astype(BF16)
    wtl = (wrt - wth.astype(F32)).astype(BF16)
    nt = (((1,), (1,)), ((), ()))
    logits_t = (lax.dot_general(wth, hh, nt, preferred_element_type=F32)
                + lax.dot_general(wth, hl, nt, preferred_element_type=F32)
                + lax.dot_general(wtl, hh, nt, preferred_element_type=F32))
    pt = jnp.exp(logits_t - jnp.max(logits_t, axis=0, keepdims=True))
    afft_ref[:, rows] = pt / jnp.sum(pt, axis=0, keepdims=True)

    h2_ref[rows] = hh.reshape(tq, ROW_TILES, LANES)


def _mix(sink, projected, x2d, mod, norm1, w_in_bf16, w_s_bf16, b_s_t, g_v, g_attn, g_mlp, w_out_bf16, norm2,
         w_router_t, batch, seq_len):
    n = x2d.shape[0]
    latent = projected is not None
    block = MIX_SUB * MIX_TILE
    const2 = lambda b, t: (0, 0)
    resident = dict(pipeline_mode=pl.Buffered(1))
    if latent:
        grid = (batch, seq_len // block)
        tiles = grid[1]
        tok = lambda b, t: (b * tiles + t, 0)
        q, k, v, ck, cv, u, gv = projected
        seq = lambda b, t: (b, 0)
        in_specs = [pl.BlockSpec(memory_space=pltpu.SMEM),
                    pl.BlockSpec((block, ATTN_WIDTH), tok),
                    pl.BlockSpec((seq_len, KV_WIDTH), seq),
                    pl.BlockSpec((seq_len, KV_WIDTH), seq),
                    pl.BlockSpec((1,) + ck.shape[1:], lambda b, t: (b, 0, 0)),
                    pl.BlockSpec((1,) + cv.shape[1:], lambda b, t: (b, 0, 0)),
                    pl.BlockSpec((block, MLP_WIDTH), tok),
                    pl.BlockSpec((block, MLP_WIDTH), tok),
                    pl.BlockSpec((block, D_MODEL), tok),
                    pl.BlockSpec(mod.shape, const2)]
        args = [sink, q, k, v, ck, cv, u, gv, x2d, mod]
    else:
        assert seq_len == MIX_TILE
        grid = (batch // MIX_SUB, 1)
        tiles = 1
        tok = lambda b, t: (b, 0)
        in_specs = [pl.BlockSpec(memory_space=pltpu.SMEM),
                    pl.BlockSpec((block, D_MODEL), tok),
                    pl.BlockSpec(mod.shape, const2),
                    pl.BlockSpec((1, D_MODEL), const2),
                    pl.BlockSpec(w_in_bf16.shape, const2, **resident)]
        args = [sink, x2d, mod, norm1, w_in_bf16]
    in_specs += [pl.BlockSpec(w_s_bf16.shape, lambda b, t: (0, 0, 0)),
                 pl.BlockSpec(b_s_t.shape, const2),
                 pl.BlockSpec((1, MLP_WIDTH), const2),
                 pl.BlockSpec((1, ATTN_WIDTH), const2),
                 pl.BlockSpec((1, MLP_WIDTH), const2),
                 pl.BlockSpec(w_out_bf16.shape, const2, **resident),
                 pl.BlockSpec((1, D_MODEL), const2),
                 pl.BlockSpec(w_router_t.shape, const2)]
    args += [w_s_bf16, b_s_t, g_v, g_attn, g_mlp, w_out_bf16, norm2, w_router_t]
    out_specs = [pl.BlockSpec((block, D_MODEL), tok),
                 pl.BlockSpec((block, ROW_TILES, LANES), lambda b, t: tok(b, t) + (0,)),
                 pl.BlockSpec((N_EXPERTS, block), lambda b, t: (0, b * tiles + t))]
    out_shape = [jax.ShapeDtypeStruct((n, D_MODEL), F32),
                 jax.ShapeDtypeStruct((n, ROW_TILES, LANES), BF16),
                 jax.ShapeDtypeStruct((N_EXPERTS, n), F32)]
    if not latent:
        out_specs += [pl.BlockSpec((block, KV_WIDTH), tok)] * 2
        out_shape += [jax.ShapeDtypeStruct((n, KV_WIDTH), F32)] * 2
    return pl.pallas_call(
        functools.partial(_mix_body, latent=latent, seq_len=seq_len),
        grid=grid,
        in_specs=in_specs,
        out_specs=out_specs,
        out_shape=out_shape,
        compiler_params=pltpu.CompilerParams(vmem_limit_bytes=58 * 2**20),
        name="mix_lat" if latent else "mix_ctx",
    )(*args)


def _exclusive_counts(flags, upper, lower):
    n_e, nb, _ = flags.shape
    f = flags.astype(BF16)
    local = jnp.dot(f.reshape(n_e * nb, LANES), upper, preferred_element_type=F32).reshape(n_e, nb, LANES)
    totals = jnp.broadcast_to(jnp.sum(flags, axis=2, keepdims=True), flags.shape).astype(BF16)
    before = jnp.stack([jnp.dot(lower, totals[e], preferred_element_type=F32) for e in range(n_e)], axis=0)
    return local, before


def _route_body(aff_ref, pos_ref, idx_ref, sgate_ref, start_ref, local_ref, gate_ref, total_ref, *, cap):
    aff = aff_ref[...]
    n_e, nb, _ = aff.shape
    reduce_tokens = lambda x: jnp.sum(jnp.sum(x, axis=2, keepdims=True), axis=1, keepdims=True)

    def bisect(i, thr):
        cand = thr | jnp.left_shift(jnp.int32(1), 30 - i)
        n_ge = reduce_tokens((aff >= lax.bitcast_convert_type(cand, F32)).astype(F32))
        return jnp.where(n_ge >= cap, cand, thr)

    thr = lax.fori_loop(0, 31, bisect, jnp.zeros((n_e, 1, 1), I32))

    upper = (lax.broadcasted_iota(I32, (LANES, LANES), 0) < lax.broadcasted_iota(I32, (LANES, LANES), 1)).astype(BF16)
    lower = (lax.broadcasted_iota(I32, (nb, nb), 1) < lax.broadcasted_iota(I32, (nb, nb), 0)).astype(BF16)

    above = aff >= lax.bitcast_convert_type(thr + 1, F32)
    tied = (aff >= lax.bitcast_convert_type(thr, F32)) & jnp.logical_not(above)
    need = cap - reduce_tokens(above.astype(F32))
    tie_local, tie_before = _exclusive_counts(tied.astype(F32), upper, lower)
    sel = above | (tied & (tie_local + tie_before < need))
    sel_f = sel.astype(F32)

    local, before = _exclusive_counts(sel_f, upper, lower)
    pos_ref[...] = jnp.where(sel, (local + before).astype(I32), -1)
    gate_ref[...] = aff

    local_ref[...] = (local + sel_f).astype(BF16)
    start_ref[...] = before[:, :, 0:1]
    total_ref[...] = jnp.sum(sel_f, axis=2, keepdims=True)
    slot = lax.broadcasted_iota(I32, (1, cap), 1).astype(F32)
    block_id = lax.broadcasted_iota(I32, (nb, 1), 0).astype(F32)
    lane_id = lax.broadcasted_iota(I32, (LANES, 1), 0).astype(F32)
    tn = (((0,), (0,)), ((), ()))

    def lists(e, carry):
        start = start_ref[e]
        end = start + total_ref[e]
        blk = jnp.sum((end <= slot).astype(F32), axis=0, keepdims=True)
        onehot = block_id == blk
        onehot_b = onehot.astype(BF16)
        rank = slot - jnp.sum(jnp.where(onehot, start, 0.0), axis=0, keepdims=True)
        counts = lax.dot_general(local_ref[e], onehot_b, tn, preferred_element_type=F32)
        tok = jnp.sum((counts <= rank).astype(F32), axis=0, keepdims=True)
        idx_ref[pl.ds(e, 1), :] = (blk * LANES + tok).astype(I32)
        g1, rest = _split_bf16_f32(gate_ref[e])
        g2, rest = _split_bf16_f32(rest)
        block_gates = sum(lax.dot_general(g, onehot_b, tn, preferred_element_type=F32)
                          for g in (g1, g2, rest.astype(BF16)))
        sgate_ref[pl.ds(e, 1), :] = jnp.sum(jnp.where(lane_id == tok, block_gates, 0.0), axis=0, keepdims=True)
        return carry

    lax.fori_loop(0, n_e, lists, 0)


def _route(aff_blocks, cap):
    n_e, nb, _ = aff_blocks.shape
    return pl.pallas_call(
        functools.partial(_route_body, cap=cap),
        out_shape=[jax.ShapeDtypeStruct(aff_blocks.shape, I32), jax.ShapeDtypeStruct((n_e, cap), I32),
                   jax.ShapeDtypeStruct((n_e, cap), F32), jax.ShapeDtypeStruct((n_e, nb, 1), F32)],
        scratch_shapes=[pltpu.VMEM(aff_blocks.shape, BF16), pltpu.VMEM(aff_blocks.shape, F32),
                        pltpu.VMEM((n_e, nb, 1), F32)],
        compiler_params=pltpu.CompilerParams(vmem_limit_bytes=48 * 2**20),
        name="route",
    )(aff_blocks)


FF_TILE = 256
N_FF_TILES = EXPERT_FF // FF_TILE
OUT_TILE = 256
N_OUT_TILES = D_MODEL // OUT_TILE
EXPERT_STEPS = N_FF_TILES + N_OUT_TILES
W_SPLIT = 4


def _expert_body(idxc_ref, idxl_ref, hc_ref, hl_ref, gc_ref, gl_ref, *refs, cap_c, cap_l):
    wg_refs = refs[0:W_SPLIT]
    wu_refs = refs[W_SPLIT:2 * W_SPLIT]
    wd_refs = refs[2 * W_SPLIT:3 * W_SPLIT]
    oc_ref, ol_ref, gbuf, xb_ref, hid_ref, sem = refs[3 * W_SPLIT:]
    band = D_MODEL // W_SPLIT
    e = pl.program_id(0)
    s = pl.program_id(1)
    cap = cap_c + cap_l
    cur = e % 2
    nxt = 1 - cur
    next_expert = (e + 1) % N_EXPERTS

    def issue(expert, part, parts):
        for idx_ref, src, n_rows, base in ((idxc_ref, hc_ref, cap_c, 0), (idxl_ref, hl_ref, cap_l, cap_c)):
            per = n_rows // parts
            for j in range(per):
                r = part * per + j
                pltpu.make_async_copy(src.at[idx_ref[expert * n_rows + r]], gbuf.at[base + r], sem).start()

    def arrived():
        pltpu.make_async_copy(hc_ref.at[pl.ds(0, cap)], gbuf, sem).wait()

    def unpack(slot, part, parts):
        per = cap // parts
        r0 = pl.multiple_of(part * per, per)
        xb_ref[slot, pl.ds(r0, per), :] = gbuf[pl.ds(r0, per)].reshape(per, D_MODEL)

    @pl.when((e == 0) & (s == 0))
    def _first():
        def gather(part, carry):
            issue(0, part, N_FF_TILES)
            return carry
        lax.fori_loop(0, N_FF_TILES, gather, 0)
        arrived()

        def relayout(part, carry):
            unpack(0, part, N_OUT_TILES)
            return carry
        lax.fori_loop(0, N_OUT_TILES, relayout, 0)

    @pl.when(s < N_FF_TILES)
    def _up():
        issue(next_expert, s, N_FF_TILES)
        def project(w_refs):
            out = None
            for i, w_ref in enumerate(w_refs):
                part = jnp.dot(xb_ref[cur, :, i * band:(i + 1) * band], w_ref[0].astype(BF16),
                               preferred_element_type=F32)
                out = part if out is None else out + part
            return out

        hid_ref[s] = (jax.nn.silu(project(wg_refs)) * project(wu_refs)).astype(BF16)

    @pl.when(s == N_FF_TILES)
    def _arrived():
        arrived()

    @pl.when(s >= N_FF_TILES)
    def _down():
        unpack(nxt, s - N_FF_TILES, N_OUT_TILES)
        acc = None
        for j in range(N_FF_TILES):
            w_ref = wd_refs[j * FF_TILE // band]
            r0 = j * FF_TILE % band
            part = jnp.dot(hid_ref[j], w_ref[0, r0:r0 + FF_TILE, :].astype(BF16), preferred_element_type=F32)
            acc = part if acc is None else acc + part
        for g_ref, o_ref, rows in ((gc_ref, oc_ref, slice(0, cap_c)), (gl_ref, ol_ref, slice(cap_c, cap))):
            gates = g_ref[...]
            lane = lax.broadcasted_iota(I32, gates.shape, 1)
            gate = jnp.sum(jnp.where(lane == e, gates, 0.0), axis=1, keepdims=True)
            o_ref[...] = acc[rows, :] * gate


def _experts(idx_c, idx_l, h2_c, h2_l, sgate_c, sgate_l, w_gate, w_up, w_down):
    cap_c = idx_c.shape[1]
    cap_l = idx_l.shape[1]
    cap = cap_c + cap_l
    idx_c = idx_c.reshape(-1)
    idx_l = idx_l.reshape(-1)
    const = lambda e, s, ic, il: (0, 0)
    band = D_MODEL // W_SPLIT
    up_idx = lambda i: lambda e, s, ic, il: (e, i, jnp.minimum(s, N_FF_TILES - 1))
    down_idx = lambda i: lambda e, s, ic, il: (e, i, jnp.maximum(s - N_FF_TILES, 0))
    out_idx = lambda e, s, ic, il: (e, jnp.maximum(s - N_FF_TILES, 0))
    up_specs = [pl.BlockSpec((1, band, FF_TILE), up_idx(i)) for i in range(W_SPLIT)]
    down_specs = [pl.BlockSpec((1, band, OUT_TILE), down_idx(i)) for i in range(W_SPLIT)]
    out_c = jax.ShapeDtypeStruct((N_EXPERTS * cap_c, D_MODEL), F32)
    out_l = jax.ShapeDtypeStruct((N_EXPERTS * cap_l, D_MODEL), F32)
    return pl.pallas_call(
        functools.partial(_expert_body, cap_c=cap_c, cap_l=cap_l),
        grid_spec=pltpu.PrefetchScalarGridSpec(
            num_scalar_prefetch=2,
            grid=(N_EXPERTS, EXPERT_STEPS),
            in_specs=[pl.BlockSpec(memory_space=pl.ANY),
                      pl.BlockSpec(memory_space=pl.ANY),
                      pl.BlockSpec(sgate_c.shape, const),
                      pl.BlockSpec(sgate_l.shape, const)] + up_specs + up_specs + down_specs,
            out_specs=[pl.BlockSpec((cap_c, OUT_TILE), out_idx), pl.BlockSpec((cap_l, OUT_TILE), out_idx)],
            scratch_shapes=[pltpu.VMEM((cap, ROW_TILES, LANES), BF16),
                            pltpu.VMEM((2, cap, D_MODEL), BF16),
                            pltpu.VMEM((N_FF_TILES, cap, FF_TILE), BF16),
                            pltpu.SemaphoreType.DMA(())]),
        out_shape=[out_c, out_l],
        compiler_params=pltpu.CompilerParams(vmem_limit_bytes=60 * 2**20),
        name="experts",
    )(idx_c, idx_l, h2_c, h2_l, sgate_c, sgate_l, *([w_gate] * W_SPLIT + [w_up] * W_SPLIT + [w_down] * W_SPLIT))


SUBLANES = 8
COMBINE_CTX = (256, 56)
COMBINE_LAT = (128, 72)


def _combine_body(bs_ref, pos_ref, eo_ref, x1_ref, mod_ref, fn_ref, y_ref, rows, sem, *,
                  tb, cap, win, n_blocks, latent, tiles_per_seq):
    i = pl.program_id(0)
    slot = i % 2

    def window_base(e, blk):
        return (bs_ref[e, blk] // SUBLANES) * SUBLANES

    def window_start(e, blk, p):
        return jnp.minimum(window_base(e, blk) + p * win, cap - win)

    def fetch(blk, p, to_slot):
        for e in range(N_EXPERTS):
            start = pl.multiple_of(e * cap + window_start(e, blk, p), SUBLANES)
            pltpu.make_async_copy(eo_ref.at[pl.ds(start, win)], rows.at[to_slot, pl.ds(e * win, win)],
                                  sem.at[to_slot]).start()

    def wait(to_slot):
        pltpu.make_async_copy(eo_ref.at[pl.ds(0, N_EXPERTS * win)], rows.at[to_slot], sem.at[to_slot]).wait()

    @pl.when(i == 0)
    def _first():
        fetch(0, 0, 0)

    @pl.when(i + 1 < n_blocks)
    def _prefetch():
        fetch(i + 1, 0, 1 - slot)

    lane = lax.broadcasted_iota(I32, (1, N_EXPERTS), 1)
    col = lax.broadcasted_iota(I32, (1, (N_EXPERTS * win)), 1)
    col_row = (col % win).astype(F32)
    expand = (lax.broadcasted_iota(I32, (N_EXPERTS, (N_EXPERTS * win)), 0) == col // win).astype(BF16)
    widen = lambda v: jnp.dot(v, expand, preferred_element_type=F32)

    pos = pos_ref[...]

    def one_pass(p, acc):
        base = jnp.zeros((1, N_EXPERTS), I32)
        start = jnp.zeros((1, N_EXPERTS), I32)
        for e in range(N_EXPERTS):
            base = jnp.where(lane == e, window_base(e, i), base)
            start = jnp.where(lane == e, window_start(e, i, p), start)
        rel = pos - base
        mine = (pos >= 0) & (rel >= p * win) & (rel < (p + 1) * win)
        window_col = jnp.where(mine, pos - start, -1).astype(F32).astype(BF16)
        select = (widen(window_col) == col_row).astype(BF16)
        r_hi, r_lo = _split_bf16(rows[slot])
        return (acc + jnp.dot(select, r_hi, preferred_element_type=F32)
                + jnp.dot(select, r_lo, preferred_element_type=F32))

    wait(slot)
    moe = one_pass(0, jnp.zeros((tb, D_MODEL), F32))

    n_pass = jnp.int32(1)
    for e in range(N_EXPERTS):
        span = bs_ref[e, i + 1] - window_base(e, i)
        n_pass = jnp.maximum(n_pass, (span + win - 1) // win)

    def extra(p, acc):
        fetch(i, p, slot)
        wait(slot)
        return one_pass(p, acc)

    moe = lax.fori_loop(1, n_pass, extra, moe)
    row = 1 + i // tiles_per_seq if latent else 0
    gate2 = mod_ref[pl.ds(row, 1), 5 * D_MODEL:6 * D_MODEL]
    y_ref[...] = _rms(x1_ref[...] + gate2 * moe) * fn_ref[...]


def _combine(block_start, pos_t, expert_out, x1, mod, final_norm, cap, latent, seq_len):
    n = x1.shape[0]
    tb, win = COMBINE_LAT if latent else COMBINE_CTX
    n_blocks = n // tb
    return pl.pallas_call(
        functools.partial(_combine_body, tb=tb, cap=cap, win=win, n_blocks=n_blocks, latent=latent,
                          tiles_per_seq=seq_len // tb),
        grid_spec=pltpu.PrefetchScalarGridSpec(
            num_scalar_prefetch=1,
            grid=(n_blocks,),
            in_specs=[pl.BlockSpec((tb, N_EXPERTS), lambda i, bs: (i, 0)),
                      pl.BlockSpec(memory_space=pl.ANY),
                      pl.BlockSpec((tb, D_MODEL), lambda i, bs: (i, 0)),
                      pl.BlockSpec(mod.shape, lambda i, bs: (0, 0)),
                      pl.BlockSpec((1, D_MODEL), lambda i, bs: (0, 0))],
            out_specs=pl.BlockSpec((tb, D_MODEL), lambda i, bs: (i, 0)),
            scratch_shapes=[pltpu.VMEM((2, N_EXPERTS * win, D_MODEL), F32), pltpu.SemaphoreType.DMA((2,))]),
        out_shape=jax.ShapeDtypeStruct((n, D_MODEL), F32),
        compiler_params=pltpu.CompilerParams(vmem_limit_bytes=52 * 2**20),
        name="combine_lat" if latent else "combine_ctx",
    )(block_start, pos_t, expert_out, x1, mod, final_norm)


def _rope_tables(seq_len):
    pos = jnp.arange(seq_len)
    half = HEAD_DIM // 2
    freqs = ROPE_BASE ** (-jnp.arange(0, half, 2, dtype=F32) / half)
    ang_r = (pos // GRID_W).astype(F32)[:, None] * freqs[None, :]
    ang_c = (pos % GRID_W).astype(F32)[:, None] * freqs[None, :]
    cos = jnp.concatenate([jnp.cos(ang_r)] * 2 + [jnp.cos(ang_c)] * 2, axis=1)
    sin = jnp.concatenate([-jnp.sin(ang_r), jnp.sin(ang_r), -jnp.sin(ang_c), jnp.sin(ang_c)], axis=1)
    return cos, sin


def kernel(x_prompt, x_sample, cache_k, cache_v, c, c_ctx, w_ada, b_ada, norm1, w_in, attn_sink, w_s, b_s, g_v,
           g_attn, g_mlp, w_out, norm2, w_router, w_gate, w_up, w_down, final_norm):
    depth = w_ada.shape[0]
    assert depth == 1, "single-layer trunk"
    batch, seq, _ = x_prompt.shape
    dec_batch, dec_seq, _ = x_sample.shape
    n_ctx = batch * seq
    n_lat = dec_batch * dec_seq
    cap_c = CAPACITY_FACTOR * n_ctx // N_EXPERTS
    cap_l = CAPACITY_FACTOR * n_lat // N_EXPERTS
    l = 0

    cvec = jnp.concatenate([c_ctx[None, :], c, jnp.zeros((MOD_ROWS - 1 - dec_batch, D_MODEL), F32)], axis=0)
    mod = _adaln(cvec, w_ada[l], b_ada[l][None, :])

    w_in_b = w_in[l].astype(BF16)
    w_out_b = w_out[l].astype(BF16)
    w_s_b = w_s[l].astype(BF16)
    b_s_t = jnp.transpose(b_s[l])
    w_router_t = jnp.transpose(w_router[l])
    row = lambda a: a.reshape(1, -1)
    shared = (mod, row(norm1[l]), w_in_b, w_s_b, b_s_t, row(g_v[l]), row(g_attn[l]), row(g_mlp[l]), w_out_b,
              row(norm2[l]), w_router_t)

    xp = x_prompt.reshape(n_ctx, D_MODEL)
    xl = x_sample.reshape(n_lat, D_MODEL)

    x1_c, h2_c, afft_c, k, v = _mix(attn_sink[l], None, xp, *shared, batch, seq)
    state_k = k.reshape(batch, 1, seq, N_KV_HEADS, HEAD_DIM)
    state_v = v.reshape(batch, 1, seq, N_KV_HEADS, HEAD_DIM)

    q, k, v, u, gv = _inproj(xl, mod, row(norm1[l]), w_in_b, _rope_tables(dec_seq), dec_seq)
    projected = (q, k, v, cache_k[:, l].reshape(dec_batch, -1, KV_WIDTH),
                 cache_v[:, l].reshape(dec_batch, -1, KV_WIDTH), u, gv)
    x1_l, h2_l, afft_l = _mix(attn_sink[l], projected, xl, *shared, dec_batch, dec_seq)

    def route(aff_t, cap, combine_tile):
        n = aff_t.shape[1]
        pos, idx, slot_gate, start = _route(aff_t.reshape(N_EXPERTS, n // LANES, LANES), cap)
        start = start.reshape(N_EXPERTS, -1)[:, ::combine_tile // LANES].astype(I32)
        starts = jnp.concatenate([start, jnp.full((N_EXPERTS, 1), cap, I32)], axis=1)
        return jnp.transpose(pos.reshape(N_EXPERTS, n)), idx, jnp.transpose(slot_gate), starts

    pos_c, idx_c, sgate_c, starts_c = route(afft_c, cap_c, COMBINE_CTX[0])
    pos_l, idx_l, sgate_l, starts_l = route(afft_l, cap_l, COMBINE_LAT[0])
    eo_c, eo_l = _experts(idx_c, idx_l, h2_c, h2_l, sgate_c, sgate_l, w_gate[l], w_up[l], w_down[l])
    fn = row(final_norm)
    y_c = _combine(starts_c, pos_c, eo_c, x1_c, mod, fn, cap_c, False, seq)
    y_l = _combine(starts_l, pos_l, eo_l, x1_l, mod, fn, cap_l, True, dec_seq)

    return (y_c.reshape(batch, seq, D_MODEL), y_l.reshape(dec_batch, dec_seq, D_MODEL), state_k, state_v)
```

```python
import functools

import jax
import jax.numpy as jnp
from jax import lax
from jax.experimental import pallas as pl
from jax.experimental.pallas import tpu as pltpu

F32 = jnp.float32
BF16 = jnp.bfloat16
I32 = jnp.int32

D_MODEL = 2048
GRID_W = 64
HEAD_DIM = 128
N_Q_HEADS = 8
N_KV_HEADS = 2
Q_PER_KV = N_Q_HEADS // N_KV_HEADS
ATTN_WIDTH = N_Q_HEADS * HEAD_DIM
KV_WIDTH = N_KV_HEADS * HEAD_DIM
N_MLP_GROUPS = 8
MLP_WIDTH = 1024
Q_END = ATTN_WIDTH
K_END = Q_END + KV_WIDTH
V_END = K_END + KV_WIDTH
U_END = V_END + MLP_WIDTH
IN_WIDTH = U_END + MLP_WIDTH
CHUNK = 128
WINDOW = 128
N_EXPERTS = 16
EXPERT_FF = 2048
CAPACITY_FACTOR = 2
ROPE_BASE = 10000.0
EPS = 1e-6
NEG_INF = -1e30
ATTN_SCALE = HEAD_DIM ** -0.5

LANES = 128
MIX_TILE = 256
MIX_SUB = 2
LAT_SPAN = MIX_TILE + 2 * WINDOW
ROW_TILES = D_MODEL // LANES
MOD_ROWS = 8


def _rms(x):
    return x * lax.rsqrt(jnp.mean(x * x, axis=-1, keepdims=True) + EPS)


def _split_bf16_f32(x):
    hi = x.astype(BF16)
    return hi, x - hi.astype(F32)


def _split_bf16(x):
    hi, rest = _split_bf16_f32(x)
    return hi, rest.astype(BF16)


def _adaln_body(c_ref, w_ref, b_ref, o_ref):
    s = jax.nn.silu(c_ref[...])
    o_ref[...] = jnp.dot(s.astype(BF16), w_ref[...].astype(BF16),
                         preferred_element_type=F32) + b_ref[...]


def _adaln(cvec, w_ada, b_ada):
    tn = 1024
    width = w_ada.shape[1]
    return pl.pallas_call(
        _adaln_body,
        grid=(width // tn,),
        in_specs=[pl.BlockSpec((MOD_ROWS, D_MODEL), lambda j: (0, 0)),
                  pl.BlockSpec((D_MODEL, tn), lambda j: (0, j)),
                  pl.BlockSpec((1, tn), lambda j: (0, j))],
        out_specs=pl.BlockSpec((MOD_ROWS, tn), lambda j: (0, j)),
        out_shape=jax.ShapeDtypeStruct((MOD_ROWS, width), F32),
        compiler_params=pltpu.CompilerParams(vmem_limit_bytes=40 * 2**20),
        name="adaln",
    )(cvec, w_ada, b_ada)


def _rope(x, cos, sin, n_heads):
    lane = lax.broadcasted_iota(I32, (x.shape[0], HEAD_DIM), 1)
    first = (lane & 32) == 0
    outs = []
    for h in range(n_heads):
        xh = x[:, h * HEAD_DIM:(h + 1) * HEAD_DIM]
        partner = jnp.where(first, pltpu.roll(xh, 96, 1), pltpu.roll(xh, 32, 1))
        outs.append(xh * cos + partner * sin)
    return jnp.concatenate(outs, axis=1)


def _inproj_body(*refs, latent, tiles_per_seq):
    if latent:
        x_ref, mod_ref, g_ref, w_ref, cos_ref, sin_ref, q_ref, k_ref, v_ref, u_ref, gv_ref = refs
    else:
        x_ref, mod_ref, g_ref, w_ref, q_ref, k_ref, v_ref, u_ref, gv_ref = refs
    i = pl.program_id(0)
    row = 1 + i // tiles_per_seq if latent else 0
    shift = mod_ref[pl.ds(row, 1), 0:D_MODEL]
    scale = mod_ref[pl.ds(row, 1), D_MODEL:2 * D_MODEL]
    h = _rms(x_ref[...]) * g_ref[...] * (1 + scale) + shift
    z = jnp.dot(h.astype(BF16), w_ref[...], preferred_element_type=F32)
    q = z[:, 0:Q_END]
    k = z[:, Q_END:K_END]
    if latent:
        cos = cos_ref[...]
        sin = sin_ref[...]
        q = _rope(q, cos, sin, N_Q_HEADS)
        k = _rope(k, cos, sin, N_KV_HEADS)
    q_ref[...] = q.astype(BF16)
    k_ref[...] = k
    v_ref[...] = z[:, K_END:V_END]
    u_ref[...] = z[:, V_END:U_END]
    gv_ref[...] = z[:, U_END:IN_WIDTH]


def _inproj(x2d, mod, norm1, w_in_bf16, rope_tables, seq_len):
    n = x2d.shape[0]
    tm = 256
    latent = rope_tables is not None
    tiles_per_seq = seq_len // tm
    in_specs = [pl.BlockSpec((tm, D_MODEL), lambda i: (i, 0)),
                pl.BlockSpec(mod.shape, lambda i: (0, 0)),
                pl.BlockSpec((1, D_MODEL), lambda i: (0, 0)),
                pl.BlockSpec((D_MODEL, IN_WIDTH), lambda i: (0, 0))]
    args = [x2d, mod, norm1, w_in_bf16]
    if latent:
        in_specs += [pl.BlockSpec((tm, HEAD_DIM), lambda i: (i % tiles_per_seq, 0))] * 2
        args += list(rope_tables)
    widths = (ATTN_WIDTH, KV_WIDTH, KV_WIDTH, MLP_WIDTH, MLP_WIDTH)
    dtypes = (BF16, F32, F32, F32, F32)
    return pl.pallas_call(
        functools.partial(_inproj_body, latent=latent, tiles_per_seq=tiles_per_seq),
        grid=(n // tm,),
        in_specs=in_specs,
        out_specs=[pl.BlockSpec((tm, w), lambda i: (i, 0)) for w in widths],
        out_shape=[jax.ShapeDtypeStruct((n, w), dt) for w, dt in zip(widths, dtypes)],
        compiler_params=pltpu.CompilerParams(vmem_limit_bytes=56 * 2**20),
        name="inproj_lat" if latent else "inproj_ctx",
    )(*args)


def _mix_body(*refs, latent, seq_len):
    if latent:
        sink_ref, q_ref, k_ref, v_ref, ck_ref, cv_ref, u_ref, gv_ref, x_ref, mod_ref, *rest = refs
        proj_refs = (q_ref, k_ref, v_ref, ck_ref, cv_ref, u_ref, gv_ref)
    else:
        sink_ref, x_ref, mod_ref, norm1_ref, win_ref, *rest = refs
        proj_refs = (norm1_ref, win_ref) + tuple(rest[-2:])
        rest = rest[:-2]
    b = pl.program_id(0)
    t = pl.program_id(1)
    row = 1 + b if latent else 0
    for sub in range(MIX_SUB):
        _mix_tile(sub, t * MIX_SUB + sub, row, latent, seq_len, sink_ref, proj_refs, x_ref, mod_ref, *rest)


def _mix_tile(sub, tile, row, latent, seq_len, sink_ref, proj_refs, x_ref,
              mod_ref, ws_ref, bst_ref, gvn_ref, gattn_ref, gmlp_ref, wout_ref, norm2_ref, wrt_ref,
              x1_ref, h2_ref, afft_ref):
    tq = MIX_TILE
    rows = slice(sub * tq, (sub + 1) * tq)

    if latent:
        q_ref, k_ref, v_ref, ck_ref, cv_ref, u_ref, gv_ref = proj_refs
        q = q_ref[rows, :]
        u = u_ref[rows, :]
        gv = gv_ref[rows, :]
        ks = pl.multiple_of(jnp.clip(tile * tq - WINDOW, 0, seq_len - LAT_SPAN), WINDOW)
        kcat = jnp.concatenate([ck_ref[0], k_ref[pl.ds(ks, LAT_SPAN), :]], axis=0).astype(BF16)
        vcat = jnp.concatenate([cv_ref[0], v_ref[pl.ds(ks, LAT_SPAN), :]], axis=0).astype(BF16)
        n_ctx = ck_ref.shape[1]
        n_keys = n_ctx + LAT_SPAN
        kidx = lax.broadcasted_iota(I32, (tq, n_keys), 1)
        qpos = tile * tq + lax.broadcasted_iota(I32, (tq, n_keys), 0)
        kpos = ks + kidx - n_ctx
        mask = (kidx < n_ctx) | (jnp.abs(qpos - kpos) <= WINDOW)
    else:
        norm1_ref, win_ref, kout_ref, vout_ref = proj_refs
        shift1 = mod_ref[pl.ds(row, 1), 0:D_MODEL]
        scale1 = mod_ref[pl.ds(row, 1), D_MODEL:2 * D_MODEL]
        h1 = _rms(x_ref[rows, :]) * norm1_ref[...] * (1 + scale1) + shift1
        z = jnp.dot(h1.astype(BF16), win_ref[...], preferred_element_type=F32)
        q = z[:, 0:Q_END].astype(BF16)
        k = z[:, Q_END:K_END]
        v = z[:, K_END:V_END]
        u = z[:, V_END:U_END]
        gv = z[:, U_END:IN_WIDTH]
        kout_ref[rows, :] = k
        vout_ref[rows, :] = v
        kcat = k.astype(BF16)
        vcat = v.astype(BF16)
        mask = None

    heads = []
    for h in range(N_Q_HEADS):
        kv = h // Q_PER_KV
        qh = q[:, h * HEAD_DIM:(h + 1) * HEAD_DIM]
        kh = kcat[:, kv * HEAD_DIM:(kv + 1) * HEAD_DIM]
        vh = vcat[:, kv * HEAD_DIM:(kv + 1) * HEAD_DIM]
        s = lax.dot_general(qh, kh, (((1,), (1,)), ((), ())), preferred_element_type=F32) * ATTN_SCALE
        if mask is not None:
            s = jnp.where(mask, s, NEG_INF)
        sink = sink_ref[h]
        m = jnp.maximum(jnp.max(s, axis=1, keepdims=True), sink)
        p = jnp.exp(s - m)
        den = jnp.sum(p, axis=1, keepdims=True) + jnp.exp(sink - m)
        heads.append(jnp.dot(p.astype(BF16), vh, preferred_element_type=F32) / den)
    attn = jnp.concatenate(heads, axis=1)

    sg = (_rms(jax.nn.gelu(gv)) * gvn_ref[...]).astype(BF16)
    bst = bst_ref[...]
    chunks = []
    for c in range(tq // CHUNK):
        groups = []
        for g in range(N_MLP_GROUPS):
            blk = sg[c * CHUNK:(c + 1) * CHUNK, g * LANES:(g + 1) * LANES]
            groups.append(jnp.dot(ws_ref[g], blk, preferred_element_type=F32) + bst[:, g:g + 1])
        chunks.append(jnp.concatenate(groups, axis=1))
    mlp = jax.nn.gelu(u) * jnp.concatenate(chunks, axis=0)

    mixed = jnp.concatenate([_rms(attn) * gattn_ref[...], _rms(mlp) * gmlp_ref[...]], axis=1)
    out = jnp.dot(mixed.astype(BF16), wout_ref[...], preferred_element_type=F32)
    gate1 = mod_ref[pl.ds(row, 1), 2 * D_MODEL:3 * D_MODEL]
    x1 = x_ref[rows, :] + gate1 * out
    x1_ref[rows, :] = x1

    shift2 = mod_ref[pl.ds(row, 1), 3 * D_MODEL:4 * D_MODEL]
    scale2 = mod_ref[pl.ds(row, 1), 4 * D_MODEL:5 * D_MODEL]
    h2 = _rms(x1) * norm2_ref[...] * (1 + scale2) + shift2

    hh = h2.astype(BF16)
    hl = (h2 - hh.astype(F32)).astype(BF16)
    wrt = wrt_ref[...]
    wth = wrt.astype(BF16)
    wtl = (wrt - wth.astype(F32)).astype(BF16)
    nt = (((1,), (1,)), ((), ()))
    logits_t = (lax.dot_general(wth, hh, nt, preferred_element_type=F32)
                + lax.dot_general(wth, hl, nt, preferred_element_type=F32)
                + lax.dot_general(wtl, hh, nt, preferred_element_type=F32))
    pt = jnp.exp(logits_t - jnp.max(logits_t, axis=0, keepdims=True))
    afft_ref[:, rows] = pt / jnp.sum(pt, axis=0, keepdims=True)

    h2_ref[rows] = hh.reshape(tq, ROW_TILES, LANES)


def _mix(sink, projected, x2d, mod, norm1, w_in_bf16, w_s_bf16, b_s_t, g_v, g_attn, g_mlp, w_out_bf16, norm2,
         w_router_t, batch, seq_len):
    n = x2d.shape[0]
    latent = projected is not None
    block = MIX_SUB * MIX_TILE
    const2 = lambda b, t: (0, 0)
    resident = dict(pipeline_mode=pl.Buffered(1))
    if latent:
        grid = (batch, seq_len // block)
        tiles = grid[1]
        tok = lambda b, t: (b * tiles + t, 0)
        q, k, v, ck, cv, u, gv = projected
        seq = lambda b, t: (b, 0)
        in_specs = [pl.BlockSpec(memory_space=pltpu.SMEM),
                    pl.BlockSpec((block, ATTN_WIDTH), tok),
                    pl.BlockSpec((seq_len, KV_WIDTH), seq),
                    pl.BlockSpec((seq_len, KV_WIDTH), seq),
                    pl.BlockSpec((1,) + ck.shape[1:], lambda b, t: (b, 0, 0)),
                    pl.BlockSpec((1,) + cv.shape[1:], lambda b, t: (b, 0, 0)),
                    pl.BlockSpec((block, MLP_WIDTH), tok),
                    pl.BlockSpec((block, MLP_WIDTH), tok),
                    pl.BlockSpec((block, D_MODEL), tok),
                    pl.BlockSpec(mod.shape, const2)]
        args = [sink, q, k, v, ck, cv, u, gv, x2d, mod]
    else:
        assert seq_len == MIX_TILE
        grid = (batch // MIX_SUB, 1)
        tiles = 1
        tok = lambda b, t: (b, 0)
        in_specs = [pl.BlockSpec(memory_space=pltpu.SMEM),
                    pl.BlockSpec((block, D_MODEL), tok),
                    pl.BlockSpec(mod.shape, const2),
                    pl.BlockSpec((1, D_MODEL), const2),
                    pl.BlockSpec(w_in_bf16.shape, const2, **resident)]
        args = [sink, x2d, mod, norm1, w_in_bf16]
    in_specs += [pl.BlockSpec(w_s_bf16.shape, lambda b, t: (0, 0, 0)),
                 pl.BlockSpec(b_s_t.shape, const2),
                 pl.BlockSpec((1, MLP_WIDTH), const2),
                 pl.BlockSpec((1, ATTN_WIDTH), const2),
                 pl.BlockSpec((1, MLP_WIDTH), const2),
                 pl.BlockSpec(w_out_bf16.shape, const2, **resident),
                 pl.BlockSpec((1, D_MODEL), const2),
                 pl.BlockSpec(w_router_t.shape, const2)]
    args += [w_s_bf16, b_s_t, g_v, g_attn, g_mlp, w_out_bf16, norm2, w_router_t]
    out_specs = [pl.BlockSpec((block, D_MODEL), tok),
                 pl.BlockSpec((block, ROW_TILES, LANES), lambda b, t: tok(b, t) + (0,)),
                 pl.BlockSpec((N_EXPERTS, block), lambda b, t: (0, b * tiles + t))]
    out_shape = [jax.ShapeDtypeStruct((n, D_MODEL), F32),
                 jax.ShapeDtypeStruct((n, ROW_TILES, LANES), BF16),
                 jax.ShapeDtypeStruct((N_EXPERTS, n), F32)]
    if not latent:
        out_specs += [pl.BlockSpec((block, KV_WIDTH), tok)] * 2
        out_shape += [jax.ShapeDtypeStruct((n, KV_WIDTH), F32)] * 2
    return pl.pallas_call(
        functools.partial(_mix_body, latent=latent, seq_len=seq_len),
        grid=grid,
        in_specs=in_specs,
        out_specs=out_specs,
        out_shape=out_shape,
        compiler_params=pltpu.CompilerParams(vmem_limit_bytes=58 * 2**20),
        name="mix_lat" if latent else "mix_ctx",
    )(*args)


def _exclusive_counts(flags, upper, lower):
    n_e, nb, _ = flags.shape
    f = flags.astype(BF16)
    local = jnp.dot(f.reshape(n_e * nb, LANES), upper, preferred_element_type=F32).reshape(n_e, nb, LANES)
    totals = jnp.broadcast_to(jnp.sum(flags, axis=2, keepdims=True), flags.shape).astype(BF16)
    before = jnp.stack([jnp.dot(lower, totals[e], preferred_element_type=F32) for e in range(n_e)], axis=0)
    return local, before


def _route_body(aff_ref, pos_ref, idx_ref, sgate_ref, start_ref, local_ref, gate_ref, total_ref, *, cap):
    aff = aff_ref[...]
    n_e, nb, _ = aff.shape
    reduce_tokens = lambda x: jnp.sum(jnp.sum(x, axis=2, keepdims=True), axis=1, keepdims=True)

    def bisect(i, thr):
        cand = thr | jnp.left_shift(jnp.int32(1), 30 - i)
        n_ge = reduce_tokens((aff >= lax.bitcast_convert_type(cand, F32)).astype(F32))
        return jnp.where(n_ge >= cap, cand, thr)

    thr = lax.fori_loop(0, 31, bisect, jnp.zeros((n_e, 1, 1), I32))

    upper = (lax.broadcasted_iota(I32, (LANES, LANES), 0) < lax.broadcasted_iota(I32, (LANES, LANES), 1)).astype(BF16)
    lower = (lax.broadcasted_iota(I32, (nb, nb), 1) < lax.broadcasted_iota(I32, (nb, nb), 0)).astype(BF16)

    above = aff >= lax.bitcast_convert_type(thr + 1, F32)
    tied = (aff >= lax.bitcast_convert_type(thr, F32)) & jnp.logical_not(above)
    need = cap - reduce_tokens(above.astype(F32))
    tie_local, tie_before = _exclusive_counts(tied.astype(F32), upper, lower)
    sel = above | (tied & (tie_local + tie_before < need))
    sel_f = sel.astype(F32)

    local, before = _exclusive_counts(sel_f, upper, lower)
    pos_ref[...] = jnp.where(sel, (local + before).astype(I32), -1)
    gate_ref[...] = aff

    local_ref[...] = (local + sel_f).astype(BF16)
    start_ref[...] = before[:, :, 0:1]
    total_ref[...] = jnp.sum(sel_f, axis=2, keepdims=True)
    slot = lax.broadcasted_iota(I32, (1, cap), 1).astype(F32)
    block_id = lax.broadcasted_iota(I32, (nb, 1), 0).astype(F32)
    lane_id = lax.broadcasted_iota(I32, (LANES, 1), 0).astype(F32)
    tn = (((0,), (0,)), ((), ()))

    def lists(e, carry):
        start = start_ref[e]
        end = start + total_ref[e]
        blk = jnp.sum((end <= slot).astype(F32), axis=0, keepdims=True)
        onehot = block_id == blk
        onehot_b = onehot.astype(BF16)
        rank = slot - jnp.sum(jnp.where(onehot, start, 0.0), axis=0, keepdims=True)
        counts = lax.dot_general(local_ref[e], onehot_b, tn, preferred_element_type=F32)
        tok = jnp.sum((counts <= rank).astype(F32), axis=0, keepdims=True)
        idx_ref[pl.ds(e, 1), :] = (blk * LANES + tok).astype(I32)
        g1, rest = _split_bf16_f32(gate_ref[e])
        g2, rest = _split_bf16_f32(rest)
        block_gates = sum(lax.dot_general(g, onehot_b, tn, preferred_element_type=F32)
                          for g in (g1, g2, rest.astype(BF16)))
        sgate_ref[pl.ds(e, 1), :] = jnp.sum(jnp.where(lane_id == tok, block_gates, 0.0), axis=0, keepdims=True)
        return carry

    lax.fori_loop(0, n_e, lists, 0)


def _route(aff_blocks, cap):
    n_e, nb, _ = aff_blocks.shape
    return pl.pallas_call(
        functools.partial(_route_body, cap=cap),
        out_shape=[jax.ShapeDtypeStruct(aff_blocks.shape, I32), jax.ShapeDtypeStruct((n_e, cap), I32),
                   jax.ShapeDtypeStruct((n_e, cap), F32), jax.ShapeDtypeStruct((n_e, nb, 1), F32)],
        scratch_shapes=[pltpu.VMEM(aff_blocks.shape, BF16), pltpu.VMEM(aff_blocks.shape, F32),
                        pltpu.VMEM((n_e, nb, 1), F32)],
        compiler_params=pltpu.CompilerParams(vmem_limit_bytes=48 * 2**20),
        name="route",
    )(aff_blocks)


FF_TILE = 256
N_FF_TILES = EXPERT_FF // FF_TILE
OUT_TILE = 256
N_OUT_TILES = D_MODEL // OUT_TILE
EXPERT_STEPS = N_FF_TILES + N_OUT_TILES


def _expert_body(idxc_ref, idxl_ref, hc_ref, hl_ref, gc_ref, gl_ref, wg_ref, wu_ref, wd_ref,
                 oc_ref, ol_ref, gbuf, xb_ref, hid_ref, sem, *, cap_c, cap_l):
    e = pl.program_id(0)
    s = pl.program_id(1)
    cap = cap_c + cap_l
    cur = e % 2
    nxt = 1 - cur
    next_expert = (e + 1) % N_EXPERTS

    def issue(expert, part, parts):
        for idx_ref, src, n_rows, base in ((idxc_ref, hc_ref, cap_c, 0), (idxl_ref, hl_ref, cap_l, cap_c)):
            per = n_rows // parts
            for j in range(per):
                r = part * per + j
                pltpu.make_async_copy(src.at[idx_ref[expert * n_rows + r]], gbuf.at[base + r], sem).start()

    def arrived():
        pltpu.make_async_copy(hc_ref.at[pl.ds(0, cap)], gbuf, sem).wait()

    def unpack(slot, part, parts):
        per = cap // parts
        r0 = pl.multiple_of(part * per, per)
        xb_ref[slot, pl.ds(r0, per), :] = gbuf[pl.ds(r0, per)].reshape(per, D_MODEL)

    @pl.when((e == 0) & (s == 0))
    def _first():
        def gather(part, carry):
            issue(0, part, N_FF_TILES)
            return carry
        lax.fori_loop(0, N_FF_TILES, gather, 0)
        arrived()

        def relayout(part, carry):
            unpack(0, part, N_OUT_TILES)
            return carry
        lax.fori_loop(0, N_OUT_TILES, relayout, 0)

    @pl.when(s < N_FF_TILES)
    def _up():
        issue(next_expert, s, N_FF_TILES)
        x = xb_ref[cur]
        a = jnp.dot(x, wg_ref[0].astype(BF16), preferred_element_type=F32)
        b = jnp.dot(x, wu_ref[0].astype(BF16), preferred_element_type=F32)
        hid_ref[s] = (jax.nn.silu(a) * b).astype(BF16)

    @pl.when(s == N_FF_TILES)
    def _arrived():
        arrived()

    @pl.when(s >= N_FF_TILES)
    def _down():
        unpack(nxt, s - N_FF_TILES, N_OUT_TILES)
        wd = wd_ref[0].astype(BF16)
        acc = jnp.dot(hid_ref[0], wd[0:FF_TILE, :], preferred_element_type=F32)
        for j in range(1, N_FF_TILES):
            acc = acc + jnp.dot(hid_ref[j], wd[j * FF_TILE:(j + 1) * FF_TILE, :], preferred_element_type=F32)
        for g_ref, o_ref, rows in ((gc_ref, oc_ref, slice(0, cap_c)), (gl_ref, ol_ref, slice(cap_c, cap))):
            gates = g_ref[...]
            lane = lax.broadcasted_iota(I32, gates.shape, 1)
            gate = jnp.sum(jnp.where(lane == e, gates, 0.0), axis=1, keepdims=True)
            o_ref[...] = (acc[rows, :] * gate).astype(BF16)


def _experts(idx_c, idx_l, h2_c, h2_l, sgate_c, sgate_l, w_gate, w_up, w_down):
    cap_c = idx_c.shape[1]
    cap_l = idx_l.shape[1]
    cap = cap_c + cap_l
    idx_c = idx_c.reshape(-1)
    idx_l = idx_l.reshape(-1)
    const = lambda e, s, ic, il: (0, 0)
    up_idx = lambda e, s, ic, il: (e, 0, jnp.minimum(s, N_FF_TILES - 1))
    down_idx = lambda e, s, ic, il: (e, 0, jnp.maximum(s - N_FF_TILES, 0))
    out_idx = lambda e, s, ic, il: (e, jnp.maximum(s - N_FF_TILES, 0))
    out_c = jax.ShapeDtypeStruct((N_EXPERTS * cap_c, D_MODEL), BF16)
    out_l = jax.ShapeDtypeStruct((N_EXPERTS * cap_l, D_MODEL), BF16)
    return pl.pallas_call(
        functools.partial(_expert_body, cap_c=cap_c, cap_l=cap_l),
        grid_spec=pltpu.PrefetchScalarGridSpec(
            num_scalar_prefetch=2,
            grid=(N_EXPERTS, EXPERT_STEPS),
            in_specs=[pl.BlockSpec(memory_space=pl.ANY),
                      pl.BlockSpec(memory_space=pl.ANY),
                      pl.BlockSpec(sgate_c.shape, const),
                      pl.BlockSpec(sgate_l.shape, const),
                      pl.BlockSpec((1, D_MODEL, FF_TILE), up_idx),
                      pl.BlockSpec((1, D_MODEL, FF_TILE), up_idx),
                      pl.BlockSpec((1, EXPERT_FF, OUT_TILE), down_idx)],
            out_specs=[pl.BlockSpec((cap_c, OUT_TILE), out_idx), pl.BlockSpec((cap_l, OUT_TILE), out_idx)],
            scratch_shapes=[pltpu.VMEM((cap, ROW_TILES, LANES), BF16),
                            pltpu.VMEM((2, cap, D_MODEL), BF16),
                            pltpu.VMEM((N_FF_TILES, cap, FF_TILE), BF16),
                            pltpu.SemaphoreType.DMA(())]),
        out_shape=[out_c, out_l],
        compiler_params=pltpu.CompilerParams(vmem_limit_bytes=60 * 2**20),
        name="experts",
    )(idx_c, idx_l, h2_c, h2_l, sgate_c, sgate_l, w_gate, w_up, w_down)


SUBLANES = 16
COMBINE_CTX = (256, 64)
COMBINE_LAT = (128, 80)


def _combine_body(bs_ref, pos_ref, eo_ref, x1_ref, mod_ref, fn_ref, y_ref, rows, sem, *,
                  tb, cap, win, n_blocks, latent, tiles_per_seq):
    i = pl.program_id(0)
    slot = i % 2

    def window_base(e, blk):
        return (bs_ref[e, blk] // SUBLANES) * SUBLANES

    def window_start(e, blk, p):
        return jnp.minimum(window_base(e, blk) + p * win, cap - win)

    def fetch(blk, p, to_slot):
        for e in range(N_EXPERTS):
            start = pl.multiple_of(e * cap + window_start(e, blk, p), SUBLANES)
            pltpu.make_async_copy(eo_ref.at[pl.ds(start, win)], rows.at[to_slot, pl.ds(e * win, win)],
                                  sem.at[to_slot]).start()

    def wait(to_slot):
        pltpu.make_async_copy(eo_ref.at[pl.ds(0, N_EXPERTS * win)], rows.at[to_slot], sem.at[to_slot]).wait()

    @pl.when(i == 0)
    def _first():
        fetch(0, 0, 0)

    @pl.when(i + 1 < n_blocks)
    def _prefetch():
        fetch(i + 1, 0, 1 - slot)

    lane = lax.broadcasted_iota(I32, (1, N_EXPERTS), 1)
    col = lax.broadcasted_iota(I32, (1, (N_EXPERTS * win)), 1)
    col_row = (col % win).astype(F32)
    expand = (lax.broadcasted_iota(I32, (N_EXPERTS, (N_EXPERTS * win)), 0) == col // win).astype(BF16)
    widen = lambda v: jnp.dot(v, expand, preferred_element_type=F32)

    pos = pos_ref[...]

    def one_pass(p, acc):
        base = jnp.zeros((1, N_EXPERTS), I32)
        start = jnp.zeros((1, N_EXPERTS), I32)
        for e in range(N_EXPERTS):
            base = jnp.where(lane == e, window_base(e, i), base)
            start = jnp.where(lane == e, window_start(e, i, p), start)
        rel = pos - base
        mine = (pos >= 0) & (rel >= p * win) & (rel < (p + 1) * win)
        window_col = jnp.where(mine, pos - start, -1).astype(F32).astype(BF16)
        select = (widen(window_col) == col_row).astype(BF16)
        return acc + jnp.dot(select, rows[slot], preferred_element_type=F32)

    wait(slot)
    moe = one_pass(0, jnp.zeros((tb, D_MODEL), F32))

    n_pass = jnp.int32(1)
    for e in range(N_EXPERTS):
        span = bs_ref[e, i + 1] - window_base(e, i)
        n_pass = jnp.maximum(n_pass, (span + win - 1) // win)

    def extra(p, acc):
        fetch(i, p, slot)
        wait(slot)
        return one_pass(p, acc)

    moe = lax.fori_loop(1, n_pass, extra, moe)
    row = 1 + i // tiles_per_seq if latent else 0
    gate2 = mod_ref[pl.ds(row, 1), 5 * D_MODEL:6 * D_MODEL]
    y_ref[...] = _rms(x1_ref[...] + gate2 * moe) * fn_ref[...]


def _combine(block_start, pos_t, expert_out, x1, mod, final_norm, cap, latent, seq_len):
    n = x1.shape[0]
    tb, win = COMBINE_LAT if latent else COMBINE_CTX
    n_blocks = n // tb
    return pl.pallas_call(
        functools.partial(_combine_body, tb=tb, cap=cap, win=win, n_blocks=n_blocks, latent=latent,
                          tiles_per_seq=seq_len // tb),
        grid_spec=pltpu.PrefetchScalarGridSpec(
            num_scalar_prefetch=1,
            grid=(n_blocks,),
            in_specs=[pl.BlockSpec((tb, N_EXPERTS), lambda i, bs: (i, 0)),
                      pl.BlockSpec(memory_space=pl.ANY),
                      pl.BlockSpec((tb, D_MODEL), lambda i, bs: (i, 0)),
                      pl.BlockSpec(mod.shape, lambda i, bs: (0, 0)),
                      pl.BlockSpec((1, D_MODEL), lambda i, bs: (0, 0))],
            out_specs=pl.BlockSpec((tb, D_MODEL), lambda i, bs: (i, 0)),
            scratch_shapes=[pltpu.VMEM((2, N_EXPERTS * win, D_MODEL), BF16), pltpu.SemaphoreType.DMA((2,))]),
        out_shape=jax.ShapeDtypeStruct((n, D_MODEL), F32),
        compiler_params=pltpu.CompilerParams(vmem_limit_bytes=52 * 2**20),
        name="combine_lat" if latent else "combine_ctx",
    )(block_start, pos_t, expert_out, x1, mod, final_norm)


def _rope_tables(seq_len):
    pos = jnp.arange(seq_len)
    half = HEAD_DIM // 2
    freqs = ROPE_BASE ** (-jnp.arange(0, half, 2, dtype=F32) / half)
    ang_r = (pos // GRID_W).astype(F32)[:, None] * freqs[None, :]
    ang_c = (pos % GRID_W).astype(F32)[:, None] * freqs[None, :]
    cos = jnp.concatenate([jnp.cos(ang_r)] * 2 + [jnp.cos(ang_c)] * 2, axis=1)
    sin = jnp.concatenate([-jnp.sin(ang_r), jnp.sin(ang_r), -jnp.sin(ang_c), jnp.sin(ang_c)], axis=1)
    return cos, sin


def kernel(x_prompt, x_sample, cache_k, cache_v, c, c_ctx, w_ada, b_ada, norm1, w_in, attn_sink, w_s, b_s, g_v,
           g_attn, g_mlp, w_out, norm2, w_router, w_gate, w_up, w_down, final_norm):
    depth = w_ada.shape[0]
    assert depth == 1, "single-layer trunk"
    batch, seq, _ = x_prompt.shape
    dec_batch, dec_seq, _ = x_sample.shape
    n_ctx = batch * seq
    n_lat = dec_batch * dec_seq
    cap_c = CAPACITY_FACTOR * n_ctx // N_EXPERTS
    cap_l = CAPACITY_FACTOR * n_lat // N_EXPERTS
    l = 0

    cvec = jnp.concatenate([c_ctx[None, :], c, jnp.zeros((MOD_ROWS - 1 - dec_batch, D_MODEL), F32)], axis=0)
    mod = _adaln(cvec, w_ada[l], b_ada[l][None, :])

    w_in_b = w_in[l].astype(BF16)
    w_out_b = w_out[l].astype(BF16)
    w_s_b = w_s[l].astype(BF16)
    b_s_t = jnp.transpose(b_s[l])
    w_router_t = jnp.transpose(w_router[l])
    row = lambda a: a.reshape(1, -1)
    shared = (mod, row(norm1[l]), w_in_b, w_s_b, b_s_t, row(g_v[l]), row(g_attn[l]), row(g_mlp[l]), w_out_b,
              row(norm2[l]), w_router_t)

    xp = x_prompt.reshape(n_ctx, D_MODEL)
    xl = x_sample.reshape(n_lat, D_MODEL)

    x1_c, h2_c, afft_c, k, v = _mix(attn_sink[l], None, xp, *shared, batch, seq)
    state_k = k.reshape(batch, 1, seq, N_KV_HEADS, HEAD_DIM)
    state_v = v.reshape(batch, 1, seq, N_KV_HEADS, HEAD_DIM)

    q, k, v, u, gv = _inproj(xl, mod, row(norm1[l]), w_in_b, _rope_tables(dec_seq), dec_seq)
    projected = (q, k, v, cache_k[:, l].reshape(dec_batch, -1, KV_WIDTH),
                 cache_v[:, l].reshape(dec_batch, -1, KV_WIDTH), u, gv)
    x1_l, h2_l, afft_l = _mix(attn_sink[l], projected, xl, *shared, dec_batch, dec_seq)

    def route(aff_t, cap, combine_tile):
        n = aff_t.shape[1]
        pos, idx, slot_gate, start = _route(aff_t.reshape(N_EXPERTS, n // LANES, LANES), cap)
        start = start.reshape(N_EXPERTS, -1)[:, ::combine_tile // LANES].astype(I32)
        starts = jnp.concatenate([start, jnp.full((N_EXPERTS, 1), cap, I32)], axis=1)
        return jnp.transpose(pos.reshape(N_EXPERTS, n)), idx, jnp.transpose(slot_gate), starts

    pos_c, idx_c, sgate_c, starts_c = route(afft_c, cap_c, COMBINE_CTX[0])
    pos_l, idx_l, sgate_l, starts_l = route(afft_l, cap_l, COMBINE_LAT[0])
    eo_c, eo_l = _experts(idx_c, idx_l, h2_c, h2_l, sgate_c, sgate_l, w_gate[l], w_up[l], w_down[l])
    fn = row(final_norm)
    y_c = _combine(starts_c, pos_c, eo_c, x1_c, mod, fn, cap_c, False, seq)
    y_l = _combine(starts_l, pos_l, eo_l, x1_l, mod, fn, cap_l, True, dec_seq)

    return (y_c.reshape(batch, seq, D_MODEL), y_l.reshape(dec_batch, dec_seq, D_MODEL), state_k, state_v)
```

```python
import functools

import jax
import jax.numpy as jnp
from jax import lax
from jax.experimental import pallas as pl
from jax.experimental.pallas import tpu as pltpu

F32 = jnp.float32
BF16 = jnp.bfloat16
I32 = jnp.int32

D_MODEL = 2048
GRID_W = 64
HEAD_DIM = 128
N_Q_HEADS = 8
N_KV_HEADS = 2
Q_PER_KV = N_Q_HEADS // N_KV_HEADS
ATTN_WIDTH = N_Q_HEADS * HEAD_DIM
KV_WIDTH = N_KV_HEADS * HEAD_DIM
N_MLP_GROUPS = 8
MLP_WIDTH = 1024
Q_END = ATTN_WIDTH
K_END = Q_END + KV_WIDTH
V_END = K_END + KV_WIDTH
U_END = V_END + MLP_WIDTH
IN_WIDTH = U_END + MLP_WIDTH
CHUNK = 128
WINDOW = 128
N_EXPERTS = 16
EXPERT_FF = 2048
CAPACITY_FACTOR = 2
ROPE_BASE = 10000.0
EPS = 1e-6
NEG_INF = -1e30
ATTN_SCALE = HEAD_DIM ** -0.5

LANES = 128
MIX_TILE = 256
MIX_SUB = 2
LAT_SPAN = MIX_TILE + 2 * WINDOW
ROW_TILES = D_MODEL // LANES
MOD_ROWS = 8


def _rms(x):
    return x * lax.rsqrt(jnp.mean(x * x, axis=-1, keepdims=True) + EPS)


def _split_bf16_f32(x):
    hi = x.astype(BF16)
    return hi, x - hi.astype(F32)


def _split_bf16(x):
    hi, rest = _split_bf16_f32(x)
    return hi, rest.astype(BF16)


def _adaln_body(c_ref, w_ref, b_ref, o_ref):
    s = jax.nn.silu(c_ref[...])
    o_ref[...] = jnp.dot(s.astype(BF16), w_ref[...].astype(BF16),
                         preferred_element_type=F32) + b_ref[...]


def _adaln(cvec, w_ada, b_ada):
    tn = 1024
    width = w_ada.shape[1]
    return pl.pallas_call(
        _adaln_body,
        grid=(width // tn,),
        in_specs=[pl.BlockSpec((MOD_ROWS, D_MODEL), lambda j: (0, 0)),
                  pl.BlockSpec((D_MODEL, tn), lambda j: (0, j)),
                  pl.BlockSpec((1, tn), lambda j: (0, j))],
        out_specs=pl.BlockSpec((MOD_ROWS, tn), lambda j: (0, j)),
        out_shape=jax.ShapeDtypeStruct((MOD_ROWS, width), F32),
        compiler_params=pltpu.CompilerParams(vmem_limit_bytes=40 * 2**20),
        name="adaln",
    )(cvec, w_ada, b_ada)


def _rope(x, cos, sin, n_heads):
    lane = lax.broadcasted_iota(I32, (x.shape[0], HEAD_DIM), 1)
    first = (lane & 32) == 0
    outs = []
    for h in range(n_heads):
        xh = x[:, h * HEAD_DIM:(h + 1) * HEAD_DIM]
        partner = jnp.where(first, pltpu.roll(xh, 96, 1), pltpu.roll(xh, 32, 1))
        outs.append(xh * cos + partner * sin)
    return jnp.concatenate(outs, axis=1)


def _inproj_body(*refs, latent, tiles_per_seq):
    if latent:
        x_ref, mod_ref, g_ref, w_ref, cos_ref, sin_ref, q_ref, k_ref, v_ref, u_ref, gv_ref = refs
    else:
        x_ref, mod_ref, g_ref, w_ref, q_ref, k_ref, v_ref, u_ref, gv_ref = refs
    i = pl.program_id(0)
    row = 1 + i // tiles_per_seq if latent else 0
    shift = mod_ref[pl.ds(row, 1), 0:D_MODEL]
    scale = mod_ref[pl.ds(row, 1), D_MODEL:2 * D_MODEL]
    h = _rms(x_ref[...]) * g_ref[...] * (1 + scale) + shift
    z = jnp.dot(h.astype(BF16), w_ref[...], preferred_element_type=F32)
    q = z[:, 0:Q_END]
    k = z[:, Q_END:K_END]
    if latent:
        cos = cos_ref[...]
        sin = sin_ref[...]
        q = _rope(q, cos, sin, N_Q_HEADS)
        k = _rope(k, cos, sin, N_KV_HEADS)
    q_ref[...] = q.astype(BF16)
    k_ref[...] = k
    v_ref[...] = z[:, K_END:V_END]
    u_ref[...] = z[:, V_END:U_END]
    gv_ref[...] = z[:, U_END:IN_WIDTH]


def _inproj(x2d, mod, norm1, w_in_bf16, rope_tables, seq_len):
    n = x2d.shape[0]
    tm = 256
    latent = rope_tables is not None
    tiles_per_seq = seq_len // tm
    in_specs = [pl.BlockSpec((tm, D_MODEL), lambda i: (i, 0)),
                pl.BlockSpec(mod.shape, lambda i: (0, 0)),
                pl.BlockSpec((1, D_MODEL), lambda i: (0, 0)),
                pl.BlockSpec((D_MODEL, IN_WIDTH), lambda i: (0, 0))]
    args = [x2d, mod, norm1, w_in_bf16]
    if latent:
        in_specs += [pl.BlockSpec((tm, HEAD_DIM), lambda i: (i % tiles_per_seq, 0))] * 2
        args += list(rope_tables)
    widths = (ATTN_WIDTH, KV_WIDTH, KV_WIDTH, MLP_WIDTH, MLP_WIDTH)
    dtypes = (BF16, F32, F32, F32, F32)
    return pl.pallas_call(
        functools.partial(_inproj_body, latent=latent, tiles_per_seq=tiles_per_seq),
        grid=(n // tm,),
        in_specs=in_specs,
        out_specs=[pl.BlockSpec((tm, w), lambda i: (i, 0)) for w in widths],
        out_shape=[jax.ShapeDtypeStruct((n, w), dt) for w, dt in zip(widths, dtypes)],
        compiler_params=pltpu.CompilerParams(vmem_limit_bytes=56 * 2**20),
        name="inproj_lat" if latent else "inproj_ctx",
    )(*args)


def _mix_body(*refs, latent, seq_len):
    if latent:
        sink_ref, q_ref, k_ref, v_ref, ck_ref, cv_ref, u_ref, gv_ref, x_ref, mod_ref, *rest = refs
        proj_refs = (q_ref, k_ref, v_ref, ck_ref, cv_ref, u_ref, gv_ref)
    else:
        sink_ref, x_ref, mod_ref, norm1_ref, win_ref, *rest = refs
        proj_refs = (norm1_ref, win_ref) + tuple(rest[-2:])
        rest = rest[:-2]
    b = pl.program_id(0)
    t = pl.program_id(1)
    row = 1 + b if latent else 0
    for sub in range(MIX_SUB):
        _mix_tile(sub, t * MIX_SUB + sub, row, latent, seq_len, sink_ref, proj_refs, x_ref, mod_ref, *rest)


def _mix_tile(sub, tile, row, latent, seq_len, sink_ref, proj_refs, x_ref,
              mod_ref, ws_ref, bst_ref, gvn_ref, gattn_ref, gmlp_ref, wout_ref, norm2_ref, wrt_ref,
              x1_ref, h2_ref, afft_ref):
    tq = MIX_TILE
    rows = slice(sub * tq, (sub + 1) * tq)

    if latent:
        q_ref, k_ref, v_ref, ck_ref, cv_ref, u_ref, gv_ref = proj_refs
        q = q_ref[rows, :]
        u = u_ref[rows, :]
        gv = gv_ref[rows, :]
        ks = pl.multiple_of(jnp.clip(tile * tq - WINDOW, 0, seq_len - LAT_SPAN), WINDOW)
        kcat = jnp.concatenate([ck_ref[0], k_ref[pl.ds(ks, LAT_SPAN), :]], axis=0).astype(BF16)
        vcat = jnp.concatenate([cv_ref[0], v_ref[pl.ds(ks, LAT_SPAN), :]], axis=0).astype(BF16)
        n_ctx = ck_ref.shape[1]
        n_keys = n_ctx + LAT_SPAN
        kidx = lax.broadcasted_iota(I32, (tq, n_keys), 1)
        qpos = tile * tq + lax.broadcasted_iota(I32, (tq, n_keys), 0)
        kpos = ks + kidx - n_ctx
        mask = (kidx < n_ctx) | (jnp.abs(qpos - kpos) <= WINDOW)
    else:
        norm1_ref, win_ref, kout_ref, vout_ref = proj_refs
        shift1 = mod_ref[pl.ds(row, 1), 0:D_MODEL]
        scale1 = mod_ref[pl.ds(row, 1), D_MODEL:2 * D_MODEL]
        h1 = _rms(x_ref[rows, :]) * norm1_ref[...] * (1 + scale1) + shift1
        z = jnp.dot(h1.astype(BF16), win_ref[...], preferred_element_type=F32)
        q = z[:, 0:Q_END].astype(BF16)
        k = z[:, Q_END:K_END]
        v = z[:, K_END:V_END]
        u = z[:, V_END:U_END]
        gv = z[:, U_END:IN_WIDTH]
        kout_ref[rows, :] = k
        vout_ref[rows, :] = v
        kcat = k.astype(BF16)
        vcat = v.astype(BF16)
        mask = None

    heads = []
    for h in range(N_Q_HEADS):
        kv = h // Q_PER_KV
        qh = q[:, h * HEAD_DIM:(h + 1) * HEAD_DIM]
        kh = kcat[:, kv * HEAD_DIM:(kv + 1) * HEAD_DIM]
        vh = vcat[:, kv * HEAD_DIM:(kv + 1) * HEAD_DIM]
        s = lax.dot_general(qh, kh, (((1,), (1,)), ((), ())), preferred_element_type=F32) * ATTN_SCALE
        if mask is not None:
            s = jnp.where(mask, s, NEG_INF)
        sink = sink_ref[h]
        m = jnp.maximum(jnp.max(s, axis=1, keepdims=True), sink)
        p = jnp.exp(s - m)
        den = jnp.sum(p, axis=1, keepdims=True) + jnp.exp(sink - m)
        heads.append(jnp.dot(p.astype(BF16), vh, preferred_element_type=F32) / den)
    attn = jnp.concatenate(heads, axis=1)

    sg = (_rms(jax.nn.gelu(gv)) * gvn_ref[...]).astype(BF16)
    bst = bst_ref[...]
    chunks = []
    for c in range(tq // CHUNK):
        groups = []
        for g in range(N_MLP_GROUPS):
            blk = sg[c * CHUNK:(c + 1) * CHUNK, g * LANES:(g + 1) * LANES]
            groups.append(jnp.dot(ws_ref[g], blk, preferred_element_type=F32) + bst[:, g:g + 1])
        chunks.append(jnp.concatenate(groups, axis=1))
    mlp = jax.nn.gelu(u) * jnp.concatenate(chunks, axis=0)

    mixed = jnp.concatenate([_rms(attn) * gattn_ref[...], _rms(mlp) * gmlp_ref[...]], axis=1)
    out = jnp.dot(mixed.astype(BF16), wout_ref[...], preferred_element_type=F32)
    gate1 = mod_ref[pl.ds(row, 1), 2 * D_MODEL:3 * D_MODEL]
    x1 = x_ref[rows, :] + gate1 * out
    x1_ref[rows, :] = x1

    shift2 = mod_ref[pl.ds(row, 1), 3 * D_MODEL:4 * D_MODEL]
    scale2 = mod_ref[pl.ds(row, 1), 4 * D_MODEL:5 * D_MODEL]
    h2 = _rms(x1) * norm2_ref[...] * (1 + scale2) + shift2

    hh = h2.astype(BF16)
    hl = (h2 - hh.astype(F32)).astype(BF16)
    wrt = wrt_ref[...]
    wth = wrt.astype(BF16)
    wtl = (wrt - wth.astype(F32)).astype(BF16)
    nt = (((1,), (1,)), ((), ()))
    both = lax.dot_general(jnp.concatenate([wth, wtl], axis=0), hh, nt, preferred_element_type=F32)
    logits_t = (both[0:N_EXPERTS] + both[N_EXPERTS:2 * N_EXPERTS]
                + lax.dot_general(wth, hl, nt, preferred_element_type=F32))
    pt = jnp.exp(logits_t - jnp.max(logits_t, axis=0, keepdims=True))
    afft_ref[:, rows] = pt / jnp.sum(pt, axis=0, keepdims=True)

    h2_ref[rows] = hh.reshape(tq, ROW_TILES, LANES)


def _mix(sink, projected, x2d, mod, norm1, w_in_bf16, w_s_bf16, b_s_t, g_v, g_attn, g_mlp, w_out_bf16, norm2,
         w_router_t, batch, seq_len):
    n = x2d.shape[0]
    latent = projected is not None
    block = MIX_SUB * MIX_TILE
    const2 = lambda b, t: (0, 0)
    resident = dict(pipeline_mode=pl.Buffered(1))
    if latent:
        grid = (batch, seq_len // block)
        tiles = grid[1]
        tok = lambda b, t: (b * tiles + t, 0)
        q, k, v, ck, cv, u, gv = projected
        seq = lambda b, t: (b, 0)
        in_specs = [pl.BlockSpec(memory_space=pltpu.SMEM),
                    pl.BlockSpec((block, ATTN_WIDTH), tok),
                    pl.BlockSpec((seq_len, KV_WIDTH), seq),
                    pl.BlockSpec((seq_len, KV_WIDTH), seq),
                    pl.BlockSpec((1,) + ck.shape[1:], lambda b, t: (b, 0, 0)),
                    pl.BlockSpec((1,) + cv.shape[1:], lambda b, t: (b, 0, 0)),
                    pl.BlockSpec((block, MLP_WIDTH), tok),
                    pl.BlockSpec((block, MLP_WIDTH), tok),
                    pl.BlockSpec((block, D_MODEL), tok),
                    pl.BlockSpec(mod.shape, const2)]
        args = [sink, q, k, v, ck, cv, u, gv, x2d, mod]
    else:
        assert seq_len == MIX_TILE
        grid = (batch // MIX_SUB, 1)
        tiles = 1
        tok = lambda b, t: (b, 0)
        in_specs = [pl.BlockSpec(memory_space=pltpu.SMEM),
                    pl.BlockSpec((block, D_MODEL), tok),
                    pl.BlockSpec(mod.shape, const2),
                    pl.BlockSpec((1, D_MODEL), const2),
                    pl.BlockSpec(w_in_bf16.shape, const2, **resident)]
        args = [sink, x2d, mod, norm1, w_in_bf16]
    in_specs += [pl.BlockSpec(w_s_bf16.shape, lambda b, t: (0, 0, 0)),
                 pl.BlockSpec(b_s_t.shape, const2),
                 pl.BlockSpec((1, MLP_WIDTH), const2),
                 pl.BlockSpec((1, ATTN_WIDTH), const2),
                 pl.BlockSpec((1, MLP_WIDTH), const2),
                 pl.BlockSpec(w_out_bf16.shape, const2, **resident),
                 pl.BlockSpec((1, D_MODEL), const2),
                 pl.BlockSpec(w_router_t.shape, const2)]
    args += [w_s_bf16, b_s_t, g_v, g_attn, g_mlp, w_out_bf16, norm2, w_router_t]
    out_specs = [pl.BlockSpec((block, D_MODEL), tok),
                 pl.BlockSpec((block, ROW_TILES, LANES), lambda b, t: tok(b, t) + (0,)),
                 pl.BlockSpec((N_EXPERTS, block), lambda b, t: (0, b * tiles + t))]
    out_shape = [jax.ShapeDtypeStruct((n, D_MODEL), F32),
                 jax.ShapeDtypeStruct((n, ROW_TILES, LANES), BF16),
                 jax.ShapeDtypeStruct((N_EXPERTS, n), F32)]
    if not latent:
        out_specs += [pl.BlockSpec((block, KV_WIDTH), tok)] * 2
        out_shape += [jax.ShapeDtypeStruct((n, KV_WIDTH), F32)] * 2
    return pl.pallas_call(
        functools.partial(_mix_body, latent=latent, seq_len=seq_len),
        grid=grid,
        in_specs=in_specs,
        out_specs=out_specs,
        out_shape=out_shape,
        compiler_params=pltpu.CompilerParams(vmem_limit_bytes=58 * 2**20),
        name="mix_lat" if latent else "mix_ctx",
    )(*args)


def _exclusive_counts(flags, upper, lower):
    n_e, nb, _ = flags.shape
    f = flags.astype(BF16)
    local = jnp.dot(f.reshape(n_e * nb, LANES), upper, preferred_element_type=F32).reshape(n_e, nb, LANES)
    totals = jnp.broadcast_to(jnp.sum(flags, axis=2, keepdims=True), flags.shape).astype(BF16)
    before = jnp.stack([jnp.dot(lower, totals[e], preferred_element_type=F32) for e in range(n_e)], axis=0)
    return local, before


def _route_body(aff_ref, pos_ref, idx_ref, sgate_ref, start_ref, local_ref, gate_ref, total_ref, *, cap):
    aff = aff_ref[...]
    n_e, nb, _ = aff.shape
    reduce_tokens = lambda x: jnp.sum(jnp.sum(x, axis=2, keepdims=True), axis=1, keepdims=True)

    def bisect(i, thr):
        cand = thr | jnp.left_shift(jnp.int32(1), 30 - i)
        n_ge = reduce_tokens((aff >= lax.bitcast_convert_type(cand, F32)).astype(F32))
        return jnp.where(n_ge >= cap, cand, thr)

    thr = lax.fori_loop(0, 31, bisect, jnp.zeros((n_e, 1, 1), I32))

    upper = (lax.broadcasted_iota(I32, (LANES, LANES), 0) < lax.broadcasted_iota(I32, (LANES, LANES), 1)).astype(BF16)
    lower = (lax.broadcasted_iota(I32, (nb, nb), 1) < lax.broadcasted_iota(I32, (nb, nb), 0)).astype(BF16)

    above = aff >= lax.bitcast_convert_type(thr + 1, F32)
    tied = (aff >= lax.bitcast_convert_type(thr, F32)) & jnp.logical_not(above)
    need = cap - reduce_tokens(above.astype(F32))
    tie_local, tie_before = _exclusive_counts(tied.astype(F32), upper, lower)
    sel = above | (tied & (tie_local + tie_before < need))
    sel_f = sel.astype(F32)

    local, before = _exclusive_counts(sel_f, upper, lower)
    pos_ref[...] = jnp.where(sel, (local + before).astype(I32), -1)
    gate_ref[...] = aff

    local_ref[...] = (local + sel_f).astype(BF16)
    start_ref[...] = before[:, :, 0:1]
    total_ref[...] = jnp.sum(sel_f, axis=2, keepdims=True)
    slot = lax.broadcasted_iota(I32, (1, cap), 1).astype(F32)
    block_id = lax.broadcasted_iota(I32, (nb, 1), 0).astype(F32)
    lane_id = lax.broadcasted_iota(I32, (LANES, 1), 0).astype(F32)
    tn = (((0,), (0,)), ((), ()))

    def lists(e, carry):
        start = start_ref[e]
        end = start + total_ref[e]
        blk = jnp.sum((end <= slot).astype(F32), axis=0, keepdims=True)
        onehot = block_id == blk
        onehot_b = onehot.astype(BF16)
        rank = slot - jnp.sum(jnp.where(onehot, start, 0.0), axis=0, keepdims=True)
        counts = lax.dot_general(local_ref[e], onehot_b, tn, preferred_element_type=F32)
        tok = jnp.sum((counts <= rank).astype(F32), axis=0, keepdims=True)
        idx_ref[pl.ds(e, 1), :] = (blk * LANES + tok).astype(I32)
        g1, rest = _split_bf16_f32(gate_ref[e])
        g2, rest = _split_bf16_f32(rest)
        block_gates = sum(lax.dot_general(g, onehot_b, tn, preferred_element_type=F32)
                          for g in (g1, g2, rest.astype(BF16)))
        sgate_ref[pl.ds(e, 1), :] = jnp.sum(jnp.where(lane_id == tok, block_gates, 0.0), axis=0, keepdims=True)
        return carry

    lax.fori_loop(0, n_e, lists, 0)


def _route(aff_blocks, cap):
    n_e, nb, _ = aff_blocks.shape
    return pl.pallas_call(
        functools.partial(_route_body, cap=cap),
        out_shape=[jax.ShapeDtypeStruct(aff_blocks.shape, I32), jax.ShapeDtypeStruct((n_e, cap), I32),
                   jax.ShapeDtypeStruct((n_e, cap), F32), jax.ShapeDtypeStruct((n_e, nb, 1), F32)],
        scratch_shapes=[pltpu.VMEM(aff_blocks.shape, BF16), pltpu.VMEM(aff_blocks.shape, F32),
                        pltpu.VMEM((n_e, nb, 1), F32)],
        compiler_params=pltpu.CompilerParams(vmem_limit_bytes=48 * 2**20),
        name="route",
    )(aff_blocks)


FF_TILE = 256
N_FF_TILES = EXPERT_FF // FF_TILE
OUT_TILE = 256
N_OUT_TILES = D_MODEL // OUT_TILE
EXPERT_STEPS = N_FF_TILES + N_OUT_TILES


def _expert_body(idxc_ref, idxl_ref, hc_ref, hl_ref, gc_ref, gl_ref, wg_ref, wu_ref, wd_ref,
                 oc_ref, ol_ref, gbuf, xb_ref, hid_ref, sem, *, cap_c, cap_l):
    e = pl.program_id(0)
    s = pl.program_id(1)
    cap = cap_c + cap_l
    cur = e % 2
    nxt = 1 - cur
    next_expert = (e + 1) % N_EXPERTS

    def issue(expert, part, parts):
        for idx_ref, src, n_rows, base in ((idxc_ref, hc_ref, cap_c, 0), (idxl_ref, hl_ref, cap_l, cap_c)):
            per = n_rows // parts
            for j in range(per):
                r = part * per + j
                pltpu.make_async_copy(src.at[idx_ref[expert * n_rows + r]], gbuf.at[base + r], sem).start()

    def arrived():
        pltpu.make_async_copy(hc_ref.at[pl.ds(0, cap)], gbuf, sem).wait()

    def unpack(slot, part, parts):
        per = cap // parts
        r0 = pl.multiple_of(part * per, per)
        xb_ref[slot, pl.ds(r0, per), :] = gbuf[pl.ds(r0, per)].reshape(per, D_MODEL)

    @pl.when((e == 0) & (s == 0))
    def _first():
        def gather(part, carry):
            issue(0, part, N_FF_TILES)
            return carry
        lax.fori_loop(0, N_FF_TILES, gather, 0)
        arrived()

        def relayout(part, carry):
            unpack(0, part, N_OUT_TILES)
            return carry
        lax.fori_loop(0, N_OUT_TILES, relayout, 0)

    @pl.when(s < N_FF_TILES)
    def _up():
        issue(next_expert, s, N_FF_TILES)
        x = xb_ref[cur]
        a = jnp.dot(x, wg_ref[0].astype(BF16), preferred_element_type=F32)
        b = jnp.dot(x, wu_ref[0].astype(BF16), preferred_element_type=F32)
        hid_ref[s] = (jax.nn.silu(a) * b).astype(BF16)

    @pl.when(s == N_FF_TILES)
    def _arrived():
        arrived()

    @pl.when(s >= N_FF_TILES)
    def _down():
        unpack(nxt, s - N_FF_TILES, N_OUT_TILES)
        wd = wd_ref[0].astype(BF16)
        acc = jnp.dot(hid_ref[0], wd[0:FF_TILE, :], preferred_element_type=F32)
        for j in range(1, N_FF_TILES):
            acc = acc + jnp.dot(hid_ref[j], wd[j * FF_TILE:(j + 1) * FF_TILE, :], preferred_element_type=F32)
        for g_ref, o_ref, rows in ((gc_ref, oc_ref, slice(0, cap_c)), (gl_ref, ol_ref, slice(cap_c, cap))):
            gates = g_ref[...]
            lane = lax.broadcasted_iota(I32, gates.shape, 1)
            gate = jnp.sum(jnp.where(lane == e, gates, 0.0), axis=1, keepdims=True)
            o_ref[...] = (acc[rows, :] * gate).astype(BF16)


def _experts(idx_c, idx_l, h2_c, h2_l, sgate_c, sgate_l, w_gate, w_up, w_down):
    cap_c = idx_c.shape[1]
    cap_l = idx_l.shape[1]
    cap = cap_c + cap_l
    idx_c = idx_c.reshape(-1)
    idx_l = idx_l.reshape(-1)
    const = lambda e, s, ic, il: (0, 0)
    up_idx = lambda e, s, ic, il: (e, 0, jnp.minimum(s, N_FF_TILES - 1))
    down_idx = lambda e, s, ic, il: (e, 0, jnp.maximum(s - N_FF_TILES, 0))
    out_idx = lambda e, s, ic, il: (e, jnp.maximum(s - N_FF_TILES, 0))
    out_c = jax.ShapeDtypeStruct((N_EXPERTS * cap_c, D_MODEL), BF16)
    out_l = jax.ShapeDtypeStruct((N_EXPERTS * cap_l, D_MODEL), BF16)
    return pl.pallas_call(
        functools.partial(_expert_body, cap_c=cap_c, cap_l=cap_l),
        grid_spec=pltpu.PrefetchScalarGridSpec(
            num_scalar_prefetch=2,
            grid=(N_EXPERTS, EXPERT_STEPS),
            in_specs=[pl.BlockSpec(memory_space=pl.ANY),
                      pl.BlockSpec(memory_space=pl.ANY),
                      pl.BlockSpec(sgate_c.shape, const),
                      pl.BlockSpec(sgate_l.shape, const),
                      pl.BlockSpec((1, D_MODEL, FF_TILE), up_idx),
                      pl.BlockSpec((1, D_MODEL, FF_TILE), up_idx),
                      pl.BlockSpec((1, EXPERT_FF, OUT_TILE), down_idx)],
            out_specs=[pl.BlockSpec((cap_c, OUT_TILE), out_idx), pl.BlockSpec((cap_l, OUT_TILE), out_idx)],
            scratch_shapes=[pltpu.VMEM((cap, ROW_TILES, LANES), BF16),
                            pltpu.VMEM((2, cap, D_MODEL), BF16),
                            pltpu.VMEM((N_FF_TILES, cap, FF_TILE), BF16),
                            pltpu.SemaphoreType.DMA(())]),
        out_shape=[out_c, out_l],
        compiler_params=pltpu.CompilerParams(vmem_limit_bytes=60 * 2**20),
        name="experts",
    )(idx_c, idx_l, h2_c, h2_l, sgate_c, sgate_l, w_gate, w_up, w_down)


SUBLANES = 16
COMBINE_CTX = (256, 64)
COMBINE_LAT = (128, 80)


def _combine_body(bs_ref, pos_ref, eo_ref, x1_ref, mod_ref, fn_ref, y_ref, rows, sem, *,
                  tb, cap, win, n_blocks, latent, tiles_per_seq):
    i = pl.program_id(0)
    slot = i % 2

    def window_base(e, blk):
        return (bs_ref[e, blk] // SUBLANES) * SUBLANES

    def window_start(e, blk, p):
        return jnp.minimum(window_base(e, blk) + p * win, cap - win)

    def fetch(blk, p, to_slot):
        for e in range(N_EXPERTS):
            start = pl.multiple_of(e * cap + window_start(e, blk, p), SUBLANES)
            pltpu.make_async_copy(eo_ref.at[pl.ds(start, win)], rows.at[to_slot, pl.ds(e * win, win)],
                                  sem.at[to_slot]).start()

    def wait(to_slot):
        pltpu.make_async_copy(eo_ref.at[pl.ds(0, N_EXPERTS * win)], rows.at[to_slot], sem.at[to_slot]).wait()

    @pl.when(i == 0)
    def _first():
        fetch(0, 0, 0)

    @pl.when(i + 1 < n_blocks)
    def _prefetch():
        fetch(i + 1, 0, 1 - slot)

    lane = lax.broadcasted_iota(I32, (1, N_EXPERTS), 1)
    col = lax.broadcasted_iota(I32, (1, (N_EXPERTS * win)), 1)
    col_row = (col % win).astype(F32)
    expand = (lax.broadcasted_iota(I32, (N_EXPERTS, (N_EXPERTS * win)), 0) == col // win).astype(BF16)
    widen = lambda v: jnp.dot(v, expand, preferred_element_type=F32)

    pos = pos_ref[...]

    def one_pass(p, acc):
        base = jnp.zeros((1, N_EXPERTS), I32)
        start = jnp.zeros((1, N_EXPERTS), I32)
        for e in range(N_EXPERTS):
            base = jnp.where(lane == e, window_base(e, i), base)
            start = jnp.where(lane == e, window_start(e, i, p), start)
        rel = pos - base
        mine = (pos >= 0) & (rel >= p * win) & (rel < (p + 1) * win)
        window_col = jnp.where(mine, pos - start, -1).astype(F32).astype(BF16)
        select = (widen(window_col) == col_row).astype(BF16)
        return acc + jnp.dot(select, rows[slot], preferred_element_type=F32)

    wait(slot)
    moe = one_pass(0, jnp.zeros((tb, D_MODEL), F32))

    n_pass = jnp.int32(1)
    for e in range(N_EXPERTS):
        span = bs_ref[e, i + 1] - window_base(e, i)
        n_pass = jnp.maximum(n_pass, (span + win - 1) // win)

    def extra(p, acc):
        fetch(i, p, slot)
        wait(slot)
        return one_pass(p, acc)

    moe = lax.fori_loop(1, n_pass, extra, moe)
    row = 1 + i // tiles_per_seq if latent else 0
    gate2 = mod_ref[pl.ds(row, 1), 5 * D_MODEL:6 * D_MODEL]
    y_ref[...] = _rms(x1_ref[...] + gate2 * moe) * fn_ref[...]


def _combine(block_start, pos_t, expert_out, x1, mod, final_norm, cap, latent, seq_len):
    n = x1.shape[0]
    tb, win = COMBINE_LAT if latent else COMBINE_CTX
    n_blocks = n // tb
    return pl.pallas_call(
        functools.partial(_combine_body, tb=tb, cap=cap, win=win, n_blocks=n_blocks, latent=latent,
                          tiles_per_seq=seq_len // tb),
        grid_spec=pltpu.PrefetchScalarGridSpec(
            num_scalar_prefetch=1,
            grid=(n_blocks,),
            in_specs=[pl.BlockSpec((tb, N_EXPERTS), lambda i, bs: (i, 0)),
                      pl.BlockSpec(memory_space=pl.ANY),
                      pl.BlockSpec((tb, D_MODEL), lambda i, bs: (i, 0)),
                      pl.BlockSpec(mod.shape, lambda i, bs: (0, 0)),
                      pl.BlockSpec((1, D_MODEL), lambda i, bs: (0, 0))],
            out_specs=pl.BlockSpec((tb, D_MODEL), lambda i, bs: (i, 0)),
            scratch_shapes=[pltpu.VMEM((2, N_EXPERTS * win, D_MODEL), BF16), pltpu.SemaphoreType.DMA((2,))]),
        out_shape=jax.ShapeDtypeStruct((n, D_MODEL), F32),
        compiler_params=pltpu.CompilerParams(vmem_limit_bytes=52 * 2**20),
        name="combine_lat" if latent else "combine_ctx",
    )(block_start, pos_t, expert_out, x1, mod, final_norm)


def _rope_tables(seq_len):
    pos = jnp.arange(seq_len)
    half = HEAD_DIM // 2
    freqs = ROPE_BASE ** (-jnp.arange(0, half, 2, dtype=F32) / half)
    ang_r = (pos // GRID_W).astype(F32)[:, None] * freqs[None, :]
    ang_c = (pos % GRID_W).astype(F32)[:, None] * freqs[None, :]
    cos = jnp.concatenate([jnp.cos(ang_r)] * 2 + [jnp.cos(ang_c)] * 2, axis=1)
    sin = jnp.concatenate([-jnp.sin(ang_r), jnp.sin(ang_r), -jnp.sin(ang_c), jnp.sin(ang_c)], axis=1)
    return cos, sin


def kernel(x_prompt, x_sample, cache_k, cache_v, c, c_ctx, w_ada, b_ada, norm1, w_in, attn_sink, w_s, b_s, g_v,
           g_attn, g_mlp, w_out, norm2, w_router, w_gate, w_up, w_down, final_norm):
    depth = w_ada.shape[0]
    assert depth == 1, "single-layer trunk"
    batch, seq, _ = x_prompt.shape
    dec_batch, dec_seq, _ = x_sample.shape
    n_ctx = batch * seq
    n_lat = dec_batch * dec_seq
    cap_c = CAPACITY_FACTOR * n_ctx // N_EXPERTS
    cap_l = CAPACITY_FACTOR * n_lat // N_EXPERTS
    l = 0

    cvec = jnp.concatenate([c_ctx[None, :], c, jnp.zeros((MOD_ROWS - 1 - dec_batch, D_MODEL), F32)], axis=0)
    mod = _adaln(cvec, w_ada[l], b_ada[l][None, :])

    w_in_b = w_in[l].astype(BF16)
    w_out_b = w_out[l].astype(BF16)
    w_s_b = w_s[l].astype(BF16)
    b_s_t = jnp.transpose(b_s[l])
    w_router_t = jnp.transpose(w_router[l])
    row = lambda a: a.reshape(1, -1)
    shared = (mod, row(norm1[l]), w_in_b, w_s_b, b_s_t, row(g_v[l]), row(g_attn[l]), row(g_mlp[l]), w_out_b,
              row(norm2[l]), w_router_t)

    xp = x_prompt.reshape(n_ctx, D_MODEL)
    xl = x_sample.reshape(n_lat, D_MODEL)

    x1_c, h2_c, afft_c, k, v = _mix(attn_sink[l], None, xp, *shared, batch, seq)
    state_k = k.reshape(batch, 1, seq, N_KV_HEADS, HEAD_DIM)
    state_v = v.reshape(batch, 1, seq, N_KV_HEADS, HEAD_DIM)

    q, k, v, u, gv = _inproj(xl, mod, row(norm1[l]), w_in_b, _rope_tables(dec_seq), dec_seq)
    projected = (q, k, v, cache_k[:, l].reshape(dec_batch, -1, KV_WIDTH),
                 cache_v[:, l].reshape(dec_batch, -1, KV_WIDTH), u, gv)
    x1_l, h2_l, afft_l = _mix(attn_sink[l], projected, xl, *shared, dec_batch, dec_seq)

    def route(aff_t, cap, combine_tile):
        n = aff_t.shape[1]
        pos, idx, slot_gate, start = _route(aff_t.reshape(N_EXPERTS, n // LANES, LANES), cap)
        start = start.reshape(N_EXPERTS, -1)[:, ::combine_tile // LANES].astype(I32)
        starts = jnp.concatenate([start, jnp.full((N_EXPERTS, 1), cap, I32)], axis=1)
        return jnp.transpose(pos.reshape(N_EXPERTS, n)), idx, jnp.transpose(slot_gate), starts

    pos_c, idx_c, sgate_c, starts_c = route(afft_c, cap_c, COMBINE_CTX[0])
    pos_l, idx_l, sgate_l, starts_l = route(afft_l, cap_l, COMBINE_LAT[0])
    eo_c, eo_l = _experts(idx_c, idx_l, h2_c, h2_l, sgate_c, sgate_l, w_gate[l], w_up[l], w_down[l])
    fn = row(final_norm)
    y_c = _combine(starts_c, pos_c, eo_c, x1_c, mod, fn, cap_c, False, seq)
    y_l = _combine(starts_l, pos_l, eo_l, x1_l, mod, fn, cap_l, True, dec_seq)

    return (y_c.reshape(batch, seq, D_MODEL), y_l.reshape(dec_batch, dec_seq, D_MODEL), state_k, state_v)
```

```python
import functools

import jax
import jax.numpy as jnp
from jax import lax
from jax.experimental import pallas as pl
from jax.experimental.pallas import tpu as pltpu

F32 = jnp.float32
BF16 = jnp.bfloat16
I32 = jnp.int32

D_MODEL = 2048
GRID_W = 64
HEAD_DIM = 128
N_Q_HEADS = 8
N_KV_HEADS = 2
Q_PER_KV = N_Q_HEADS // N_KV_HEADS
ATTN_WIDTH = N_Q_HEADS * HEAD_DIM
KV_WIDTH = N_KV_HEADS * HEAD_DIM
N_MLP_GROUPS = 8
MLP_WIDTH = 1024
Q_END = ATTN_WIDTH
K_END = Q_END + KV_WIDTH
V_END = K_END + KV_WIDTH
U_END = V_END + MLP_WIDTH
IN_WIDTH = U_END + MLP_WIDTH
CHUNK = 128
WINDOW = 128
N_EXPERTS = 16
EXPERT_FF = 2048
CAPACITY_FACTOR = 2
ROPE_BASE = 10000.0
EPS = 1e-6
NEG_INF = -1e30
ATTN_SCALE = HEAD_DIM ** -0.5

LANES = 128
MIX_TILE = 256
MIX_SUB = 2
LAT_SPAN = MIX_TILE + 2 * WINDOW
ROW_TILES = D_MODEL // LANES
MOD_ROWS = 8


def _rms(x):
    return x * lax.rsqrt(jnp.mean(x * x, axis=-1, keepdims=True) + EPS)


def _split_bf16_f32(x):
    hi = x.astype(BF16)
    return hi, x - hi.astype(F32)


def _adaln_body(c_ref, w_ref, b_ref, o_ref):
    s = jax.nn.silu(c_ref[...])
    o_ref[...] = jnp.dot(s.astype(BF16), w_ref[...].astype(BF16),
                         preferred_element_type=F32) + b_ref[...]


def _adaln(cvec, w_ada, b_ada):
    tn = 1024
    width = w_ada.shape[1]
    return pl.pallas_call(
        _adaln_body,
        grid=(width // tn,),
        in_specs=[pl.BlockSpec((MOD_ROWS, D_MODEL), lambda j: (0, 0)),
                  pl.BlockSpec((D_MODEL, tn), lambda j: (0, j)),
                  pl.BlockSpec((1, tn), lambda j: (0, j))],
        out_specs=pl.BlockSpec((MOD_ROWS, tn), lambda j: (0, j)),
        out_shape=jax.ShapeDtypeStruct((MOD_ROWS, width), F32),
        compiler_params=pltpu.CompilerParams(vmem_limit_bytes=40 * 2**20),
        name="adaln",
    )(cvec, w_ada, b_ada)


def _rope(x, cos, sin, n_heads):
    lane = lax.broadcasted_iota(I32, (x.shape[0], HEAD_DIM), 1)
    first = (lane & 32) == 0
    outs = []
    for h in range(n_heads):
        xh = x[:, h * HEAD_DIM:(h + 1) * HEAD_DIM]
        partner = jnp.where(first, pltpu.roll(xh, 96, 1), pltpu.roll(xh, 32, 1))
        outs.append(xh * cos + partner * sin)
    return jnp.concatenate(outs, axis=1)


def _inproj_body(*refs, latent, tiles_per_seq):
    if latent:
        x_ref, mod_ref, g_ref, w_ref, cos_ref, sin_ref, q_ref, k_ref, v_ref, u_ref, gv_ref = refs
    else:
        x_ref, mod_ref, g_ref, w_ref, q_ref, k_ref, v_ref, u_ref, gv_ref = refs
    i = pl.program_id(0)
    row = 1 + i // tiles_per_seq if latent else 0
    shift = mod_ref[pl.ds(row, 1), 0:D_MODEL]
    scale = mod_ref[pl.ds(row, 1), D_MODEL:2 * D_MODEL]
    h = _rms(x_ref[...]) * g_ref[...] * (1 + scale) + shift
    z = jnp.dot(h.astype(BF16), w_ref[...], preferred_element_type=F32)
    q = z[:, 0:Q_END]
    k = z[:, Q_END:K_END]
    if latent:
        cos = cos_ref[...]
        sin = sin_ref[...]
        q = _rope(q, cos, sin, N_Q_HEADS)
        k = _rope(k, cos, sin, N_KV_HEADS)
    q_ref[...] = q.astype(BF16)
    k_ref[...] = k
    v_ref[...] = z[:, K_END:V_END]
    u_ref[...] = z[:, V_END:U_END]
    gv_ref[...] = z[:, U_END:IN_WIDTH]


def _inproj(x2d, mod, norm1, w_in_bf16, rope_tables, seq_len):
    n = x2d.shape[0]
    tm = 256
    latent = rope_tables is not None
    tiles_per_seq = seq_len // tm
    in_specs = [pl.BlockSpec((tm, D_MODEL), lambda i: (i, 0)),
                pl.BlockSpec(mod.shape, lambda i: (0, 0)),
                pl.BlockSpec((1, D_MODEL), lambda i: (0, 0)),
                pl.BlockSpec((D_MODEL, IN_WIDTH), lambda i: (0, 0))]
    args = [x2d, mod, norm1, w_in_bf16]
    if latent:
        in_specs += [pl.BlockSpec((tm, HEAD_DIM), lambda i: (i % tiles_per_seq, 0))] * 2
        args += list(rope_tables)
    widths = (ATTN_WIDTH, KV_WIDTH, KV_WIDTH, MLP_WIDTH, MLP_WIDTH)
    dtypes = (BF16, F32, F32, F32, F32)
    return pl.pallas_call(
        functools.partial(_inproj_body, latent=latent, tiles_per_seq=tiles_per_seq),
        grid=(n // tm,),
        in_specs=in_specs,
        out_specs=[pl.BlockSpec((tm, w), lambda i: (i, 0)) for w in widths],
        out_shape=[jax.ShapeDtypeStruct((n, w), dt) for w, dt in zip(widths, dtypes)],
        compiler_params=pltpu.CompilerParams(vmem_limit_bytes=56 * 2**20),
        name="inproj_lat" if latent else "inproj_ctx",
    )(*args)


def _mix_body(*refs, latent, seq_len):
    if latent:
        sink_ref, q_ref, k_ref, v_ref, ck_ref, cv_ref, u_ref, gv_ref, x_ref, mod_ref, *rest = refs
        proj_refs = (q_ref, k_ref, v_ref, ck_ref, cv_ref, u_ref, gv_ref)
    else:
        sink_ref, x_ref, mod_ref, norm1_ref, win_ref, *rest = refs
        proj_refs = (norm1_ref, win_ref) + tuple(rest[-2:])
        rest = rest[:-2]
    b = pl.program_id(0)
    t = pl.program_id(1)
    row = 1 + b if latent else 0
    for sub in range(MIX_SUB):
        _mix_tile(sub, t * MIX_SUB + sub, row, latent, seq_len, sink_ref, proj_refs, x_ref, mod_ref, *rest)


def _mix_tile(sub, tile, row, latent, seq_len, sink_ref, proj_refs, x_ref,
              mod_ref, ws_ref, bst_ref, gvn_ref, gattn_ref, gmlp_ref, wout_ref, norm2_ref, wrt_ref,
              x1_ref, h2_ref, afft_ref):
    tq = MIX_TILE
    rows = slice(sub * tq, (sub + 1) * tq)

    if latent:
        q_ref, k_ref, v_ref, ck_ref, cv_ref, u_ref, gv_ref = proj_refs
        q = q_ref[rows, :]
        u = u_ref[rows, :]
        gv = gv_ref[rows, :]
        ks = pl.multiple_of(jnp.clip(tile * tq - WINDOW, 0, seq_len - LAT_SPAN), WINDOW)
        kcat = jnp.concatenate([ck_ref[0], k_ref[pl.ds(ks, LAT_SPAN), :]], axis=0).astype(BF16)
        vcat = jnp.concatenate([cv_ref[0], v_ref[pl.ds(ks, LAT_SPAN), :]], axis=0).astype(BF16)
        n_ctx = ck_ref.shape[1]
        n_keys = n_ctx + LAT_SPAN
        kidx = lax.broadcasted_iota(I32, (tq, n_keys), 1)
        qpos = tile * tq + lax.broadcasted_iota(I32, (tq, n_keys), 0)
        kpos = ks + kidx - n_ctx
        mask = (kidx < n_ctx) | (jnp.abs(qpos - kpos) <= WINDOW)
    else:
        norm1_ref, win_ref, kout_ref, vout_ref = proj_refs
        shift1 = mod_ref[pl.ds(row, 1), 0:D_MODEL]
        scale1 = mod_ref[pl.ds(row, 1), D_MODEL:2 * D_MODEL]
        h1 = _rms(x_ref[rows, :]) * norm1_ref[...] * (1 + scale1) + shift1
        z = jnp.dot(h1.astype(BF16), win_ref[...], preferred_element_type=F32)
        q = z[:, 0:Q_END].astype(BF16)
        k = z[:, Q_END:K_END]
        v = z[:, K_END:V_END]
        u = z[:, V_END:U_END]
        gv = z[:, U_END:IN_WIDTH]
        kout_ref[rows, :] = k
        vout_ref[rows, :] = v
        kcat = k.astype(BF16)
        vcat = v.astype(BF16)
        mask = None

    heads = []
    for h in range(N_Q_HEADS):
        kv = h // Q_PER_KV
        qh = q[:, h * HEAD_DIM:(h + 1) * HEAD_DIM]
        kh = kcat[:, kv * HEAD_DIM:(kv + 1) * HEAD_DIM]
        vh = vcat[:, kv * HEAD_DIM:(kv + 1) * HEAD_DIM]
        s = lax.dot_general(qh, kh, (((1,), (1,)), ((), ())), preferred_element_type=F32) * ATTN_SCALE
        if mask is not None:
            s = jnp.where(mask, s, NEG_INF)
        sink = sink_ref[h]
        m = jnp.maximum(jnp.max(s, axis=1, keepdims=True), sink)
        p = jnp.exp(s - m)
        den = jnp.sum(p, axis=1, keepdims=True) + jnp.exp(sink - m)
        heads.append(jnp.dot(p.astype(BF16), vh, preferred_element_type=F32) / den)
    attn = jnp.concatenate(heads, axis=1)

    sg = (_rms(jax.nn.gelu(gv)) * gvn_ref[...]).astype(BF16)
    bst = bst_ref[...]
    chunks = []
    for c in range(tq // CHUNK):
        groups = []
        for g in range(N_MLP_GROUPS):
            blk = sg[c * CHUNK:(c + 1) * CHUNK, g * LANES:(g + 1) * LANES]
            groups.append(jnp.dot(ws_ref[g], blk, preferred_element_type=F32) + bst[:, g:g + 1])
        chunks.append(jnp.concatenate(groups, axis=1))
    mlp = jax.nn.gelu(u) * jnp.concatenate(chunks, axis=0)

    mixed = jnp.concatenate([_rms(attn) * gattn_ref[...], _rms(mlp) * gmlp_ref[...]], axis=1)
    out = jnp.dot(mixed.astype(BF16), wout_ref[...], preferred_element_type=F32)
    gate1 = mod_ref[pl.ds(row, 1), 2 * D_MODEL:3 * D_MODEL]
    x1 = x_ref[rows, :] + gate1 * out
    x1_ref[rows, :] = x1

    shift2 = mod_ref[pl.ds(row, 1), 3 * D_MODEL:4 * D_MODEL]
    scale2 = mod_ref[pl.ds(row, 1), 4 * D_MODEL:5 * D_MODEL]
    h2 = _rms(x1) * norm2_ref[...] * (1 + scale2) + shift2

    hh = h2.astype(BF16)
    hl = (h2 - hh.astype(F32)).astype(BF16)
    wrt = wrt_ref[...]
    wth = wrt.astype(BF16)
    wtl = (wrt - wth.astype(F32)).astype(BF16)
    nt = (((1,), (1,)), ((), ()))
    both = lax.dot_general(jnp.concatenate([wth, wtl], axis=0), hh, nt, preferred_element_type=F32)
    logits_t = (both[0:N_EXPERTS] + both[N_EXPERTS:2 * N_EXPERTS]
                + lax.dot_general(wth, hl, nt, preferred_element_type=F32))
    pt = jnp.exp(logits_t - jnp.max(logits_t, axis=0, keepdims=True))
    afft_ref[:, rows] = pt / jnp.sum(pt, axis=0, keepdims=True)

    h2_ref[rows] = hh.reshape(tq, ROW_TILES, LANES)


def _mix(sink, projected, x2d, mod, norm1, w_in_bf16, w_s_bf16, b_s_t, g_v, g_attn, g_mlp, w_out_bf16, norm2,
         w_router_t, batch, seq_len):
    n = x2d.shape[0]
    latent = projected is not None
    block = MIX_SUB * MIX_TILE
    const2 = lambda b, t: (0, 0)
    resident = dict(pipeline_mode=pl.Buffered(1))
    if latent:
        grid = (batch, seq_len // block)
        tiles = grid[1]
        tok = lambda b, t: (b * tiles + t, 0)
        q, k, v, ck, cv, u, gv = projected
        seq = lambda b, t: (b, 0)
        in_specs = [pl.BlockSpec(memory_space=pltpu.SMEM),
                    pl.BlockSpec((block, ATTN_WIDTH), tok),
                    pl.BlockSpec((seq_len, KV_WIDTH), seq),
                    pl.BlockSpec((seq_len, KV_WIDTH), seq),
                    pl.BlockSpec((1,) + ck.shape[1:], lambda b, t: (b, 0, 0)),
                    pl.BlockSpec((1,) + cv.shape[1:], lambda b, t: (b, 0, 0)),
                    pl.BlockSpec((block, MLP_WIDTH), tok),
                    pl.BlockSpec((block, MLP_WIDTH), tok),
                    pl.BlockSpec((block, D_MODEL), tok),
                    pl.BlockSpec(mod.shape, const2)]
        args = [sink, q, k, v, ck, cv, u, gv, x2d, mod]
    else:
        assert seq_len == MIX_TILE
        grid = (batch // MIX_SUB, 1)
        tiles = 1
        tok = lambda b, t: (b, 0)
        in_specs = [pl.BlockSpec(memory_space=pltpu.SMEM),
                    pl.BlockSpec((block, D_MODEL), tok),
                    pl.BlockSpec(mod.shape, const2),
                    pl.BlockSpec((1, D_MODEL), const2),
                    pl.BlockSpec(w_in_bf16.shape, const2, **resident)]
        args = [sink, x2d, mod, norm1, w_in_bf16]
    in_specs += [pl.BlockSpec(w_s_bf16.shape, lambda b, t: (0, 0, 0)),
                 pl.BlockSpec(b_s_t.shape, const2),
                 pl.BlockSpec((1, MLP_WIDTH), const2),
                 pl.BlockSpec((1, ATTN_WIDTH), const2),
                 pl.BlockSpec((1, MLP_WIDTH), const2),
                 pl.BlockSpec(w_out_bf16.shape, const2, **resident),
                 pl.BlockSpec((1, D_MODEL), const2),
                 pl.BlockSpec(w_router_t.shape, const2)]
    args += [w_s_bf16, b_s_t, g_v, g_attn, g_mlp, w_out_bf16, norm2, w_router_t]
    out_specs = [pl.BlockSpec((block, D_MODEL), tok),
                 pl.BlockSpec((block, ROW_TILES, LANES), lambda b, t: tok(b, t) + (0,)),
                 pl.BlockSpec((N_EXPERTS, block), lambda b, t: (0, b * tiles + t))]
    out_shape = [jax.ShapeDtypeStruct((n, D_MODEL), F32),
                 jax.ShapeDtypeStruct((n, ROW_TILES, LANES), BF16),
                 jax.ShapeDtypeStruct((N_EXPERTS, n), F32)]
    if not latent:
        out_specs += [pl.BlockSpec((block, KV_WIDTH), tok)] * 2
        out_shape += [jax.ShapeDtypeStruct((n, KV_WIDTH), F32)] * 2
    return pl.pallas_call(
        functools.partial(_mix_body, latent=latent, seq_len=seq_len),
        grid=grid,
        in_specs=in_specs,
        out_specs=out_specs,
        out_shape=out_shape,
        compiler_params=pltpu.CompilerParams(vmem_limit_bytes=58 * 2**20),
        name="mix_lat" if latent else "mix_ctx",
    )(*args)


def _exclusive_counts(flags, upper, lower):
    n_e, nb, _ = flags.shape
    f = flags.astype(BF16)
    local = jnp.dot(f.reshape(n_e * nb, LANES), upper, preferred_element_type=F32).reshape(n_e, nb, LANES)
    totals = jnp.broadcast_to(jnp.sum(flags, axis=2, keepdims=True), flags.shape).astype(BF16)
    before = jnp.stack([jnp.dot(lower, totals[e], preferred_element_type=F32) for e in range(n_e)], axis=0)
    return local, before


def _route_body(aff_ref, pos_ref, idx_ref, sgate_ref, start_ref, local_ref, gate_ref, total_ref, *, cap):
    aff = aff_ref[...]
    n_e, nb, _ = aff.shape
    reduce_tokens = lambda x: jnp.sum(jnp.sum(x, axis=2, keepdims=True), axis=1, keepdims=True)

    def bisect(i, thr):
        cand = thr | jnp.left_shift(jnp.int32(1), 30 - i)
        n_ge = reduce_tokens((aff >= lax.bitcast_convert_type(cand, F32)).astype(F32))
        return jnp.where(n_ge >= cap, cand, thr)

    thr = lax.fori_loop(0, 31, bisect, jnp.zeros((n_e, 1, 1), I32))

    upper = (lax.broadcasted_iota(I32, (LANES, LANES), 0) < lax.broadcasted_iota(I32, (LANES, LANES), 1)).astype(BF16)
    lower = (lax.broadcasted_iota(I32, (nb, nb), 1) < lax.broadcasted_iota(I32, (nb, nb), 0)).astype(BF16)

    above = aff >= lax.bitcast_convert_type(thr + 1, F32)
    tied = (aff >= lax.bitcast_convert_type(thr, F32)) & jnp.logical_not(above)
    need = cap - reduce_tokens(above.astype(F32))
    tie_local, tie_before = _exclusive_counts(tied.astype(F32), upper, lower)
    sel = above | (tied & (tie_local + tie_before < need))
    sel_f = sel.astype(F32)

    local, before = _exclusive_counts(sel_f, upper, lower)
    pos_ref[...] = jnp.where(sel, (local + before).astype(I32), -1)
    gate_ref[...] = aff

    local_ref[...] = (local + sel_f).astype(BF16)
    start_ref[...] = before[:, :, 0:1]
    total_ref[...] = jnp.sum(sel_f, axis=2, keepdims=True)
    slot = lax.broadcasted_iota(I32, (1, cap), 1).astype(F32)
    block_id = lax.broadcasted_iota(I32, (nb, 1), 0).astype(F32)
    lane_id = lax.broadcasted_iota(I32, (LANES, 1), 0).astype(F32)
    tn = (((0,), (0,)), ((), ()))

    def lists(e, carry):
        start = start_ref[e]
        end = start + total_ref[e]
        blk = jnp.sum((end <= slot).astype(F32), axis=0, keepdims=True)
        onehot = block_id == blk
        onehot_b = onehot.astype(BF16)
        rank = slot - jnp.sum(jnp.where(onehot, start, 0.0), axis=0, keepdims=True)
        counts = lax.dot_general(local_ref[e], onehot_b, tn, preferred_element_type=F32)
        tok = jnp.sum((counts <= rank).astype(F32), axis=0, keepdims=True)
        idx_ref[pl.ds(e, 1), :] = (blk * LANES + tok).astype(I32)
        g1, rest = _split_bf16_f32(gate_ref[e])
        g2, rest = _split_bf16_f32(rest)
        block_gates = sum(lax.dot_general(g, onehot_b, tn, preferred_element_type=F32)
                          for g in (g1, g2, rest.astype(BF16)))
        sgate_ref[pl.ds(e, 1), :] = jnp.sum(jnp.where(lane_id == tok, block_gates, 0.0), axis=0, keepdims=True)
        return carry

    lax.fori_loop(0, n_e, lists, 0)


def _route(aff_blocks, cap):
    n_e, nb, _ = aff_blocks.shape
    return pl.pallas_call(
        functools.partial(_route_body, cap=cap),
        out_shape=[jax.ShapeDtypeStruct(aff_blocks.shape, I32), jax.ShapeDtypeStruct((n_e, cap), I32),
                   jax.ShapeDtypeStruct((n_e, cap), F32), jax.ShapeDtypeStruct((n_e, nb, 1), F32)],
        scratch_shapes=[pltpu.VMEM(aff_blocks.shape, BF16), pltpu.VMEM(aff_blocks.shape, F32),
                        pltpu.VMEM((n_e, nb, 1), F32)],
        compiler_params=pltpu.CompilerParams(vmem_limit_bytes=48 * 2**20),
        name="route",
    )(aff_blocks)


FF_TILE = 256
N_FF_TILES = EXPERT_FF // FF_TILE
OUT_TILE = 256
N_OUT_TILES = D_MODEL // OUT_TILE
EXPERT_STEPS = N_FF_TILES + N_OUT_TILES


def _expert_body(idxc_ref, idxl_ref, hc_ref, hl_ref, gc_ref, gl_ref, wg_ref, wu_ref, wd_ref,
                 oc_ref, ol_ref, gbuf, xb_ref, hid_ref, sem, *, cap_c, cap_l):
    e = pl.program_id(0)
    s = pl.program_id(1)
    cap = cap_c + cap_l
    cur = e % 2
    nxt = 1 - cur
    next_expert = (e + 1) % N_EXPERTS

    def issue(expert, part, parts):
        for idx_ref, src, n_rows, base in ((idxc_ref, hc_ref, cap_c, 0), (idxl_ref, hl_ref, cap_l, cap_c)):
            per = n_rows // parts
            for j in range(per):
                r = part * per + j
                pltpu.make_async_copy(src.at[idx_ref[expert * n_rows + r]], gbuf.at[base + r], sem).start()

    def arrived():
        pltpu.make_async_copy(hc_ref.at[pl.ds(0, cap)], gbuf, sem).wait()

    def unpack(slot, part, parts):
        per = cap // parts
        r0 = pl.multiple_of(part * per, per)
        xb_ref[slot, pl.ds(r0, per), :] = gbuf[pl.ds(r0, per)].reshape(per, D_MODEL)

    @pl.when((e == 0) & (s == 0))
    def _first():
        def gather(part, carry):
            issue(0, part, N_FF_TILES)
            return carry
        lax.fori_loop(0, N_FF_TILES, gather, 0)
        arrived()

        def relayout(part, carry):
            unpack(0, part, N_OUT_TILES)
            return carry
        lax.fori_loop(0, N_OUT_TILES, relayout, 0)

    @pl.when(s < N_FF_TILES)
    def _up():
        issue(next_expert, s, N_FF_TILES)
        x = xb_ref[cur]
        a = jnp.dot(x, wg_ref[0].astype(BF16), preferred_element_type=F32)
        b = jnp.dot(x, wu_ref[0].astype(BF16), preferred_element_type=F32)
        hid_ref[s] = (jax.nn.silu(a) * b).astype(BF16)

    @pl.when(s == N_FF_TILES)
    def _arrived():
        arrived()

    @pl.when(s >= N_FF_TILES)
    def _down():
        unpack(nxt, s - N_FF_TILES, N_OUT_TILES)
        wd = wd_ref[0].astype(BF16)
        acc = jnp.dot(hid_ref[0], wd[0:FF_TILE, :], preferred_element_type=F32)
        for j in range(1, N_FF_TILES):
            acc = acc + jnp.dot(hid_ref[j], wd[j * FF_TILE:(j + 1) * FF_TILE, :], preferred_element_type=F32)
        for g_ref, o_ref, rows in ((gc_ref, oc_ref, slice(0, cap_c)), (gl_ref, ol_ref, slice(cap_c, cap))):
            gates = g_ref[...]
            lane = lax.broadcasted_iota(I32, gates.shape, 1)
            gate = jnp.sum(jnp.where(lane == e, gates, 0.0), axis=1, keepdims=True)
            o_ref[...] = (acc[rows, :] * gate).astype(BF16)


def _experts(idx_c, idx_l, h2_c, h2_l, sgate_c, sgate_l, w_gate, w_up, w_down):
    cap_c = idx_c.shape[1]
    cap_l = idx_l.shape[1]
    cap = cap_c + cap_l
    idx_c = idx_c.reshape(-1)
    idx_l = idx_l.reshape(-1)
    const = lambda e, s, ic, il: (0, 0)
    up_idx = lambda e, s, ic, il: (e, 0, jnp.minimum(s, N_FF_TILES - 1))
    down_idx = lambda e, s, ic, il: (e, 0, jnp.maximum(s - N_FF_TILES, 0))
    out_idx = lambda e, s, ic, il: (e, jnp.maximum(s - N_FF_TILES, 0))
    out_c = jax.ShapeDtypeStruct((N_EXPERTS * cap_c, D_MODEL), BF16)
    out_l = jax.ShapeDtypeStruct((N_EXPERTS * cap_l, D_MODEL), BF16)
    return pl.pallas_call(
        functools.partial(_expert_body, cap_c=cap_c, cap_l=cap_l),
        grid_spec=pltpu.PrefetchScalarGridSpec(
            num_scalar_prefetch=2,
            grid=(N_EXPERTS, EXPERT_STEPS),
            in_specs=[pl.BlockSpec(memory_space=pl.ANY),
                      pl.BlockSpec(memory_space=pl.ANY),
                      pl.BlockSpec(sgate_c.shape, const),
                      pl.BlockSpec(sgate_l.shape, const),
                      pl.BlockSpec((1, D_MODEL, FF_TILE), up_idx),
                      pl.BlockSpec((1, D_MODEL, FF_TILE), up_idx),
                      pl.BlockSpec((1, EXPERT_FF, OUT_TILE), down_idx)],
            out_specs=[pl.BlockSpec((cap_c, OUT_TILE), out_idx), pl.BlockSpec((cap_l, OUT_TILE), out_idx)],
            scratch_shapes=[pltpu.VMEM((cap, ROW_TILES, LANES), BF16),
                            pltpu.VMEM((2, cap, D_MODEL), BF16),
                            pltpu.VMEM((N_FF_TILES, cap, FF_TILE), BF16),
                            pltpu.SemaphoreType.DMA(())]),
        out_shape=[out_c, out_l],
        compiler_params=pltpu.CompilerParams(vmem_limit_bytes=60 * 2**20),
        name="experts",
    )(idx_c, idx_l, h2_c, h2_l, sgate_c, sgate_l, w_gate, w_up, w_down)


SUBLANES = 16
COMBINE_CTX = (256, 64)
COMBINE_LAT = (128, 80)


def _combine_body(bs_ref, pos_ref, eo_ref, x1_ref, mod_ref, fn_ref, y_ref, rows, sem, *,
                  tb, cap, win, n_blocks, latent, tiles_per_seq):
    i = pl.program_id(0)
    slot = i % 2

    def window_base(e, blk):
        return (bs_ref[e, blk] // SUBLANES) * SUBLANES

    def window_start(e, blk, p):
        return jnp.minimum(window_base(e, blk) + p * win, cap - win)

    def fetch(blk, p, to_slot):
        for e in range(N_EXPERTS):
            start = pl.multiple_of(e * cap + window_start(e, blk, p), SUBLANES)
            pltpu.make_async_copy(eo_ref.at[pl.ds(start, win)], rows.at[to_slot, pl.ds(e * win, win)],
                                  sem.at[to_slot]).start()

    def wait(to_slot):
        pltpu.make_async_copy(eo_ref.at[pl.ds(0, N_EXPERTS * win)], rows.at[to_slot], sem.at[to_slot]).wait()

    @pl.when(i == 0)
    def _first():
        fetch(0, 0, 0)

    @pl.when(i + 1 < n_blocks)
    def _prefetch():
        fetch(i + 1, 0, 1 - slot)

    lane = lax.broadcasted_iota(I32, (1, N_EXPERTS), 1)
    col = lax.broadcasted_iota(I32, (1, (N_EXPERTS * win)), 1)
    col_row = (col % win).astype(F32)
    expand = (lax.broadcasted_iota(I32, (N_EXPERTS, (N_EXPERTS * win)), 0) == col // win).astype(BF16)
    widen = lambda v: jnp.dot(v, expand, preferred_element_type=F32)

    pos = pos_ref[...]

    def one_pass(p, acc):
        base = jnp.zeros((1, N_EXPERTS), I32)
        start = jnp.zeros((1, N_EXPERTS), I32)
        for e in range(N_EXPERTS):
            base = jnp.where(lane == e, window_base(e, i), base)
            start = jnp.where(lane == e, window_start(e, i, p), start)
        rel = pos - base
        mine = (pos >= 0) & (rel >= p * win) & (rel < (p + 1) * win)
        window_col = jnp.where(mine, pos - start, -1).astype(F32).astype(BF16)
        select = (widen(window_col) == col_row).astype(BF16)
        return acc + jnp.dot(select, rows[slot], preferred_element_type=F32)

    wait(slot)
    moe = one_pass(0, jnp.zeros((tb, D_MODEL), F32))

    n_pass = jnp.int32(1)
    for e in range(N_EXPERTS):
        span = bs_ref[e, i + 1] - window_base(e, i)
        n_pass = jnp.maximum(n_pass, (span + win - 1) // win)

    def extra(p, acc):
        fetch(i, p, slot)
        wait(slot)
        return one_pass(p, acc)

    moe = lax.fori_loop(1, n_pass, extra, moe)
    row = 1 + i // tiles_per_seq if latent else 0
    gate2 = mod_ref[pl.ds(row, 1), 5 * D_MODEL:6 * D_MODEL]
    y_ref[...] = _rms(x1_ref[...] + gate2 * moe) * fn_ref[...]


def _combine(block_start, pos_t, expert_out, x1, mod, final_norm, cap, latent, seq_len):
    n = x1.shape[0]
    tb, win = COMBINE_LAT if latent else COMBINE_CTX
    n_blocks = n // tb
    return pl.pallas_call(
        functools.partial(_combine_body, tb=tb, cap=cap, win=win, n_blocks=n_blocks, latent=latent,
                          tiles_per_seq=seq_len // tb),
        grid_spec=pltpu.PrefetchScalarGridSpec(
            num_scalar_prefetch=1,
            grid=(n_blocks,),
            in_specs=[pl.BlockSpec((tb, N_EXPERTS), lambda i, bs: (i, 0)),
                      pl.BlockSpec(memory_space=pl.ANY),
                      pl.BlockSpec((tb, D_MODEL), lambda i, bs: (i, 0)),
                      pl.BlockSpec(mod.shape, lambda i, bs: (0, 0)),
                      pl.BlockSpec((1, D_MODEL), lambda i, bs: (0, 0))],
            out_specs=pl.BlockSpec((tb, D_MODEL), lambda i, bs: (i, 0)),
            scratch_shapes=[pltpu.VMEM((2, N_EXPERTS * win, D_MODEL), BF16), pltpu.SemaphoreType.DMA((2,))]),
        out_shape=jax.ShapeDtypeStruct((n, D_MODEL), F32),
        compiler_params=pltpu.CompilerParams(vmem_limit_bytes=52 * 2**20),
        name="combine_lat" if latent else "combine_ctx",
    )(block_start, pos_t, expert_out, x1, mod, final_norm)


def _rope_tables(seq_len):
    pos = jnp.arange(seq_len)
    half = HEAD_DIM // 2
    freqs = ROPE_BASE ** (-jnp.arange(0, half, 2, dtype=F32) / half)
    ang_r = (pos // GRID_W).astype(F32)[:, None] * freqs[None, :]
    ang_c = (pos % GRID_W).astype(F32)[:, None] * freqs[None, :]
    cos = jnp.concatenate([jnp.cos(ang_r)] * 2 + [jnp.cos(ang_c)] * 2, axis=1)
    sin = jnp.concatenate([-jnp.sin(ang_r), jnp.sin(ang_r), -jnp.sin(ang_c), jnp.sin(ang_c)], axis=1)
    return cos, sin


def kernel(x_prompt, x_sample, cache_k, cache_v, c, c_ctx, w_ada, b_ada, norm1, w_in, attn_sink, w_s, b_s, g_v,
           g_attn, g_mlp, w_out, norm2, w_router, w_gate, w_up, w_down, final_norm):
    depth = w_ada.shape[0]
    assert depth == 1, "single-layer trunk"
    batch, seq, _ = x_prompt.shape
    dec_batch, dec_seq, _ = x_sample.shape
    n_ctx = batch * seq
    n_lat = dec_batch * dec_seq
    cap_c = CAPACITY_FACTOR * n_ctx // N_EXPERTS
    cap_l = CAPACITY_FACTOR * n_lat // N_EXPERTS
    l = 0

    cvec = jnp.concatenate([c_ctx[None, :], c, jnp.zeros((MOD_ROWS - 1 - dec_batch, D_MODEL), F32)], axis=0)
    mod = _adaln(cvec, w_ada[l], b_ada[l][None, :])

    w_in_b = w_in[l].astype(BF16)
    w_out_b = w_out[l].astype(BF16)
    w_s_b = w_s[l].astype(BF16)
    b_s_t = jnp.transpose(b_s[l])
    w_router_t = jnp.transpose(w_router[l])
    row = lambda a: a.reshape(1, -1)
    shared = (mod, row(norm1[l]), w_in_b, w_s_b, b_s_t, row(g_v[l]), row(g_attn[l]), row(g_mlp[l]), w_out_b,
              row(norm2[l]), w_router_t)

    xp = x_prompt.reshape(n_ctx, D_MODEL)
    xl = x_sample.reshape(n_lat, D_MODEL)

    x1_c, h2_c, afft_c, k, v = _mix(attn_sink[l], None, xp, *shared, batch, seq)
    state_k = k.reshape(batch, 1, seq, N_KV_HEADS, HEAD_DIM)
    state_v = v.reshape(batch, 1, seq, N_KV_HEADS, HEAD_DIM)

    q, k, v, u, gv = _inproj(xl, mod, row(norm1[l]), w_in_b, _rope_tables(dec_seq), dec_seq)
    projected = (q, k, v, cache_k[:, l].reshape(dec_batch, -1, KV_WIDTH),
                 cache_v[:, l].reshape(dec_batch, -1, KV_WIDTH), u, gv)
    x1_l, h2_l, afft_l = _mix(attn_sink[l], projected, xl, *shared, dec_batch, dec_seq)

    def route(aff_t, cap, combine_tile):
        n = aff_t.shape[1]
        pos, idx, slot_gate, start = _route(aff_t.reshape(N_EXPERTS, n // LANES, LANES), cap)
        start = start.reshape(N_EXPERTS, -1)[:, ::combine_tile // LANES].astype(I32)
        starts = jnp.concatenate([start, jnp.full((N_EXPERTS, 1), cap, I32)], axis=1)
        return jnp.transpose(pos.reshape(N_EXPERTS, n)), idx, jnp.transpose(slot_gate), starts

    pos_c, idx_c, sgate_c, starts_c = route(afft_c, cap_c, COMBINE_CTX[0])
    pos_l, idx_l, sgate_l, starts_l = route(afft_l, cap_l, COMBINE_LAT[0])
    eo_c, eo_l = _experts(idx_c, idx_l, h2_c, h2_l, sgate_c, sgate_l, w_gate[l], w_up[l], w_down[l])
    fn = row(final_norm)
    y_c = _combine(starts_c, pos_c, eo_c, x1_c, mod, fn, cap_c, False, seq)
    y_l = _combine(starts_l, pos_l, eo_l, x1_l, mod, fn, cap_l, True, dec_seq)

    return (y_c.reshape(batch, seq, D_MODEL), y_l.reshape(dec_batch, dec_seq, D_MODEL), state_k, state_v)
```

```python
import functools

import jax
import jax.numpy as jnp
from jax import lax
from jax.experimental import pallas as pl
from jax.experimental.pallas import tpu as pltpu

F32 = jnp.float32
BF16 = jnp.bfloat16
I32 = jnp.int32

D_MODEL = 2048
GRID_W = 64
HEAD_DIM = 128
N_Q_HEADS = 8
N_KV_HEADS = 2
Q_PER_KV = N_Q_HEADS // N_KV_HEADS
ATTN_WIDTH = N_Q_HEADS * HEAD_DIM
KV_WIDTH = N_KV_HEADS * HEAD_DIM
N_MLP_GROUPS = 8
MLP_WIDTH = 1024
Q_END = ATTN_WIDTH
K_END = Q_END + KV_WIDTH
V_END = K_END + KV_WIDTH
U_END = V_END + MLP_WIDTH
IN_WIDTH = U_END + MLP_WIDTH
CHUNK = 128
WINDOW = 128
N_EXPERTS = 16
EXPERT_FF = 2048
CAPACITY_FACTOR = 2
ROPE_BASE = 10000.0
EPS = 1e-6
NEG_INF = -1e30
ATTN_SCALE = HEAD_DIM ** -0.5

LANES = 128
MIX_TILE = 256
MIX_SUB = 2
LAT_SPAN = MIX_TILE + 2 * WINDOW
ROW_TILES = D_MODEL // LANES
MOD_ROWS = 8


def _rms(x):
    return x * lax.rsqrt(jnp.mean(x * x, axis=-1, keepdims=True) + EPS)


def _split_bf16_f32(x):
    hi = x.astype(BF16)
    return hi, x - hi.astype(F32)


def _adaln_body(c_ref, w_ref, b_ref, o_ref):
    s = jax.nn.silu(c_ref[...])
    o_ref[...] = jnp.dot(s.astype(BF16), w_ref[...].astype(BF16),
                         preferred_element_type=F32) + b_ref[...]


def _adaln(cvec, w_ada, b_ada):
    tn = 1024
    width = w_ada.shape[1]
    return pl.pallas_call(
        _adaln_body,
        grid=(width // tn,),
        in_specs=[pl.BlockSpec((MOD_ROWS, D_MODEL), lambda j: (0, 0)),
                  pl.BlockSpec((D_MODEL, tn), lambda j: (0, j)),
                  pl.BlockSpec((1, tn), lambda j: (0, j))],
        out_specs=pl.BlockSpec((MOD_ROWS, tn), lambda j: (0, j)),
        out_shape=jax.ShapeDtypeStruct((MOD_ROWS, width), F32),
        compiler_params=pltpu.CompilerParams(vmem_limit_bytes=40 * 2**20),
        name="adaln",
    )(cvec, w_ada, b_ada)


def _rope(x, cos, sin, n_heads):
    lane = lax.broadcasted_iota(I32, (x.shape[0], HEAD_DIM), 1)
    first = (lane & 32) == 0
    outs = []
    for h in range(n_heads):
        xh = x[:, h * HEAD_DIM:(h + 1) * HEAD_DIM]
        partner = jnp.where(first, pltpu.roll(xh, 96, 1), pltpu.roll(xh, 32, 1))
        outs.append(xh * cos + partner * sin)
    return jnp.concatenate(outs, axis=1)


def _inproj_body(*refs, latent, tiles_per_seq):
    if latent:
        x_ref, mod_ref, g_ref, w_ref, cos_ref, sin_ref, q_ref, k_ref, v_ref, u_ref, gv_ref = refs
    else:
        x_ref, mod_ref, g_ref, w_ref, q_ref, k_ref, v_ref, u_ref, gv_ref = refs
    i = pl.program_id(0)
    row = 1 + i // tiles_per_seq if latent else 0
    shift = mod_ref[pl.ds(row, 1), 0:D_MODEL]
    scale = mod_ref[pl.ds(row, 1), D_MODEL:2 * D_MODEL]
    h = _rms(x_ref[...]) * g_ref[...] * (1 + scale) + shift
    z = jnp.dot(h.astype(BF16), w_ref[...], preferred_element_type=F32)
    q = z[:, 0:Q_END]
    k = z[:, Q_END:K_END]
    if latent:
        cos = cos_ref[...]
        sin = sin_ref[...]
        q = _rope(q, cos, sin, N_Q_HEADS)
        k = _rope(k, cos, sin, N_KV_HEADS)
    q_ref[...] = q.astype(BF16)
    k_ref[...] = k
    v_ref[...] = z[:, K_END:V_END]
    u_ref[...] = z[:, V_END:U_END]
    gv_ref[...] = z[:, U_END:IN_WIDTH]


def _inproj(x2d, mod, norm1, w_in_bf16, rope_tables, seq_len):
    n = x2d.shape[0]
    tm = 256
    latent = rope_tables is not None
    tiles_per_seq = seq_len // tm
    in_specs = [pl.BlockSpec((tm, D_MODEL), lambda i: (i, 0)),
                pl.BlockSpec(mod.shape, lambda i: (0, 0)),
                pl.BlockSpec((1, D_MODEL), lambda i: (0, 0)),
                pl.BlockSpec((D_MODEL, IN_WIDTH), lambda i: (0, 0))]
    args = [x2d, mod, norm1, w_in_bf16]
    if latent:
        in_specs += [pl.BlockSpec((tm, HEAD_DIM), lambda i: (i % tiles_per_seq, 0))] * 2
        args += list(rope_tables)
    widths = (ATTN_WIDTH, KV_WIDTH, KV_WIDTH, MLP_WIDTH, MLP_WIDTH)
    dtypes = (BF16, F32, F32, F32, F32)
    return pl.pallas_call(
        functools.partial(_inproj_body, latent=latent, tiles_per_seq=tiles_per_seq),
        grid=(n // tm,),
        in_specs=in_specs,
        out_specs=[pl.BlockSpec((tm, w), lambda i: (i, 0)) for w in widths],
        out_shape=[jax.ShapeDtypeStruct((n, w), dt) for w, dt in zip(widths, dtypes)],
        compiler_params=pltpu.CompilerParams(vmem_limit_bytes=56 * 2**20),
        name="inproj_lat" if latent else "inproj_ctx",
    )(*args)


def _mix_body(*refs, latent, seq_len):
    if latent:
        sink_ref, q_ref, k_ref, v_ref, ck_ref, cv_ref, u_ref, gv_ref, x_ref, mod_ref, *rest = refs
        proj_refs = (q_ref, k_ref, v_ref, ck_ref, cv_ref, u_ref, gv_ref)
    else:
        sink_ref, x_ref, mod_ref, norm1_ref, win_ref, *rest = refs
        proj_refs = (norm1_ref, win_ref) + tuple(rest[-2:])
        rest = rest[:-2]
    b = pl.program_id(0)
    t = pl.program_id(1)
    row = 1 + b if latent else 0
    for sub in range(MIX_SUB):
        _mix_tile(sub, t * MIX_SUB + sub, row, latent, seq_len, sink_ref, proj_refs, x_ref, mod_ref, *rest)


def _mix_tile(sub, tile, row, latent, seq_len, sink_ref, proj_refs, x_ref,
              mod_ref, ws_ref, bst_ref, gvn_ref, gattn_ref, gmlp_ref, wout_ref, norm2_ref, wrt_ref,
              x1_ref, h2_ref, afft_ref):
    tq = MIX_TILE
    rows = slice(sub * tq, (sub + 1) * tq)

    if latent:
        q_ref, k_ref, v_ref, ck_ref, cv_ref, u_ref, gv_ref = proj_refs
        q = q_ref[rows, :]
        u = u_ref[rows, :]
        gv = gv_ref[rows, :]
        ks = pl.multiple_of(jnp.clip(tile * tq - WINDOW, 0, seq_len - LAT_SPAN), WINDOW)
        kcat = jnp.concatenate([ck_ref[0], k_ref[pl.ds(ks, LAT_SPAN), :]], axis=0).astype(BF16)
        vcat = jnp.concatenate([cv_ref[0], v_ref[pl.ds(ks, LAT_SPAN), :]], axis=0).astype(BF16)
        n_ctx = ck_ref.shape[1]
        n_keys = n_ctx + LAT_SPAN
        kidx = lax.broadcasted_iota(I32, (tq, n_keys), 1)
        qpos = tile * tq + lax.broadcasted_iota(I32, (tq, n_keys), 0)
        kpos = ks + kidx - n_ctx
        mask = (kidx < n_ctx) | (jnp.abs(qpos - kpos) <= WINDOW)
    else:
        norm1_ref, win_ref, kout_ref, vout_ref = proj_refs
        shift1 = mod_ref[pl.ds(row, 1), 0:D_MODEL]
        scale1 = mod_ref[pl.ds(row, 1), D_MODEL:2 * D_MODEL]
        h1 = _rms(x_ref[rows, :]) * norm1_ref[...] * (1 + scale1) + shift1
        z = jnp.dot(h1.astype(BF16), win_ref[...], preferred_element_type=F32)
        q = z[:, 0:Q_END].astype(BF16)
        k = z[:, Q_END:K_END]
        v = z[:, K_END:V_END]
        u = z[:, V_END:U_END]
        gv = z[:, U_END:IN_WIDTH]
        kout_ref[rows, :] = k
        vout_ref[rows, :] = v
        kcat = k.astype(BF16)
        vcat = v.astype(BF16)
        mask = None

    heads = []
    for h in range(N_Q_HEADS):
        kv = h // Q_PER_KV
        qh = q[:, h * HEAD_DIM:(h + 1) * HEAD_DIM]
        kh = kcat[:, kv * HEAD_DIM:(kv + 1) * HEAD_DIM]
        vh = vcat[:, kv * HEAD_DIM:(kv + 1) * HEAD_DIM]
        s = lax.dot_general(qh, kh, (((1,), (1,)), ((), ())), preferred_element_type=F32) * ATTN_SCALE
        if mask is not None:
            s = jnp.where(mask, s, NEG_INF)
        sink = sink_ref[h]
        m = jnp.maximum(jnp.max(s, axis=1, keepdims=True), sink)
        p = jnp.exp(s - m)
        den = jnp.sum(p, axis=1, keepdims=True) + jnp.exp(sink - m)
        heads.append(jnp.dot(p.astype(BF16), vh, preferred_element_type=F32) / den)
    attn = jnp.concatenate(heads, axis=1)

    sg = (_rms(jax.nn.gelu(gv)) * gvn_ref[...]).astype(BF16)
    bst = bst_ref[...]
    chunks = []
    for c in range(tq // CHUNK):
        groups = []
        for g in range(N_MLP_GROUPS):
            blk = sg[c * CHUNK:(c + 1) * CHUNK, g * LANES:(g + 1) * LANES]
            groups.append(jnp.dot(ws_ref[g], blk, preferred_element_type=F32) + bst[:, g:g + 1])
        chunks.append(jnp.concatenate(groups, axis=1))
    mlp = jax.nn.gelu(u) * jnp.concatenate(chunks, axis=0)

    mixed = jnp.concatenate([_rms(attn) * gattn_ref[...], _rms(mlp) * gmlp_ref[...]], axis=1)
    out = jnp.dot(mixed.astype(BF16), wout_ref[...], preferred_element_type=F32)
    gate1 = mod_ref[pl.ds(row, 1), 2 * D_MODEL:3 * D_MODEL]
    x1 = x_ref[rows, :] + gate1 * out
    x1_ref[rows, :] = x1

    shift2 = mod_ref[pl.ds(row, 1), 3 * D_MODEL:4 * D_MODEL]
    scale2 = mod_ref[pl.ds(row, 1), 4 * D_MODEL:5 * D_MODEL]
    h2 = _rms(x1) * norm2_ref[...] * (1 + scale2) + shift2

    hh = h2.astype(BF16)
    hl = (h2 - hh.astype(F32)).astype(BF16)
    wrt = wrt_ref[...]
    wth = wrt.astype(BF16)
    wtl = (wrt - wth.astype(F32)).astype(BF16)
    nt = (((1,), (1,)), ((), ()))
    both = lax.dot_general(jnp.concatenate([wth, wtl], axis=0), hh, nt, preferred_element_type=F32)
    logits_t = (both[0:N_EXPERTS] + both[N_EXPERTS:2 * N_EXPERTS]
                + lax.dot_general(wth, hl, nt, preferred_element_type=F32))
    pt = jnp.exp(logits_t - jnp.max(logits_t, axis=0, keepdims=True))
    afft_ref[:, rows] = pt / jnp.sum(pt, axis=0, keepdims=True)

    h2_ref[rows] = hh.reshape(tq, ROW_TILES, LANES)


def _mix(sink, projected, x2d, mod, norm1, w_in_bf16, w_s_bf16, b_s_t, g_v, g_attn, g_mlp, w_out_bf16, norm2,
         w_router_t, batch, seq_len):
    n = x2d.shape[0]
    latent = projected is not None
    block = MIX_SUB * MIX_TILE
    const2 = lambda b, t: (0, 0)
    resident = dict(pipeline_mode=pl.Buffered(1))
    if latent:
        grid = (batch, seq_len // block)
        tiles = grid[1]
        tok = lambda b, t: (b * tiles + t, 0)
        q, k, v, ck, cv, u, gv = projected
        seq = lambda b, t: (b, 0)
        in_specs = [pl.BlockSpec(memory_space=pltpu.SMEM),
                    pl.BlockSpec((block, ATTN_WIDTH), tok),
                    pl.BlockSpec((seq_len, KV_WIDTH), seq),
                    pl.BlockSpec((seq_len, KV_WIDTH), seq),
                    pl.BlockSpec((1,) + ck.shape[1:], lambda b, t: (b, 0, 0)),
                    pl.BlockSpec((1,) + cv.shape[1:], lambda b, t: (b, 0, 0)),
                    pl.BlockSpec((block, MLP_WIDTH), tok),
                    pl.BlockSpec((block, MLP_WIDTH), tok),
                    pl.BlockSpec((block, D_MODEL), tok),
                    pl.BlockSpec(mod.shape, const2)]
        args = [sink, q, k, v, ck, cv, u, gv, x2d, mod]
    else:
        assert seq_len == MIX_TILE
        grid = (batch // MIX_SUB, 1)
        tiles = 1
        tok = lambda b, t: (b, 0)
        in_specs = [pl.BlockSpec(memory_space=pltpu.SMEM),
                    pl.BlockSpec((block, D_MODEL), tok),
                    pl.BlockSpec(mod.shape, const2),
                    pl.BlockSpec((1, D_MODEL), const2),
                    pl.BlockSpec(w_in_bf16.shape, const2, **resident)]
        args = [sink, x2d, mod, norm1, w_in_bf16]
    in_specs += [pl.BlockSpec(w_s_bf16.shape, lambda b, t: (0, 0, 0)),
                 pl.BlockSpec(b_s_t.shape, const2),
                 pl.BlockSpec((1, MLP_WIDTH), const2),
                 pl.BlockSpec((1, ATTN_WIDTH), const2),
                 pl.BlockSpec((1, MLP_WIDTH), const2),
                 pl.BlockSpec(w_out_bf16.shape, const2, **resident),
                 pl.BlockSpec((1, D_MODEL), const2),
                 pl.BlockSpec(w_router_t.shape, const2)]
    args += [w_s_bf16, b_s_t, g_v, g_attn, g_mlp, w_out_bf16, norm2, w_router_t]
    out_specs = [pl.BlockSpec((block, D_MODEL), tok),
                 pl.BlockSpec((block, ROW_TILES, LANES), lambda b, t: tok(b, t) + (0,)),
                 pl.BlockSpec((N_EXPERTS, block), lambda b, t: (0, b * tiles + t))]
    out_shape = [jax.ShapeDtypeStruct((n, D_MODEL), F32),
                 jax.ShapeDtypeStruct((n, ROW_TILES, LANES), BF16),
                 jax.ShapeDtypeStruct((N_EXPERTS, n), F32)]
    if not latent:
        out_specs += [pl.BlockSpec((block, KV_WIDTH), tok)] * 2
        out_shape += [jax.ShapeDtypeStruct((n, KV_WIDTH), F32)] * 2
    return pl.pallas_call(
        functools.partial(_mix_body, latent=latent, seq_len=seq_len),
        grid=grid,
        in_specs=in_specs,
        out_specs=out_specs,
        out_shape=out_shape,
        compiler_params=pltpu.CompilerParams(vmem_limit_bytes=58 * 2**20),
        name="mix_lat" if latent else "mix_ctx",
    )(*args)


def _exclusive_counts(flags, upper, lower):
    n_e, nb, _ = flags.shape
    f = flags.astype(BF16)
    local = jnp.dot(f.reshape(n_e * nb, LANES), upper, preferred_element_type=F32).reshape(n_e, nb, LANES)
    totals = jnp.broadcast_to(jnp.sum(flags, axis=2, keepdims=True), flags.shape).astype(BF16)
    before = jnp.stack([jnp.dot(lower, totals[e], preferred_element_type=F32) for e in range(n_e)], axis=0)
    return local, before


def _route_body(aff_ref, pos_ref, idx_ref, sgate_ref, start_ref, local_ref, gate_ref, total_ref, *, cap):
    aff = aff_ref[...]
    n_e, nb, _ = aff.shape
    reduce_tokens = lambda x: jnp.sum(jnp.sum(x, axis=2, keepdims=True), axis=1, keepdims=True)

    def bisect(i, thr):
        cand = thr | jnp.left_shift(jnp.int32(1), 30 - i)
        n_ge = reduce_tokens((aff >= lax.bitcast_convert_type(cand, F32)).astype(F32))
        return jnp.where(n_ge >= cap, cand, thr)

    thr = lax.fori_loop(0, 31, bisect, jnp.zeros((n_e, 1, 1), I32))

    upper = (lax.broadcasted_iota(I32, (LANES, LANES), 0) < lax.broadcasted_iota(I32, (LANES, LANES), 1)).astype(BF16)
    lower = (lax.broadcasted_iota(I32, (nb, nb), 1) < lax.broadcasted_iota(I32, (nb, nb), 0)).astype(BF16)

    above = aff >= lax.bitcast_convert_type(thr + 1, F32)
    tied = (aff >= lax.bitcast_convert_type(thr, F32)) & jnp.logical_not(above)
    need = cap - reduce_tokens(above.astype(F32))
    tie_local, tie_before = _exclusive_counts(tied.astype(F32), upper, lower)
    sel = above | (tied & (tie_local + tie_before < need))
    sel_f = sel.astype(F32)

    local, before = _exclusive_counts(sel_f, upper, lower)
    pos_ref[...] = jnp.where(sel, (local + before).astype(I32), -1)
    gate_ref[...] = aff

    local_ref[...] = (local + sel_f).astype(BF16)
    start_ref[...] = before[:, :, 0:1]
    total_ref[...] = jnp.sum(sel_f, axis=2, keepdims=True)
    slot = lax.broadcasted_iota(I32, (1, cap), 1).astype(F32)
    block_id = lax.broadcasted_iota(I32, (nb, 1), 0).astype(F32)
    lane_id = lax.broadcasted_iota(I32, (LANES, 1), 0).astype(F32)
    tn = (((0,), (0,)), ((), ()))

    def lists(e, carry):
        start = start_ref[e]
        end = start + total_ref[e]
        blk = jnp.sum((end <= slot).astype(F32), axis=0, keepdims=True)
        onehot = block_id == blk
        onehot_b = onehot.astype(BF16)
        rank = slot - jnp.sum(jnp.where(onehot, start, 0.0), axis=0, keepdims=True)
        counts = lax.dot_general(local_ref[e], onehot_b, tn, preferred_element_type=F32)
        tok = jnp.sum((counts <= rank).astype(F32), axis=0, keepdims=True)
        idx_ref[pl.ds(e, 1), :] = (blk * LANES + tok).astype(I32)
        g1, rest = _split_bf16_f32(gate_ref[e])
        g2, rest = _split_bf16_f32(rest)
        block_gates = sum(lax.dot_general(g, onehot_b, tn, preferred_element_type=F32)
                          for g in (g1, g2, rest.astype(BF16)))
        sgate_ref[pl.ds(e, 1), :] = jnp.sum(jnp.where(lane_id == tok, block_gates, 0.0), axis=0, keepdims=True)
        return carry

    lax.fori_loop(0, n_e, lists, 0)


def _route(aff_blocks, cap):
    n_e, nb, _ = aff_blocks.shape
    return pl.pallas_call(
        functools.partial(_route_body, cap=cap),
        out_shape=[jax.ShapeDtypeStruct(aff_blocks.shape, I32), jax.ShapeDtypeStruct((n_e, cap), I32),
                   jax.ShapeDtypeStruct((n_e, cap), F32), jax.ShapeDtypeStruct((n_e, nb, 1), F32)],
        scratch_shapes=[pltpu.VMEM(aff_blocks.shape, BF16), pltpu.VMEM(aff_blocks.shape, F32),
                        pltpu.VMEM((n_e, nb, 1), F32)],
        compiler_params=pltpu.CompilerParams(vmem_limit_bytes=48 * 2**20),
        name="route",
    )(aff_blocks)


FF_TILE = 256
N_FF_TILES = EXPERT_FF // FF_TILE
OUT_TILE = 512
N_OUT_TILES = D_MODEL // OUT_TILE
EXPERT_STEPS = N_FF_TILES + N_OUT_TILES


def _expert_body(idxc_ref, idxl_ref, hc_ref, hl_ref, gc_ref, gl_ref, wg_ref, wu_ref, wd_ref,
                 oc_ref, ol_ref, gbuf, xb_ref, hid_ref, sem, *, cap_c, cap_l):
    e = pl.program_id(0)
    s = pl.program_id(1)
    cap = cap_c + cap_l
    cur = e % 2
    nxt = 1 - cur
    next_expert = (e + 1) % N_EXPERTS

    def issue(expert, part, parts):
        for idx_ref, src, n_rows, base in ((idxc_ref, hc_ref, cap_c, 0), (idxl_ref, hl_ref, cap_l, cap_c)):
            per = n_rows // parts
            for j in range(per):
                r = part * per + j
                pltpu.make_async_copy(src.at[idx_ref[expert * n_rows + r]], gbuf.at[base + r], sem).start()

    def arrived():
        pltpu.make_async_copy(hc_ref.at[pl.ds(0, cap)], gbuf, sem).wait()

    def unpack(slot, part, parts):
        per = cap // parts
        r0 = pl.multiple_of(part * per, per)
        xb_ref[slot, pl.ds(r0, per), :] = gbuf[pl.ds(r0, per)].reshape(per, D_MODEL)

    @pl.when((e == 0) & (s == 0))
    def _first():
        def gather(part, carry):
            issue(0, part, N_FF_TILES)
            return carry
        lax.fori_loop(0, N_FF_TILES, gather, 0)
        arrived()

        def relayout(part, carry):
            unpack(0, part, N_OUT_TILES)
            return carry
        lax.fori_loop(0, N_OUT_TILES, relayout, 0)

    @pl.when(s < N_FF_TILES)
    def _up():
        issue(next_expert, s, N_FF_TILES)
        x = xb_ref[cur]
        a = jnp.dot(x, wg_ref[0].astype(BF16), preferred_element_type=F32)
        b = jnp.dot(x, wu_ref[0].astype(BF16), preferred_element_type=F32)
        hid_ref[s] = (jax.nn.silu(a) * b).astype(BF16)

    @pl.when(s == N_FF_TILES)
    def _arrived():
        arrived()

    @pl.when(s >= N_FF_TILES)
    def _down():
        unpack(nxt, s - N_FF_TILES, N_OUT_TILES)
        wd = wd_ref[0].astype(BF16)
        acc = jnp.dot(hid_ref[0], wd[0:FF_TILE, :], preferred_element_type=F32)
        for j in range(1, N_FF_TILES):
            acc = acc + jnp.dot(hid_ref[j], wd[j * FF_TILE:(j + 1) * FF_TILE, :], preferred_element_type=F32)
        for g_ref, o_ref, rows in ((gc_ref, oc_ref, slice(0, cap_c)), (gl_ref, ol_ref, slice(cap_c, cap))):
            gates = g_ref[...]
            lane = lax.broadcasted_iota(I32, gates.shape, 1)
            gate = jnp.sum(jnp.where(lane == e, gates, 0.0), axis=1, keepdims=True)
            o_ref[...] = (acc[rows, :] * gate).astype(BF16)


def _experts(idx_c, idx_l, h2_c, h2_l, sgate_c, sgate_l, w_gate, w_up, w_down):
    cap_c = idx_c.shape[1]
    cap_l = idx_l.shape[1]
    cap = cap_c + cap_l
    idx_c = idx_c.reshape(-1)
    idx_l = idx_l.reshape(-1)
    const = lambda e, s, ic, il: (0, 0)
    up_idx = lambda e, s, ic, il: (e, 0, jnp.minimum(s, N_FF_TILES - 1))
    down_idx = lambda e, s, ic, il: (e, 0, jnp.maximum(s - N_FF_TILES, 0))
    out_idx = lambda e, s, ic, il: (e, jnp.maximum(s - N_FF_TILES, 0))
    out_c = jax.ShapeDtypeStruct((N_EXPERTS * cap_c, D_MODEL), BF16)
    out_l = jax.ShapeDtypeStruct((N_EXPERTS * cap_l, D_MODEL), BF16)
    return pl.pallas_call(
        functools.partial(_expert_body, cap_c=cap_c, cap_l=cap_l),
        grid_spec=pltpu.PrefetchScalarGridSpec(
            num_scalar_prefetch=2,
            grid=(N_EXPERTS, EXPERT_STEPS),
            in_specs=[pl.BlockSpec(memory_space=pl.ANY),
                      pl.BlockSpec(memory_space=pl.ANY),
                      pl.BlockSpec(sgate_c.shape, const),
                      pl.BlockSpec(sgate_l.shape, const),
                      pl.BlockSpec((1, D_MODEL, FF_TILE), up_idx),
                      pl.BlockSpec((1, D_MODEL, FF_TILE), up_idx),
                      pl.BlockSpec((1, EXPERT_FF, OUT_TILE), down_idx)],
            out_specs=[pl.BlockSpec((cap_c, OUT_TILE), out_idx), pl.BlockSpec((cap_l, OUT_TILE), out_idx)],
            scratch_shapes=[pltpu.VMEM((cap, ROW_TILES, LANES), BF16),
                            pltpu.VMEM((2, cap, D_MODEL), BF16),
                            pltpu.VMEM((N_FF_TILES, cap, FF_TILE), BF16),
                            pltpu.SemaphoreType.DMA(())]),
        out_shape=[out_c, out_l],
        compiler_params=pltpu.CompilerParams(vmem_limit_bytes=60 * 2**20),
        name="experts",
    )(idx_c, idx_l, h2_c, h2_l, sgate_c, sgate_l, w_gate, w_up, w_down)


SUBLANES = 16
COMBINE_CTX = (256, 64)
COMBINE_LAT = (128, 80)


def _combine_body(bs_ref, pos_ref, eo_ref, x1_ref, mod_ref, fn_ref, y_ref, rows, sem, *,
                  tb, cap, win, n_blocks, latent, tiles_per_seq):
    i = pl.program_id(0)
    slot = i % 2

    def window_base(e, blk):
        return (bs_ref[e, blk] // SUBLANES) * SUBLANES

    def window_start(e, blk, p):
        return jnp.minimum(window_base(e, blk) + p * win, cap - win)

    def fetch(blk, p, to_slot):
        for e in range(N_EXPERTS):
            start = pl.multiple_of(e * cap + window_start(e, blk, p), SUBLANES)
            pltpu.make_async_copy(eo_ref.at[pl.ds(start, win)], rows.at[to_slot, pl.ds(e * win, win)],
                                  sem.at[to_slot]).start()

    def wait(to_slot):
        pltpu.make_async_copy(eo_ref.at[pl.ds(0, N_EXPERTS * win)], rows.at[to_slot], sem.at[to_slot]).wait()

    @pl.when(i == 0)
    def _first():
        fetch(0, 0, 0)

    @pl.when(i + 1 < n_blocks)
    def _prefetch():
        fetch(i + 1, 0, 1 - slot)

    lane = lax.broadcasted_iota(I32, (1, N_EXPERTS), 1)
    col = lax.broadcasted_iota(I32, (1, (N_EXPERTS * win)), 1)
    col_row = (col % win).astype(F32)
    expand = (lax.broadcasted_iota(I32, (N_EXPERTS, (N_EXPERTS * win)), 0) == col // win).astype(BF16)
    widen = lambda v: jnp.dot(v, expand, preferred_element_type=F32)

    pos = pos_ref[...]

    def one_pass(p, acc):
        base = jnp.zeros((1, N_EXPERTS), I32)
        start = jnp.zeros((1, N_EXPERTS), I32)
        for e in range(N_EXPERTS):
            base = jnp.where(lane == e, window_base(e, i), base)
            start = jnp.where(lane == e, window_start(e, i, p), start)
        rel = pos - base
        mine = (pos >= 0) & (rel >= p * win) & (rel < (p + 1) * win)
        window_col = jnp.where(mine, pos - start, -1).astype(F32).astype(BF16)
        select = (widen(window_col) == col_row).astype(BF16)
        return acc + jnp.dot(select, rows[slot], preferred_element_type=F32)

    wait(slot)
    moe = one_pass(0, jnp.zeros((tb, D_MODEL), F32))

    n_pass = jnp.int32(1)
    for e in range(N_EXPERTS):
        span = bs_ref[e, i + 1] - window_base(e, i)
        n_pass = jnp.maximum(n_pass, (span + win - 1) // win)

    def extra(p, acc):
        fetch(i, p, slot)
        wait(slot)
        return one_pass(p, acc)

    moe = lax.fori_loop(1, n_pass, extra, moe)
    row = 1 + i // tiles_per_seq if latent else 0
    gate2 = mod_ref[pl.ds(row, 1), 5 * D_MODEL:6 * D_MODEL]
    y_ref[...] = _rms(x1_ref[...] + gate2 * moe) * fn_ref[...]


def _combine(block_start, pos_t, expert_out, x1, mod, final_norm, cap, latent, seq_len):
    n = x1.shape[0]
    tb, win = COMBINE_LAT if latent else COMBINE_CTX
    n_blocks = n // tb
    return pl.pallas_call(
        functools.partial(_combine_body, tb=tb, cap=cap, win=win, n_blocks=n_blocks, latent=latent,
                          tiles_per_seq=seq_len // tb),
        grid_spec=pltpu.PrefetchScalarGridSpec(
            num_scalar_prefetch=1,
            grid=(n_blocks,),
            in_specs=[pl.BlockSpec((tb, N_EXPERTS), lambda i, bs: (i, 0)),
                      pl.BlockSpec(memory_space=pl.ANY),
                      pl.BlockSpec((tb, D_MODEL), lambda i, bs: (i, 0)),
                      pl.BlockSpec(mod.shape, lambda i, bs: (0, 0)),
                      pl.BlockSpec((1, D_MODEL), lambda i, bs: (0, 0))],
            out_specs=pl.BlockSpec((tb, D_MODEL), lambda i, bs: (i, 0)),
            scratch_shapes=[pltpu.VMEM((2, N_EXPERTS * win, D_MODEL), BF16), pltpu.SemaphoreType.DMA((2,))]),
        out_shape=jax.ShapeDtypeStruct((n, D_MODEL), F32),
        compiler_params=pltpu.CompilerParams(vmem_limit_bytes=52 * 2**20),
        name="combine_lat" if latent else "combine_ctx",
    )(block_start, pos_t, expert_out, x1, mod, final_norm)


def _rope_tables(seq_len):
    pos = jnp.arange(seq_len)
    half = HEAD_DIM // 2
    freqs = ROPE_BASE ** (-jnp.arange(0, half, 2, dtype=F32) / half)
    ang_r = (pos // GRID_W).astype(F32)[:, None] * freqs[None, :]
    ang_c = (pos % GRID_W).astype(F32)[:, None] * freqs[None, :]
    cos = jnp.concatenate([jnp.cos(ang_r)] * 2 + [jnp.cos(ang_c)] * 2, axis=1)
    sin = jnp.concatenate([-jnp.sin(ang_r), jnp.sin(ang_r), -jnp.sin(ang_c), jnp.sin(ang_c)], axis=1)
    return cos, sin


def kernel(x_prompt, x_sample, cache_k, cache_v, c, c_ctx, w_ada, b_ada, norm1, w_in, attn_sink, w_s, b_s, g_v,
           g_attn, g_mlp, w_out, norm2, w_router, w_gate, w_up, w_down, final_norm):
    depth = w_ada.shape[0]
    assert depth == 1, "single-layer trunk"
    batch, seq, _ = x_prompt.shape
    dec_batch, dec_seq, _ = x_sample.shape
    n_ctx = batch * seq
    n_lat = dec_batch * dec_seq
    cap_c = CAPACITY_FACTOR * n_ctx // N_EXPERTS
    cap_l = CAPACITY_FACTOR * n_lat // N_EXPERTS
    l = 0

    cvec = jnp.concatenate([c_ctx[None, :], c, jnp.zeros((MOD_ROWS - 1 - dec_batch, D_MODEL), F32)], axis=0)
    mod = _adaln(cvec, w_ada[l], b_ada[l][None, :])

    w_in_b = w_in[l].astype(BF16)
    w_out_b = w_out[l].astype(BF16)
    w_s_b = w_s[l].astype(BF16)
    b_s_t = jnp.transpose(b_s[l])
    w_router_t = jnp.transpose(w_router[l])
    row = lambda a: a.reshape(1, -1)
    shared = (mod, row(norm1[l]), w_in_b, w_s_b, b_s_t, row(g_v[l]), row(g_attn[l]), row(g_mlp[l]), w_out_b,
              row(norm2[l]), w_router_t)

    xp = x_prompt.reshape(n_ctx, D_MODEL)
    xl = x_sample.reshape(n_lat, D_MODEL)

    x1_c, h2_c, afft_c, k, v = _mix(attn_sink[l], None, xp, *shared, batch, seq)
    state_k = k.reshape(batch, 1, seq, N_KV_HEADS, HEAD_DIM)
    state_v = v.reshape(batch, 1, seq, N_KV_HEADS, HEAD_DIM)

    q, k, v, u, gv = _inproj(xl, mod, row(norm1[l]), w_in_b, _rope_tables(dec_seq), dec_seq)
    projected = (q, k, v, cache_k[:, l].reshape(dec_batch, -1, KV_WIDTH),
                 cache_v[:, l].reshape(dec_batch, -1, KV_WIDTH), u, gv)
    x1_l, h2_l, afft_l = _mix(attn_sink[l], projected, xl, *shared, dec_batch, dec_seq)

    def route(aff_t, cap, combine_tile):
        n = aff_t.shape[1]
        pos, idx, slot_gate, start = _route(aff_t.reshape(N_EXPERTS, n // LANES, LANES), cap)
        start = start.reshape(N_EXPERTS, -1)[:, ::combine_tile // LANES].astype(I32)
        starts = jnp.concatenate([start, jnp.full((N_EXPERTS, 1), cap, I32)], axis=1)
        return jnp.transpose(pos.reshape(N_EXPERTS, n)), idx, jnp.transpose(slot_gate), starts

    pos_c, idx_c, sgate_c, starts_c = route(afft_c, cap_c, COMBINE_CTX[0])
    pos_l, idx_l, sgate_l, starts_l = route(afft_l, cap_l, COMBINE_LAT[0])
    eo_c, eo_l = _experts(idx_c, idx_l, h2_c, h2_l, sgate_c, sgate_l, w_gate[l], w_up[l], w_down[l])
    fn = row(final_norm)
    y_c = _combine(starts_c, pos_c, eo_c, x1_c, mod, fn, cap_c, False, seq)
    y_l = _combine(starts_l, pos_l, eo_l, x1_l, mod, fn, cap_l, True, dec_seq)

    return (y_c.reshape(batch, seq, D_MODEL), y_l.reshape(dec_batch, dec_seq, D_MODEL), state_k, state_v)
```

```python
import functools

import jax
import jax.numpy as jnp
from jax import lax
from jax.experimental import pallas as pl
from jax.experimental.pallas import tpu as pltpu

F32 = jnp.float32
BF16 = jnp.bfloat16
I32 = jnp.int32

D_MODEL = 2048
GRID_W = 64
HEAD_DIM = 128
N_Q_HEADS = 8
N_KV_HEADS = 2
Q_PER_KV = N_Q_HEADS // N_KV_HEADS
ATTN_WIDTH = N_Q_HEADS * HEAD_DIM
KV_WIDTH = N_KV_HEADS * HEAD_DIM
N_MLP_GROUPS = 8
MLP_WIDTH = 1024
Q_END = ATTN_WIDTH
K_END = Q_END + KV_WIDTH
V_END = K_END + KV_WIDTH
U_END = V_END + MLP_WIDTH
IN_WIDTH = U_END + MLP_WIDTH
CHUNK = 128
WINDOW = 128
N_EXPERTS = 16
EXPERT_FF = 2048
CAPACITY_FACTOR = 2
ROPE_BASE = 10000.0
EPS = 1e-6
NEG_INF = -1e30
ATTN_SCALE = HEAD_DIM ** -0.5

LANES = 128
MIX_TILE = 256
MIX_SUB = 2
LAT_SPAN = MIX_TILE + 2 * WINDOW
ROW_TILES = D_MODEL // LANES
MOD_ROWS = 8


def _rms(x):
    return x * lax.rsqrt(jnp.mean(x * x, axis=-1, keepdims=True) + EPS)


def _split_bf16_f32(x):
    hi = x.astype(BF16)
    return hi, x - hi.astype(F32)


def _adaln_body(c_ref, w_ref, b_ref, o_ref):
    s = jax.nn.silu(c_ref[...])
    o_ref[...] = jnp.dot(s.astype(BF16), w_ref[...].astype(BF16),
                         preferred_element_type=F32) + b_ref[...]


def _adaln(cvec, w_ada, b_ada):
    tn = 1024
    width = w_ada.shape[1]
    return pl.pallas_call(
        _adaln_body,
        grid=(width // tn,),
        in_specs=[pl.BlockSpec((MOD_ROWS, D_MODEL), lambda j: (0, 0)),
                  pl.BlockSpec((D_MODEL, tn), lambda j: (0, j)),
                  pl.BlockSpec((1, tn), lambda j: (0, j))],
        out_specs=pl.BlockSpec((MOD_ROWS, tn), lambda j: (0, j)),
        out_shape=jax.ShapeDtypeStruct((MOD_ROWS, width), F32),
        compiler_params=pltpu.CompilerParams(vmem_limit_bytes=40 * 2**20),
        name="adaln",
    )(cvec, w_ada, b_ada)


def _rope(x, cos, sin, n_heads):
    lane = lax.broadcasted_iota(I32, (x.shape[0], HEAD_DIM), 1)
    first = (lane & 32) == 0
    outs = []
    for h in range(n_heads):
        xh = x[:, h * HEAD_DIM:(h + 1) * HEAD_DIM]
        partner = jnp.where(first, pltpu.roll(xh, 96, 1), pltpu.roll(xh, 32, 1))
        outs.append(xh * cos + partner * sin)
    return jnp.concatenate(outs, axis=1)


def _inproj_body(*refs, latent, tiles_per_seq):
    if latent:
        x_ref, mod_ref, g_ref, w_ref, cos_ref, sin_ref, q_ref, k_ref, v_ref, u_ref, gv_ref = refs
    else:
        x_ref, mod_ref, g_ref, w_ref, q_ref, k_ref, v_ref, u_ref, gv_ref = refs
    i = pl.program_id(0)
    row = 1 + i // tiles_per_seq if latent else 0
    shift = mod_ref[pl.ds(row, 1), 0:D_MODEL]
    scale = mod_ref[pl.ds(row, 1), D_MODEL:2 * D_MODEL]
    h = _rms(x_ref[...]) * g_ref[...] * (1 + scale) + shift
    z = jnp.dot(h.astype(BF16), w_ref[...], preferred_element_type=F32)
    q = z[:, 0:Q_END]
    k = z[:, Q_END:K_END]
    if latent:
        cos = cos_ref[...]
        sin = sin_ref[...]
        q = _rope(q, cos, sin, N_Q_HEADS)
        k = _rope(k, cos, sin, N_KV_HEADS)
    q_ref[...] = q.astype(BF16)
    k_ref[...] = k
    v_ref[...] = z[:, K_END:V_END]
    u_ref[...] = z[:, V_END:U_END]
    gv_ref[...] = z[:, U_END:IN_WIDTH]


def _inproj(x2d, mod, norm1, w_in_bf16, rope_tables, seq_len):
    n = x2d.shape[0]
    tm = 256
    latent = rope_tables is not None
    tiles_per_seq = seq_len // tm
    in_specs = [pl.BlockSpec((tm, D_MODEL), lambda i: (i, 0)),
                pl.BlockSpec(mod.shape, lambda i: (0, 0)),
                pl.BlockSpec((1, D_MODEL), lambda i: (0, 0)),
                pl.BlockSpec((D_MODEL, IN_WIDTH), lambda i: (0, 0))]
    args = [x2d, mod, norm1, w_in_bf16]
    if latent:
        in_specs += [pl.BlockSpec((tm, HEAD_DIM), lambda i: (i % tiles_per_seq, 0))] * 2
        args += list(rope_tables)
    widths = (ATTN_WIDTH, KV_WIDTH, KV_WIDTH, MLP_WIDTH, MLP_WIDTH)
    dtypes = (BF16, F32, F32, F32, F32)
    return pl.pallas_call(
        functools.partial(_inproj_body, latent=latent, tiles_per_seq=tiles_per_seq),
        grid=(n // tm,),
        in_specs=in_specs,
        out_specs=[pl.BlockSpec((tm, w), lambda i: (i, 0)) for w in widths],
        out_shape=[jax.ShapeDtypeStruct((n, w), dt) for w, dt in zip(widths, dtypes)],
        compiler_params=pltpu.CompilerParams(vmem_limit_bytes=56 * 2**20),
        name="inproj_lat" if latent else "inproj_ctx",
    )(*args)


def _mix_body(*refs, latent, seq_len):
    if latent:
        sink_ref, q_ref, k_ref, v_ref, ck_ref, cv_ref, u_ref, gv_ref, x_ref, mod_ref, *rest = refs
        proj_refs = (q_ref, k_ref, v_ref, ck_ref, cv_ref, u_ref, gv_ref)
    else:
        sink_ref, x_ref, mod_ref, norm1_ref, win_ref, *rest = refs
        proj_refs = (norm1_ref, win_ref) + tuple(rest[-2:])
        rest = rest[:-2]
    b = pl.program_id(0)
    t = pl.program_id(1)
    row = 1 + b if latent else 0
    for sub in range(MIX_SUB):
        _mix_tile(sub, t * MIX_SUB + sub, row, latent, seq_len, sink_ref, proj_refs, x_ref, mod_ref, *rest)


def _mix_tile(sub, tile, row, latent, seq_len, sink_ref, proj_refs, x_ref,
              mod_ref, ws_ref, bst_ref, gvn_ref, gattn_ref, gmlp_ref, wout_ref, norm2_ref, wrt_ref,
              x1_ref, h2_ref, afft_ref):
    tq = MIX_TILE
    rows = slice(sub * tq, (sub + 1) * tq)

    if latent:
        q_ref, k_ref, v_ref, ck_ref, cv_ref, u_ref, gv_ref = proj_refs
        q = q_ref[rows, :]
        u = u_ref[rows, :]
        gv = gv_ref[rows, :]
        ks = pl.multiple_of(jnp.clip(tile * tq - WINDOW, 0, seq_len - LAT_SPAN), WINDOW)
        kcat = jnp.concatenate([ck_ref[0], k_ref[pl.ds(ks, LAT_SPAN), :]], axis=0).astype(BF16)
        vcat = jnp.concatenate([cv_ref[0], v_ref[pl.ds(ks, LAT_SPAN), :]], axis=0).astype(BF16)
        n_ctx = ck_ref.shape[1]
        n_keys = n_ctx + LAT_SPAN
        kidx = lax.broadcasted_iota(I32, (tq, n_keys), 1)
        qpos = tile * tq + lax.broadcasted_iota(I32, (tq, n_keys), 0)
        kpos = ks + kidx - n_ctx
        mask = (kidx < n_ctx) | (jnp.abs(qpos - kpos) <= WINDOW)
    else:
        norm1_ref, win_ref, kout_ref, vout_ref = proj_refs
        shift1 = mod_ref[pl.ds(row, 1), 0:D_MODEL]
        scale1 = mod_ref[pl.ds(row, 1), D_MODEL:2 * D_MODEL]
        h1 = _rms(x_ref[rows, :]) * norm1_ref[...] * (1 + scale1) + shift1
        z = jnp.dot(h1.astype(BF16), win_ref[...], preferred_element_type=F32)
        q = z[:, 0:Q_END].astype(BF16)
        k = z[:, Q_END:K_END]
        v = z[:, K_END:V_END]
        u = z[:, V_END:U_END]
        gv = z[:, U_END:IN_WIDTH]
        kout_ref[rows, :] = k
        vout_ref[rows, :] = v
        kcat = k.astype(BF16)
        vcat = v.astype(BF16)
        mask = None

    heads = []
    for h in range(N_Q_HEADS):
        kv = h // Q_PER_KV
        qh = q[:, h * HEAD_DIM:(h + 1) * HEAD_DIM]
        kh = kcat[:, kv * HEAD_DIM:(kv + 1) * HEAD_DIM]
        vh = vcat[:, kv * HEAD_DIM:(kv + 1) * HEAD_DIM]
        s = lax.dot_general(qh, kh, (((1,), (1,)), ((), ())), preferred_element_type=F32) * ATTN_SCALE
        if mask is not None:
            s = jnp.where(mask, s, NEG_INF)
        sink = sink_ref[h]
        m = jnp.maximum(jnp.max(s, axis=1, keepdims=True), sink)
        p = jnp.exp(s - m)
        den = jnp.sum(p, axis=1, keepdims=True) + jnp.exp(sink - m)
        heads.append(jnp.dot(p.astype(BF16), vh, preferred_element_type=F32) / den)
    attn = jnp.concatenate(heads, axis=1)

    sg = (_rms(jax.nn.gelu(gv)) * gvn_ref[...]).astype(BF16)
    bst = bst_ref[...]
    chunks = []
    for c in range(tq // CHUNK):
        groups = []
        for g in range(N_MLP_GROUPS):
            blk = sg[c * CHUNK:(c + 1) * CHUNK, g * LANES:(g + 1) * LANES]
            groups.append(jnp.dot(ws_ref[g], blk, preferred_element_type=F32) + bst[:, g:g + 1])
        chunks.append(jnp.concatenate(groups, axis=1))
    mlp = jax.nn.gelu(u) * jnp.concatenate(chunks, axis=0)

    mixed = jnp.concatenate([_rms(attn) * gattn_ref[...], _rms(mlp) * gmlp_ref[...]], axis=1)
    out = jnp.dot(mixed.astype(BF16), wout_ref[...], preferred_element_type=F32)
    gate1 = mod_ref[pl.ds(row, 1), 2 * D_MODEL:3 * D_MODEL]
    x1 = x_ref[rows, :] + gate1 * out
    x1_ref[rows, :] = x1

    shift2 = mod_ref[pl.ds(row, 1), 3 * D_MODEL:4 * D_MODEL]
    scale2 = mod_ref[pl.ds(row, 1), 4 * D_MODEL:5 * D_MODEL]
    h2 = _rms(x1) * norm2_ref[...] * (1 + scale2) + shift2

    hh = h2.astype(BF16)
    hl = (h2 - hh.astype(F32)).astype(BF16)
    wrt = wrt_ref[...]
    wth = wrt.astype(BF16)
    wtl = (wrt - wth.astype(F32)).astype(BF16)
    nt = (((1,), (1,)), ((), ()))
    both = lax.dot_general(jnp.concatenate([wth, wtl], axis=0), hh, nt, preferred_element_type=F32)
    logits_t = (both[0:N_EXPERTS] + both[N_EXPERTS:2 * N_EXPERTS]
                + lax.dot_general(wth, hl, nt, preferred_element_type=F32))
    pt = jnp.exp(logits_t - jnp.max(logits_t, axis=0, keepdims=True))
    afft_ref[:, rows] = pt / jnp.sum(pt, axis=0, keepdims=True)

    h2_ref[rows] = hh.reshape(tq, ROW_TILES, LANES)


def _mix(sink, projected, x2d, mod, norm1, w_in_bf16, w_s_bf16, b_s_t, g_v, g_attn, g_mlp, w_out_bf16, norm2,
         w_router_t, batch, seq_len):
    n = x2d.shape[0]
    latent = projected is not None
    block = MIX_SUB * MIX_TILE
    const2 = lambda b, t: (0, 0)
    resident = dict(pipeline_mode=pl.Buffered(1))
    if latent:
        grid = (batch, seq_len // block)
        tiles = grid[1]
        tok = lambda b, t: (b * tiles + t, 0)
        q, k, v, ck, cv, u, gv = projected
        seq = lambda b, t: (b, 0)
        in_specs = [pl.BlockSpec(memory_space=pltpu.SMEM),
                    pl.BlockSpec((block, ATTN_WIDTH), tok),
                    pl.BlockSpec((seq_len, KV_WIDTH), seq),
                    pl.BlockSpec((seq_len, KV_WIDTH), seq),
                    pl.BlockSpec((1,) + ck.shape[1:], lambda b, t: (b, 0, 0)),
                    pl.BlockSpec((1,) + cv.shape[1:], lambda b, t: (b, 0, 0)),
                    pl.BlockSpec((block, MLP_WIDTH), tok),
                    pl.BlockSpec((block, MLP_WIDTH), tok),
                    pl.BlockSpec((block, D_MODEL), tok),
                    pl.BlockSpec(mod.shape, const2)]
        args = [sink, q, k, v, ck, cv, u, gv, x2d, mod]
    else:
        assert seq_len == MIX_TILE
        grid = (batch // MIX_SUB, 1)
        tiles = 1
        tok = lambda b, t: (b, 0)
        in_specs = [pl.BlockSpec(memory_space=pltpu.SMEM),
                    pl.BlockSpec((block, D_MODEL), tok),
                    pl.BlockSpec(mod.shape, const2),
                    pl.BlockSpec((1, D_MODEL), const2),
                    pl.BlockSpec(w_in_bf16.shape, const2, **resident)]
        args = [sink, x2d, mod, norm1, w_in_bf16]
    in_specs += [pl.BlockSpec(w_s_bf16.shape, lambda b, t: (0, 0, 0)),
                 pl.BlockSpec(b_s_t.shape, const2),
                 pl.BlockSpec((1, MLP_WIDTH), const2),
                 pl.BlockSpec((1, ATTN_WIDTH), const2),
                 pl.BlockSpec((1, MLP_WIDTH), const2),
                 pl.BlockSpec(w_out_bf16.shape, const2, **resident),
                 pl.BlockSpec((1, D_MODEL), const2),
                 pl.BlockSpec(w_router_t.shape, const2)]
    args += [w_s_bf16, b_s_t, g_v, g_attn, g_mlp, w_out_bf16, norm2, w_router_t]
    out_specs = [pl.BlockSpec((block, D_MODEL), tok),
                 pl.BlockSpec((block, ROW_TILES, LANES), lambda b, t: tok(b, t) + (0,)),
                 pl.BlockSpec((N_EXPERTS, block), lambda b, t: (0, b * tiles + t))]
    out_shape = [jax.ShapeDtypeStruct((n, D_MODEL), F32),
                 jax.ShapeDtypeStruct((n, ROW_TILES, LANES), BF16),
                 jax.ShapeDtypeStruct((N_EXPERTS, n), F32)]
    if not latent:
        out_specs += [pl.BlockSpec((block, KV_WIDTH), tok)] * 2
        out_shape += [jax.ShapeDtypeStruct((n, KV_WIDTH), F32)] * 2
    return pl.pallas_call(
        functools.partial(_mix_body, latent=latent, seq_len=seq_len),
        grid=grid,
        in_specs=in_specs,
        out_specs=out_specs,
        out_shape=out_shape,
        compiler_params=pltpu.CompilerParams(vmem_limit_bytes=58 * 2**20),
        name="mix_lat" if latent else "mix_ctx",
    )(*args)


def _exclusive_counts(flags, upper, lower):
    n_e, nb, _ = flags.shape
    f = flags.astype(BF16)
    local = jnp.dot(f.reshape(n_e * nb, LANES), upper, preferred_element_type=F32).reshape(n_e, nb, LANES)
    totals = jnp.broadcast_to(jnp.sum(flags, axis=2, keepdims=True), flags.shape).astype(BF16)
    before = jnp.stack([jnp.dot(lower, totals[e], preferred_element_type=F32) for e in range(n_e)], axis=0)
    return local, before


def _route_body(aff_ref, pos_ref, idx_ref, sgate_ref, start_ref, local_ref, gate_ref, total_ref, *, cap):
    aff = aff_ref[...]
    n_e, nb, _ = aff.shape
    reduce_tokens = lambda x: jnp.sum(jnp.sum(x, axis=2, keepdims=True), axis=1, keepdims=True)

    def bisect(i, thr):
        cand = thr | jnp.left_shift(jnp.int32(1), 30 - i)
        n_ge = reduce_tokens((aff >= lax.bitcast_convert_type(cand, F32)).astype(F32))
        return jnp.where(n_ge >= cap, cand, thr)

    thr = lax.fori_loop(0, 31, bisect, jnp.zeros((n_e, 1, 1), I32))

    upper = (lax.broadcasted_iota(I32, (LANES, LANES), 0) < lax.broadcasted_iota(I32, (LANES, LANES), 1)).astype(BF16)
    lower = (lax.broadcasted_iota(I32, (nb, nb), 1) < lax.broadcasted_iota(I32, (nb, nb), 0)).astype(BF16)

    above = aff >= lax.bitcast_convert_type(thr + 1, F32)
    tied = (aff >= lax.bitcast_convert_type(thr, F32)) & jnp.logical_not(above)
    need = cap - reduce_tokens(above.astype(F32))
    tie_local, tie_before = _exclusive_counts(tied.astype(F32), upper, lower)
    sel = above | (tied & (tie_local + tie_before < need))
    sel_f = sel.astype(F32)

    local, before = _exclusive_counts(sel_f, upper, lower)
    pos_ref[...] = jnp.where(sel, (local + before).astype(I32), -1)
    gate_ref[...] = aff

    local_ref[...] = (local + sel_f).astype(BF16)
    start_ref[...] = before[:, :, 0:1]
    total_ref[...] = jnp.sum(sel_f, axis=2, keepdims=True)
    slot = lax.broadcasted_iota(I32, (1, cap), 1).astype(F32)
    block_id = lax.broadcasted_iota(I32, (nb, 1), 0).astype(F32)
    lane_id = lax.broadcasted_iota(I32, (LANES, 1), 0).astype(F32)
    tn = (((0,), (0,)), ((), ()))

    def lists(e, carry):
        start = start_ref[e]
        end = start + total_ref[e]
        blk = jnp.sum((end <= slot).astype(F32), axis=0, keepdims=True)
        onehot = block_id == blk
        onehot_b = onehot.astype(BF16)
        rank = slot - jnp.sum(jnp.where(onehot, start, 0.0), axis=0, keepdims=True)
        counts = lax.dot_general(local_ref[e], onehot_b, tn, preferred_element_type=F32)
        tok = jnp.sum((counts <= rank).astype(F32), axis=0, keepdims=True)
        idx_ref[pl.ds(e, 1), :] = (blk * LANES + tok).astype(I32)
        g1, rest = _split_bf16_f32(gate_ref[e])
        g2, rest = _split_bf16_f32(rest)
        block_gates = sum(lax.dot_general(g, onehot_b, tn, preferred_element_type=F32)
                          for g in (g1, g2, rest.astype(BF16)))
        sgate_ref[pl.ds(e, 1), :] = jnp.sum(jnp.where(lane_id == tok, block_gates, 0.0), axis=0, keepdims=True)
        return carry

    lax.fori_loop(0, n_e, lists, 0)


def _route(aff_blocks, cap):
    n_e, nb, _ = aff_blocks.shape
    return pl.pallas_call(
        functools.partial(_route_body, cap=cap),
        out_shape=[jax.ShapeDtypeStruct(aff_blocks.shape, I32), jax.ShapeDtypeStruct((n_e, cap), I32),
                   jax.ShapeDtypeStruct((n_e, cap), F32), jax.ShapeDtypeStruct((n_e, nb, 1), F32)],
        scratch_shapes=[pltpu.VMEM(aff_blocks.shape, BF16), pltpu.VMEM(aff_blocks.shape, F32),
                        pltpu.VMEM((n_e, nb, 1), F32)],
        compiler_params=pltpu.CompilerParams(vmem_limit_bytes=48 * 2**20),
        name="route",
    )(aff_blocks)


FF_TILE = 512
N_FF_TILES = EXPERT_FF // FF_TILE
OUT_TILE = 512
N_OUT_TILES = D_MODEL // OUT_TILE
EXPERT_STEPS = N_FF_TILES + N_OUT_TILES


def _expert_body(idxc_ref, idxl_ref, hc_ref, hl_ref, gc_ref, gl_ref, wg_ref, wu_ref, wd_ref,
                 oc_ref, ol_ref, gbuf, xb_ref, hid_ref, sem, *, cap_c, cap_l):
    e = pl.program_id(0)
    s = pl.program_id(1)
    cap = cap_c + cap_l
    next_expert = (e + 1) % N_EXPERTS

    def issue(expert, part, parts):
        for idx_ref, src, n_rows, base in ((idxc_ref, hc_ref, cap_c, 0), (idxl_ref, hl_ref, cap_l, cap_c)):
            per = n_rows // parts
            for j in range(per):
                r = part * per + j
                pltpu.make_async_copy(src.at[idx_ref[expert * n_rows + r]], gbuf.at[base + r], sem).start()

    def arrived():
        pltpu.make_async_copy(hc_ref.at[pl.ds(0, cap)], gbuf, sem).wait()

    def unpack(part, parts):
        per = cap // parts
        r0 = pl.multiple_of(part * per, per)
        xb_ref[pl.ds(r0, per), :] = gbuf[pl.ds(r0, per)].reshape(per, D_MODEL)

    @pl.when((e == 0) & (s == 0))
    def _first():
        def gather(part, carry):
            issue(0, part, N_FF_TILES)
            return carry
        lax.fori_loop(0, N_FF_TILES, gather, 0)
        arrived()

        def relayout(part, carry):
            unpack(part, N_OUT_TILES)
            return carry
        lax.fori_loop(0, N_OUT_TILES, relayout, 0)

    @pl.when(s < N_FF_TILES)
    def _up():
        issue(next_expert, s, N_FF_TILES)
        x = xb_ref[...]
        a = jnp.dot(x, wg_ref[0].astype(BF16), preferred_element_type=F32)
        b = jnp.dot(x, wu_ref[0].astype(BF16), preferred_element_type=F32)
        hid_ref[s] = (jax.nn.silu(a) * b).astype(BF16)

    @pl.when(s == N_FF_TILES)
    def _arrived():
        arrived()

    @pl.when(s >= N_FF_TILES)
    def _down():
        unpack(s - N_FF_TILES, N_OUT_TILES)
        wd = wd_ref[0].astype(BF16)
        acc = jnp.dot(hid_ref[0], wd[0:FF_TILE, :], preferred_element_type=F32)
        for j in range(1, N_FF_TILES):
            acc = acc + jnp.dot(hid_ref[j], wd[j * FF_TILE:(j + 1) * FF_TILE, :], preferred_element_type=F32)
        for g_ref, o_ref, rows in ((gc_ref, oc_ref, slice(0, cap_c)), (gl_ref, ol_ref, slice(cap_c, cap))):
            gates = g_ref[...]
            lane = lax.broadcasted_iota(I32, gates.shape, 1)
            gate = jnp.sum(jnp.where(lane == e, gates, 0.0), axis=1, keepdims=True)
            o_ref[...] = (acc[rows, :] * gate).astype(BF16)


def _experts(idx_c, idx_l, h2_c, h2_l, sgate_c, sgate_l, w_gate, w_up, w_down):
    cap_c = idx_c.shape[1]
    cap_l = idx_l.shape[1]
    cap = cap_c + cap_l
    idx_c = idx_c.reshape(-1)
    idx_l = idx_l.reshape(-1)
    const = lambda e, s, ic, il: (0, 0)
    up_idx = lambda e, s, ic, il: (e, 0, jnp.minimum(s, N_FF_TILES - 1))
    down_idx = lambda e, s, ic, il: (e, 0, jnp.maximum(s - N_FF_TILES, 0))
    out_idx = lambda e, s, ic, il: (e, jnp.maximum(s - N_FF_TILES, 0))
    out_c = jax.ShapeDtypeStruct((N_EXPERTS * cap_c, D_MODEL), BF16)
    out_l = jax.ShapeDtypeStruct((N_EXPERTS * cap_l, D_MODEL), BF16)
    return pl.pallas_call(
        functools.partial(_expert_body, cap_c=cap_c, cap_l=cap_l),
        grid_spec=pltpu.PrefetchScalarGridSpec(
            num_scalar_prefetch=2,
            grid=(N_EXPERTS, EXPERT_STEPS),
            in_specs=[pl.BlockSpec(memory_space=pl.ANY),
                      pl.BlockSpec(memory_space=pl.ANY),
                      pl.BlockSpec(sgate_c.shape, const),
                      pl.BlockSpec(sgate_l.shape, const),
                      pl.BlockSpec((1, D_MODEL, FF_TILE), up_idx),
                      pl.BlockSpec((1, D_MODEL, FF_TILE), up_idx),
                      pl.BlockSpec((1, EXPERT_FF, OUT_TILE), down_idx)],
            out_specs=[pl.BlockSpec((cap_c, OUT_TILE), out_idx), pl.BlockSpec((cap_l, OUT_TILE), out_idx)],
            scratch_shapes=[pltpu.VMEM((cap, ROW_TILES, LANES), BF16),
                            pltpu.VMEM((cap, D_MODEL), BF16),
                            pltpu.VMEM((N_FF_TILES, cap, FF_TILE), BF16),
                            pltpu.SemaphoreType.DMA(())]),
        out_shape=[out_c, out_l],
        compiler_params=pltpu.CompilerParams(vmem_limit_bytes=60 * 2**20),
        name="experts",
    )(idx_c, idx_l, h2_c, h2_l, sgate_c, sgate_l, w_gate, w_up, w_down)


SUBLANES = 16
COMBINE_CTX = (256, 64)
COMBINE_LAT = (128, 80)


def _combine_body(bs_ref, pos_ref, eo_ref, x1_ref, mod_ref, fn_ref, y_ref, rows, sem, *,
                  tb, cap, win, n_blocks, latent, tiles_per_seq):
    i = pl.program_id(0)
    slot = i % 2

    def window_base(e, blk):
        return (bs_ref[e, blk] // SUBLANES) * SUBLANES

    def window_start(e, blk, p):
        return jnp.minimum(window_base(e, blk) + p * win, cap - win)

    def fetch(blk, p, to_slot):
        for e in range(N_EXPERTS):
            start = pl.multiple_of(e * cap + window_start(e, blk, p), SUBLANES)
            pltpu.make_async_copy(eo_ref.at[pl.ds(start, win)], rows.at[to_slot, pl.ds(e * win, win)],
                                  sem.at[to_slot]).start()

    def wait(to_slot):
        pltpu.make_async_copy(eo_ref.at[pl.ds(0, N_EXPERTS * win)], rows.at[to_slot], sem.at[to_slot]).wait()

    @pl.when(i == 0)
    def _first():
        fetch(0, 0, 0)

    @pl.when(i + 1 < n_blocks)
    def _prefetch():
        fetch(i + 1, 0, 1 - slot)

    lane = lax.broadcasted_iota(I32, (1, N_EXPERTS), 1)
    col = lax.broadcasted_iota(I32, (1, (N_EXPERTS * win)), 1)
    col_row = (col % win).astype(F32)
    expand = (lax.broadcasted_iota(I32, (N_EXPERTS, (N_EXPERTS * win)), 0) == col // win).astype(BF16)
    widen = lambda v: jnp.dot(v, expand, preferred_element_type=F32)

    pos = pos_ref[...]

    def one_pass(p, acc):
        base = jnp.zeros((1, N_EXPERTS), I32)
        start = jnp.zeros((1, N_EXPERTS), I32)
        for e in range(N_EXPERTS):
            base = jnp.where(lane == e, window_base(e, i), base)
            start = jnp.where(lane == e, window_start(e, i, p), start)
        rel = pos - base
        mine = (pos >= 0) & (rel >= p * win) & (rel < (p + 1) * win)
        window_col = jnp.where(mine, pos - start, -1).astype(F32).astype(BF16)
        select = (widen(window_col) == col_row).astype(BF16)
        return acc + jnp.dot(select, rows[slot], preferred_element_type=F32)

    wait(slot)
    moe = one_pass(0, jnp.zeros((tb, D_MODEL), F32))

    n_pass = jnp.int32(1)
    for e in range(N_EXPERTS):
        span = bs_ref[e, i + 1] - window_base(e, i)
        n_pass = jnp.maximum(n_pass, (span + win - 1) // win)

    def extra(p, acc):
        fetch(i, p, slot)
        wait(slot)
        return one_pass(p, acc)

    moe = lax.fori_loop(1, n_pass, extra, moe)
    row = 1 + i // tiles_per_seq if latent else 0
    gate2 = mod_ref[pl.ds(row, 1), 5 * D_MODEL:6 * D_MODEL]
    y_ref[...] = _rms(x1_ref[...] + gate2 * moe) * fn_ref[...]


def _combine(block_start, pos_t, expert_out, x1, mod, final_norm, cap, latent, seq_len):
    n = x1.shape[0]
    tb, win = COMBINE_LAT if latent else COMBINE_CTX
    n_blocks = n // tb
    return pl.pallas_call(
        functools.partial(_combine_body, tb=tb, cap=cap, win=win, n_blocks=n_blocks, latent=latent,
                          tiles_per_seq=seq_len // tb),
        grid_spec=pltpu.PrefetchScalarGridSpec(
            num_scalar_prefetch=1,
            grid=(n_blocks,),
            in_specs=[pl.BlockSpec((tb, N_EXPERTS), lambda i, bs: (i, 0)),
                      pl.BlockSpec(memory_space=pl.ANY),
                      pl.BlockSpec((tb, D_MODEL), lambda i, bs: (i, 0)),
                      pl.BlockSpec(mod.shape, lambda i, bs: (0, 0)),
                      pl.BlockSpec((1, D_MODEL), lambda i, bs: (0, 0))],
            out_specs=pl.BlockSpec((tb, D_MODEL), lambda i, bs: (i, 0)),
            scratch_shapes=[pltpu.VMEM((2, N_EXPERTS * win, D_MODEL), BF16), pltpu.SemaphoreType.DMA((2,))]),
        out_shape=jax.ShapeDtypeStruct((n, D_MODEL), F32),
        compiler_params=pltpu.CompilerParams(vmem_limit_bytes=52 * 2**20),
        name="combine_lat" if latent else "combine_ctx",
    )(block_start, pos_t, expert_out, x1, mod, final_norm)


def _rope_tables(seq_len):
    pos = jnp.arange(seq_len)
    half = HEAD_DIM // 2
    freqs = ROPE_BASE ** (-jnp.arange(0, half, 2, dtype=F32) / half)
    ang_r = (pos // GRID_W).astype(F32)[:, None] * freqs[None, :]
    ang_c = (pos % GRID_W).astype(F32)[:, None] * freqs[None, :]
    cos = jnp.concatenate([jnp.cos(ang_r)] * 2 + [jnp.cos(ang_c)] * 2, axis=1)
    sin = jnp.concatenate([-jnp.sin(ang_r), jnp.sin(ang_r), -jnp.sin(ang_c), jnp.sin(ang_c)], axis=1)
    return cos, sin


def kernel(x_prompt, x_sample, cache_k, cache_v, c, c_ctx, w_ada, b_ada, norm1, w_in, attn_sink, w_s, b_s, g_v,
           g_attn, g_mlp, w_out, norm2, w_router, w_gate, w_up, w_down, final_norm):
    depth = w_ada.shape[0]
    assert depth == 1, "single-layer trunk"
    batch, seq, _ = x_prompt.shape
    dec_batch, dec_seq, _ = x_sample.shape
    n_ctx = batch * seq
    n_lat = dec_batch * dec_seq
    cap_c = CAPACITY_FACTOR * n_ctx // N_EXPERTS
    cap_l = CAPACITY_FACTOR * n_lat // N_EXPERTS
    l = 0

    cvec = jnp.concatenate([c_ctx[None, :], c, jnp.zeros((MOD_ROWS - 1 - dec_batch, D_MODEL), F32)], axis=0)
    mod = _adaln(cvec, w_ada[l], b_ada[l][None, :])

    w_in_b = w_in[l].astype(BF16)
    w_out_b = w_out[l].astype(BF16)
    w_s_b = w_s[l].astype(BF16)
    b_s_t = jnp.transpose(b_s[l])
    w_router_t = jnp.transpose(w_router[l])
    row = lambda a: a.reshape(1, -1)
    shared = (mod, row(norm1[l]), w_in_b, w_s_b, b_s_t, row(g_v[l]), row(g_attn[l]), row(g_mlp[l]), w_out_b,
              row(norm2[l]), w_router_t)

    xp = x_prompt.reshape(n_ctx, D_MODEL)
    xl = x_sample.reshape(n_lat, D_MODEL)

    x1_c, h2_c, afft_c, k, v = _mix(attn_sink[l], None, xp, *shared, batch, seq)
    state_k = k.reshape(batch, 1, seq, N_KV_HEADS, HEAD_DIM)
    state_v = v.reshape(batch, 1, seq, N_KV_HEADS, HEAD_DIM)

    q, k, v, u, gv = _inproj(xl, mod, row(norm1[l]), w_in_b, _rope_tables(dec_seq), dec_seq)
    projected = (q, k, v, cache_k[:, l].reshape(dec_batch, -1, KV_WIDTH),
                 cache_v[:, l].reshape(dec_batch, -1, KV_WIDTH), u, gv)
    x1_l, h2_l, afft_l = _mix(attn_sink[l], projected, xl, *shared, dec_batch, dec_seq)

    def route(aff_t, cap, combine_tile):
        n = aff_t.shape[1]
        pos, idx, slot_gate, start = _route(aff_t.reshape(N_EXPERTS, n // LANES, LANES), cap)
        start = start.reshape(N_EXPERTS, -1)[:, ::combine_tile // LANES].astype(I32)
        starts = jnp.concatenate([start, jnp.full((N_EXPERTS, 1), cap, I32)], axis=1)
        return jnp.transpose(pos.reshape(N_EXPERTS, n)), idx, jnp.transpose(slot_gate), starts

    pos_c, idx_c, sgate_c, starts_c = route(afft_c, cap_c, COMBINE_CTX[0])
    pos_l, idx_l, sgate_l, starts_l = route(afft_l, cap_l, COMBINE_LAT[0])
    eo_c, eo_l = _experts(idx_c, idx_l, h2_c, h2_l, sgate_c, sgate_l, w_gate[l], w_up[l], w_down[l])
    fn = row(final_norm)
    y_c = _combine(starts_c, pos_c, eo_c, x1_c, mod, fn, cap_c, False, seq)
    y_l = _combine(starts_l, pos_l, eo_l, x1_l, mod, fn, cap_l, True, dec_seq)

    return (y_c.reshape(batch, seq, D_MODEL), y_l.reshape(dec_batch, dec_seq, D_MODEL), state_k, state_v)
```

```python
import functools

import jax
import jax.numpy as jnp
from jax import lax
from jax.experimental import pallas as pl
from jax.experimental.pallas import tpu as pltpu

F32 = jnp.float32
BF16 = jnp.bfloat16
I32 = jnp.int32

D_MODEL = 2048
GRID_W = 64
HEAD_DIM = 128
N_Q_HEADS = 8
N_KV_HEADS = 2
Q_PER_KV = N_Q_HEADS // N_KV_HEADS
ATTN_WIDTH = N_Q_HEADS * HEAD_DIM
KV_WIDTH = N_KV_HEADS * HEAD_DIM
N_MLP_GROUPS = 8
MLP_WIDTH = 1024
Q_END = ATTN_WIDTH
K_END = Q_END + KV_WIDTH
V_END = K_END + KV_WIDTH
U_END = V_END + MLP_WIDTH
IN_WIDTH = U_END + MLP_WIDTH
CHUNK = 128
WINDOW = 128
N_EXPERTS = 16
EXPERT_FF = 2048
CAPACITY_FACTOR = 2
ROPE_BASE = 10000.0
EPS = 1e-6
NEG_INF = -1e30
ATTN_SCALE = HEAD_DIM ** -0.5

LANES = 128
MIX_TILE = 256
MIX_SUB = 2
LAT_SPAN = MIX_TILE + 2 * WINDOW
ROW_TILES = D_MODEL // LANES
MOD_ROWS = 8


def _rms(x):
    return x * lax.rsqrt(jnp.mean(x * x, axis=-1, keepdims=True) + EPS)


def _split_bf16_f32(x):
    hi = x.astype(BF16)
    return hi, x - hi.astype(F32)


def _adaln_body(c_ref, w_ref, b_ref, o_ref):
    s = jax.nn.silu(c_ref[...])
    o_ref[...] = jnp.dot(s.astype(BF16), w_ref[...].astype(BF16),
                         preferred_element_type=F32) + b_ref[...]


def _adaln(cvec, w_ada, b_ada):
    tn = 1024
    width = w_ada.shape[1]
    return pl.pallas_call(
        _adaln_body,
        grid=(width // tn,),
        in_specs=[pl.BlockSpec((MOD_ROWS, D_MODEL), lambda j: (0, 0)),
                  pl.BlockSpec((D_MODEL, tn), lambda j: (0, j)),
                  pl.BlockSpec((1, tn), lambda j: (0, j))],
        out_specs=pl.BlockSpec((MOD_ROWS, tn), lambda j: (0, j)),
        out_shape=jax.ShapeDtypeStruct((MOD_ROWS, width), F32),
        compiler_params=pltpu.CompilerParams(vmem_limit_bytes=40 * 2**20),
        name="adaln",
    )(cvec, w_ada, b_ada)


def _rope(x, cos, sin, n_heads):
    lane = lax.broadcasted_iota(I32, (x.shape[0], HEAD_DIM), 1)
    first = (lane & 32) == 0
    outs = []
    for h in range(n_heads):
        xh = x[:, h * HEAD_DIM:(h + 1) * HEAD_DIM]
        partner = jnp.where(first, pltpu.roll(xh, 96, 1), pltpu.roll(xh, 32, 1))
        outs.append(xh * cos + partner * sin)
    return jnp.concatenate(outs, axis=1)


def _inproj_body(*refs, latent, tiles_per_seq):
    if latent:
        x_ref, mod_ref, g_ref, w_ref, cos_ref, sin_ref, q_ref, k_ref, v_ref, u_ref, gv_ref = refs
    else:
        x_ref, mod_ref, g_ref, w_ref, q_ref, k_ref, v_ref, u_ref, gv_ref = refs
    i = pl.program_id(0)
    row = 1 + i // tiles_per_seq if latent else 0
    shift = mod_ref[pl.ds(row, 1), 0:D_MODEL]
    scale = mod_ref[pl.ds(row, 1), D_MODEL:2 * D_MODEL]
    h = _rms(x_ref[...]) * g_ref[...] * (1 + scale) + shift
    z = jnp.dot(h.astype(BF16), w_ref[...], preferred_element_type=F32)
    q = z[:, 0:Q_END]
    k = z[:, Q_END:K_END]
    if latent:
        cos = cos_ref[...]
        sin = sin_ref[...]
        q = _rope(q, cos, sin, N_Q_HEADS)
        k = _rope(k, cos, sin, N_KV_HEADS)
    q_ref[...] = q.astype(BF16)
    k_ref[...] = k
    v_ref[...] = z[:, K_END:V_END]
    u_ref[...] = z[:, V_END:U_END]
    gv_ref[...] = z[:, U_END:IN_WIDTH]


def _inproj(x2d, mod, norm1, w_in_bf16, rope_tables, seq_len):
    n = x2d.shape[0]
    tm = 256
    latent = rope_tables is not None
    tiles_per_seq = seq_len // tm
    in_specs = [pl.BlockSpec((tm, D_MODEL), lambda i: (i, 0)),
                pl.BlockSpec(mod.shape, lambda i: (0, 0)),
                pl.BlockSpec((1, D_MODEL), lambda i: (0, 0)),
                pl.BlockSpec((D_MODEL, IN_WIDTH), lambda i: (0, 0))]
    args = [x2d, mod, norm1, w_in_bf16]
    if latent:
        in_specs += [pl.BlockSpec((tm, HEAD_DIM), lambda i: (i % tiles_per_seq, 0))] * 2
        args += list(rope_tables)
    widths = (ATTN_WIDTH, KV_WIDTH, KV_WIDTH, MLP_WIDTH, MLP_WIDTH)
    dtypes = (BF16, F32, F32, F32, F32)
    return pl.pallas_call(
        functools.partial(_inproj_body, latent=latent, tiles_per_seq=tiles_per_seq),
        grid=(n // tm,),
        in_specs=in_specs,
        out_specs=[pl.BlockSpec((tm, w), lambda i: (i, 0)) for w in widths],
        out_shape=[jax.ShapeDtypeStruct((n, w), dt) for w, dt in zip(widths, dtypes)],
        compiler_params=pltpu.CompilerParams(vmem_limit_bytes=56 * 2**20),
        name="inproj_lat" if latent else "inproj_ctx",
    )(*args)


def _mix_body(*refs, latent, seq_len):
    if latent:
        sink_ref, q_ref, k_ref, v_ref, ck_ref, cv_ref, u_ref, gv_ref, x_ref, mod_ref, *rest = refs
        proj_refs = (q_ref, k_ref, v_ref, ck_ref, cv_ref, u_ref, gv_ref)
    else:
        sink_ref, x_ref, mod_ref, norm1_ref, win_ref, *rest = refs
        proj_refs = (norm1_ref, win_ref) + tuple(rest[-2:])
        rest = rest[:-2]
    b = pl.program_id(0)
    t = pl.program_id(1)
    row = 1 + b if latent else 0
    for sub in range(MIX_SUB):
        _mix_tile(sub, t * MIX_SUB + sub, row, latent, seq_len, sink_ref, proj_refs, x_ref, mod_ref, *rest)


def _mix_tile(sub, tile, row, latent, seq_len, sink_ref, proj_refs, x_ref,
              mod_ref, ws_ref, bst_ref, gvn_ref, gattn_ref, gmlp_ref, wout_ref, norm2_ref, wrt_ref,
              x1_ref, h2_ref, afft_ref):
    tq = MIX_TILE
    rows = slice(sub * tq, (sub + 1) * tq)

    if latent:
        q_ref, k_ref, v_ref, ck_ref, cv_ref, u_ref, gv_ref = proj_refs
        q = q_ref[rows, :]
        u = u_ref[rows, :]
        gv = gv_ref[rows, :]
        ks = pl.multiple_of(jnp.clip(tile * tq - WINDOW, 0, seq_len - LAT_SPAN), WINDOW)
        kcat = jnp.concatenate([ck_ref[0], k_ref[pl.ds(ks, LAT_SPAN), :]], axis=0).astype(BF16)
        vcat = jnp.concatenate([cv_ref[0], v_ref[pl.ds(ks, LAT_SPAN), :]], axis=0).astype(BF16)
        n_ctx = ck_ref.shape[1]
        n_keys = n_ctx + LAT_SPAN
        kidx = lax.broadcasted_iota(I32, (tq, n_keys), 1)
        qpos = tile * tq + lax.broadcasted_iota(I32, (tq, n_keys), 0)
        kpos = ks + kidx - n_ctx
        mask = (kidx < n_ctx) | (jnp.abs(qpos - kpos) <= WINDOW)
    else:
        norm1_ref, win_ref, kout_ref, vout_ref = proj_refs
        shift1 = mod_ref[pl.ds(row, 1), 0:D_MODEL]
        scale1 = mod_ref[pl.ds(row, 1), D_MODEL:2 * D_MODEL]
        h1 = _rms(x_ref[rows, :]) * norm1_ref[...] * (1 + scale1) + shift1
        z = jnp.dot(h1.astype(BF16), win_ref[...], preferred_element_type=F32)
        q = z[:, 0:Q_END].astype(BF16)
        k = z[:, Q_END:K_END]
        v = z[:, K_END:V_END]
        u = z[:, V_END:U_END]
        gv = z[:, U_END:IN_WIDTH]
        kout_ref[rows, :] = k
        vout_ref[rows, :] = v
        kcat = k.astype(BF16)
        vcat = v.astype(BF16)
        mask = None

    heads = []
    for h in range(N_Q_HEADS):
        kv = h // Q_PER_KV
        qh = q[:, h * HEAD_DIM:(h + 1) * HEAD_DIM]
        kh = kcat[:, kv * HEAD_DIM:(kv + 1) * HEAD_DIM]
        vh = vcat[:, kv * HEAD_DIM:(kv + 1) * HEAD_DIM]
        s = lax.dot_general(qh, kh, (((1,), (1,)), ((), ())), preferred_element_type=F32) * ATTN_SCALE
        if mask is not None:
            s = jnp.where(mask, s, NEG_INF)
        sink = sink_ref[h]
        m = jnp.maximum(jnp.max(s, axis=1, keepdims=True), sink)
        p = jnp.exp(s - m)
        den = jnp.sum(p, axis=1, keepdims=True) + jnp.exp(sink - m)
        heads.append(jnp.dot(p.astype(BF16), vh, preferred_element_type=F32) / den)
    attn = jnp.concatenate(heads, axis=1)

    sg = (_rms(jax.nn.gelu(gv)) * gvn_ref[...]).astype(BF16)
    bst = bst_ref[...]
    chunks = []
    for c in range(tq // CHUNK):
        groups = []
        for g in range(N_MLP_GROUPS):
            blk = sg[c * CHUNK:(c + 1) * CHUNK, g * LANES:(g + 1) * LANES]
            groups.append(jnp.dot(ws_ref[g], blk, preferred_element_type=F32) + bst[:, g:g + 1])
        chunks.append(jnp.concatenate(groups, axis=1))
    mlp = jax.nn.gelu(u) * jnp.concatenate(chunks, axis=0)

    mixed = jnp.concatenate([_rms(attn) * gattn_ref[...], _rms(mlp) * gmlp_ref[...]], axis=1)
    out = jnp.dot(mixed.astype(BF16), wout_ref[...], preferred_element_type=F32)
    gate1 = mod_ref[pl.ds(row, 1), 2 * D_MODEL:3 * D_MODEL]
    x1 = x_ref[rows, :] + gate1 * out
    x1_ref[rows, :] = x1

    shift2 = mod_ref[pl.ds(row, 1), 3 * D_MODEL:4 * D_MODEL]
    scale2 = mod_ref[pl.ds(row, 1), 4 * D_MODEL:5 * D_MODEL]
    h2 = _rms(x1) * norm2_ref[...] * (1 + scale2) + shift2

    hh = h2.astype(BF16)
    hl = (h2 - hh.astype(F32)).astype(BF16)
    wrt = wrt_ref[...]
    wth = wrt.astype(BF16)
    wtl = (wrt - wth.astype(F32)).astype(BF16)
    nt = (((1,), (1,)), ((), ()))
    both = lax.dot_general(jnp.concatenate([wth, wtl], axis=0), hh, nt, preferred_element_type=F32)
    logits_t = (both[0:N_EXPERTS] + both[N_EXPERTS:2 * N_EXPERTS]
                + lax.dot_general(wth, hl, nt, preferred_element_type=F32))
    pt = jnp.exp(logits_t - jnp.max(logits_t, axis=0, keepdims=True))
    afft_ref[:, rows] = pt / jnp.sum(pt, axis=0, keepdims=True)

    h2_ref[rows] = hh.reshape(tq, ROW_TILES, LANES)


def _mix(sink, projected, x2d, mod, norm1, w_in_bf16, w_s_bf16, b_s_t, g_v, g_attn, g_mlp, w_out_bf16, norm2,
         w_router_t, batch, seq_len):
    n = x2d.shape[0]
    latent = projected is not None
    block = MIX_SUB * MIX_TILE
    const2 = lambda b, t: (0, 0)
    resident = dict(pipeline_mode=pl.Buffered(1))
    if latent:
        grid = (batch, seq_len // block)
        tiles = grid[1]
        tok = lambda b, t: (b * tiles + t, 0)
        q, k, v, ck, cv, u, gv = projected
        seq = lambda b, t: (b, 0)
        in_specs = [pl.BlockSpec(memory_space=pltpu.SMEM),
                    pl.BlockSpec((block, ATTN_WIDTH), tok),
                    pl.BlockSpec((seq_len, KV_WIDTH), seq),
                    pl.BlockSpec((seq_len, KV_WIDTH), seq),
                    pl.BlockSpec((1,) + ck.shape[1:], lambda b, t: (b, 0, 0)),
                    pl.BlockSpec((1,) + cv.shape[1:], lambda b, t: (b, 0, 0)),
                    pl.BlockSpec((block, MLP_WIDTH), tok),
                    pl.BlockSpec((block, MLP_WIDTH), tok),
                    pl.BlockSpec((block, D_MODEL), tok),
                    pl.BlockSpec(mod.shape, const2)]
        args = [sink, q, k, v, ck, cv, u, gv, x2d, mod]
    else:
        assert seq_len == MIX_TILE
        grid = (batch // MIX_SUB, 1)
        tiles = 1
        tok = lambda b, t: (b, 0)
        in_specs = [pl.BlockSpec(memory_space=pltpu.SMEM),
                    pl.BlockSpec((block, D_MODEL), tok),
                    pl.BlockSpec(mod.shape, const2),
                    pl.BlockSpec((1, D_MODEL), const2),
                    pl.BlockSpec(w_in_bf16.shape, const2, **resident)]
        args = [sink, x2d, mod, norm1, w_in_bf16]
    in_specs += [pl.BlockSpec(w_s_bf16.shape, lambda b, t: (0, 0, 0)),
                 pl.BlockSpec(b_s_t.shape, const2),
                 pl.BlockSpec((1, MLP_WIDTH), const2),
                 pl.BlockSpec((1, ATTN_WIDTH), const2),
                 pl.BlockSpec((1, MLP_WIDTH), const2),
                 pl.BlockSpec(w_out_bf16.shape, const2, **resident),
                 pl.BlockSpec((1, D_MODEL), const2),
                 pl.BlockSpec(w_router_t.shape, const2)]
    args += [w_s_bf16, b_s_t, g_v, g_attn, g_mlp, w_out_bf16, norm2, w_router_t]
    out_specs = [pl.BlockSpec((block, D_MODEL), tok),
                 pl.BlockSpec((block, ROW_TILES, LANES), lambda b, t: tok(b, t) + (0,)),
                 pl.BlockSpec((N_EXPERTS, block), lambda b, t: (0, b * tiles + t))]
    out_shape = [jax.ShapeDtypeStruct((n, D_MODEL), F32),
                 jax.ShapeDtypeStruct((n, ROW_TILES, LANES), BF16),
                 jax.ShapeDtypeStruct((N_EXPERTS, n), F32)]
    if not latent:
        out_specs += [pl.BlockSpec((block, KV_WIDTH), tok)] * 2
        out_shape += [jax.ShapeDtypeStruct((n, KV_WIDTH), F32)] * 2
    return pl.pallas_call(
        functools.partial(_mix_body, latent=latent, seq_len=seq_len),
        grid=grid,
        in_specs=in_specs,
        out_specs=out_specs,
        out_shape=out_shape,
        compiler_params=pltpu.CompilerParams(vmem_limit_bytes=58 * 2**20),
        name="mix_lat" if latent else "mix_ctx",
    )(*args)


def _exclusive_counts(flags, upper, lower):
    n_e, nb, _ = flags.shape
    f = flags.astype(BF16)
    local = jnp.dot(f.reshape(n_e * nb, LANES), upper, preferred_element_type=F32).reshape(n_e, nb, LANES)
    totals = jnp.broadcast_to(jnp.sum(flags, axis=2, keepdims=True), flags.shape).astype(BF16)
    before = jnp.stack([jnp.dot(lower, totals[e], preferred_element_type=F32) for e in range(n_e)], axis=0)
    return local, before


def _route_body(aff_ref, pos_ref, idx_ref, sgate_ref, start_ref, local_ref, gate_ref, total_ref, *, cap):
    aff = aff_ref[...]
    n_e, nb, _ = aff.shape
    reduce_tokens = lambda x: jnp.sum(jnp.sum(x, axis=2, keepdims=True), axis=1, keepdims=True)

    def bisect(i, thr):
        cand = thr | jnp.left_shift(jnp.int32(1), 30 - i)
        n_ge = reduce_tokens((aff >= lax.bitcast_convert_type(cand, F32)).astype(F32))
        return jnp.where(n_ge >= cap, cand, thr)

    thr = lax.fori_loop(0, 31, bisect, jnp.zeros((n_e, 1, 1), I32))

    upper = (lax.broadcasted_iota(I32, (LANES, LANES), 0) < lax.broadcasted_iota(I32, (LANES, LANES), 1)).astype(BF16)
    lower = (lax.broadcasted_iota(I32, (nb, nb), 1) < lax.broadcasted_iota(I32, (nb, nb), 0)).astype(BF16)

    above = aff >= lax.bitcast_convert_type(thr + 1, F32)
    tied = (aff >= lax.bitcast_convert_type(thr, F32)) & jnp.logical_not(above)
    need = cap - reduce_tokens(above.astype(F32))
    tie_local, tie_before = _exclusive_counts(tied.astype(F32), upper, lower)
    sel = above | (tied & (tie_local + tie_before < need))
    sel_f = sel.astype(F32)

    local, before = _exclusive_counts(sel_f, upper, lower)
    pos_ref[...] = jnp.where(sel, (local + before).astype(I32), -1)
    gate_ref[...] = aff

    local_ref[...] = (local + sel_f).astype(BF16)
    start_ref[...] = before[:, :, 0:1]
    total_ref[...] = jnp.sum(sel_f, axis=2, keepdims=True)
    slot = lax.broadcasted_iota(I32, (1, cap), 1).astype(F32)
    block_id = lax.broadcasted_iota(I32, (nb, 1), 0).astype(F32)
    lane_id = lax.broadcasted_iota(I32, (LANES, 1), 0).astype(F32)
    tn = (((0,), (0,)), ((), ()))

    def lists(e, carry):
        start = start_ref[e]
        end = start + total_ref[e]
        blk = jnp.sum((end <= slot).astype(F32), axis=0, keepdims=True)
        onehot = block_id == blk
        onehot_b = onehot.astype(BF16)
        rank = slot - jnp.sum(jnp.where(onehot, start, 0.0), axis=0, keepdims=True)
        counts = lax.dot_general(local_ref[e], onehot_b, tn, preferred_element_type=F32)
        tok = jnp.sum((counts <= rank).astype(F32), axis=0, keepdims=True)
        idx_ref[pl.ds(e, 1), :] = (blk * LANES + tok).astype(I32)
        g1, rest = _split_bf16_f32(gate_ref[e])
        g2, rest = _split_bf16_f32(rest)
        block_gates = sum(lax.dot_general(g, onehot_b, tn, preferred_element_type=F32)
                          for g in (g1, g2, rest.astype(BF16)))
        sgate_ref[pl.ds(e, 1), :] = jnp.sum(jnp.where(lane_id == tok, block_gates, 0.0), axis=0, keepdims=True)
        return carry

    lax.fori_loop(0, n_e, lists, 0)


def _route(aff_blocks, cap):
    n_e, nb, _ = aff_blocks.shape
    return pl.pallas_call(
        functools.partial(_route_body, cap=cap),
        out_shape=[jax.ShapeDtypeStruct(aff_blocks.shape, I32), jax.ShapeDtypeStruct((n_e, cap), I32),
                   jax.ShapeDtypeStruct((n_e, cap), F32), jax.ShapeDtypeStruct((n_e, nb, 1), F32)],
        scratch_shapes=[pltpu.VMEM(aff_blocks.shape, BF16), pltpu.VMEM(aff_blocks.shape, F32),
                        pltpu.VMEM((n_e, nb, 1), F32)],
        compiler_params=pltpu.CompilerParams(vmem_limit_bytes=48 * 2**20),
        name="route",
    )(aff_blocks)


FF_TILE = 512
N_FF_TILES = EXPERT_FF // FF_TILE
OUT_TILE = 512
N_OUT_TILES = D_MODEL // OUT_TILE
EXPERT_STEPS = N_FF_TILES + N_OUT_TILES


def _expert_body(idxc_ref, idxl_ref, hc_ref, hl_ref, gc_ref, gl_ref, wg_ref, wu_ref, wd_ref,
                 oc_ref, ol_ref, gbuf, xb_ref, hid_ref, sem, *, cap_c, cap_l):
    e = pl.program_id(0)
    s = pl.program_id(1)
    cap = cap_c + cap_l
    next_expert = (e + 1) % N_EXPERTS

    def issue(expert, part, parts):
        for idx_ref, src, n_rows, base in ((idxc_ref, hc_ref, cap_c, 0), (idxl_ref, hl_ref, cap_l, cap_c)):
            per = n_rows // parts
            for j in range(per):
                r = part * per + j
                pltpu.make_async_copy(src.at[idx_ref[expert * n_rows + r]], gbuf.at[base + r],
                                      sem).start(priority=j % 2)

    def arrived():
        pltpu.make_async_copy(hc_ref.at[pl.ds(0, cap)], gbuf, sem).wait()

    def unpack(part, parts):
        per = cap // parts
        r0 = pl.multiple_of(part * per, per)
        xb_ref[pl.ds(r0, per), :] = gbuf[pl.ds(r0, per)].reshape(per, D_MODEL)

    @pl.when((e == 0) & (s == 0))
    def _first():
        def gather(part, carry):
            issue(0, part, N_FF_TILES)
            return carry
        lax.fori_loop(0, N_FF_TILES, gather, 0)
        arrived()

        def relayout(part, carry):
            unpack(part, N_OUT_TILES)
            return carry
        lax.fori_loop(0, N_OUT_TILES, relayout, 0)

    @pl.when(s < N_FF_TILES)
    def _up():
        issue(next_expert, s, N_FF_TILES)
        x = xb_ref[...]
        a = jnp.dot(x, wg_ref[0].astype(BF16), preferred_element_type=F32)
        b = jnp.dot(x, wu_ref[0].astype(BF16), preferred_element_type=F32)
        hid_ref[s] = (jax.nn.silu(a) * b).astype(BF16)

    @pl.when(s == N_FF_TILES)
    def _arrived():
        arrived()

    @pl.when(s >= N_FF_TILES)
    def _down():
        unpack(s - N_FF_TILES, N_OUT_TILES)
        wd = wd_ref[0].astype(BF16)
        acc = jnp.dot(hid_ref[0], wd[0:FF_TILE, :], preferred_element_type=F32)
        for j in range(1, N_FF_TILES):
            acc = acc + jnp.dot(hid_ref[j], wd[j * FF_TILE:(j + 1) * FF_TILE, :], preferred_element_type=F32)
        for g_ref, o_ref, rows in ((gc_ref, oc_ref, slice(0, cap_c)), (gl_ref, ol_ref, slice(cap_c, cap))):
            gates = g_ref[...]
            lane = lax.broadcasted_iota(I32, gates.shape, 1)
            gate = jnp.sum(jnp.where(lane == e, gates, 0.0), axis=1, keepdims=True)
            o_ref[...] = (acc[rows, :] * gate).astype(BF16)


def _experts(idx_c, idx_l, h2_c, h2_l, sgate_c, sgate_l, w_gate, w_up, w_down):
    cap_c = idx_c.shape[1]
    cap_l = idx_l.shape[1]
    cap = cap_c + cap_l
    idx_c = idx_c.reshape(-1)
    idx_l = idx_l.reshape(-1)
    const = lambda e, s, ic, il: (0, 0)
    up_idx = lambda e, s, ic, il: (e, 0, jnp.minimum(s, N_FF_TILES - 1))
    down_idx = lambda e, s, ic, il: (e, 0, jnp.maximum(s - N_FF_TILES, 0))
    out_idx = lambda e, s, ic, il: (e, jnp.maximum(s - N_FF_TILES, 0))
    out_c = jax.ShapeDtypeStruct((N_EXPERTS * cap_c, D_MODEL), BF16)
    out_l = jax.ShapeDtypeStruct((N_EXPERTS * cap_l, D_MODEL), BF16)
    return pl.pallas_call(
        functools.partial(_expert_body, cap_c=cap_c, cap_l=cap_l),
        grid_spec=pltpu.PrefetchScalarGridSpec(
            num_scalar_prefetch=2,
            grid=(N_EXPERTS, EXPERT_STEPS),
            in_specs=[pl.BlockSpec(memory_space=pl.ANY),
                      pl.BlockSpec(memory_space=pl.ANY),
                      pl.BlockSpec(sgate_c.shape, const),
                      pl.BlockSpec(sgate_l.shape, const),
                      pl.BlockSpec((1, D_MODEL, FF_TILE), up_idx),
                      pl.BlockSpec((1, D_MODEL, FF_TILE), up_idx),
                      pl.BlockSpec((1, EXPERT_FF, OUT_TILE), down_idx)],
            out_specs=[pl.BlockSpec((cap_c, OUT_TILE), out_idx), pl.BlockSpec((cap_l, OUT_TILE), out_idx)],
            scratch_shapes=[pltpu.VMEM((cap, ROW_TILES, LANES), BF16),
                            pltpu.VMEM((cap, D_MODEL), BF16),
                            pltpu.VMEM((N_FF_TILES, cap, FF_TILE), BF16),
                            pltpu.SemaphoreType.DMA(())]),
        out_shape=[out_c, out_l],
        compiler_params=pltpu.CompilerParams(vmem_limit_bytes=60 * 2**20),
        name="experts",
    )(idx_c, idx_l, h2_c, h2_l, sgate_c, sgate_l, w_gate, w_up, w_down)


SUBLANES = 16
COMBINE_CTX = (256, 64)
COMBINE_LAT = (128, 80)


def _combine_body(bs_ref, pos_ref, eo_ref, x1_ref, mod_ref, fn_ref, y_ref, rows, sem, *,
                  tb, cap, win, n_blocks, latent, tiles_per_seq):
    i = pl.program_id(0)
    slot = i % 2

    def window_base(e, blk):
        return (bs_ref[e, blk] // SUBLANES) * SUBLANES

    def window_start(e, blk, p):
        return jnp.minimum(window_base(e, blk) + p * win, cap - win)

    def fetch(blk, p, to_slot):
        for e in range(N_EXPERTS):
            start = pl.multiple_of(e * cap + window_start(e, blk, p), SUBLANES)
            pltpu.make_async_copy(eo_ref.at[pl.ds(start, win)], rows.at[to_slot, pl.ds(e * win, win)],
                                  sem.at[to_slot]).start()

    def wait(to_slot):
        pltpu.make_async_copy(eo_ref.at[pl.ds(0, N_EXPERTS * win)], rows.at[to_slot], sem.at[to_slot]).wait()

    @pl.when(i == 0)
    def _first():
        fetch(0, 0, 0)

    @pl.when(i + 1 < n_blocks)
    def _prefetch():
        fetch(i + 1, 0, 1 - slot)

    lane = lax.broadcasted_iota(I32, (1, N_EXPERTS), 1)
    col = lax.broadcasted_iota(I32, (1, (N_EXPERTS * win)), 1)
    col_row = (col % win).astype(F32)
    expand = (lax.broadcasted_iota(I32, (N_EXPERTS, (N_EXPERTS * win)), 0) == col // win).astype(BF16)
    widen = lambda v: jnp.dot(v, expand, preferred_element_type=F32)

    pos = pos_ref[...]

    def one_pass(p, acc):
        base = jnp.zeros((1, N_EXPERTS), I32)
        start = jnp.zeros((1, N_EXPERTS), I32)
        for e in range(N_EXPERTS):
            base = jnp.where(lane == e, window_base(e, i), base)
            start = jnp.where(lane == e, window_start(e, i, p), start)
        rel = pos - base
        mine = (pos >= 0) & (rel >= p * win) & (rel < (p + 1) * win)
        window_col = jnp.where(mine, pos - start, -1).astype(F32).astype(BF16)
        select = (widen(window_col) == col_row).astype(BF16)
        return acc + jnp.dot(select, rows[slot], preferred_element_type=F32)

    wait(slot)
    moe = one_pass(0, jnp.zeros((tb, D_MODEL), F32))

    n_pass = jnp.int32(1)
    for e in range(N_EXPERTS):
        span = bs_ref[e, i + 1] - window_base(e, i)
        n_pass = jnp.maximum(n_pass, (span + win - 1) // win)

    def extra(p, acc):
        fetch(i, p, slot)
        wait(slot)
        return one_pass(p, acc)

    moe = lax.fori_loop(1, n_pass, extra, moe)
    row = 1 + i // tiles_per_seq if latent else 0
    gate2 = mod_ref[pl.ds(row, 1), 5 * D_MODEL:6 * D_MODEL]
    y_ref[...] = _rms(x1_ref[...] + gate2 * moe) * fn_ref[...]


def _combine(block_start, pos_t, expert_out, x1, mod, final_norm, cap, latent, seq_len):
    n = x1.shape[0]
    tb, win = COMBINE_LAT if latent else COMBINE_CTX
    n_blocks = n // tb
    return pl.pallas_call(
        functools.partial(_combine_body, tb=tb, cap=cap, win=win, n_blocks=n_blocks, latent=latent,
                          tiles_per_seq=seq_len // tb),
        grid_spec=pltpu.PrefetchScalarGridSpec(
            num_scalar_prefetch=1,
            grid=(n_blocks,),
            in_specs=[pl.BlockSpec((tb, N_EXPERTS), lambda i, bs: (i, 0)),
                      pl.BlockSpec(memory_space=pl.ANY),
                      pl.BlockSpec((tb, D_MODEL), lambda i, bs: (i, 0)),
                      pl.BlockSpec(mod.shape, lambda i, bs: (0, 0)),
                      pl.BlockSpec((1, D_MODEL), lambda i, bs: (0, 0))],
            out_specs=pl.BlockSpec((tb, D_MODEL), lambda i, bs: (i, 0)),
            scratch_shapes=[pltpu.VMEM((2, N_EXPERTS * win, D_MODEL), BF16), pltpu.SemaphoreType.DMA((2,))]),
        out_shape=jax.ShapeDtypeStruct((n, D_MODEL), F32),
        compiler_params=pltpu.CompilerParams(vmem_limit_bytes=52 * 2**20),
        name="combine_lat" if latent else "combine_ctx",
    )(block_start, pos_t, expert_out, x1, mod, final_norm)


def _rope_tables(seq_len):
    pos = jnp.arange(seq_len)
    half = HEAD_DIM // 2
    freqs = ROPE_BASE ** (-jnp.arange(0, half, 2, dtype=F32) / half)
    ang_r = (pos // GRID_W).astype(F32)[:, None] * freqs[None, :]
    ang_c = (pos % GRID_W).astype(F32)[:, None] * freqs[None, :]
    cos = jnp.concatenate([jnp.cos(ang_r)] * 2 + [jnp.cos(ang_c)] * 2, axis=1)
    sin = jnp.concatenate([-jnp.sin(ang_r), jnp.sin(ang_r), -jnp.sin(ang_c), jnp.sin(ang_c)], axis=1)
    return cos, sin


def kernel(x_prompt, x_sample, cache_k, cache_v, c, c_ctx, w_ada, b_ada, norm1, w_in, attn_sink, w_s, b_s, g_v,
           g_attn, g_mlp, w_out, norm2, w_router, w_gate, w_up, w_down, final_norm):
    depth = w_ada.shape[0]
    assert depth == 1, "single-layer trunk"
    batch, seq, _ = x_prompt.shape
    dec_batch, dec_seq, _ = x_sample.shape
    n_ctx = batch * seq
    n_lat = dec_batch * dec_seq
    cap_c = CAPACITY_FACTOR * n_ctx // N_EXPERTS
    cap_l = CAPACITY_FACTOR * n_lat // N_EXPERTS
    l = 0

    cvec = jnp.concatenate([c_ctx[None, :], c, jnp.zeros((MOD_ROWS - 1 - dec_batch, D_MODEL), F32)], axis=0)
    mod = _adaln(cvec, w_ada[l], b_ada[l][None, :])

    w_in_b = w_in[l].astype(BF16)
    w_out_b = w_out[l].astype(BF16)
    w_s_b = w_s[l].astype(BF16)
    b_s_t = jnp.transpose(b_s[l])
    w_router_t = jnp.transpose(w_router[l])
    row = lambda a: a.reshape(1, -1)
    shared = (mod, row(norm1[l]), w_in_b, w_s_b, b_s_t, row(g_v[l]), row(g_attn[l]), row(g_mlp[l]), w_out_b,
              row(norm2[l]), w_router_t)

    xp = x_prompt.reshape(n_ctx, D_MODEL)
    xl = x_sample.reshape(n_lat, D_MODEL)

    x1_c, h2_c, afft_c, k, v = _mix(attn_sink[l], None, xp, *shared, batch, seq)
    state_k = k.reshape(batch, 1, seq, N_KV_HEADS, HEAD_DIM)
    state_v = v.reshape(batch, 1, seq, N_KV_HEADS, HEAD_DIM)

    q, k, v, u, gv = _inproj(xl, mod, row(norm1[l]), w_in_b, _rope_tables(dec_seq), dec_seq)
    projected = (q, k, v, cache_k[:, l].reshape(dec_batch, -1, KV_WIDTH),
                 cache_v[:, l].reshape(dec_batch, -1, KV_WIDTH), u, gv)
    x1_l, h2_l, afft_l = _mix(attn_sink[l], projected, xl, *shared, dec_batch, dec_seq)

    def route(aff_t, cap, combine_tile):
        n = aff_t.shape[1]
        pos, idx, slot_gate, start = _route(aff_t.reshape(N_EXPERTS, n // LANES, LANES), cap)
        start = start.reshape(N_EXPERTS, -1)[:, ::combine_tile // LANES].astype(I32)
        starts = jnp.concatenate([start, jnp.full((N_EXPERTS, 1), cap, I32)], axis=1)
        return jnp.transpose(pos.reshape(N_EXPERTS, n)), idx, jnp.transpose(slot_gate), starts

    pos_c, idx_c, sgate_c, starts_c = route(afft_c, cap_c, COMBINE_CTX[0])
    pos_l, idx_l, sgate_l, starts_l = route(afft_l, cap_l, COMBINE_LAT[0])
    eo_c, eo_l = _experts(idx_c, idx_l, h2_c, h2_l, sgate_c, sgate_l, w_gate[l], w_up[l], w_down[l])
    fn = row(final_norm)
    y_c = _combine(starts_c, pos_c, eo_c, x1_c, mod, fn, cap_c, False, seq)
    y_l = _combine(starts_l, pos_l, eo_l, x1_l, mod, fn, cap_l, True, dec_seq)

    return (y_c.reshape(batch, seq, D_MODEL), y_l.reshape(dec_batch, dec_seq, D_MODEL), state_k, state_v)
```
